```python
import math
import jax
import jax.numpy as jnp
from jax import lax
import numpy as np

D_MODEL = 1024
BATCH = 4
SEQ = 8192
DEPTH = 2

CTX_LEN = 256
GRID_W = 64
HEAD_DIM = 64
ROPE_THETA = 10000.0
Q_BLOCK = 128
A_HEADS = 8
A_KV_HEADS = 2
W_HEADS = 8
W_KV_HEADS = 2
WINDOW = 128
SSM_HEADS = 16
SSM_HEAD_DIM = 64
SSM_INNER = SSM_HEADS * SSM_HEAD_DIM
SSM_GROUPS = 2
SSM_STATE = 128
SSM_CONV = 3
SSM_CONV_CH = SSM_INNER + 2 * SSM_GROUPS * SSM_STATE
SSM_CHUNK = 128
N_BRANCH = 3
IN_SPLITS = (A_HEADS * HEAD_DIM, A_KV_HEADS * HEAD_DIM, A_KV_HEADS * HEAD_DIM,
             W_HEADS * HEAD_DIM, W_KV_HEADS * HEAD_DIM, W_KV_HEADS * HEAD_DIM,
             SSM_INNER, SSM_CONV_CH, 2 * SSM_HEADS, N_BRANCH * D_MODEL)
IN_PROJ_WIDTH = sum(IN_SPLITS)
N_EXPERTS = 64
TOP_K = 8
N_EXPERT_GROUPS = 8
TOPK_GROUPS = 4
D_EXPERT = 256
D_SHARED = 256
ROUTED_SCALE = 2.5
MOE_BLOCK = 128
EPS = 1e-6

kernel_name = 'hybrid_gqa_ssd_swa_moe_dit'


def rmsnorm(x, g):
    xf = x.astype(jnp.float32)
    y = xf * lax.rsqrt(jnp.mean(xf * xf, axis=-1, keepdims=True) + EPS)
    return (y * g.astype(jnp.float32)).astype(x.dtype)


def modulate(h, shift, scale):
    return h * (1 + scale) + shift


def split_in_proj(p):
    points = [int(s) for s in np.cumsum(IN_SPLITS)[:-1]]
    return jnp.split(p, points, axis=-1)


def to_heads(t):
    return t.reshape(t.shape[:-1] + (t.shape[-1] // HEAD_DIM, HEAD_DIM))


def axial_rope_tables(n):
    ROWS = n // GRID_W
    row = jnp.repeat(jnp.arange(ROWS, dtype=jnp.float32), GRID_W)
    col = jnp.tile(jnp.arange(GRID_W, dtype=jnp.float32), ROWS)
    axis_dim = HEAD_DIM // 2
    inv_freq = ROPE_THETA ** (-jnp.arange(0, axis_dim, 2, dtype=jnp.float32) / axis_dim)
    ang_r = row[:, None] * inv_freq[None, :]
    ang_c = col[:, None] * inv_freq[None, :]
    return (jnp.cos(ang_r), jnp.sin(ang_r), jnp.cos(ang_c), jnp.sin(ang_c))


def rope_1d(t, cos, sin):
    m = t.shape[-1] // 2
    t1, t2 = t[..., :m], t[..., m:]
    cos = cos[:, None, :].astype(t.dtype)
    sin = sin[:, None, :].astype(t.dtype)
    return jnp.concatenate([t1 * cos - t2 * sin, t2 * cos + t1 * sin], axis=-1)


def apply_axial_rope(t, tables):
    cr, sr, cc, sc = tables
    a = HEAD_DIM // 2
    return jnp.concatenate([rope_1d(t[..., :a], cr, sr), rope_1d(t[..., a:], cc, sc)], axis=-1)


def dense_block_attn(q, k, v, sink=None):
    bsz, n, h, hd = q.shape
    hkv = k.shape[2]
    g = h // hkv
    nb = n // Q_BLOCK
    scale = hd ** -0.5
    qb = jnp.swapaxes(q.reshape(bsz, nb, Q_BLOCK, hkv, g, hd), 0, 1)

    def one(qblk):
        s = jnp.einsum('bqkgd,bmkd->bkgqm', qblk, k).astype(jnp.float32) * scale
        if sink is not None:
            sk = jnp.broadcast_to(sink.astype(jnp.float32).reshape(1, hkv, g, 1, 1), s.shape[:-1] + (1,))
            s = jnp.concatenate([s, sk], axis=-1)
        p = jax.nn.softmax(s, axis=-1)
        if sink is not None:
            p = p[..., :-1]
        o = jnp.einsum('bkgqm,bmkd->bqkgd', p.astype(v.dtype), v)
        return o.reshape(bsz, Q_BLOCK, h * hd)

    o = lax.map(one, qb)
    return jnp.swapaxes(o, 0, 1).reshape(bsz, n, h * hd)


def window_block_attn(q, k, v, kc, vc, sink):
    bsz, n, h, hd = q.shape
    hkv = k.shape[2]
    g = h // hkv
    nb = n // Q_BLOCK
    span = Q_BLOCK + 2 * WINDOW
    scale = hd ** -0.5
    pad = ((0, 0), (WINDOW, WINDOW), (0, 0), (0, 0))
    kp = jnp.pad(k, pad)
    vp = jnp.pad(v, pad)
    sink_l = sink.astype(jnp.float32).reshape(1, hkv, g, 1, 1)

    def one(b):
        q0 = b * Q_BLOCK
        qb = lax.dynamic_slice_in_dim(q, q0, Q_BLOCK, axis=1).reshape(bsz, Q_BLOCK, hkv, g, hd)
        kb = lax.dynamic_slice_in_dim(kp, q0, span, axis=1)
        vb = lax.dynamic_slice_in_dim(vp, q0, span, axis=1)
        qpos = q0 + jnp.arange(Q_BLOCK)
        kpos = q0 - WINDOW + jnp.arange(span)
        ok = (jnp.abs(qpos[:, None] - kpos[None, :]) <= WINDOW) & (kpos[None, :] >= 0) & (kpos[None, :] < n)
        s_loc = jnp.einsum('bqkgd,bmkd->bkgqm', qb, kb).astype(jnp.float32) * scale
        s_loc = jnp.where(ok, s_loc, -jnp.inf)
        s_ctx = jnp.einsum('bqkgd,bmkd->bkgqm', qb, kc).astype(jnp.float32) * scale
        s_snk = jnp.broadcast_to(sink_l, s_loc.shape[:-1] + (1,))
        p = jax.nn.softmax(jnp.concatenate([s_loc, s_ctx, s_snk], axis=-1), axis=-1).astype(v.dtype)
        o = (jnp.einsum('bkgqm,bmkd->bqkgd', p[..., :span], vb)
             + jnp.einsum('bkgqm,bmkd->bqkgd', p[..., span:span + kc.shape[1]], vc))
        return o.reshape(bsz, Q_BLOCK, h * hd)

    o = lax.map(one, jnp.arange(nb))
    return jnp.swapaxes(o, 0, 1).reshape(bsz, n, h * hd)


def dwconv_centred(u, w, b):
    kw = w.shape[0]
    out = lax.conv_general_dilated(u, w[:, None, :].astype(u.dtype), window_strides=(1,),
                                   padding=[(kw // 2, kw // 2)], dimension_numbers=('NWC', 'WIO', 'NWC'),
                                   feature_group_count=u.shape[-1])
    return out + b.astype(u.dtype)


def ssm_prep(xbc, dt, conv_w, conv_b, dt_bias):
    u = jax.nn.silu(dwconv_centred(xbc, conv_w, conv_b))
    xs, bm, cm = jnp.split(u, [SSM_INNER, SSM_INNER + SSM_GROUPS * SSM_STATE], axis=-1)
    shp = xs.shape[:-1]
    xs = xs.reshape(shp + (SSM_HEADS, SSM_HEAD_DIM))
    bm = bm.reshape(shp + (SSM_GROUPS, SSM_STATE))
    cm = cm.reshape(shp + (SSM_GROUPS, SSM_STATE))
    dt = jax.nn.softplus(dt.astype(jnp.float32) + dt_bias.reshape(-1).astype(jnp.float32))
    return xs, bm, cm, dt[..., :SSM_HEADS], dt[..., SSM_HEADS:]


def ssd_scan(x, dt, a_neg, bm, cm, s0):
    bsz, seq, nh, hp = x.shape
    ng, ns = bm.shape[2], bm.shape[3]
    r = nh // ng
    q = SSM_CHUNK
    nc = seq // q
    a = (dt * a_neg).reshape(bsz, nc, q, ng, r)
    ac = jnp.cumsum(a, axis=2).transpose(0, 1, 3, 4, 2)
    xd = (x.astype(jnp.float32) * dt[..., None]).reshape(bsz, nc, q, ng, r, hp)
    bc = bm.astype(jnp.float32).reshape(bsz, nc, q, ng, ns)
    cc = cm.astype(jnp.float32).reshape(bsz, nc, q, ng, ns)
    tri = jnp.tril(jnp.ones((q, q), dtype=bool))
    decay = jnp.exp(jnp.where(tri, ac[..., :, None] - ac[..., None, :], -jnp.inf))
    cb = jnp.einsum('bcign,bcjgn->bcgij', cc, bc)
    y_diag = jnp.einsum('bcgrij,bcjgrp->bcigrp', cb[:, :, :, None] * decay, xd)
    end_decay = jnp.exp(ac[..., -1:] - ac).transpose(0, 1, 4, 2, 3)
    chunk_states = jnp.einsum('bcjgn,bcjgrp->bcgrpn', bc, xd * end_decay[..., None])
    chunk_decay = jnp.exp(ac[..., -1])

    def step(s, inp):
        st, dec = inp
        return s * dec[..., None, None] + st, s

    s_fin, s_start = lax.scan(step, s0, (jnp.moveaxis(chunk_states, 1, 0), jnp.moveaxis(chunk_decay, 1, 0)))
    s_start = jnp.moveaxis(s_start, 0, 1)
    in_decay = jnp.exp(ac).transpose(0, 1, 4, 2, 3)[..., None]
    y_off = jnp.einsum('bcign,bcgrpn->bcigrp', cc, s_start) * in_decay
    y = (y_diag + y_off).reshape(bsz, seq, nh, hp)
    return y.astype(x.dtype), s_fin


def flip_seq(t):
    return jnp.flip(t, axis=1)


def ssd_bidir(xs, dt_f, dt_b, bm, cm, a_neg, d_skip, s0_f, s0_b):
    y_f, s_f = ssd_scan(xs, dt_f, a_neg[0], bm, cm, s0_f)
    y_b, s_b = ssd_scan(flip_seq(xs), flip_seq(dt_b), a_neg[1], flip_seq(bm), flip_seq(cm), s0_b)
    y = y_f + flip_seq(y_b) + d_skip.astype(xs.dtype)[:, None] * xs
    return y, s_f, s_b


def gated_group_rmsnorm(y, z, g):
    u = (y * jax.nn.silu(z)).astype(jnp.float32)
    shp = u.shape
    ug = u.reshape(shp[:-1] + (SSM_GROUPS, shp[-1] // SSM_GROUPS))
    ug = ug * lax.rsqrt(jnp.mean(ug * ug, axis=-1, keepdims=True) + EPS)
    return (ug.reshape(shp) * g.astype(jnp.float32)).astype(y.dtype)


def merge_branches(oa, ow, ys, gates, w_br_a, w_br_w, w_br_s, w_out):
    ga, gw, gs = jnp.split(jax.nn.sigmoid(gates), N_BRANCH, axis=-1)
    return (ga * (oa @ w_br_a) + gw * (ow @ w_br_w) + gs * (ys @ w_br_s)) @ w_out


def hybrid_mixer(hl, hc, rope, w_in, g_q, g_k, sink, conv_w, conv_b, dt_bias, a_log, d_skip, norm_w,
                 w_br_a, w_br_w, w_br_s, w_out, with_ctx_out):
    pl = split_in_proj(hl @ w_in)
    pc = split_in_proj(hc @ w_in)
    qa = apply_axial_rope(rmsnorm(to_heads(pl[0]), g_q), rope)
    ka = apply_axial_rope(rmsnorm(to_heads(pl[1]), g_k), rope)
    va = to_heads(pl[2])
    ka_c = rmsnorm(to_heads(pc[1]), g_k)
    va_c = to_heads(pc[2])
    oa = dense_block_attn(qa, jnp.concatenate([ka, ka_c], axis=1), jnp.concatenate([va, va_c], axis=1))
    qw = apply_axial_rope(to_heads(pl[3]), rope)
    kw = apply_axial_rope(to_heads(pl[4]), rope)
    vw = to_heads(pl[5])
    kw_c = to_heads(pc[4])
    vw_c = to_heads(pc[5])
    ow = window_block_attn(qw, kw, vw, kw_c, vw_c, sink)
    a_neg = -jnp.exp(a_log.astype(jnp.float32))
    bsz = hl.shape[0]
    s0 = jnp.zeros((bsz, SSM_GROUPS, SSM_HEADS // SSM_GROUPS, SSM_HEAD_DIM, SSM_STATE), jnp.float32)
    xs_c, bm_c, cm_c, dtf_c, dtb_c = ssm_prep(pc[7], pc[8], conv_w, conv_b, dt_bias)
    ys_c, s_f, s_b = ssd_bidir(xs_c, dtf_c, dtb_c, bm_c, cm_c, a_neg, d_skip, s0, s0)
    xs_l, bm_l, cm_l, dtf_l, dtb_l = ssm_prep(pl[7], pl[8], conv_w, conv_b, dt_bias)
    ys_l, _, _ = ssd_bidir(xs_l, dtf_l, dtb_l, bm_l, cm_l, a_neg, d_skip, s_f, s_b)
    ys = gated_group_rmsnorm(ys_l.reshape(ys_l.shape[:2] + (SSM_INNER,)), pl[6], norm_w)
    ml = merge_branches(oa, ow, ys, pl[9], w_br_a, w_br_w, w_br_s, w_out)
    if not with_ctx_out:
        return ml, None
    qa_c = rmsnorm(to_heads(pc[0]), g_q)
    oa_c = dense_block_attn(qa_c, ka_c, va_c)
    ow_c = dense_block_attn(to_heads(pc[3]), kw_c, vw_c, sink)
    ysc = gated_group_rmsnorm(ys_c.reshape(ys_c.shape[:2] + (SSM_INNER,)), pc[6], norm_w)
    mc = merge_branches(oa_c, ow_c, ysc, pc[9], w_br_a, w_br_w, w_br_s, w_out)
    return ml, mc


def moe_ffn(x, w_router, b_router, we_gate, we_up, we_down, ws_gate, ws_up, ws_down):
    t_count, d = x.shape
    scores = jax.nn.sigmoid((x @ w_router).astype(jnp.float32))
    sel = scores + b_router.astype(jnp.float32)
    per_group = N_EXPERTS // N_EXPERT_GROUPS
    group_score = lax.top_k(sel.reshape(t_count, N_EXPERT_GROUPS, per_group), 2)[0].sum(-1)
    _, gidx = lax.top_k(group_score, TOPK_GROUPS)
    gmask = jnp.any(gidx[:, :, None] == jnp.arange(N_EXPERT_GROUPS)[None, None, :], axis=1)
    sel = jnp.where(jnp.repeat(gmask, per_group, axis=1), sel, -jnp.inf)
    _, eidx = lax.top_k(sel, TOP_K)
    wts = jnp.take_along_axis(scores, eidx, axis=1)
    wts = wts / jnp.sum(wts, axis=-1, keepdims=True) * ROUTED_SCALE
    tk = t_count * TOP_K
    flat_e = eidx.reshape(-1)
    order = jnp.argsort(flat_e)
    se = flat_e[order]
    st = (order // TOP_K).astype(jnp.int32)
    sw = wts.reshape(-1)[order]
    counts = jnp.bincount(flat_e, length=N_EXPERTS)
    padded = (counts + MOE_BLOCK - 1) // MOE_BLOCK * MOE_BLOCK
    pad_end = jnp.cumsum(padded)
    pad_start = pad_end - padded
    start = jnp.cumsum(counts) - counts
    dest = pad_start[se] + jnp.arange(tk) - start[se]
    n_blocks = -(-(tk + N_EXPERTS * (MOE_BLOCK - 1)) // MOE_BLOCK)
    cap = n_blocks * MOE_BLOCK
    buf_t = jnp.zeros((cap,), jnp.int32).at[dest].set(st)
    buf_w = jnp.zeros((cap,), jnp.float32).at[dest].set(sw)
    blk_e = jnp.minimum(jnp.searchsorted(pad_end, jnp.arange(n_blocks) * MOE_BLOCK, side='right'), N_EXPERTS - 1)

    def body(acc, inp):
        t_blk, w_blk, e = inp
        xb = x[t_blk]
        hb = jax.nn.silu(xb @ we_gate[e]) * (xb @ we_up[e])
        yb = (hb @ we_down[e]).astype(jnp.float32) * w_blk[:, None]
        return acc.at[t_blk].add(yb), None

    routed, _ = lax.scan(body, jnp.zeros((t_count, d), jnp.float32),
                         (buf_t.reshape(n_blocks, MOE_BLOCK), buf_w.reshape(n_blocks, MOE_BLOCK), blk_e))
    shared = (jax.nn.silu(x @ ws_gate) * (x @ ws_up)) @ ws_down
    return (routed + shared.astype(jnp.float32)).astype(x.dtype)


def setup_inputs(seed: int = 0) -> dict:
    key = jax.random.key(seed)
    ks = iter(jax.random.split(key, 40))
    f32 = jnp.float32
    D = D_MODEL
    L = DEPTH

    def nrm(shape, scale):
        return scale * jax.random.normal(next(ks), shape, f32)

    x = nrm((BATCH, SEQ, D), 1.0)
    c = nrm((BATCH, D), 1.0)
    ctx = nrm((BATCH, CTX_LEN, D), 1.0)
    c_ctx = nrm((D,), 1.0)
    w_ada = nrm((L, D, 6 * D), 0.5 * D ** -0.5)
    b_ada = nrm((L, 6 * D), 0.02)
    g_mix_pre = 1.0 + nrm((L, D), 0.05)
    g_mix_post = 1.0 + nrm((L, D), 0.05)
    g_ffn_pre = 1.0 + nrm((L, D), 0.05)
    g_ffn_post = 1.0 + nrm((L, D), 0.05)
    w_in = nrm((L, D, IN_PROJ_WIDTH), D ** -0.5)
    g_q_a = 1.0 + nrm((L, HEAD_DIM), 0.05)
    g_k_a = 1.0 + nrm((L, HEAD_DIM), 0.05)
    sink_w = nrm((L, W_HEADS), 0.5)
    ssm_conv_w = nrm((L, SSM_CONV, SSM_CONV_CH), SSM_CONV ** -0.5)
    ssm_conv_b = nrm((L, SSM_CONV_CH), 0.02)
    dt0 = jnp.exp(jax.random.uniform(next(ks), (L, 2, SSM_HEADS), f32, math.log(1e-3), math.log(1e-1)))
    ssm_dt_bias = dt0 + jnp.log(-jnp.expm1(-dt0))
    ssm_a_log = jnp.log(jax.random.uniform(next(ks), (L, 2, SSM_HEADS), f32, 1.0, 16.0))
    ssm_d = 1.0 + nrm((L, SSM_HEADS), 0.1)
    ssm_norm = 1.0 + nrm((L, SSM_INNER), 0.05)
    w_br_a = nrm((L, A_HEADS * HEAD_DIM, D), (A_HEADS * HEAD_DIM) ** -0.5)
    w_br_w = nrm((L, W_HEADS * HEAD_DIM, D), (W_HEADS * HEAD_DIM) ** -0.5)
    w_br_s = nrm((L, SSM_INNER, D), SSM_INNER ** -0.5)
    w_out = nrm((L, D, D), D ** -0.5)
    w_router = nrm((L, D, N_EXPERTS), D ** -0.5)
    b_router = nrm((L, N_EXPERTS), 0.01)
    we_gate = nrm((L, N_EXPERTS, D, D_EXPERT), D ** -0.5)
    we_up = nrm((L, N_EXPERTS, D, D_EXPERT), D ** -0.5)
    we_down = nrm((L, N_EXPERTS, D_EXPERT, D), D_EXPERT ** -0.5)
    ws_gate = nrm((L, D, D_SHARED), D ** -0.5)
    ws_up = nrm((L, D, D_SHARED), D ** -0.5)
    ws_down = nrm((L, D_SHARED, D), D_SHARED ** -0.5)
    return {'x': x, 'c': c, 'ctx': ctx, 'c_ctx': c_ctx, 'w_ada': w_ada, 'b_ada': b_ada,
            'g_mix_pre': g_mix_pre, 'g_mix_post': g_mix_post, 'g_ffn_pre': g_ffn_pre, 'g_ffn_post': g_ffn_post,
            'w_in': w_in, 'g_q_a': g_q_a, 'g_k_a': g_k_a, 'sink_w': sink_w,
            'ssm_conv_w': ssm_conv_w, 'ssm_conv_b': ssm_conv_b, 'ssm_dt_bias': ssm_dt_bias,
            'ssm_a_log': ssm_a_log, 'ssm_d': ssm_d, 'ssm_norm': ssm_norm,
            'w_br_a': w_br_a, 'w_br_w': w_br_w, 'w_br_s': w_br_s, 'w_out': w_out,
            'w_router': w_router, 'b_router': b_router, 'we_gate': we_gate, 'we_up': we_up, 'we_down': we_down,
            'ws_gate': ws_gate, 'ws_up': ws_up, 'ws_down': ws_down}


def reference(x, c, ctx, c_ctx, w_ada, b_ada, g_mix_pre, g_mix_post, g_ffn_pre, g_ffn_post,
              w_in, g_q_a, g_k_a, sink_w, ssm_conv_w, ssm_conv_b, ssm_dt_bias, ssm_a_log, ssm_d, ssm_norm,
              w_br_a, w_br_w, w_br_s, w_out, w_router, b_router, we_gate, we_up, we_down,
              ws_gate, ws_up, ws_down):
    d = x.shape[-1]
    rope = axial_rope_tables(x.shape[1])
    xl, xc = x, ctx
    for i in range(DEPTH):
        last = i == DEPTH - 1
        mod_l = jnp.split((jax.nn.silu(c) @ w_ada[i] + b_ada[i])[:, None, :], 6, axis=-1)
        mod_c = jnp.split(jax.nn.silu(c_ctx) @ w_ada[i] + b_ada[i], 6, axis=-1)
        hl = modulate(rmsnorm(xl, g_mix_pre[i]), mod_l[0], mod_l[1])
        hc = modulate(rmsnorm(xc, g_mix_pre[i]), mod_c[0], mod_c[1])
        ml, mc = hybrid_mixer(hl, hc, rope, w_in[i], g_q_a[i], g_k_a[i], sink_w[i],
                              ssm_conv_w[i], ssm_conv_b[i], ssm_dt_bias[i], ssm_a_log[i], ssm_d[i], ssm_norm[i],
                              w_br_a[i], w_br_w[i], w_br_s[i], w_out[i], not last)
        xl = xl + mod_l[2] * rmsnorm(ml, g_mix_post[i])
        hl = modulate(rmsnorm(xl, g_ffn_pre[i]), mod_l[3], mod_l[4])
        moe_w = (w_router[i], b_router[i], we_gate[i], we_up[i], we_down[i], ws_gate[i], ws_up[i], ws_down[i])
        if last:
            fl = moe_ffn(hl.reshape(-1, d), *moe_w).reshape(hl.shape)
        else:
            xc = xc + mod_c[2] * rmsnorm(mc, g_mix_post[i])
            hc = modulate(rmsnorm(xc, g_ffn_pre[i]), mod_c[3], mod_c[4])
            n_lat = hl.shape[0] * hl.shape[1]
            f = moe_ffn(jnp.concatenate([hl.reshape(-1, d), hc.reshape(-1, d)], axis=0), *moe_w)
            fl = f[:n_lat].reshape(hl.shape)
            xc = xc + mod_c[5] * rmsnorm(f[n_lat:].reshape(hc.shape), g_ffn_post[i])
        xl = xl + mod_l[5] * rmsnorm(fl, g_ffn_post[i])
    return xl
```

```python
import functools
import math

import jax
import jax.numpy as jnp
import numpy as np
from jax import lax
from jax.experimental import pallas as pl
from jax.experimental.pallas import tpu as pltpu

F32 = jnp.float32
BF16 = jnp.bfloat16

HEAD_DIM = 64
N_HEADS = 8
N_KV = 2
GRID_W = 64
ROPE_THETA = 10000.0
WINDOW = 128
SSM_HEADS = 16
SSM_P = 64
SSM_INNER = SSM_HEADS * SSM_P
SSM_GROUPS = 2
SSM_STATE = 128
SSM_CHUNK = 128
N_EXPERTS = 64
TOP_K = 8
N_EXPERT_GROUPS = 8
TOPK_GROUPS = 4
ROUTED_SCALE = 2.5
EPS = 1e-6

LANES = 128
HALF = LANES // 2
VMEM_LIMIT = 56 * 1024 * 1024
NEG_BIG = -1e30

C_QA, C_QW, C_Z, C_XS, C_GATES = 0, 512, 1024, 2048, 3072
C_KA, C_VA, C_KW, C_VW, C_BC = 6144, 6272, 6400, 6528, 6656
P_WIDTH = 7168


def _params(sem, vmem=VMEM_LIMIT):
    return pltpu.CompilerParams(dimension_semantics=sem, vmem_limit_bytes=vmem)


def _silu(x):
    return x * jax.nn.sigmoid(x)


def _softplus(x):
    return jnp.maximum(x, 0.0) + jnp.log(1.0 + jnp.exp(-jnp.abs(x)))


def _rms(x, eps=EPS):
    return x * lax.rsqrt(jnp.mean(x * x, axis=-1, keepdims=True) + eps)


def _split3(a):
    a1 = a.astype(BF16)
    r1 = a - a1.astype(F32)
    a2 = r1.astype(BF16)
    a3 = (r1 - a2.astype(F32)).astype(BF16)
    return a1, a2, a3


def _dot(a, b):
    return jnp.dot(a, b, preferred_element_type=F32)


def _dot_nt(a, b):
    return lax.dot_general(a, b, (((1,), (1,)), ((), ())), preferred_element_type=F32)


def _exact_right(a, r01):
    a1, a2, a3 = _split3(a)
    return _dot(a1, r01) + _dot(a2, r01) + _dot(a3, r01)


def _exact_left(m01, a):
    a1, a2, a3 = _split3(a)
    return _dot(m01, a1) + _dot(m01, a2) + _dot(m01, a3)


def _ada_kernel(c_ref, w_ref, b_ref, o_ref):
    h = _silu(c_ref[...])
    o_ref[...] = jnp.dot(h, w_ref[...], preferred_element_type=F32,
                         precision=lax.Precision.HIGHEST) + b_ref[...]


def ada_mod(c8, w, b):
    d, n = w.shape
    tn = 1536
    return pl.pallas_call(
        _ada_kernel,
        grid=(n // tn,),
        in_specs=[pl.BlockSpec((8, d), lambda j: (0, 0)),
                  pl.BlockSpec((d, tn), lambda j: (0, j)),
                  pl.BlockSpec((1, tn), lambda j: (0, j))],
        out_specs=pl.BlockSpec((8, tn), lambda j: (0, j)),
        out_shape=jax.ShapeDtypeStruct((8, n), F32),
        compiler_params=_params(("parallel",)),
        name="ada_mod",
    )(c8, w, b.reshape(1, n))


def _inproj_kernel(x_ref, g_ref, sh_ref, sc_ref, w_ref, wdt_ref, o_ref, odt_ref, h_scr):
    @pl.when(pl.program_id(1) == 0)
    def _():
        h = _rms(x_ref[...]) * g_ref[...]
        h = h * (1.0 + sc_ref[...]) + sh_ref[...]
        hb = h.astype(BF16)
        h_scr[...] = hb
        odt_ref[...] = _dot(hb, wdt_ref[...])

    o_ref[...] = _dot(h_scr[...], w_ref[...]).astype(BF16)


def in_proj(x, g, shift, scale, w_main, w_dt, rows_per_batch):
    t, d = x.shape
    n = w_main.shape[1]
    tm = min(512, rows_per_batch)
    tn = 1024
    tpb = rows_per_batch // tm
    mod_spec = pl.BlockSpec((None, 1, d), lambda i, j: (i // tpb, 0, 0))
    return pl.pallas_call(
        _inproj_kernel,
        grid=(t // tm, n // tn),
        in_specs=[pl.BlockSpec((tm, d), lambda i, j: (i, 0)),
                  pl.BlockSpec((1, d), lambda i, j: (0, 0)),
                  mod_spec, mod_spec,
                  pl.BlockSpec((d, tn), lambda i, j: (0, j)),
                  pl.BlockSpec((d, 2 * LANES), lambda i, j: (0, 0))],
        out_specs=[pl.BlockSpec((tm, tn), lambda i, j: (i, j)),
                   pl.BlockSpec((tm, 2 * LANES), lambda i, j: (i, 0))],
        out_shape=[jax.ShapeDtypeStruct((t, n), BF16),
                   jax.ShapeDtypeStruct((t, 2 * LANES), F32)],
        scratch_shapes=[pltpu.VMEM((tm, d), BF16)],
        compiler_params=_params(("parallel", "arbitrary")),
        name="in_proj",
    )(x, g.reshape(1, d), shift, scale, w_main, w_dt)


def _rope(x, cos, sin):
    w = x.shape[-1]
    lane = lax.broadcasted_iota(jnp.int32, x.shape, 1)
    first = (lane % 32) < 16
    swapped = jnp.where(first, pltpu.roll(x, w - 16, 1), pltpu.roll(x, 16, 1))
    return x * cos + swapped * sin


def _dup_halves(x):
    lane = lax.broadcasted_iota(jnp.int32, x.shape, 1)
    lo = lane < HALF
    r = pltpu.roll(x, HALF, 1)
    return jnp.concatenate([jnp.where(lo, x, r), jnp.where(lo, r, x)], axis=1)


def _prep_kernel(qa_ref, qw_ref, ka_ref, va_ref, kw_ref, vw_ref, cos_ref, sin_ref,
                 gq_ref, gk_ref, bdq_ref, bdk_ref,
                 qa_o, qw_o, kda_o, vda_o, kdw_o, vdw_o, *, rope):
    scale = HEAD_DIM ** -0.5
    inv_hd = 1.0 / HEAD_DIM

    def headnorm(x, g, bd):
        ss = _dot((x * x).astype(BF16), bd) * inv_hd
        return x * lax.rsqrt(ss + EPS) * g

    qa = headnorm(qa_ref[...].astype(F32), gq_ref[...], bdq_ref[...])
    ka = headnorm(ka_ref[...].astype(F32), gk_ref[...], bdk_ref[...])
    qw = qw_ref[...].astype(F32)
    kw = kw_ref[...].astype(F32)
    if rope:
        cos = cos_ref[...]
        sin = sin_ref[...]
        qa = _rope(qa, cos, sin)
        qw = _rope(qw, cos, sin)
        ka = _rope(ka, cos[:, :LANES], sin[:, :LANES])
        kw = _rope(kw, cos[:, :LANES], sin[:, :LANES])
    qa_o[...] = (qa * scale).astype(BF16)
    qw_o[...] = (qw * scale).astype(BF16)
    kda_o[...] = _dup_halves(ka).astype(BF16)
    kdw_o[...] = _dup_halves(kw).astype(BF16)
    vda_o[...] = _dup_halves(va_ref[...].astype(F32)).astype(BF16)
    vdw_o[...] = _dup_halves(vw_ref[...].astype(F32)).astype(BF16)


def attn_prep(p, cos, sin, gq, gk, nb, n, rope):
    t = nb * n
    tm = min(512, n)
    spb = n // tm
    hq = N_HEADS * HEAD_DIM
    hk = N_KV * HEAD_DIM
    bdq = (np.arange(hq)[:, None] // HEAD_DIM == np.arange(hq)[None, :] // HEAD_DIM)
    bdq = jnp.asarray(bdq, BF16)
    bdk = bdq[:hk, :hk]
    qspec = lambda c: pl.BlockSpec((tm, hq), lambda s, b: (b * spb + s, c // hq))
    kspec = lambda c: pl.BlockSpec((tm, hk), lambda s, b: (b * spb + s, c // hk))
    tab = pl.BlockSpec((tm, hq), lambda s, b: (s, 0))
    const = lambda shp: pl.BlockSpec(shp, lambda s, b: (0, 0))
    oq = pl.BlockSpec((tm, hq), lambda s, b: (b * spb + s, 0))
    ok = pl.BlockSpec((tm, 2 * hk), lambda s, b: (b * spb + s, 0))
    return pl.pallas_call(
        functools.partial(_prep_kernel, rope=rope),
        grid=(spb, nb),
        in_specs=[qspec(C_QA), qspec(C_QW), kspec(C_KA), kspec(C_VA), kspec(C_KW), kspec(C_VW),
                  tab, tab, const((1, hq)), const((1, hk)), const((hq, hq)), const((hk, hk))],
        out_specs=[oq, oq, ok, ok, ok, ok],
        out_shape=[jax.ShapeDtypeStruct((t, hq), BF16)] * 2 + [jax.ShapeDtypeStruct((t, 2 * hk), BF16)] * 4,
        compiler_params=_params(("parallel", "arbitrary")),
        name="attn_prep",
    )(p, p, p, p, p, p, cos, sin, gq, gk, bdq, bdk)


def rope_tables(n):
    rows = n // GRID_W
    row = jnp.repeat(jnp.arange(rows, dtype=F32), GRID_W)
    col = jnp.tile(jnp.arange(GRID_W, dtype=F32), rows)
    axis_dim = HEAD_DIM // 2
    inv_freq = ROPE_THETA ** (-jnp.arange(0, axis_dim, 2, dtype=F32) / axis_dim)
    ang_r = row[:, None] * inv_freq[None, :]
    ang_c = col[:, None] * inv_freq[None, :]
    cr, sr, cc, sc = jnp.cos(ang_r), jnp.sin(ang_r), jnp.cos(ang_c), jnp.sin(ang_c)
    cos = jnp.concatenate([cr, cr, cc, cc], axis=1)
    sin = jnp.concatenate([-sr, sr, -sc, sc], axis=1)
    return jnp.tile(cos, (1, N_HEADS)), jnp.tile(sin, (1, N_HEADS))


def _pair_operands(kd, vd):
    lane = lax.broadcasted_iota(jnp.int32, kd.shape, 1)
    lo = lane < HALF
    zero = jnp.zeros_like(kd)
    kmats = (jnp.where(lo, kd, zero), jnp.where(lo, zero, kd))
    vstack = jnp.concatenate([jnp.where(lo, vd, zero), jnp.where(lo, zero, vd)], axis=0)
    return kmats, vstack


def _flash_kernel(sink_ref, q_ref, k_ref, v_ref, o_ref, m_scr, l_scr, acc_scr, *, has_sink, nk):
    ki = pl.program_id(2)
    tq = q_ref.shape[0]

    @pl.when(ki == 0)
    def _():
        m_scr[...] = jnp.full(m_scr.shape, NEG_BIG, F32)
        l_scr[...] = jnp.zeros(l_scr.shape, F32)
        acc_scr[...] = jnp.zeros(acc_scr.shape, F32)

    lane_q = lax.broadcasted_iota(jnp.int32, (tq, LANES), 1)
    lo_q = lane_q < HALF
    pairs_per_kv = N_HEADS // N_KV // 2
    for j in range(N_KV):
        kmats, vstack = _pair_operands(k_ref[:, j * LANES:(j + 1) * LANES],
                                       v_ref[:, j * LANES:(j + 1) * LANES])
        for pp in range(pairs_per_kv):
            hp = j * pairs_per_kv + pp
            qp = q_ref[:, hp * LANES:(hp + 1) * LANES]
            ps, alphas = [], []
            for par in range(2):
                h = 2 * hp + par
                s = _dot_nt(qp, kmats[par])
                m_prev = m_scr[h]
                m_new = jnp.maximum(m_prev, jnp.max(s, axis=1, keepdims=True))
                alpha = jnp.exp(m_prev - m_new)
                p = jnp.exp(s - m_new[:, :1])
                l_scr[h] = alpha * l_scr[h] + jnp.sum(p, axis=1, keepdims=True)
                m_scr[h] = m_new
                ps.append(p.astype(BF16))
                alphas.append(alpha)
            pv = _dot(jnp.concatenate(ps, axis=1), vstack)
            sl = slice(hp * LANES, (hp + 1) * LANES)
            acc_scr[:, sl] = acc_scr[:, sl] * jnp.where(lo_q, alphas[0], alphas[1]) + pv

    @pl.when(ki == nk - 1)
    def _():
        for hp in range(N_HEADS // 2):
            ls = []
            for par in range(2):
                h = 2 * hp + par
                l = l_scr[h]
                if has_sink:
                    l = l + jnp.exp(sink_ref[h] - m_scr[h])
                ls.append(l)
            sl = slice(hp * LANES, (hp + 1) * LANES)
            o_ref[:, sl] = (acc_scr[:, sl] / jnp.where(lo_q, ls[0], ls[1])).astype(BF16)


def _pick_tile(m, cands):
    for c in cands:
        if m % c == 0:
            return c
    raise ValueError(f"no tile for {m}")


def flash_attn(q, kd, vd, sink, nb, n, m, has_sink):
    tq = min(512, n)
    tk = _pick_tile(m, (768, 512, 256))
    nq, nk = n // tq, m // tk
    hq = N_HEADS * HEAD_DIM
    return pl.pallas_call(
        functools.partial(_flash_kernel, has_sink=has_sink, nk=nk),
        grid=(nb, nq, nk),
        in_specs=[pl.BlockSpec(memory_space=pltpu.SMEM),
                  pl.BlockSpec((tq, hq), lambda b, i, k: (b * nq + i, 0)),
                  pl.BlockSpec((tk, 2 * LANES), lambda b, i, k: (b * nk + k, 0)),
                  pl.BlockSpec((tk, 2 * LANES), lambda b, i, k: (b * nk + k, 0))],
        out_specs=pl.BlockSpec((tq, hq), lambda b, i, k: (b * nq + i, 0)),
        out_shape=jax.ShapeDtypeStruct((nb * n, hq), BF16),
        scratch_shapes=[pltpu.VMEM((N_HEADS, tq, LANES), F32),
                        pltpu.VMEM((N_HEADS, tq, LANES), F32),
                        pltpu.VMEM((tq, hq), F32)],
        compiler_params=_params(("parallel", "parallel", "arbitrary")),
        name="flash_attn",
    )(sink, q, kd, vd)


def _window_kernel(sink_ref, q_ref, kp_ref, km_ref, kn_ref, vp_ref, vm_ref, vn_ref, kc_ref, vc_ref,
                   o_ref, *, n, tq):
    i = pl.program_id(1)
    span = tq + 2 * WINDOW
    q0 = i * tq
    r = lax.broadcasted_iota(jnp.int32, (tq, span), 0)
    c = lax.broadcasted_iota(jnp.int32, (tq, span), 1)
    kpos = c + (q0 - WINDOW)
    ok = (c >= r) & (c <= r + 2 * WINDOW) & (kpos >= 0) & (kpos < n)
    lane_q = lax.broadcasted_iota(jnp.int32, (tq, LANES), 1)
    lo_q = lane_q < HALF
    kloc = jnp.concatenate([kp_ref[...], km_ref[...], kn_ref[...]], axis=0)
    vloc = jnp.concatenate([vp_ref[...], vm_ref[...], vn_ref[...]], axis=0)
    pairs_per_kv = N_HEADS // N_KV // 2
    for j in range(N_KV):
        js = slice(j * LANES, (j + 1) * LANES)
        kl, vl_stack = _pair_operands(kloc[:, js], vloc[:, js])
        kc, vc_stack = _pair_operands(kc_ref[:, js], vc_ref[:, js])
        for pp in range(pairs_per_kv):
            hp = j * pairs_per_kv + pp
            qp = q_ref[:, hp * LANES:(hp + 1) * LANES]
            pl_, pc_, ls = [], [], []
            for par in range(2):
                h = 2 * hp + par
                s_loc = jnp.where(ok, _dot_nt(qp, kl[par]), NEG_BIG)
                s_ctx = _dot_nt(qp, kc[par])
                snk = sink_ref[h]
                m = jnp.maximum(jnp.maximum(jnp.max(s_loc, axis=1, keepdims=True),
                                            jnp.max(s_ctx, axis=1, keepdims=True)), snk)
                p_loc = jnp.exp(s_loc - m)
                p_ctx = jnp.exp(s_ctx - m)
                l = (jnp.sum(p_loc, axis=1, keepdims=True) + jnp.sum(p_ctx, axis=1, keepdims=True)
                     + jnp.exp(snk - m))
                inv = 1.0 / l
                pl_.append((p_loc * inv).astype(BF16))
                pc_.append((p_ctx * inv).astype(BF16))
            o = _dot(jnp.concatenate(pl_, axis=1), vl_stack) + _dot(jnp.concatenate(pc_, axis=1), vc_stack)
            o_ref[:, hp * LANES:(hp + 1) * LANES] = o.astype(BF16)


def window_attn(q, kd, vd, kdc, vdc, sink, nb, n, mc):
    tq = 2 * WINDOW
    nq = n // tq
    wb = n // WINDOW
    hq = N_HEADS * HEAD_DIM
    prev = pl.BlockSpec((WINDOW, 2 * LANES), lambda b, i: (b * wb + jnp.maximum(2 * i - 1, 0), 0))
    main = pl.BlockSpec((tq, 2 * LANES), lambda b, i: (b * nq + i, 0))
    nxt = pl.BlockSpec((WINDOW, 2 * LANES), lambda b, i: (b * wb + jnp.minimum(2 * i + 2, wb - 1), 0))
    ctx = pl.BlockSpec((mc, 2 * LANES), lambda b, i: (b, 0))
    return pl.pallas_call(
        functools.partial(_window_kernel, n=n, tq=tq),
        grid=(nb, nq),
        in_specs=[pl.BlockSpec(memory_space=pltpu.SMEM),
                  pl.BlockSpec((tq, hq), lambda b, i: (b * nq + i, 0)),
                  prev, main, nxt, prev, main, nxt, ctx, ctx],
        out_specs=pl.BlockSpec((tq, hq), lambda b, i: (b * nq + i, 0)),
        out_shape=jax.ShapeDtypeStruct((nb * n, hq), BF16),
        compiler_params=_params(("parallel", "parallel")),
        name="window_attn",
    )(sink, q, kd, kd, kd, vd, vd, vd, kdc, vdc)


HALO = 16


def _conv_kernel(xm_ref, xp_ref, xn_ref, bm_ref, bp_ref, bn_ref, wx_ref, bx_ref, wb_ref, bb_ref,
                 ox_ref, ob_ref, *, nt):
    i = pl.program_id(1)
    has_prev = jnp.where(i > 0, 1.0, 0.0)
    has_next = jnp.where(i < nt - 1, 1.0, 0.0)

    def conv(m_ref, p_ref, n_ref, w_ref, b_ref, o_ref):
        x = m_ref[...].astype(F32)
        tl = x.shape[0]
        row = lax.broadcasted_iota(jnp.int32, x.shape, 0)
        before = p_ref[...].astype(F32)[HALO - 1:HALO, :] * has_prev
        after = n_ref[...].astype(F32)[0:1, :] * has_next
        xm1 = jnp.where(row == 0, before, pltpu.roll(x, 1, 0))
        xp1 = jnp.where(row == tl - 1, after, pltpu.roll(x, tl - 1, 0))
        w = w_ref[...]
        y = xm1 * w[0:1, :] + x * w[1:2, :] + xp1 * w[2:3, :] + b_ref[...]
        o_ref[...] = _silu(y).astype(BF16)

    conv(xm_ref, xp_ref, xn_ref, wx_ref, bx_ref, ox_ref)
    conv(bm_ref, bp_ref, bn_ref, wb_ref, bb_ref, ob_ref)


def ssm_conv(p, conv_w, conv_b, nb, n):
    tl = min(512, n)
    nt = n // tl
    hb = n // HALO
    hpt = tl // HALO
    cx, cb = SSM_INNER, 2 * SSM_GROUPS * SSM_STATE

    def specs(width, col):
        cblk = col // width
        return (pl.BlockSpec((tl, width), lambda b, i: (b * nt + i, cblk)),
                pl.BlockSpec((HALO, width), lambda b, i: (b * hb + jnp.maximum(i * hpt - 1, 0), cblk)),
                pl.BlockSpec((HALO, width), lambda b, i: (b * hb + jnp.minimum((i + 1) * hpt, hb - 1), cblk)))

    const = lambda shp: pl.BlockSpec(shp, lambda b, i: (0, 0))
    xm, xp, xn = specs(cx, C_XS)
    bm, bp, bn = specs(cb, C_BC)
    return pl.pallas_call(
        functools.partial(_conv_kernel, nt=nt),
        grid=(nb, nt),
        in_specs=[xm, xp, xn, bm, bp, bn, const((3, cx)), const((1, cx)), const((3, cb)), const((1, cb))],
        out_specs=[pl.BlockSpec((tl, cx), lambda b, i: (b * nt + i, 0)),
                   pl.BlockSpec((tl, cb), lambda b, i: (b * nt + i, 0))],
        out_shape=[jax.ShapeDtypeStruct((nb * n, cx), BF16), jax.ShapeDtypeStruct((nb * n, cb), BF16)],
        compiler_params=_params(("parallel", "parallel")),
        name="ssm_conv",
    )(p, p, p, p, p, p, conv_w[:, :cx], conv_b[:cx].reshape(1, cx), conv_w[:, cx:], conv_b[cx:].reshape(1, cb))


def _ssd_kernel(xs_ref, bc_ref, dt_ref, bias_ref, aneg_ref, tri_ref, rep_ref, s0_ref,
                y_ref, sfin_ref, st_scr, *, nc):
    d = pl.program_id(1)
    k = pl.program_id(2)
    q = SSM_CHUNK
    gw = SSM_INNER // SSM_GROUPS

    @pl.when(k == 0)
    def _():
        st_scr[...] = s0_ref[...]

    tri = tri_ref[...]
    rep = rep_ref[...]
    dt = _softplus(dt_ref[...] + bias_ref[...])
    a = dt * aneg_ref[...]
    ac = _exact_left(tri, a)
    act = ac.T
    acx = _exact_right(ac, rep)
    dtx = _exact_right(dt, rep)
    totx = jnp.where(d == 0, acx[q - 1:q, :], acx[0:1, :])
    xd = xs_ref[...].astype(F32) * dtx
    xd_b = xd.astype(BF16)
    xe = (xd * jnp.exp(totx - acx)).astype(BF16)
    ein = jnp.exp(acx)
    keep = tri > 0
    lane = lax.broadcasted_iota(jnp.int32, (q, LANES), 1)
    lo = lane < HALF
    zero = jnp.zeros((q, LANES), BF16)
    hpg = SSM_HEADS // SSM_GROUPS
    for g in range(SSM_GROUPS):
        bg = bc_ref[:, g * SSM_STATE:(g + 1) * SSM_STATE]
        cg = bc_ref[:, (SSM_GROUPS + g) * SSM_STATE:(SSM_GROUPS + g + 1) * SSM_STATE]
        cb = _dot_nt(cg, bg)
        st = st_scr[g]
        yoff = _dot(cg, st.astype(BF16)) * ein[:, g * gw:(g + 1) * gw]
        for hp in range(hpg // 2):
            gs = []
            for par in range(2):
                h = g * hpg + 2 * hp + par
                seg = ac[:, h:h + 1] - act[h:h + 1, :]
                gs.append((cb * jnp.exp(jnp.where(keep, seg, NEG_BIG))).astype(BF16))
            c0 = g * gw + hp * LANES
            xp = xd_b[:, c0:c0 + LANES]
            xstack = jnp.concatenate([jnp.where(lo, xp, zero), jnp.where(lo, zero, xp)], axis=0)
            ydiag = _dot(jnp.concatenate(gs, axis=1), xstack)
            y_ref[:, c0:c0 + LANES] = (ydiag + yoff[:, hp * LANES:(hp + 1) * LANES]).astype(BF16)
        bgt = bg.astype(F32).T.astype(BF16)
        cs = _dot(bgt, xe[:, g * gw:(g + 1) * gw])
        st_scr[g] = st * jnp.exp(totx[:, g * gw:(g + 1) * gw]) + cs

    @pl.when(k == nc - 1)
    def _():
        sfin_ref[...] = st_scr[...]


def ssd_scan(u_xs, u_bc, dt_raw, bias2, aneg2, s0, nb, n):
    q = SSM_CHUNK
    nc = n // q
    gw = SSM_INNER // SSM_GROUPS
    idx = np.arange(q)
    tri = np.stack([idx[:, None] >= idx[None, :], idx[:, None] <= idx[None, :]]).astype(np.float32)
    rep = (np.arange(LANES)[:, None] == np.arange(SSM_INNER)[None, :] // SSM_P).astype(np.float32)

    def chunk(b, d, k):
        return b * nc + k + d * (nc - 1 - 2 * k)

    return pl.pallas_call(
        functools.partial(_ssd_kernel, nc=nc),
        grid=(nb, 2, nc),
        in_specs=[pl.BlockSpec((q, SSM_INNER), lambda b, d, k: (chunk(b, d, k), 0)),
                  pl.BlockSpec((q, 2 * SSM_GROUPS * SSM_STATE), lambda b, d, k: (chunk(b, d, k), 0)),
                  pl.BlockSpec((q, LANES), lambda b, d, k: (chunk(b, d, k), d)),
                  pl.BlockSpec((None, 1, LANES), lambda b, d, k: (d, 0, 0)),
                  pl.BlockSpec((None, 1, LANES), lambda b, d, k: (d, 0, 0)),
                  pl.BlockSpec((None, q, q), lambda b, d, k: (d, 0, 0)),
                  pl.BlockSpec((LANES, SSM_INNER), lambda b, d, k: (0, 0)),
                  pl.BlockSpec((None, None, SSM_GROUPS, SSM_STATE, gw), lambda b, d, k: (b, d, 0, 0, 0))],
        out_specs=[pl.BlockSpec((None, q, SSM_INNER), lambda b, d, k: (d, chunk(b, d, k), 0)),
                   pl.BlockSpec((None, None, SSM_GROUPS, SSM_STATE, gw), lambda b, d, k: (b, d, 0, 0, 0))],
        out_shape=[jax.ShapeDtypeStruct((2, nb * n, SSM_INNER), BF16),
                   jax.ShapeDtypeStruct((nb, 2, SSM_GROUPS, SSM_STATE, gw), F32)],
        scratch_shapes=[pltpu.VMEM((SSM_GROUPS, SSM_STATE, gw), F32)],
        compiler_params=_params(("parallel", "parallel", "arbitrary")),
        name="ssd_scan",
    )(u_xs, u_bc, dt_raw, bias2, aneg2, jnp.asarray(tri, BF16), jnp.asarray(rep, BF16), s0)


def _post_kernel(oa_ref, ow_ref, yf_ref, yb_ref, xs_ref, z_ref, gt_ref, x_ref,
                 dsk_ref, nw_ref, wa_ref, ww_ref, ws_ref, wo_ref, gpost_ref, gate_ref,
                 gpre_ref, sh_ref, sc_ref, wrh_ref, wrl_ref,
                 xo_ref, h_ref, lg_ref):
    y = yf_ref[...].astype(F32) + yb_ref[...].astype(F32) + dsk_ref[...] * xs_ref[...].astype(F32)
    u = y * _silu(z_ref[...].astype(F32))
    gw = SSM_INNER // SSM_GROUPS
    ys = jnp.concatenate([_rms(u[:, g * gw:(g + 1) * gw]) for g in range(SSM_GROUPS)], axis=1)
    ys = (ys * nw_ref[...]).astype(BF16)
    d = x_ref.shape[1]
    ga = jax.nn.sigmoid(gt_ref[:, 0:d].astype(F32))
    gw_ = jax.nn.sigmoid(gt_ref[:, d:2 * d].astype(F32))
    gs = jax.nn.sigmoid(gt_ref[:, 2 * d:3 * d].astype(F32))
    m = ga * _dot(oa_ref[...], wa_ref[...]) + gw_ * _dot(ow_ref[...], ww_ref[...]) + gs * _dot(ys, ws_ref[...])
    ml = _dot(m.astype(BF16), wo_ref[...])
    xn = x_ref[...] + gate_ref[...] * (_rms(ml) * gpost_ref[...])
    xo_ref[...] = xn
    h = (_rms(xn) * gpre_ref[...]) * (1.0 + sc_ref[...]) + sh_ref[...]
    hb = h.astype(BF16)
    h_ref[...] = hb
    hl = (h - hb.astype(F32)).astype(BF16)
    lg_ref[...] = _dot_nt(wrh_ref[...], hb) + _dot_nt(wrh_ref[...], hl) + _dot_nt(wrl_ref[...], hb)


def post_mixer(oa, ow, y2, u_xs, p, x, dskip, norm_w, wa, ww, ws, wo, gpost, gate, gpre, shift, scale,
               wr_hi, wr_lo, rows_per_batch):
    t, d = x.shape
    tm = min(256, rows_per_batch)
    tpb = rows_per_batch // tm
    nt = t // tm
    hq = N_HEADS * HEAD_DIM
    row = lambda w, c=0: pl.BlockSpec((tm, w), lambda i: (i, c // w))
    const = lambda shp: pl.BlockSpec(shp, lambda i: (0,) * len(shp))
    mod = pl.BlockSpec((None, 1, d), lambda i: (i // tpb, 0, 0))
    return pl.pallas_call(
        _post_kernel,
        grid=(nt,),
        in_specs=[row(hq), row(hq),
                  pl.BlockSpec((None, tm, SSM_INNER), lambda i: (0, i, 0)),
                  pl.BlockSpec((None, tm, SSM_INNER), lambda i: (1, i, 0)),
                  row(SSM_INNER), row(SSM_INNER, C_Z), row(3 * d, C_GATES), row(d),
                  const((1, SSM_INNER)), const((1, SSM_INNER)),
                  const((hq, d)), const((hq, d)), const((SSM_INNER, d)), const((d, d)),
                  const((1, d)), mod, const((1, d)), mod, mod,
                  const((N_EXPERTS, d)), const((N_EXPERTS, d))],
        out_specs=[row(d), row(d), pl.BlockSpec((N_EXPERTS, tm), lambda i: (0, i))],
        out_shape=[jax.ShapeDtypeStruct((t, d), F32), jax.ShapeDtypeStruct((t, d), BF16),
                   jax.ShapeDtypeStruct((N_EXPERTS, t), F32)],
        compiler_params=_params(("parallel",)),
        name="post_mixer",
    )(oa, ow, y2, y2, u_xs, p, p, x, dskip, norm_w, wa, ww, ws, wo, gpost, gate, gpre, shift, scale,
      wr_hi, wr_lo)


def _route_kernel(lg_ref, b_ref, ei_ref, w_ref):
    scores = jax.nn.sigmoid(lg_ref[...])
    sel = scores + b_ref[...]
    tt = sel.shape[1]
    per = N_EXPERTS // N_EXPERT_GROUPS
    r8 = lax.broadcasted_iota(jnp.int32, (per, tt), 0).astype(F32)
    ninf = -jnp.inf

    def argmax_rows(x, rows, nrows):
        m = jnp.max(x, axis=0, keepdims=True)
        idx = jnp.min(jnp.where(x == m, rows, float(nrows)), axis=0, keepdims=True)
        return m, idx

    gscores = []
    for g in range(N_EXPERT_GROUPS):
        blk = sel[g * per:(g + 1) * per, :]
        m1, i1 = argmax_rows(blk, r8, per)
        m2 = jnp.max(jnp.where(r8 == i1, ninf, blk), axis=0, keepdims=True)
        gscores.append(m1 + m2)
    cur = jnp.concatenate(gscores, axis=0)
    rg = lax.broadcasted_iota(jnp.int32, cur.shape, 0).astype(F32)
    chosen = jnp.zeros(cur.shape, F32)
    for _ in range(TOPK_GROUPS):
        _, gi = argmax_rows(cur, rg, N_EXPERT_GROUPS)
        hit = rg == gi
        chosen = jnp.where(hit, 1.0, chosen)
        cur = jnp.where(hit, ninf, cur)
    gmask = jnp.concatenate([jnp.broadcast_to(chosen[g:g + 1, :], (per, tt)) for g in range(N_EXPERT_GROUPS)],
                            axis=0)
    cur = jnp.where(gmask > 0, sel, ninf)
    re = lax.broadcasted_iota(jnp.int32, cur.shape, 0).astype(F32)
    idxs, ws = [], []
    for _ in range(TOP_K):
        _, ei = argmax_rows(cur, re, N_EXPERTS)
        hit = re == ei
        ws.append(jnp.sum(jnp.where(hit, scores, 0.0), axis=0, keepdims=True))
        idxs.append(ei)
        cur = jnp.where(hit, ninf, cur)
    w = jnp.concatenate(ws, axis=0)
    w_ref[...] = w / jnp.sum(w, axis=0, keepdims=True) * ROUTED_SCALE
    ei_ref[...] = jnp.concatenate(idxs, axis=0).astype(jnp.int32)


def route(logits_t, b_router):
    e, t = logits_t.shape
    tt = 512
    return pl.pallas_call(
        _route_kernel,
        grid=(t // tt,),
        in_specs=[pl.BlockSpec((e, tt), lambda i: (0, i)), pl.BlockSpec((e, 1), lambda i: (0, 0))],
        out_specs=[pl.BlockSpec((TOP_K, tt), lambda i: (0, i)), pl.BlockSpec((TOP_K, tt), lambda i: (0, i))],
        out_shape=[jax.ShapeDtypeStruct((TOP_K, t), jnp.int32), jax.ShapeDtypeStruct((TOP_K, t), F32)],
        compiler_params=_params(("parallel",)),
        name="route",
    )(logits_t, b_router.reshape(e, 1))


MOE_ROWS = 256


def _expert_kernel(be_ref, x_ref, w_ref, wg_ref, wu_ref, wd_ref, o_ref):
    x = x_ref[...]
    hg = _dot(x, wg_ref[...].astype(BF16))
    hu = _dot(x, wu_ref[...].astype(BF16))
    hb = (_silu(hg) * hu).astype(BF16)
    o_ref[...] = (_dot(hb, wd_ref[...].astype(BF16)) * w_ref[...]).astype(BF16)


def expert_ffn(blk_e, xs, ws, we_gate, we_up, we_down):
    cap, d = xs.shape
    de = we_gate.shape[2]
    nblk = cap // MOE_ROWS
    grid_spec = pltpu.PrefetchScalarGridSpec(
        num_scalar_prefetch=1,
        grid=(nblk,),
        in_specs=[pl.BlockSpec((MOE_ROWS, d), lambda i, be: (i, 0)),
                  pl.BlockSpec((MOE_ROWS, 1), lambda i, be: (i, 0)),
                  pl.BlockSpec((None, d, de), lambda i, be: (be[i], 0, 0)),
                  pl.BlockSpec((None, d, de), lambda i, be: (be[i], 0, 0)),
                  pl.BlockSpec((None, de, d), lambda i, be: (be[i], 0, 0))],
        out_specs=pl.BlockSpec((MOE_ROWS, d), lambda i, be: (i, 0)),
    )
    return pl.pallas_call(
        _expert_kernel,
        grid_spec=grid_spec,
        out_shape=jax.ShapeDtypeStruct((cap, d), BF16),
        compiler_params=_params(("arbitrary",)),
        name="expert_ffn",
    )(blk_e, xs, ws, we_gate, we_up, we_down)


def _ffn_tail_kernel(h_ref, r_ref, x_ref, wg_ref, wu_ref, wd_ref, gpost_ref, gate_ref, xo_ref):
    h = h_ref[...]
    hs = (_silu(_dot(h, wg_ref[...])) * _dot(h, wu_ref[...])).astype(BF16)
    f = r_ref[...] + _dot(hs, wd_ref[...])
    xo_ref[...] = x_ref[...] + gate_ref[...] * (_rms(f) * gpost_ref[...])


def ffn_tail(h, routed, x, wsg, wsu, wsd, gpost, gate, rows_per_batch):
    t, d = x.shape
    ds = wsg.shape[1]
    tm = min(512, rows_per_batch)
    tpb = rows_per_batch // tm
    row = pl.BlockSpec((tm, d), lambda i: (i, 0))
    const = lambda shp: pl.BlockSpec(shp, lambda i: (0,) * len(shp))
    return pl.pallas_call(
        _ffn_tail_kernel,
        grid=(t // tm,),
        in_specs=[row, row, row, const((d, ds)), const((d, ds)), const((ds, d)), const((1, d)),
                  pl.BlockSpec((None, 1, d), lambda i: (i // tpb, 0, 0))],
        out_specs=row,
        out_shape=jax.ShapeDtypeStruct((t, d), F32),
        compiler_params=_params(("parallel",)),
        name="ffn_tail",
    )(h, routed, x, wsg, wsu, wsd, gpost, gate)


def _dispatch_plan(eidx, wts):
    t = eidx.shape[0]
    tk = t * TOP_K
    flat_e = eidx.reshape(-1)
    onehot = (flat_e[:, None] == jnp.arange(N_EXPERTS, dtype=jnp.int32)[None, :]).astype(jnp.int32)
    rank = jnp.take_along_axis(jnp.cumsum(onehot, axis=0) - onehot, flat_e[:, None], axis=1)[:, 0]
    counts = jnp.sum(onehot, axis=0)
    padded = (counts + MOE_ROWS - 1) // MOE_ROWS * MOE_ROWS
    pad_end = jnp.cumsum(padded)
    pad_start = pad_end - padded
    dest = pad_start[flat_e] + rank
    nblk = -(-(tk + N_EXPERTS * (MOE_ROWS - 1)) // MOE_ROWS)
    cap = nblk * MOE_ROWS
    tok = (jnp.arange(tk, dtype=jnp.int32) // TOP_K)
    buf_t = jnp.zeros((cap,), jnp.int32).at[dest].set(tok)
    buf_w = jnp.zeros((cap,), F32).at[dest].set(wts.reshape(-1))
    blk_e = jnp.minimum(jnp.searchsorted(pad_end, jnp.arange(nblk) * MOE_ROWS, side='right'),
                        N_EXPERTS - 1).astype(jnp.int32)
    return dest.reshape(t, TOP_K), buf_t, buf_w, blk_e


def _moe_routed(h, logits_t, b_router, we_gate, we_up, we_down):
    eidx_t, wts_t = route(logits_t, b_router)
    dest, buf_t, buf_w, blk_e = _dispatch_plan(eidx_t.T, wts_t.T)
    xs = jnp.take(h, buf_t, axis=0)
    ys = expert_ffn(blk_e, xs, buf_w[:, None], we_gate, we_up, we_down)
    return jnp.sum(jnp.take(ys, dest, axis=0).astype(F32), axis=1)


def _reorder_w_in(w):
    qa, ka, va, qw, kw, vw, z, xs, bm, cm, dt, gates = jnp.split(
        w, [512, 640, 768, 1280, 1408, 1536, 2560, 3584, 3840, 4096, 4128], axis=1)
    w_main = jnp.concatenate([qa, qw, z, xs, gates, ka, va, kw, vw, bm, cm], axis=1).astype(BF16)
    pad = jnp.zeros((w.shape[0], LANES - SSM_HEADS), w.dtype)
    w_dt = jnp.concatenate([dt[:, :SSM_HEADS], pad, dt[:, SSM_HEADS:], pad], axis=1).astype(BF16)
    return w_main, w_dt


def _pad_lanes(v):
    return jnp.pad(v, ((0, 0), (0, LANES - v.shape[1])))


def kernel(x, c, ctx, c_ctx, w_ada, b_ada, g_mix_pre, g_mix_post, g_ffn_pre, g_ffn_post, w_in, g_q_a, g_k_a, sink_w, ssm_conv_w, ssm_conv_b, ssm_dt_bias, ssm_a_log, ssm_d, ssm_norm, w_br_a, w_br_w, w_br_s, w_out, w_router, b_router, we_gate, we_up, we_down, ws_gate, ws_up, ws_down):
    nb, n, d = x.shape
    mc = ctx.shape[1]
    depth = w_in.shape[0]
    t_lat, t_ctx = nb * n, nb * mc
    cos, sin = rope_tables(n)
    xl = x.reshape(t_lat, d)
    xc = ctx.reshape(t_ctx, d)
    c8 = jnp.concatenate([c, c_ctx[None, :], jnp.zeros((8 - nb - 1, d), F32)], axis=0)
    zeros_sink = jnp.zeros((N_HEADS,), F32)
    s_zero = jnp.zeros((nb, 2, SSM_GROUPS, SSM_STATE, SSM_INNER // SSM_GROUPS), F32)
    dummy_tab = jnp.zeros((mc, N_HEADS * HEAD_DIM), F32)

    for i in range(depth):
        last = i == depth - 1
        mod = ada_mod(c8, w_ada[i], b_ada[i])
        mod_l = [mod[:nb, k * d:(k + 1) * d].reshape(nb, 1, d) for k in range(6)]
        mod_c = [mod[nb:nb + 1, k * d:(k + 1) * d].reshape(1, 1, d) for k in range(6)]
        w_main, w_dt = _reorder_w_in(w_in[i])
        gq = jnp.tile(g_q_a[i], N_HEADS)[None, :]
        gk = jnp.tile(g_k_a[i], N_KV)[None, :]
        bias2 = _pad_lanes(ssm_dt_bias[i].reshape(2, SSM_HEADS)).reshape(2, 1, LANES)
        aneg2 = _pad_lanes(-jnp.exp(ssm_a_log[i].astype(F32))).reshape(2, 1, LANES)
        dskip = jnp.repeat(ssm_d[i], SSM_P)[None, :]
        norm_w = ssm_norm[i][None, :]
        wa, ww, ws, wo = (w_br_a[i].astype(BF16), w_br_w[i].astype(BF16), w_br_s[i].astype(BF16),
                          w_out[i].astype(BF16))
        wr_t = w_router[i].T
        wr_hi = wr_t.astype(BF16)
        wr_lo = (wr_t - wr_hi.astype(F32)).astype(BF16)
        sink = sink_w[i].astype(F32)

        p_c, dt_c = in_proj(xc, g_mix_pre[i], mod_c[0], mod_c[1], w_main, w_dt, t_ctx)
        qa_c, qw_c, kda_c, vda_c, kdw_c, vdw_c = attn_prep(p_c, dummy_tab, dummy_tab, gq, gk, nb, mc, rope=False)
        uxs_c, ubc_c = ssm_conv(p_c, ssm_conv_w[i], ssm_conv_b[i], nb, mc)
        y_c, s_fin = ssd_scan(uxs_c, ubc_c, dt_c, bias2, aneg2, s_zero, nb, mc)

        p_l, dt_l = in_proj(xl, g_mix_pre[i], mod_l[0], mod_l[1], w_main, w_dt, n)
        qa, qw, kda, vda, kdw, vdw = attn_prep(p_l, cos, sin, gq, gk, nb, n, rope=True)
        m_all = n + mc
        kd_all = jnp.concatenate([kda.reshape(nb, n, -1), kda_c.reshape(nb, mc, -1)], axis=1).reshape(nb * m_all, -1)
        vd_all = jnp.concatenate([vda.reshape(nb, n, -1), vda_c.reshape(nb, mc, -1)], axis=1).reshape(nb * m_all, -1)
        oa = flash_attn(qa, kd_all, vd_all, zeros_sink, nb, n, m_all, has_sink=False)
        ow = window_attn(qw, kdw, vdw, kdw_c, vdw_c, sink, nb, n, mc)
        uxs, ubc = ssm_conv(p_l, ssm_conv_w[i], ssm_conv_b[i], nb, n)
        y_l, _ = ssd_scan(uxs, ubc, dt_l, bias2, aneg2, s_fin, nb, n)
        xl, h_l, lg_l = post_mixer(oa, ow, y_l, uxs, p_l, xl, dskip, norm_w, wa, ww, ws, wo,
                                   g_mix_post[i][None, :], mod_l[2], g_ffn_pre[i][None, :], mod_l[3], mod_l[4],
                                   wr_hi, wr_lo, n)
        wsg, wsu, wsd = ws_gate[i].astype(BF16), ws_up[i].astype(BF16), ws_down[i].astype(BF16)
        if last:
            routed = _moe_routed(h_l, lg_l, b_router[i], we_gate[i], we_up[i], we_down[i])
            xl = ffn_tail(h_l, routed, xl, wsg, wsu, wsd, g_ffn_post[i][None, :], mod_l[5], n)
        else:
            oa_c = flash_attn(qa_c, kda_c, vda_c, zeros_sink, nb, mc, mc, has_sink=False)
            ow_c = flash_attn(qw_c, kdw_c, vdw_c, sink, nb, mc, mc, has_sink=True)
            xc, h_c, lg_c = post_mixer(oa_c, ow_c, y_c, uxs_c, p_c, xc, dskip, norm_w, wa, ww, ws, wo,
                                       g_mix_post[i][None, :], mod_c[2], g_ffn_pre[i][None, :], mod_c[3], mod_c[4],
                                       wr_hi, wr_lo, t_ctx)
            h_all = jnp.concatenate([h_l, h_c], axis=0)
            lg_all = jnp.concatenate([lg_l, lg_c], axis=1)
            routed = _moe_routed(h_all, lg_all, b_router[i], we_gate[i], we_up[i], we_down[i])
            xl = ffn_tail(h_l, routed[:t_lat], xl, wsg, wsu, wsd, g_ffn_post[i][None, :], mod_l[5], n)
            xc = ffn_tail(h_c, routed[t_lat:], xc, wsg, wsu, wsd, g_ffn_post[i][None, :], mod_c[5], t_ctx)
    return xl.reshape(nb, n, d)
```

```python
import functools
import math

import jax
import jax.numpy as jnp
import numpy as np
from jax import lax
from jax.experimental import pallas as pl
from jax.experimental.pallas import tpu as pltpu

F32 = jnp.float32
BF16 = jnp.bfloat16

HEAD_DIM = 64
N_HEADS = 8
N_KV = 2
GRID_W = 64
ROPE_THETA = 10000.0
WINDOW = 128
SSM_HEADS = 16
SSM_P = 64
SSM_INNER = SSM_HEADS * SSM_P
SSM_GROUPS = 2
SSM_STATE = 128
SSM_CHUNK = 128
N_EXPERTS = 64
TOP_K = 8
N_EXPERT_GROUPS = 8
TOPK_GROUPS = 4
ROUTED_SCALE = 2.5
EPS = 1e-6

LANES = 128
HALF = LANES // 2
VMEM_LIMIT = 56 * 1024 * 1024
NEG_BIG = -1e30

C_QA, C_QW, C_Z, C_XS, C_GATES = 0, 512, 1024, 2048, 3072
C_KA, C_VA, C_KW, C_VW, C_BC = 6144, 6272, 6400, 6528, 6656
P_WIDTH = 7168


def _params(sem, vmem=VMEM_LIMIT):
    return pltpu.CompilerParams(dimension_semantics=sem, vmem_limit_bytes=vmem)


def _silu(x):
    return x * jax.nn.sigmoid(x)


def _softplus(x):
    return jnp.maximum(x, 0.0) + jnp.log(1.0 + jnp.exp(-jnp.abs(x)))


def _rms(x, eps=EPS):
    return x * lax.rsqrt(jnp.mean(x * x, axis=-1, keepdims=True) + eps)


def _split3(a):
    a1 = a.astype(BF16)
    r1 = a - a1.astype(F32)
    a2 = r1.astype(BF16)
    a3 = (r1 - a2.astype(F32)).astype(BF16)
    return a1, a2, a3


def _dot(a, b):
    return jnp.dot(a, b, preferred_element_type=F32)


def _dot_nt(a, b):
    return lax.dot_general(a, b, (((1,), (1,)), ((), ())), preferred_element_type=F32)


def _exact_right(a, r01):
    a1, a2, a3 = _split3(a)
    return _dot(a1, r01) + _dot(a2, r01) + _dot(a3, r01)


def _exact_left(m01, a):
    a1, a2, a3 = _split3(a)
    return _dot(m01, a1) + _dot(m01, a2) + _dot(m01, a3)


def _ada_kernel(c_ref, w_ref, b_ref, o_ref):
    h = _silu(c_ref[...])
    o_ref[...] = jnp.dot(h, w_ref[...], preferred_element_type=F32,
                         precision=lax.Precision.HIGHEST) + b_ref[...]


def ada_mod(c8, w, b):
    d, n = w.shape
    tn = 1536
    return pl.pallas_call(
        _ada_kernel,
        grid=(n // tn,),
        in_specs=[pl.BlockSpec((8, d), lambda j: (0, 0)),
                  pl.BlockSpec((d, tn), lambda j: (0, j)),
                  pl.BlockSpec((1, tn), lambda j: (0, j))],
        out_specs=pl.BlockSpec((8, tn), lambda j: (0, j)),
        out_shape=jax.ShapeDtypeStruct((8, n), F32),
        compiler_params=_params(("parallel",)),
        name="ada_mod",
    )(c8, w, b.reshape(1, n))


def _inproj_kernel(x_ref, g_ref, sh_ref, sc_ref, w_ref, wdt_ref, o_ref, odt_ref, h_scr):
    @pl.when(pl.program_id(1) == 0)
    def _():
        h = _rms(x_ref[...]) * g_ref[...]
        h = h * (1.0 + sc_ref[...]) + sh_ref[...]
        hb = h.astype(BF16)
        h_scr[...] = hb
        odt_ref[...] = _dot(hb, wdt_ref[...])

    o_ref[...] = _dot(h_scr[...], w_ref[...]).astype(BF16)


def in_proj(x, g, shift, scale, w_main, w_dt, rows_per_batch):
    t, d = x.shape
    n = w_main.shape[1]
    tm = min(512, rows_per_batch)
    tn = 1024
    tpb = rows_per_batch // tm
    mod_spec = pl.BlockSpec((None, 1, d), lambda i, j: (i // tpb, 0, 0))
    return pl.pallas_call(
        _inproj_kernel,
        grid=(t // tm, n // tn),
        in_specs=[pl.BlockSpec((tm, d), lambda i, j: (i, 0)),
                  pl.BlockSpec((1, d), lambda i, j: (0, 0)),
                  mod_spec, mod_spec,
                  pl.BlockSpec((d, tn), lambda i, j: (0, j)),
                  pl.BlockSpec((d, 2 * LANES), lambda i, j: (0, 0))],
        out_specs=[pl.BlockSpec((tm, tn), lambda i, j: (i, j)),
                   pl.BlockSpec((tm, 2 * LANES), lambda i, j: (i, 0))],
        out_shape=[jax.ShapeDtypeStruct((t, n), BF16),
                   jax.ShapeDtypeStruct((t, 2 * LANES), F32)],
        scratch_shapes=[pltpu.VMEM((tm, d), BF16)],
        compiler_params=_params(("parallel", "arbitrary")),
        name="in_proj",
    )(x, g.reshape(1, d), shift, scale, w_main, w_dt)


def _rope(x, cos, sin):
    w = x.shape[-1]
    lane = lax.broadcasted_iota(jnp.int32, x.shape, 1)
    first = (lane % 32) < 16
    swapped = jnp.where(first, pltpu.roll(x, w - 16, 1), pltpu.roll(x, 16, 1))
    return x * cos + swapped * sin


def _dup_halves(x):
    lane = lax.broadcasted_iota(jnp.int32, x.shape, 1)
    lo = lane < HALF
    r = pltpu.roll(x, HALF, 1)
    return jnp.concatenate([jnp.where(lo, x, r), jnp.where(lo, r, x)], axis=1)


def _prep_kernel(qa_ref, qw_ref, ka_ref, va_ref, kw_ref, vw_ref, cos_ref, sin_ref,
                 gq_ref, gk_ref, bdq_ref, bdk_ref,
                 qa_o, qw_o, kda_o, vda_o, kdw_o, vdw_o, *, rope):
    scale = HEAD_DIM ** -0.5
    inv_hd = 1.0 / HEAD_DIM

    def headnorm(x, g, bd):
        ss = _dot((x * x).astype(BF16), bd) * inv_hd
        return x * lax.rsqrt(ss + EPS) * g

    qa = headnorm(qa_ref[...].astype(F32), gq_ref[...], bdq_ref[...])
    ka = headnorm(ka_ref[...].astype(F32), gk_ref[...], bdk_ref[...])
    qw = qw_ref[...].astype(F32)
    kw = kw_ref[...].astype(F32)
    if rope:
        cos = cos_ref[...]
        sin = sin_ref[...]
        qa = _rope(qa, cos, sin)
        qw = _rope(qw, cos, sin)
        ka = _rope(ka, cos[:, :LANES], sin[:, :LANES])
        kw = _rope(kw, cos[:, :LANES], sin[:, :LANES])
    qa_o[...] = (qa * scale).astype(BF16)
    qw_o[...] = (qw * scale).astype(BF16)
    kda_o[...] = _dup_halves(ka).astype(BF16)
    kdw_o[...] = _dup_halves(kw).astype(BF16)
    vda_o[...] = _dup_halves(va_ref[...].astype(F32)).astype(BF16)
    vdw_o[...] = _dup_halves(vw_ref[...].astype(F32)).astype(BF16)


def attn_prep(p, cos, sin, gq, gk, nb, n, rope):
    t = nb * n
    tm = min(512, n)
    spb = n // tm
    hq = N_HEADS * HEAD_DIM
    hk = N_KV * HEAD_DIM
    bdq = (np.arange(hq)[:, None] // HEAD_DIM == np.arange(hq)[None, :] // HEAD_DIM)
    bdq = jnp.asarray(bdq, BF16)
    bdk = bdq[:hk, :hk]
    qspec = lambda c: pl.BlockSpec((tm, hq), lambda s, b: (b * spb + s, c // hq))
    kspec = lambda c: pl.BlockSpec((tm, hk), lambda s, b: (b * spb + s, c // hk))
    tab = pl.BlockSpec((tm, hq), lambda s, b: (s, 0))
    const = lambda shp: pl.BlockSpec(shp, lambda s, b: (0, 0))
    oq = pl.BlockSpec((tm, hq), lambda s, b: (b * spb + s, 0))
    ok = pl.BlockSpec((tm, 2 * hk), lambda s, b: (b * spb + s, 0))
    return pl.pallas_call(
        functools.partial(_prep_kernel, rope=rope),
        grid=(spb, nb),
        in_specs=[qspec(C_QA), qspec(C_QW), kspec(C_KA), kspec(C_VA), kspec(C_KW), kspec(C_VW),
                  tab, tab, const((1, hq)), const((1, hk)), const((hq, hq)), const((hk, hk))],
        out_specs=[oq, oq, ok, ok, ok, ok],
        out_shape=[jax.ShapeDtypeStruct((t, hq), BF16)] * 2 + [jax.ShapeDtypeStruct((t, 2 * hk), BF16)] * 4,
        compiler_params=_params(("parallel", "arbitrary")),
        name="attn_prep",
    )(p, p, p, p, p, p, cos, sin, gq, gk, bdq, bdk)


def rope_tables(n):
    rows = n // GRID_W
    row = jnp.repeat(jnp.arange(rows, dtype=F32), GRID_W)
    col = jnp.tile(jnp.arange(GRID_W, dtype=F32), rows)
    axis_dim = HEAD_DIM // 2
    inv_freq = ROPE_THETA ** (-jnp.arange(0, axis_dim, 2, dtype=F32) / axis_dim)
    ang_r = row[:, None] * inv_freq[None, :]
    ang_c = col[:, None] * inv_freq[None, :]
    cr, sr, cc, sc = jnp.cos(ang_r), jnp.sin(ang_r), jnp.cos(ang_c), jnp.sin(ang_c)
    cos = jnp.concatenate([cr, cr, cc, cc], axis=1)
    sin = jnp.concatenate([-sr, sr, -sc, sc], axis=1)
    return jnp.tile(cos, (1, N_HEADS)), jnp.tile(sin, (1, N_HEADS))


def _pair_operands(kd, vd):
    lane = lax.broadcasted_iota(jnp.int32, kd.shape, 1)
    lo = lane < HALF
    zero = jnp.zeros_like(kd)
    kmats = (jnp.where(lo, kd, zero), jnp.where(lo, zero, kd))
    vstack = jnp.concatenate([jnp.where(lo, vd, zero), jnp.where(lo, zero, vd)], axis=0)
    return kmats, vstack


def _flash_kernel(sink_ref, q_ref, k_ref, v_ref, o_ref, m_scr, l_scr, acc_scr, *, has_sink, nk):
    ki = pl.program_id(2)
    tq = q_ref.shape[0]

    @pl.when(ki == 0)
    def _():
        m_scr[...] = jnp.full(m_scr.shape, NEG_BIG, F32)
        l_scr[...] = jnp.zeros(l_scr.shape, F32)
        acc_scr[...] = jnp.zeros(acc_scr.shape, F32)

    lane_q = lax.broadcasted_iota(jnp.int32, (tq, LANES), 1)
    lo_q = lane_q < HALF
    pairs_per_kv = N_HEADS // N_KV // 2
    for j in range(N_KV):
        kmats, vstack = _pair_operands(k_ref[:, j * LANES:(j + 1) * LANES],
                                       v_ref[:, j * LANES:(j + 1) * LANES])
        for pp in range(pairs_per_kv):
            hp = j * pairs_per_kv + pp
            qp = q_ref[:, hp * LANES:(hp + 1) * LANES]
            ps, alphas = [], []
            for par in range(2):
                h = 2 * hp + par
                s = _dot_nt(qp, kmats[par])
                m_prev = m_scr[h]
                m_new = jnp.maximum(m_prev, jnp.max(s, axis=1, keepdims=True))
                alpha = jnp.exp(m_prev - m_new)
                p = jnp.exp(s - m_new[:, :1])
                l_scr[h] = alpha * l_scr[h] + jnp.sum(p, axis=1, keepdims=True)
                m_scr[h] = m_new
                ps.append(p.astype(BF16))
                alphas.append(alpha)
            pv = _dot(jnp.concatenate(ps, axis=1), vstack)
            sl = slice(hp * LANES, (hp + 1) * LANES)
            acc_scr[:, sl] = acc_scr[:, sl] * jnp.where(lo_q, alphas[0], alphas[1]) + pv

    @pl.when(ki == nk - 1)
    def _():
        for hp in range(N_HEADS // 2):
            ls = []
            for par in range(2):
                h = 2 * hp + par
                l = l_scr[h]
                if has_sink:
                    l = l + jnp.exp(sink_ref[h] - m_scr[h])
                ls.append(l)
            sl = slice(hp * LANES, (hp + 1) * LANES)
            o_ref[:, sl] = (acc_scr[:, sl] / jnp.where(lo_q, ls[0], ls[1])).astype(BF16)


def _pick_tile(m, cands):
    for c in cands:
        if m % c == 0:
            return c
    raise ValueError(f"no tile for {m}")


def flash_attn(q, kd, vd, sink, nb, n, m, has_sink):
    tq = min(512, n)
    tk = _pick_tile(m, (768, 512, 256))
    nq, nk = n // tq, m // tk
    hq = N_HEADS * HEAD_DIM
    return pl.pallas_call(
        functools.partial(_flash_kernel, has_sink=has_sink, nk=nk),
        grid=(nb, nq, nk),
        in_specs=[pl.BlockSpec(memory_space=pltpu.SMEM),
                  pl.BlockSpec((tq, hq), lambda b, i, k: (b * nq + i, 0)),
                  pl.BlockSpec((tk, 2 * LANES), lambda b, i, k: (b * nk + k, 0)),
                  pl.BlockSpec((tk, 2 * LANES), lambda b, i, k: (b * nk + k, 0))],
        out_specs=pl.BlockSpec((tq, hq), lambda b, i, k: (b * nq + i, 0)),
        out_shape=jax.ShapeDtypeStruct((nb * n, hq), BF16),
        scratch_shapes=[pltpu.VMEM((N_HEADS, tq, LANES), F32),
                        pltpu.VMEM((N_HEADS, tq, LANES), F32),
                        pltpu.VMEM((tq, hq), F32)],
        compiler_params=_params(("parallel", "parallel", "arbitrary")),
        name="flash_attn",
    )(sink, q, kd, vd)


def _window_kernel(sink_ref, q_ref, kp_ref, km_ref, kn_ref, vp_ref, vm_ref, vn_ref, kc_ref, vc_ref,
                   o_ref, *, n, tq):
    i = pl.program_id(1)
    span = tq + 2 * WINDOW
    q0 = i * tq
    r = lax.broadcasted_iota(jnp.int32, (tq, span), 0)
    c = lax.broadcasted_iota(jnp.int32, (tq, span), 1)
    kpos = c + (q0 - WINDOW)
    ok = (c >= r) & (c <= r + 2 * WINDOW) & (kpos >= 0) & (kpos < n)
    lane_q = lax.broadcasted_iota(jnp.int32, (tq, LANES), 1)
    lo_q = lane_q < HALF
    kloc = jnp.concatenate([kp_ref[...], km_ref[...], kn_ref[...]], axis=0)
    vloc = jnp.concatenate([vp_ref[...], vm_ref[...], vn_ref[...]], axis=0)
    pairs_per_kv = N_HEADS // N_KV // 2
    for j in range(N_KV):
        js = slice(j * LANES, (j + 1) * LANES)
        kl, vl_stack = _pair_operands(kloc[:, js], vloc[:, js])
        kc, vc_stack = _pair_operands(kc_ref[:, js], vc_ref[:, js])
        for pp in range(pairs_per_kv):
            hp = j * pairs_per_kv + pp
            qp = q_ref[:, hp * LANES:(hp + 1) * LANES]
            pl_, pc_, ls = [], [], []
            for par in range(2):
                h = 2 * hp + par
                s_loc = jnp.where(ok, _dot_nt(qp, kl[par]), NEG_BIG)
                s_ctx = _dot_nt(qp, kc[par])
                snk = sink_ref[h]
                m = jnp.maximum(jnp.maximum(jnp.max(s_loc, axis=1, keepdims=True),
                                            jnp.max(s_ctx, axis=1, keepdims=True)), snk)
                p_loc = jnp.exp(s_loc - m)
                p_ctx = jnp.exp(s_ctx - m)
                l = (jnp.sum(p_loc, axis=1, keepdims=True) + jnp.sum(p_ctx, axis=1, keepdims=True)
                     + jnp.exp(snk - m))
                inv = 1.0 / l
                pl_.append((p_loc * inv).astype(BF16))
                pc_.append((p_ctx * inv).astype(BF16))
            o = _dot(jnp.concatenate(pl_, axis=1), vl_stack) + _dot(jnp.concatenate(pc_, axis=1), vc_stack)
            o_ref[:, hp * LANES:(hp + 1) * LANES] = o.astype(BF16)


def window_attn(q, kd, vd, kdc, vdc, sink, nb, n, mc):
    tq = 2 * WINDOW
    nq = n // tq
    wb = n // WINDOW
    hq = N_HEADS * HEAD_DIM
    prev = pl.BlockSpec((WINDOW, 2 * LANES), lambda b, i: (b * wb + jnp.maximum(2 * i - 1, 0), 0))
    main = pl.BlockSpec((tq, 2 * LANES), lambda b, i: (b * nq + i, 0))
    nxt = pl.BlockSpec((WINDOW, 2 * LANES), lambda b, i: (b * wb + jnp.minimum(2 * i + 2, wb - 1), 0))
    ctx = pl.BlockSpec((mc, 2 * LANES), lambda b, i: (b, 0))
    return pl.pallas_call(
        functools.partial(_window_kernel, n=n, tq=tq),
        grid=(nb, nq),
        in_specs=[pl.BlockSpec(memory_space=pltpu.SMEM),
                  pl.BlockSpec((tq, hq), lambda b, i: (b * nq + i, 0)),
                  prev, main, nxt, prev, main, nxt, ctx, ctx],
        out_specs=pl.BlockSpec((tq, hq), lambda b, i: (b * nq + i, 0)),
        out_shape=jax.ShapeDtypeStruct((nb * n, hq), BF16),
        compiler_params=_params(("parallel", "parallel")),
        name="window_attn",
    )(sink, q, kd, kd, kd, vd, vd, vd, kdc, vdc)


HALO = 16


def _conv_kernel(xm_ref, xp_ref, xn_ref, bm_ref, bp_ref, bn_ref, wx_ref, bx_ref, wb_ref, bb_ref,
                 ox_ref, ob_ref, *, nt):
    i = pl.program_id(1)
    has_prev = jnp.where(i > 0, 1.0, 0.0)
    has_next = jnp.where(i < nt - 1, 1.0, 0.0)

    def conv(m_ref, p_ref, n_ref, w_ref, b_ref, o_ref):
        x = m_ref[...].astype(F32)
        tl = x.shape[0]
        row = lax.broadcasted_iota(jnp.int32, x.shape, 0)
        before = p_ref[...].astype(F32)[HALO - 1:HALO, :] * has_prev
        after = n_ref[...].astype(F32)[0:1, :] * has_next
        xm1 = jnp.where(row == 0, before, pltpu.roll(x, 1, 0))
        xp1 = jnp.where(row == tl - 1, after, pltpu.roll(x, tl - 1, 0))
        w = w_ref[...]
        y = xm1 * w[0:1, :] + x * w[1:2, :] + xp1 * w[2:3, :] + b_ref[...]
        o_ref[...] = _silu(y).astype(BF16)

    conv(xm_ref, xp_ref, xn_ref, wx_ref, bx_ref, ox_ref)
    conv(bm_ref, bp_ref, bn_ref, wb_ref, bb_ref, ob_ref)


def ssm_conv(p, conv_w, conv_b, nb, n):
    tl = min(512, n)
    nt = n // tl
    hb = n // HALO
    hpt = tl // HALO
    cx, cb = SSM_INNER, 2 * SSM_GROUPS * SSM_STATE

    def specs(width, col):
        cblk = col // width
        return (pl.BlockSpec((tl, width), lambda b, i: (b * nt + i, cblk)),
                pl.BlockSpec((HALO, width), lambda b, i: (b * hb + jnp.maximum(i * hpt - 1, 0), cblk)),
                pl.BlockSpec((HALO, width), lambda b, i: (b * hb + jnp.minimum((i + 1) * hpt, hb - 1), cblk)))

    const = lambda shp: pl.BlockSpec(shp, lambda b, i: (0, 0))
    xm, xp, xn = specs(cx, C_XS)
    bm, bp, bn = specs(cb, C_BC)
    return pl.pallas_call(
        functools.partial(_conv_kernel, nt=nt),
        grid=(nb, nt),
        in_specs=[xm, xp, xn, bm, bp, bn, const((3, cx)), const((1, cx)), const((3, cb)), const((1, cb))],
        out_specs=[pl.BlockSpec((tl, cx), lambda b, i: (b * nt + i, 0)),
                   pl.BlockSpec((tl, cb), lambda b, i: (b * nt + i, 0))],
        out_shape=[jax.ShapeDtypeStruct((nb * n, cx), BF16), jax.ShapeDtypeStruct((nb * n, cb), BF16)],
        compiler_params=_params(("parallel", "parallel")),
        name="ssm_conv",
    )(p, p, p, p, p, p, conv_w[:, :cx], conv_b[:cx].reshape(1, cx), conv_w[:, cx:], conv_b[cx:].reshape(1, cb))


def _ssd_kernel(xs_ref, bc_ref, dt_ref, bias_ref, aneg_ref, tri_ref, rep_ref, s0_ref,
                y_ref, sfin_ref, st_scr, *, nc):
    d = pl.program_id(1)
    k = pl.program_id(2)
    q = SSM_CHUNK
    gw = SSM_INNER // SSM_GROUPS

    @pl.when(k == 0)
    def _():
        st_scr[...] = s0_ref[...]

    tri = tri_ref[...]
    rep = rep_ref[...]
    dt = _softplus(dt_ref[...] + bias_ref[...])
    a = dt * aneg_ref[...]
    ac = _exact_left(tri, a)
    act = ac.T
    acx = _exact_right(ac, rep)
    dtx = _exact_right(dt, rep)
    totx = jnp.where(d == 0, acx[q - 1:q, :], acx[0:1, :])
    xd = xs_ref[...].astype(F32) * dtx
    xd_b = xd.astype(BF16)
    xe = (xd * jnp.exp(totx - acx)).astype(BF16)
    ein = jnp.exp(acx)
    keep = tri > 0
    lane = lax.broadcasted_iota(jnp.int32, (q, LANES), 1)
    lo = lane < HALF
    zero = jnp.zeros((q, LANES), BF16)
    hpg = SSM_HEADS // SSM_GROUPS
    for g in range(SSM_GROUPS):
        bg = bc_ref[:, g * SSM_STATE:(g + 1) * SSM_STATE]
        cg = bc_ref[:, (SSM_GROUPS + g) * SSM_STATE:(SSM_GROUPS + g + 1) * SSM_STATE]
        cb = _dot_nt(cg, bg)
        st = st_scr[g]
        yoff = _dot(cg, st.astype(BF16)) * ein[:, g * gw:(g + 1) * gw]
        for hp in range(hpg // 2):
            gs = []
            for par in range(2):
                h = g * hpg + 2 * hp + par
                seg = ac[:, h:h + 1] - act[h:h + 1, :]
                gs.append((cb * jnp.exp(jnp.where(keep, seg, NEG_BIG))).astype(BF16))
            c0 = g * gw + hp * LANES
            xp = xd_b[:, c0:c0 + LANES]
            xstack = jnp.concatenate([jnp.where(lo, xp, zero), jnp.where(lo, zero, xp)], axis=0)
            ydiag = _dot(jnp.concatenate(gs, axis=1), xstack)
            y_ref[:, c0:c0 + LANES] = (ydiag + yoff[:, hp * LANES:(hp + 1) * LANES]).astype(BF16)
        bgt = bg.astype(F32).T.astype(BF16)
        cs = _dot(bgt, xe[:, g * gw:(g + 1) * gw])
        st_scr[g] = st * jnp.exp(totx[:, g * gw:(g + 1) * gw]) + cs

    @pl.when(k == nc - 1)
    def _():
        sfin_ref[...] = st_scr[...]


def ssd_scan(u_xs, u_bc, dt_raw, bias2, aneg2, s0, nb, n):
    q = SSM_CHUNK
    nc = n // q
    gw = SSM_INNER // SSM_GROUPS
    idx = np.arange(q)
    tri = np.stack([idx[:, None] >= idx[None, :], idx[:, None] <= idx[None, :]]).astype(np.float32)
    rep = (np.arange(LANES)[:, None] == np.arange(SSM_INNER)[None, :] // SSM_P).astype(np.float32)

    def chunk(b, d, k):
        return b * nc + k + d * (nc - 1 - 2 * k)

    return pl.pallas_call(
        functools.partial(_ssd_kernel, nc=nc),
        grid=(nb, 2, nc),
        in_specs=[pl.BlockSpec((q, SSM_INNER), lambda b, d, k: (chunk(b, d, k), 0)),
                  pl.BlockSpec((q, 2 * SSM_GROUPS * SSM_STATE), lambda b, d, k: (chunk(b, d, k), 0)),
                  pl.BlockSpec((q, LANES), lambda b, d, k: (chunk(b, d, k), d)),
                  pl.BlockSpec((None, 1, LANES), lambda b, d, k: (d, 0, 0)),
                  pl.BlockSpec((None, 1, LANES), lambda b, d, k: (d, 0, 0)),
                  pl.BlockSpec((None, q, q), lambda b, d, k: (d, 0, 0)),
                  pl.BlockSpec((LANES, SSM_INNER), lambda b, d, k: (0, 0)),
                  pl.BlockSpec((None, None, SSM_GROUPS, SSM_STATE, gw), lambda b, d, k: (b, d, 0, 0, 0))],
        out_specs=[pl.BlockSpec((None, q, SSM_INNER), lambda b, d, k: (d, chunk(b, d, k), 0)),
                   pl.BlockSpec((None, None, SSM_GROUPS, SSM_STATE, gw), lambda b, d, k: (b, d, 0, 0, 0))],
        out_shape=[jax.ShapeDtypeStruct((2, nb * n, SSM_INNER), BF16),
                   jax.ShapeDtypeStruct((nb, 2, SSM_GROUPS, SSM_STATE, gw), F32)],
        scratch_shapes=[pltpu.VMEM((SSM_GROUPS, SSM_STATE, gw), F32)],
        compiler_params=_params(("parallel", "parallel", "arbitrary")),
        name="ssd_scan",
    )(u_xs, u_bc, dt_raw, bias2, aneg2, jnp.asarray(tri, BF16), jnp.asarray(rep, BF16), s0)


def _post_kernel(oa_ref, ow_ref, yf_ref, yb_ref, xs_ref, z_ref, gt_ref, x_ref,
                 dsk_ref, nw_ref, wa_ref, ww_ref, ws_ref, wo_ref, gpost_ref, gate_ref,
                 gpre_ref, sh_ref, sc_ref, wrh_ref, wrl_ref,
                 xo_ref, h_ref, lg_ref):
    y = yf_ref[...].astype(F32) + yb_ref[...].astype(F32) + dsk_ref[...] * xs_ref[...].astype(F32)
    u = y * _silu(z_ref[...].astype(F32))
    gw = SSM_INNER // SSM_GROUPS
    ys = jnp.concatenate([_rms(u[:, g * gw:(g + 1) * gw]) for g in range(SSM_GROUPS)], axis=1)
    ys = (ys * nw_ref[...]).astype(BF16)
    d = x_ref.shape[1]
    ga = jax.nn.sigmoid(gt_ref[:, 0:d].astype(F32))
    gw_ = jax.nn.sigmoid(gt_ref[:, d:2 * d].astype(F32))
    gs = jax.nn.sigmoid(gt_ref[:, 2 * d:3 * d].astype(F32))
    m = ga * _dot(oa_ref[...], wa_ref[...]) + gw_ * _dot(ow_ref[...], ww_ref[...]) + gs * _dot(ys, ws_ref[...])
    ml = _dot(m.astype(BF16), wo_ref[...])
    xn = x_ref[...] + gate_ref[...] * (_rms(ml) * gpost_ref[...])
    xo_ref[...] = xn
    h = (_rms(xn) * gpre_ref[...]) * (1.0 + sc_ref[...]) + sh_ref[...]
    h_ref[...] = h
    hb = h.astype(BF16)
    hl =(h - hb.astype(F32)).astype(BF16)
    lg_ref[...] = _dot_nt(wrh_ref[...], hb) + _dot_nt(wrh_ref[...], hl) + _dot_nt(wrl_ref[...], hb)


def post_mixer(oa, ow, y2, u_xs, p, x, dskip, norm_w, wa, ww, ws, wo, gpost, gate, gpre, shift, scale,
               wr_hi, wr_lo, rows_per_batch):
    t, d = x.shape
    tm = min(256, rows_per_batch)
    tpb = rows_per_batch // tm
    nt = t // tm
    hq = N_HEADS * HEAD_DIM
    row = lambda w, c=0: pl.BlockSpec((tm, w), lambda i: (i, c // w))
    const = lambda shp: pl.BlockSpec(shp, lambda i: (0,) * len(shp))
    mod = pl.BlockSpec((None, 1, d), lambda i: (i // tpb, 0, 0))
    return pl.pallas_call(
        _post_kernel,
        grid=(nt,),
        in_specs=[row(hq), row(hq),
                  pl.BlockSpec((None, tm, SSM_INNER), lambda i: (0, i, 0)),
                  pl.BlockSpec((None, tm, SSM_INNER), lambda i: (1, i, 0)),
                  row(SSM_INNER), row(SSM_INNER, C_Z), row(3 * d, C_GATES), row(d),
                  const((1, SSM_INNER)), const((1, SSM_INNER)),
                  const((hq, d)), const((hq, d)), const((SSM_INNER, d)), const((d, d)),
                  const((1, d)), mod, const((1, d)), mod, mod,
                  const((N_EXPERTS, d)), const((N_EXPERTS, d))],
        out_specs=[row(d), row(d), pl.BlockSpec((N_EXPERTS, tm), lambda i: (0, i))],
        out_shape=[jax.ShapeDtypeStruct((t, d), F32), jax.ShapeDtypeStruct((t, d), F32),
                   jax.ShapeDtypeStruct((N_EXPERTS, t), F32)],
        compiler_params=_params(("parallel",)),
        name="post_mixer",
    )(oa, ow, y2, y2, u_xs, p, p, x, dskip, norm_w, wa, ww, ws, wo, gpost, gate, gpre, shift, scale,
      wr_hi, wr_lo)


def _route_kernel(lg_ref, b_ref, ei_ref, w_ref):
    scores = jax.nn.sigmoid(lg_ref[...])
    sel = scores + b_ref[...]
    tt = sel.shape[1]
    per = N_EXPERTS // N_EXPERT_GROUPS
    r8 = lax.broadcasted_iota(jnp.int32, (per, tt), 0).astype(F32)
    ninf = -jnp.inf

    def argmax_rows(x, rows, nrows):
        m = jnp.max(x, axis=0, keepdims=True)
        idx = jnp.min(jnp.where(x == m, rows, float(nrows)), axis=0, keepdims=True)
        return m, idx

    gscores = []
    for g in range(N_EXPERT_GROUPS):
        blk = sel[g * per:(g + 1) * per, :]
        m1, i1 = argmax_rows(blk, r8, per)
        m2 = jnp.max(jnp.where(r8 == i1, ninf, blk), axis=0, keepdims=True)
        gscores.append(m1 + m2)
    cur = jnp.concatenate(gscores, axis=0)
    rg = lax.broadcasted_iota(jnp.int32, cur.shape, 0).astype(F32)
    chosen = jnp.zeros(cur.shape, F32)
    for _ in range(TOPK_GROUPS):
        _, gi = argmax_rows(cur, rg, N_EXPERT_GROUPS)
        hit = rg == gi
        chosen = jnp.where(hit, 1.0, chosen)
        cur = jnp.where(hit, ninf, cur)
    gmask = jnp.concatenate([jnp.broadcast_to(chosen[g:g + 1, :], (per, tt)) for g in range(N_EXPERT_GROUPS)],
                            axis=0)
    cur = jnp.where(gmask > 0, sel, ninf)
    re = lax.broadcasted_iota(jnp.int32, cur.shape, 0).astype(F32)
    idxs, ws = [], []
    for _ in range(TOP_K):
        _, ei = argmax_rows(cur, re, N_EXPERTS)
        hit = re == ei
        ws.append(jnp.sum(jnp.where(hit, scores, 0.0), axis=0, keepdims=True))
        idxs.append(ei)
        cur = jnp.where(hit, ninf, cur)
    w = jnp.concatenate(ws, axis=0)
    w_ref[...] = w / jnp.sum(w, axis=0, keepdims=True) * ROUTED_SCALE
    ei_ref[...] = jnp.concatenate(idxs, axis=0).astype(jnp.int32)


def route(logits_t, b_router):
    e, t = logits_t.shape
    tt = 512
    return pl.pallas_call(
        _route_kernel,
        grid=(t // tt,),
        in_specs=[pl.BlockSpec((e, tt), lambda i: (0, i)), pl.BlockSpec((e, 1), lambda i: (0, 0))],
        out_specs=[pl.BlockSpec((TOP_K, tt), lambda i: (0, i)), pl.BlockSpec((TOP_K, tt), lambda i: (0, i))],
        out_shape=[jax.ShapeDtypeStruct((TOP_K, t), jnp.int32), jax.ShapeDtypeStruct((TOP_K, t), F32)],
        compiler_params=_params(("parallel",)),
        name="route",
    )(logits_t, b_router.reshape(e, 1))


MOE_ROWS = 256
PLAN_TOKENS = 512


def _moe_geometry(t):
    nblk = -(-(t * TOP_K + N_EXPERTS * (MOE_ROWS - 1)) // MOE_ROWS)
    return nblk, nblk * MOE_ROWS


def _plan_kernel(ei_ref, ut_ref, tril_ref, dest_ref, tab_ref, be_ref, cnt_scr, run_scr):
    ph = pl.program_id(0)
    i = pl.program_id(1)
    ei = ei_ref[...]
    tt = ei.shape[1]
    re = lax.broadcasted_iota(jnp.int32, (N_EXPERTS, tt), 0)
    hits = [re == ei[k:k + 1, :] for k in range(TOP_K)]
    oh = jnp.zeros((N_EXPERTS, tt), F32)
    for k in range(TOP_K):
        oh = oh + jnp.where(hits[k], 1.0, 0.0)

    @pl.when((ph == 0) & (i == 0))
    def _():
        cnt_scr[...] = jnp.zeros(cnt_scr.shape, F32)

    @pl.when(ph == 0)
    def _():
        cnt_scr[...] = cnt_scr[...] + jnp.sum(oh, axis=1, keepdims=True)

    @pl.when((ph == 1) & (i == 0))
    def _():
        cnt = cnt_scr[...]
        padded = ((cnt.astype(jnp.int32) + (MOE_ROWS - 1)) & (-MOE_ROWS)).astype(F32)
        pad_end = _exact_left(tril_ref[...], padded)
        pad_start = pad_end - padded
        run_scr[...] = pad_start
        tab_ref[0] = pad_start
        tab_ref[1] = pad_end
        nbp = be_ref.shape[1]
        blk0 = (lax.broadcasted_iota(jnp.int32, (N_EXPERTS, nbp), 1) * MOE_ROWS).astype(F32)
        be = jnp.sum(jnp.where(pad_end[:, :1] <= blk0, 1.0, 0.0), axis=0, keepdims=True)
        be_ref[...] = jnp.broadcast_to(jnp.minimum(be, N_EXPERTS - 1.0), be_ref.shape).astype(jnp.int32)

    @pl.when(ph == 1)
    def _():
        cin = _dot(oh.astype(BF16), ut_ref[...])
        pos = run_scr[:, :1] + (cin - oh)
        rows = [jnp.sum(jnp.where(hits[k], pos, 0.0), axis=0, keepdims=True) for k in range(TOP_K)]
        dest_ref[...] = jnp.concatenate(rows, axis=0).astype(jnp.int32)
        run_scr[...] = run_scr[...] + cin[:, tt - 1:tt]


def moe_plan(eidx_t):
    k, t = eidx_t.shape
    tt = PLAN_TOKENS
    nt = t // tt
    nblk, _ = _moe_geometry(t)
    nbp = -(-nblk // LANES) * LANES
    ut = jnp.asarray(np.arange(tt)[:, None] <= np.arange(tt)[None, :], BF16)
    tril = jnp.asarray(np.arange(N_EXPERTS)[:, None] >= np.arange(N_EXPERTS)[None, :], BF16)
    return pl.pallas_call(
        _plan_kernel,
        grid=(2, nt),
        in_specs=[pl.BlockSpec((k, tt), lambda p, i: (0, i)),
                  pl.BlockSpec((tt, tt), lambda p, i: (0, 0)),
                  pl.BlockSpec((N_EXPERTS, N_EXPERTS), lambda p, i: (0, 0))],
        out_specs=[pl.BlockSpec((k, tt), lambda p, i: (0, i * p)),
                   pl.BlockSpec((2, N_EXPERTS, LANES), lambda p, i: (0, 0, 0)),
                   pl.BlockSpec((8, nbp), lambda p, i: (0, 0))],
        out_shape=[jax.ShapeDtypeStruct((k, t), jnp.int32),
                   jax.ShapeDtypeStruct((2, N_EXPERTS, LANES), F32),
                   jax.ShapeDtypeStruct((8, nbp), jnp.int32)],
        scratch_shapes=[pltpu.VMEM((N_EXPERTS, LANES), F32), pltpu.VMEM((N_EXPERTS, LANES), F32)],
        compiler_params=_params(("arbitrary", "arbitrary")),
        name="moe_plan",
    )(eidx_t, ut, tril)


def _dispatch_kernel(dest_ref, pstart_ref, pend_ref, h_ref, xs_hbm, zero_scr, sem):
    i = pl.program_id(0)
    tt = h_ref.shape[0]

    @pl.when(i == 0)
    def _():
        zero_scr[...] = jnp.zeros(zero_scr.shape, F32)
        nblk = xs_hbm.shape[0] // MOE_ROWS
        n_used = pend_ref[N_EXPERTS - 1] // MOE_ROWS

        def zero_block(row0):
            return pltpu.make_async_copy(
                zero_scr, xs_hbm.at[pl.ds(pl.multiple_of(row0, MOE_ROWS), MOE_ROWS), :], sem)

        def seg_start(e, c):
            @pl.when(pend_ref[e] > pstart_ref[e])
            def _():
                zero_block(pend_ref[e] - MOE_ROWS).start()
            return c

        def seg_wait(e, c):
            @pl.when(pend_ref[e] > pstart_ref[e])
            def _():
                zero_block(pend_ref[e] - MOE_ROWS).wait()
            return c

        def tail_start(b, c):
            zero_block(b * MOE_ROWS).start()
            return c

        def tail_wait(b, c):
            zero_block(b * MOE_ROWS).wait()
            return c

        lax.fori_loop(0, N_EXPERTS, seg_start, 0)
        lax.fori_loop(n_used, nblk, tail_start, 0)
        lax.fori_loop(0, N_EXPERTS, seg_wait, 0)
        lax.fori_loop(n_used, nblk, tail_wait, 0)

    def issue(t, c):
        for k in range(TOP_K):
            pltpu.make_async_copy(h_ref.at[pl.ds(t, 1), :], xs_hbm.at[pl.ds(dest_ref[k, t], 1), :], sem).start()
        return c

    lax.fori_loop(0, tt, issue, 0)
    for k in range(TOP_K):
        pltpu.make_async_copy(h_ref, xs_hbm.at[pl.ds(0, tt), :], sem).wait()


def moe_dispatch(dest_t, pad_start, pad_end, h):
    t, d = h.shape
    tt = PLAN_TOKENS
    _, cap = _moe_geometry(t)
    smem = pl.BlockSpec(memory_space=pltpu.SMEM)
    return pl.pallas_call(
        _dispatch_kernel,
        grid=(t // tt,),
        in_specs=[pl.BlockSpec((TOP_K, tt), lambda i: (0, i), memory_space=pltpu.SMEM), smem, smem,
                  pl.BlockSpec((tt, d), lambda i: (i, 0))],
        out_specs=pl.BlockSpec(memory_space=pl.ANY),
        out_shape=jax.ShapeDtypeStruct((cap, d), F32),
        scratch_shapes=[pltpu.VMEM((MOE_ROWS, d), F32), pltpu.SemaphoreType.DMA(())],
        compiler_params=_params(("arbitrary",)),
        name="moe_dispatch",
    )(dest_t, pad_start, pad_end, h)


def _expert_kernel(be_ref, nu_ref, x_ref, wg_ref, wu_ref, wd_ref, o_ref):
    used = pl.program_id(0) < nu_ref[0]

    @pl.when(used)
    def _():
        x = x_ref[...].astype(BF16)
        hg = _dot(x, wg_ref[...].astype(BF16))
        hu = _dot(x, wu_ref[...].astype(BF16))
        hb = (_silu(hg) * hu).astype(BF16)
        o_ref[...] = _dot(hb, wd_ref[...].astype(BF16))

    @pl.when(jnp.logical_not(used))
    def _():
        o_ref[...] = jnp.zeros(o_ref.shape, F32)


def expert_ffn(blk_e, n_used, xs, we_gate, we_up, we_down):
    cap, d = xs.shape
    de = we_gate.shape[2]
    nblk = cap // MOE_ROWS
    blk = lambda i, nu: jnp.minimum(i, nu[0] - 1)
    grid_spec = pltpu.PrefetchScalarGridSpec(
        num_scalar_prefetch=2,
        grid=(nblk,),
        in_specs=[pl.BlockSpec((MOE_ROWS, d), lambda i, be, nu: (blk(i, nu), 0)),
                  pl.BlockSpec((None, d, de), lambda i, be, nu: (be[blk(i, nu)], 0, 0)),
                  pl.BlockSpec((None, d, de), lambda i, be, nu: (be[blk(i, nu)], 0, 0)),
                  pl.BlockSpec((None, de, d), lambda i, be, nu: (be[blk(i, nu)], 0, 0))],
        out_specs=pl.BlockSpec((MOE_ROWS, d), lambda i, be, nu: (i, 0)),
    )
    return pl.pallas_call(
        _expert_kernel,
        grid_spec=grid_spec,
        out_shape=jax.ShapeDtypeStruct((cap, d), F32),
        compiler_params=_params(("arbitrary",)),
        name="expert_ffn",
    )(blk_e, n_used, xs, we_gate, we_up, we_down)


def _ffn_tail_kernel(dest_ref, w_ref, h_ref, x_ref, wg_ref, wu_ref, wd_ref, gpost_ref, gate_ref, ys_hbm,
                     xo_ref, buf, sem):
    tt = h_ref.shape[0]

    def issue(t, c):
        for k in range(TOP_K):
            pltpu.make_async_copy(ys_hbm.at[pl.ds(dest_ref[k, t], 1), :], buf.at[k, pl.ds(t, 1), :], sem).start()
        return c

    lax.fori_loop(0, tt, issue, 0)
    h = h_ref[...].astype(BF16)
    hs = (_silu(_dot(h, wg_ref[...])) * _dot(h, wu_ref[...])).astype(BF16)
    f = _dot(hs, wd_ref[...])
    w = w_ref[...]
    for k in range(TOP_K):
        pltpu.make_async_copy(ys_hbm.at[pl.ds(0, tt), :], buf.at[k], sem).wait()
    for k in range(TOP_K):
        f = f + buf[k] * w[:, k:k + 1]
    xo_ref[...] = x_ref[...] + gate_ref[...] * (_rms(f) * gpost_ref[...])


def ffn_tail(dest_t, wts, h, x, ys, wsg, wsu, wsd, gpost, gate, rows_per_batch):
    t, d = x.shape
    ds = wsg.shape[1]
    tm = min(256, rows_per_batch)
    tpb = rows_per_batch // tm
    row = pl.BlockSpec((tm, d), lambda i: (i, 0))
    const = lambda shp: pl.BlockSpec(shp, lambda i: (0,) * len(shp))
    return pl.pallas_call(
        _ffn_tail_kernel,
        grid=(t // tm,),
        in_specs=[pl.BlockSpec((TOP_K, tm), lambda i: (0, i), memory_space=pltpu.SMEM),
                  pl.BlockSpec((tm, TOP_K), lambda i: (i, 0)),
                  row, row, const((d, ds)), const((d, ds)), const((ds, d)), const((1, d)),
                  pl.BlockSpec((None, 1, d), lambda i: (i // tpb, 0, 0)),
                  pl.BlockSpec(memory_space=pl.ANY)],
        out_specs=row,
        out_shape=jax.ShapeDtypeStruct((t, d), F32),
        scratch_shapes=[pltpu.VMEM((TOP_K, tm, d), F32), pltpu.SemaphoreType.DMA(())],
        compiler_params=_params(("arbitrary",)),
        name="ffn_tail",
    )(dest_t, wts, h, x, wsg, wsu, wsd, gpost, gate, ys)


def _moe_routed(h, logits_t, b_router, we_gate, we_up, we_down):
    eidx_t, wts_t = route(logits_t, b_router)
    dest_t, tabs, blk_e = moe_plan(eidx_t)
    pad_start = tabs[0, :, 0].astype(jnp.int32)
    pad_end = tabs[1, :, 0].astype(jnp.int32)
    n_used = (pad_end[N_EXPERTS - 1:] // MOE_ROWS).astype(jnp.int32)
    xs = moe_dispatch(dest_t, pad_start, pad_end, h)
    ys = expert_ffn(blk_e[0], n_used, xs, we_gate, we_up, we_down)
    return dest_t, wts_t.T, ys


def _reorder_w_in(w):
    qa, ka, va, qw, kw, vw, z, xs, bm, cm, dt, gates = jnp.split(
        w, [512, 640, 768, 1280, 1408, 1536, 2560, 3584, 3840, 4096, 4128], axis=1)
    w_main = jnp.concatenate([qa, qw, z, xs, gates, ka, va, kw, vw, bm, cm], axis=1).astype(BF16)
    pad = jnp.zeros((w.shape[0], LANES - SSM_HEADS), w.dtype)
    w_dt = jnp.concatenate([dt[:, :SSM_HEADS], pad, dt[:, SSM_HEADS:], pad], axis=1).astype(BF16)
    return w_main, w_dt


def _pad_lanes(v):
    return jnp.pad(v, ((0, 0), (0, LANES - v.shape[1])))


def kernel(x, c, ctx, c_ctx, w_ada, b_ada, g_mix_pre, g_mix_post, g_ffn_pre, g_ffn_post, w_in, g_q_a, g_k_a, sink_w, ssm_conv_w, ssm_conv_b, ssm_dt_bias, ssm_a_log, ssm_d, ssm_norm, w_br_a, w_br_w, w_br_s, w_out, w_router, b_router, we_gate, we_up, we_down, ws_gate, ws_up, ws_down):
    nb, n, d = x.shape
    mc = ctx.shape[1]
    depth = w_in.shape[0]
    t_lat, t_ctx = nb * n, nb * mc
    cos, sin = rope_tables(n)
    xl = x.reshape(t_lat, d)
    xc = ctx.reshape(t_ctx, d)
    c8 = jnp.concatenate([c, c_ctx[None, :], jnp.zeros((8 - nb - 1, d), F32)], axis=0)
    zeros_sink = jnp.zeros((N_HEADS,), F32)
    s_zero = jnp.zeros((nb, 2, SSM_GROUPS, SSM_STATE, SSM_INNER // SSM_GROUPS), F32)
    dummy_tab = jnp.zeros((mc, N_HEADS * HEAD_DIM), F32)

    for i in range(depth):
        last = i == depth - 1
        mod = ada_mod(c8, w_ada[i], b_ada[i])
        mod_l = [mod[:nb, k * d:(k + 1) * d].reshape(nb, 1, d) for k in range(6)]
        mod_c = [mod[nb:nb + 1, k * d:(k + 1) * d].reshape(1, 1, d) for k in range(6)]
        w_main, w_dt = _reorder_w_in(w_in[i])
        gq = jnp.tile(g_q_a[i], N_HEADS)[None, :]
        gk = jnp.tile(g_k_a[i], N_KV)[None, :]
        bias2 = _pad_lanes(ssm_dt_bias[i].reshape(2, SSM_HEADS)).reshape(2, 1, LANES)
        aneg2 = _pad_lanes(-jnp.exp(ssm_a_log[i].astype(F32))).reshape(2, 1, LANES)
        dskip = jnp.repeat(ssm_d[i], SSM_P)[None, :]
        norm_w = ssm_norm[i][None, :]
        wa, ww, ws, wo = (w_br_a[i].astype(BF16), w_br_w[i].astype(BF16), w_br_s[i].astype(BF16),
                          w_out[i].astype(BF16))
        wr_t = w_router[i].T
        wr_hi = wr_t.astype(BF16)
        wr_lo = (wr_t - wr_hi.astype(F32)).astype(BF16)
        sink = sink_w[i].astype(F32)

        p_c, dt_c = in_proj(xc, g_mix_pre[i], mod_c[0], mod_c[1], w_main, w_dt, t_ctx)
        qa_c, qw_c, kda_c, vda_c, kdw_c, vdw_c = attn_prep(p_c, dummy_tab, dummy_tab, gq, gk, nb, mc, rope=False)
        uxs_c, ubc_c = ssm_conv(p_c, ssm_conv_w[i], ssm_conv_b[i], nb, mc)
        y_c, s_fin = ssd_scan(uxs_c, ubc_c, dt_c, bias2, aneg2, s_zero, nb, mc)

        p_l, dt_l = in_proj(xl, g_mix_pre[i], mod_l[0], mod_l[1], w_main, w_dt, n)
        qa, qw, kda, vda, kdw, vdw = attn_prep(p_l, cos, sin, gq, gk, nb, n, rope=True)
        m_all = n + mc
        kd_all = jnp.concatenate([kda.reshape(nb, n, -1), kda_c.reshape(nb, mc, -1)], axis=1).reshape(nb * m_all, -1)
        vd_all = jnp.concatenate([vda.reshape(nb, n, -1), vda_c.reshape(nb, mc, -1)], axis=1).reshape(nb * m_all, -1)
        oa = flash_attn(qa, kd_all, vd_all, zeros_sink, nb, n, m_all, has_sink=False)
        ow = window_attn(qw, kdw, vdw, kdw_c, vdw_c, sink, nb, n, mc)
        uxs, ubc = ssm_conv(p_l, ssm_conv_w[i], ssm_conv_b[i], nb, n)
        y_l, _ = ssd_scan(uxs, ubc, dt_l, bias2, aneg2, s_fin, nb, n)
        xl, h_l, lg_l = post_mixer(oa, ow, y_l, uxs, p_l, xl, dskip, norm_w, wa, ww, ws, wo,
                                   g_mix_post[i][None, :], mod_l[2], g_ffn_pre[i][None, :], mod_l[3], mod_l[4],
                                   wr_hi, wr_lo, n)
        wsg, wsu, wsd = ws_gate[i].astype(BF16), ws_up[i].astype(BF16), ws_down[i].astype(BF16)
        if last:
            dest_t, wts, ys = _moe_routed(h_l, lg_l, b_router[i], we_gate[i], we_up[i], we_down[i])
            xl = ffn_tail(dest_t, wts, h_l, xl, ys, wsg, wsu, wsd, g_ffn_post[i][None, :], mod_l[5], n)
        else:
            oa_c = flash_attn(qa_c, kda_c, vda_c, zeros_sink, nb, mc, mc, has_sink=False)
            ow_c = flash_attn(qw_c, kdw_c, vdw_c, sink, nb, mc, mc, has_sink=True)
            xc, h_c, lg_c = post_mixer(oa_c, ow_c, y_c, uxs_c, p_c, xc, dskip, norm_w, wa, ww, ws, wo,
                                       g_mix_post[i][None, :], mod_c[2], g_ffn_pre[i][None, :], mod_c[3], mod_c[4],
                                       wr_hi, wr_lo, t_ctx)
            h_all = jnp.concatenate([h_l, h_c], axis=0)
            lg_all = jnp.concatenate([lg_l, lg_c], axis=1)
            dest_t, wts, ys = _moe_routed(h_all, lg_all, b_router[i], we_gate[i], we_up[i], we_down[i])
            xl = ffn_tail(dest_t[:, :t_lat], wts[:t_lat], h_l, xl, ys, wsg, wsu, wsd,
                          g_ffn_post[i][None, :], mod_l[5], n)
            xc = ffn_tail(dest_t[:, t_lat:], wts[t_lat:], h_c, xc, ys, wsg, wsu, wsd,
                          g_ffn_post[i][None, :], mod_c[5], t_ctx)
    return xl.reshape(nb, n, d)
```

```python
import functools
import math

import jax
import jax.numpy as jnp
import numpy as np
from jax import lax
from jax.experimental import pallas as pl
from jax.experimental.pallas import tpu as pltpu

F32 = jnp.float32
BF16 = jnp.bfloat16

HEAD_DIM = 64
N_HEADS = 8
N_KV = 2
GRID_W = 64
ROPE_THETA = 10000.0
WINDOW = 128
SSM_HEADS = 16
SSM_P = 64
SSM_INNER = SSM_HEADS * SSM_P
SSM_GROUPS = 2
SSM_STATE = 128
SSM_CHUNK = 128
N_EXPERTS = 64
TOP_K = 8
N_EXPERT_GROUPS = 8
TOPK_GROUPS = 4
ROUTED_SCALE = 2.5
EPS = 1e-6

LANES = 128
HALF = LANES // 2
VMEM_LIMIT = 56 * 1024 * 1024
NEG_BIG = -1e30

C_QA, C_QW, C_Z, C_XS, C_GATES = 0, 512, 1024, 2048, 3072
C_KA, C_VA, C_KW, C_VW, C_BC = 6144, 6272, 6400, 6528, 6656
P_WIDTH = 7168


def _params(sem, vmem=VMEM_LIMIT):
    return pltpu.CompilerParams(dimension_semantics=sem, vmem_limit_bytes=vmem)


def _silu(x):
    return x * jax.nn.sigmoid(x)


def _softplus(x):
    return jnp.maximum(x, 0.0) + jnp.log(1.0 + jnp.exp(-jnp.abs(x)))


def _rms(x, eps=EPS):
    return x * lax.rsqrt(jnp.mean(x * x, axis=-1, keepdims=True) + eps)


def _split3(a):
    a1 = a.astype(BF16)
    r1 = a - a1.astype(F32)
    a2 = r1.astype(BF16)
    a3 = (r1 - a2.astype(F32)).astype(BF16)
    return a1, a2, a3


def _dot(a, b):
    return jnp.dot(a, b, preferred_element_type=F32)


def _dot_nt(a, b):
    return lax.dot_general(a, b, (((1,), (1,)), ((), ())), preferred_element_type=F32)


def _exact_right(a, r01):
    a1, a2, a3 = _split3(a)
    return _dot(a1, r01) + _dot(a2, r01) + _dot(a3, r01)


def _exact_left(m01, a):
    a1, a2, a3 = _split3(a)
    return _dot(m01, a1) + _dot(m01, a2) + _dot(m01, a3)


def _ada_kernel(c_ref, w_ref, b_ref, o_ref):
    h = _silu(c_ref[...])
    o_ref[...] = jnp.dot(h, w_ref[...], preferred_element_type=F32,
                         precision=lax.Precision.HIGHEST) + b_ref[...]


def ada_mod(c8, w, b):
    d, n = w.shape
    tn = 1536
    return pl.pallas_call(
        _ada_kernel,
        grid=(n // tn,),
        in_specs=[pl.BlockSpec((8, d), lambda j: (0, 0)),
                  pl.BlockSpec((d, tn), lambda j: (0, j)),
                  pl.BlockSpec((1, tn), lambda j: (0, j))],
        out_specs=pl.BlockSpec((8, tn), lambda j: (0, j)),
        out_shape=jax.ShapeDtypeStruct((8, n), F32),
        compiler_params=_params(("parallel",)),
        name="ada_mod",
    )(c8, w, b.reshape(1, n))


def _inproj_kernel(x_ref, g_ref, sh_ref, sc_ref, w_ref, wdt_ref, o_ref, odt_ref, h_scr):
    @pl.when(pl.program_id(1) == 0)
    def _():
        h = _rms(x_ref[...]) * g_ref[...]
        h = h * (1.0 + sc_ref[...]) + sh_ref[...]
        hb = h.astype(BF16)
        h_scr[...] = hb
        odt_ref[...] = _dot(hb, wdt_ref[...])

    o_ref[...] = _dot(h_scr[...], w_ref[...]).astype(BF16)


def in_proj(x, g, shift, scale, w_main, w_dt, rows_per_batch):
    t, d = x.shape
    n = w_main.shape[1]
    tm = min(512, rows_per_batch)
    tn = n // 2
    tpb = rows_per_batch // tm
    mod_spec = pl.BlockSpec((None, 1, d), lambda i, j: (i // tpb, 0, 0))
    return pl.pallas_call(
        _inproj_kernel,
        grid=(t // tm, n // tn),
        in_specs=[pl.BlockSpec((tm, d), lambda i, j: (i, 0)),
                  pl.BlockSpec((1, d), lambda i, j: (0, 0)),
                  mod_spec, mod_spec,
                  pl.BlockSpec((d, tn), lambda i, j: (0, j)),
                  pl.BlockSpec((d, 2 * LANES), lambda i, j: (0, 0))],
        out_specs=[pl.BlockSpec((tm, tn), lambda i, j: (i, j)),
                   pl.BlockSpec((tm, 2 * LANES), lambda i, j: (i, 0))],
        out_shape=[jax.ShapeDtypeStruct((t, n), BF16),
                   jax.ShapeDtypeStruct((t, 2 * LANES), F32)],
        scratch_shapes=[pltpu.VMEM((tm, d), BF16)],
        compiler_params=_params(("parallel", "arbitrary")),
        name="in_proj",
    )(x, g.reshape(1, d), shift, scale, w_main, w_dt)


def _rope(x, cos, sin):
    w = x.shape[-1]
    lane = lax.broadcasted_iota(jnp.int32, x.shape, 1)
    first = (lane % 32) < 16
    swapped = jnp.where(first, pltpu.roll(x, w - 16, 1), pltpu.roll(x, 16, 1))
    return x * cos + swapped * sin


def _dup_halves(x):
    lane = lax.broadcasted_iota(jnp.int32, x.shape, 1)
    lo = lane < HALF
    r = pltpu.roll(x, HALF, 1)
    return jnp.concatenate([jnp.where(lo, x, r), jnp.where(lo, r, x)], axis=1)


def _prep_kernel(qa_ref, qw_ref, ka_ref, va_ref, kw_ref, vw_ref, cos_ref, sin_ref,
                 gq_ref, gk_ref, bdq_ref, bdk_ref,
                 qa_o, qw_o, kda_o, vda_o, kdw_o, vdw_o, *, rope):
    scale = HEAD_DIM ** -0.5
    inv_hd = 1.0 / HEAD_DIM

    def headnorm(x, g, bd):
        ss = _dot((x * x).astype(BF16), bd) * inv_hd
        return x * lax.rsqrt(ss + EPS) * g

    qa = headnorm(qa_ref[...].astype(F32), gq_ref[...], bdq_ref[...])
    ka = headnorm(ka_ref[...].astype(F32), gk_ref[...], bdk_ref[...])
    qw = qw_ref[...].astype(F32)
    kw = kw_ref[...].astype(F32)
    if rope:
        cos = cos_ref[...]
        sin = sin_ref[...]
        qa = _rope(qa, cos, sin)
        qw = _rope(qw, cos, sin)
        ka = _rope(ka, cos[:, :LANES], sin[:, :LANES])
        kw = _rope(kw, cos[:, :LANES], sin[:, :LANES])
    qa_o[...] = (qa * scale).astype(BF16)
    qw_o[...] = (qw * scale).astype(BF16)
    kda_o[...] = _dup_halves(ka).astype(BF16)
    kdw_o[...] = _dup_halves(kw).astype(BF16)
    vda_o[...] = _dup_halves(va_ref[...].astype(F32)).astype(BF16)
    vdw_o[...] = _dup_halves(vw_ref[...].astype(F32)).astype(BF16)


def attn_prep(p, cos, sin, gq, gk, nb, n, rope):
    t = nb * n
    tm = min(512, n)
    spb = n // tm
    hq = N_HEADS * HEAD_DIM
    hk = N_KV * HEAD_DIM
    bdq = (np.arange(hq)[:, None] // HEAD_DIM == np.arange(hq)[None, :] // HEAD_DIM)
    bdq = jnp.asarray(bdq, BF16)
    bdk = bdq[:hk, :hk]
    qspec = lambda c: pl.BlockSpec((tm, hq), lambda s, b: (b * spb + s, c // hq))
    kspec = lambda c: pl.BlockSpec((tm, hk), lambda s, b: (b * spb + s, c // hk))
    tab = pl.BlockSpec((tm, hq), lambda s, b: (s, 0))
    const = lambda shp: pl.BlockSpec(shp, lambda s, b: (0, 0))
    oq = pl.BlockSpec((tm, hq), lambda s, b: (b * spb + s, 0))
    ok = pl.BlockSpec((tm, 2 * hk), lambda s, b: (b * spb + s, 0))
    return pl.pallas_call(
        functools.partial(_prep_kernel, rope=rope),
        grid=(spb, nb),
        in_specs=[qspec(C_QA), qspec(C_QW), kspec(C_KA), kspec(C_VA), kspec(C_KW), kspec(C_VW),
                  tab, tab, const((1, hq)), const((1, hk)), const((hq, hq)), const((hk, hk))],
        out_specs=[oq, oq, ok, ok, ok, ok],
        out_shape=[jax.ShapeDtypeStruct((t, hq), BF16)] * 2 + [jax.ShapeDtypeStruct((t, 2 * hk), BF16)] * 4,
        compiler_params=_params(("parallel", "arbitrary")),
        name="attn_prep",
    )(p, p, p, p, p, p, cos, sin, gq, gk, bdq, bdk)


def rope_tables(n):
    rows = n // GRID_W
    row = jnp.repeat(jnp.arange(rows, dtype=F32), GRID_W)
    col = jnp.tile(jnp.arange(GRID_W, dtype=F32), rows)
    axis_dim = HEAD_DIM // 2
    inv_freq = ROPE_THETA ** (-jnp.arange(0, axis_dim, 2, dtype=F32) / axis_dim)
    ang_r = row[:, None] * inv_freq[None, :]
    ang_c = col[:, None] * inv_freq[None, :]
    cr, sr, cc, sc = jnp.cos(ang_r), jnp.sin(ang_r), jnp.cos(ang_c), jnp.sin(ang_c)
    cos = jnp.concatenate([cr, cr, cc, cc], axis=1)
    sin = jnp.concatenate([-sr, sr, -sc, sc], axis=1)
    return jnp.tile(cos, (1, N_HEADS)), jnp.tile(sin, (1, N_HEADS))


def _pair_operands(kd, vd):
    lane = lax.broadcasted_iota(jnp.int32, kd.shape, 1)
    lo = lane < HALF
    zero = jnp.zeros_like(kd)
    kmats = (jnp.where(lo, kd, zero), jnp.where(lo, zero, kd))
    vstack = jnp.concatenate([jnp.where(lo, vd, zero), jnp.where(lo, zero, vd)], axis=0)
    return kmats, vstack


def _flash_kernel(sink_ref, q_ref, k_ref, v_ref, o_ref, m_scr, l_scr, acc_scr, *, has_sink, nk):
    ki = pl.program_id(2)
    tq = q_ref.shape[0]

    @pl.when(ki == 0)
    def _():
        m_scr[...] = jnp.full(m_scr.shape, NEG_BIG, F32)
        l_scr[...] = jnp.zeros(l_scr.shape, F32)
        acc_scr[...] = jnp.zeros(acc_scr.shape, F32)

    lane_q = lax.broadcasted_iota(jnp.int32, (tq, LANES), 1)
    lo_q = lane_q < HALF
    pairs_per_kv = N_HEADS // N_KV // 2
    for j in range(N_KV):
        kmats, vstack = _pair_operands(k_ref[:, j * LANES:(j + 1) * LANES],
                                       v_ref[:, j * LANES:(j + 1) * LANES])
        for pp in range(pairs_per_kv):
            hp = j * pairs_per_kv + pp
            qp = q_ref[:, hp * LANES:(hp + 1) * LANES]
            ps, alphas = [], []
            for par in range(2):
                h = 2 * hp + par
                s = _dot_nt(qp, kmats[par])
                m_prev = m_scr[h]
                m_new = jnp.maximum(m_prev, jnp.max(s, axis=1, keepdims=True))
                alpha = jnp.exp(m_prev - m_new)
                p = jnp.exp(s - m_new[:, :1])
                l_scr[h] = alpha * l_scr[h] + jnp.sum(p, axis=1, keepdims=True)
                m_scr[h] = m_new
                ps.append(p.astype(BF16))
                alphas.append(alpha)
            pv = _dot(jnp.concatenate(ps, axis=1), vstack)
            sl = slice(hp * LANES, (hp + 1) * LANES)
            acc_scr[:, sl] = acc_scr[:, sl] * jnp.where(lo_q, alphas[0], alphas[1]) + pv

    @pl.when(ki == nk - 1)
    def _():
        for hp in range(N_HEADS // 2):
            ls = []
            for par in range(2):
                h = 2 * hp + par
                l = l_scr[h]
                if has_sink:
                    l = l + jnp.exp(sink_ref[h] - m_scr[h])
                ls.append(l)
            sl = slice(hp * LANES, (hp + 1) * LANES)
            o_ref[:, sl] = (acc_scr[:, sl] / jnp.where(lo_q, ls[0], ls[1])).astype(BF16)


def _flash_bounded_kernel(c_ref, q_ref, k_ref, v_ref, o_ref, lmin_ref, l_scr, acc_scr, *, nk):
    ki = pl.program_id(2)
    tq = q_ref.shape[0]
    tk = k_ref.shape[0]

    @pl.when(ki == 0)
    def _():
        l_scr[...] = jnp.zeros(l_scr.shape, F32)
        acc_scr[...] = jnp.zeros(acc_scr.shape, F32)

    c = c_ref[0]
    pairs_per_kv = N_HEADS // N_KV // 2
    for j in range(N_KV):
        kmats, vstack = _pair_operands(k_ref[:, j * LANES:(j + 1) * LANES],
                                       v_ref[:, j * LANES:(j + 1) * LANES])
        for pp in range(pairs_per_kv):
            hp = j * pairs_per_kv + pp
            qp = q_ref[:, hp * LANES:(hp + 1) * LANES]
            ps = []
            for par in range(2):
                h = 2 * hp + par
                p = jnp.exp(_dot_nt(qp, kmats[par]) - c)
                part = p[:, 0:LANES]
                for cb in range(1, tk // LANES):
                    part = part + p[:, cb * LANES:(cb + 1) * LANES]
                l_scr[h] = l_scr[h] + part
                ps.append(p.astype(BF16))
            sl = slice(hp * LANES, (hp + 1) * LANES)
            acc_scr[:, sl] = acc_scr[:, sl] + _dot(jnp.concatenate(ps, axis=1), vstack)

    @pl.when(ki == nk - 1)
    def _():
        lo_q = lax.broadcasted_iota(jnp.int32, (tq, LANES), 1) < HALF
        mins = []
        for hp in range(N_HEADS // 2):
            ls = [jnp.sum(l_scr[2 * hp + par], axis=1, keepdims=True) for par in range(2)]
            sl = slice(hp * LANES, (hp + 1) * LANES)
            o_ref[:, sl] = (acc_scr[:, sl] / jnp.where(lo_q, ls[0], ls[1])).astype(BF16)
            mins += [jnp.broadcast_to(jnp.min(l, axis=0, keepdims=True), (1, LANES)) for l in ls]
        lmin_ref[...] = jnp.concatenate(mins, axis=0)


FLASH_MIN_DENOM = 1e-30


def flash_attn_bounded(q, kd, vd, bound, nb, n, m):
    tq = min(512, n)
    tk = _pick_tile(m, (768, 512, 256))
    nq, nk = n // tq, m // tk
    hq = N_HEADS * HEAD_DIM
    o, lmin = pl.pallas_call(
        functools.partial(_flash_bounded_kernel, nk=nk),
        grid=(nb, nq, nk),
        in_specs=[pl.BlockSpec(memory_space=pltpu.SMEM),
                  pl.BlockSpec((tq, hq), lambda b, i, k: (b * nq + i, 0)),
                  pl.BlockSpec((tk, 2 * LANES), lambda b, i, k: (b * nk + k, 0)),
                  pl.BlockSpec((tk, 2 * LANES), lambda b, i, k: (b * nk + k, 0))],
        out_specs=[pl.BlockSpec((tq, hq), lambda b, i, k: (b * nq + i, 0)),
                   pl.BlockSpec((N_HEADS, LANES), lambda b, i, k: (b * nq + i, 0))],
        out_shape=[jax.ShapeDtypeStruct((nb * n, hq), BF16),
                   jax.ShapeDtypeStruct((nb * nq * N_HEADS, LANES), F32)],
        scratch_shapes=[pltpu.VMEM((N_HEADS, tq, LANES), F32),
                        pltpu.VMEM((tq, hq), F32)],
        compiler_params=_params(("parallel", "parallel", "arbitrary")),
        name="flash_attn_bounded",
    )(bound, q, kd, vd)
    ok = jnp.min(lmin) > FLASH_MIN_DENOM
    return lax.cond(ok, lambda: o,
                    lambda: flash_attn(q, kd, vd, jnp.zeros((N_HEADS,), F32), nb, n, m, has_sink=False))


def _pick_tile(m, cands):
    for c in cands:
        if m % c == 0:
            return c
    raise ValueError(f"no tile for {m}")


def flash_attn(q, kd, vd, sink, nb, n, m, has_sink):
    tq = min(512, n)
    tk = _pick_tile(m, (768, 512, 256))
    nq, nk = n // tq, m // tk
    hq = N_HEADS * HEAD_DIM
    return pl.pallas_call(
        functools.partial(_flash_kernel, has_sink=has_sink, nk=nk),
        grid=(nb, nq, nk),
        in_specs=[pl.BlockSpec(memory_space=pltpu.SMEM),
                  pl.BlockSpec((tq, hq), lambda b, i, k: (b * nq + i, 0)),
                  pl.BlockSpec((tk, 2 * LANES), lambda b, i, k: (b * nk + k, 0)),
                  pl.BlockSpec((tk, 2 * LANES), lambda b, i, k: (b * nk + k, 0))],
        out_specs=pl.BlockSpec((tq, hq), lambda b, i, k: (b * nq + i, 0)),
        out_shape=jax.ShapeDtypeStruct((nb * n, hq), BF16),
        scratch_shapes=[pltpu.VMEM((N_HEADS, tq, LANES), F32),
                        pltpu.VMEM((N_HEADS, tq, LANES), F32),
                        pltpu.VMEM((tq, hq), F32)],
        compiler_params=_params(("parallel", "parallel", "arbitrary")),
        name="flash_attn",
    )(sink, q, kd, vd)


def _window_kernel(sink_ref, q_ref, kp_ref, km_ref, kn_ref, vp_ref, vm_ref, vn_ref, kc_ref, vc_ref,
                   o_ref, *, n, tq):
    i = pl.program_id(1)
    span = tq + 2 * WINDOW
    q0 = i * tq
    r = lax.broadcasted_iota(jnp.int32, (tq, span), 0)
    c = lax.broadcasted_iota(jnp.int32, (tq, span), 1)
    kpos = c + (q0 - WINDOW)
    ok = (c >= r) & (c <= r + 2 * WINDOW) & (kpos >= 0) & (kpos < n)
    lane_q = lax.broadcasted_iota(jnp.int32, (tq, LANES), 1)
    lo_q = lane_q < HALF
    kloc = jnp.concatenate([kp_ref[...], km_ref[...], kn_ref[...]], axis=0)
    vloc = jnp.concatenate([vp_ref[...], vm_ref[...], vn_ref[...]], axis=0)
    pairs_per_kv = N_HEADS // N_KV // 2
    for j in range(N_KV):
        js = slice(j * LANES, (j + 1) * LANES)
        kl, vl_stack = _pair_operands(kloc[:, js], vloc[:, js])
        kc, vc_stack = _pair_operands(kc_ref[:, js], vc_ref[:, js])
        for pp in range(pairs_per_kv):
            hp = j * pairs_per_kv + pp
            qp = q_ref[:, hp * LANES:(hp + 1) * LANES]
            pl_, pc_, ls = [], [], []
            for par in range(2):
                h = 2 * hp + par
                s_loc = jnp.where(ok, _dot_nt(qp, kl[par]), NEG_BIG)
                s_ctx = _dot_nt(qp, kc[par])
                snk = sink_ref[h]
                m = jnp.maximum(jnp.maximum(jnp.max(s_loc, axis=1, keepdims=True),
                                            jnp.max(s_ctx, axis=1, keepdims=True)), snk)
                p_loc = jnp.exp(s_loc - m)
                p_ctx = jnp.exp(s_ctx - m)
                l = (jnp.sum(p_loc, axis=1, keepdims=True) + jnp.sum(p_ctx, axis=1, keepdims=True)
                     + jnp.exp(snk - m))
                inv = 1.0 / l
                pl_.append((p_loc * inv).astype(BF16))
                pc_.append((p_ctx * inv).astype(BF16))
            o = _dot(jnp.concatenate(pl_, axis=1), vl_stack) + _dot(jnp.concatenate(pc_, axis=1), vc_stack)
            o_ref[:, hp * LANES:(hp + 1) * LANES] = o.astype(BF16)


def window_attn(q, kd, vd, kdc, vdc, sink, nb, n, mc):
    tq = 2 * WINDOW
    nq = n // tq
    wb = n // WINDOW
    hq = N_HEADS * HEAD_DIM
    prev = pl.BlockSpec((WINDOW, 2 * LANES), lambda b, i: (b * wb + jnp.maximum(2 * i - 1, 0), 0))
    main = pl.BlockSpec((tq, 2 * LANES), lambda b, i: (b * nq + i, 0))
    nxt = pl.BlockSpec((WINDOW, 2 * LANES), lambda b, i: (b * wb + jnp.minimum(2 * i + 2, wb - 1), 0))
    ctx = pl.BlockSpec((mc, 2 * LANES), lambda b, i: (b, 0))
    return pl.pallas_call(
        functools.partial(_window_kernel, n=n, tq=tq),
        grid=(nb, nq),
        in_specs=[pl.BlockSpec(memory_space=pltpu.SMEM),
                  pl.BlockSpec((tq, hq), lambda b, i: (b * nq + i, 0)),
                  prev, main, nxt, prev, main, nxt, ctx, ctx],
        out_specs=pl.BlockSpec((tq, hq), lambda b, i: (b * nq + i, 0)),
        out_shape=jax.ShapeDtypeStruct((nb * n, hq), BF16),
        compiler_params=_params(("parallel", "parallel")),
        name="window_attn",
    )(sink, q, kd, kd, kd, vd, vd, vd, kdc, vdc)


HALO = 16


def _conv_kernel(xm_ref, xp_ref, xn_ref, bm_ref, bp_ref, bn_ref, wx_ref, bx_ref, wb_ref, bb_ref,
                 ox_ref, ob_ref, *, nt):
    i = pl.program_id(1)
    has_prev = jnp.where(i > 0, 1.0, 0.0)
    has_next = jnp.where(i < nt - 1, 1.0, 0.0)

    def conv(m_ref, p_ref, n_ref, w_ref, b_ref, o_ref):
        x = m_ref[...].astype(F32)
        tl = x.shape[0]
        row = lax.broadcasted_iota(jnp.int32, x.shape, 0)
        before = p_ref[...].astype(F32)[HALO - 1:HALO, :] * has_prev
        after = n_ref[...].astype(F32)[0:1, :] * has_next
        xm1 = jnp.where(row == 0, before, pltpu.roll(x, 1, 0))
        xp1 = jnp.where(row == tl - 1, after, pltpu.roll(x, tl - 1, 0))
        w = w_ref[...]
        y = xm1 * w[0:1, :] + x * w[1:2, :] + xp1 * w[2:3, :] + b_ref[...]
        o_ref[...] = _silu(y).astype(BF16)

    conv(xm_ref, xp_ref, xn_ref, wx_ref, bx_ref, ox_ref)
    conv(bm_ref, bp_ref, bn_ref, wb_ref, bb_ref, ob_ref)


def ssm_conv(p, conv_w, conv_b, nb, n):
    tl = min(512, n)
    nt = n // tl
    hb = n // HALO
    hpt = tl // HALO
    cx, cb = SSM_INNER, 2 * SSM_GROUPS * SSM_STATE

    def specs(width, col):
        cblk = col // width
        return (pl.BlockSpec((tl, width), lambda b, i: (b * nt + i, cblk)),
                pl.BlockSpec((HALO, width), lambda b, i: (b * hb + jnp.maximum(i * hpt - 1, 0), cblk)),
                pl.BlockSpec((HALO, width), lambda b, i: (b * hb + jnp.minimum((i + 1) * hpt, hb - 1), cblk)))

    const = lambda shp: pl.BlockSpec(shp, lambda b, i: (0, 0))
    xm, xp, xn = specs(cx, C_XS)
    bm, bp, bn = specs(cb, C_BC)
    return pl.pallas_call(
        functools.partial(_conv_kernel, nt=nt),
        grid=(nb, nt),
        in_specs=[xm, xp, xn, bm, bp, bn, const((3, cx)), const((1, cx)), const((3, cb)), const((1, cb))],
        out_specs=[pl.BlockSpec((tl, cx), lambda b, i: (b * nt + i, 0)),
                   pl.BlockSpec((tl, cb), lambda b, i: (b * nt + i, 0))],
        out_shape=[jax.ShapeDtypeStruct((nb * n, cx), BF16), jax.ShapeDtypeStruct((nb * n, cb), BF16)],
        compiler_params=_params(("parallel", "parallel")),
        name="ssm_conv",
    )(p, p, p, p, p, p, conv_w[:, :cx], conv_b[:cx].reshape(1, cx), conv_w[:, cx:], conv_b[cx:].reshape(1, cb))


def _ssd_kernel(xs_ref, bc_ref, dt_ref, bias_ref, aneg_ref, tri_ref, rep_ref, s0_ref,
                y_ref, sfin_ref, st_scr, *, nc):
    d = pl.program_id(1)
    k = pl.program_id(2)
    q = SSM_CHUNK
    gw = SSM_INNER // SSM_GROUPS

    @pl.when(k == 0)
    def _():
        st_scr[...] = s0_ref[...]

    tri = tri_ref[...]
    rep = rep_ref[...]
    dt = _softplus(dt_ref[...] + bias_ref[...])
    a = dt * aneg_ref[...]
    ac = _exact_left(tri, a)
    act = ac.T
    acx = _exact_right(ac, rep)
    dtx = _exact_right(dt, rep)
    totx = jnp.where(d == 0, acx[q - 1:q, :], acx[0:1, :])
    xd = xs_ref[...].astype(F32) * dtx
    xd_b = xd.astype(BF16)
    xe = (xd * jnp.exp(totx - acx)).astype(BF16)
    ein = jnp.exp(acx)
    keep = tri > 0
    lane = lax.broadcasted_iota(jnp.int32, (q, LANES), 1)
    lo = lane < HALF
    zero = jnp.zeros((q, LANES), BF16)
    hpg = SSM_HEADS // SSM_GROUPS
    for g in range(SSM_GROUPS):
        bg = bc_ref[:, g * SSM_STATE:(g + 1) * SSM_STATE]
        cg = bc_ref[:, (SSM_GROUPS + g) * SSM_STATE:(SSM_GROUPS + g + 1) * SSM_STATE]
        cb = _dot_nt(cg, bg)
        st = st_scr[g]
        yoff = _dot(cg, st.astype(BF16)) * ein[:, g * gw:(g + 1) * gw]
        for hp in range(hpg // 2):
            gs = []
            for par in range(2):
                h = g * hpg + 2 * hp + par
                seg = ac[:, h:h + 1] - act[h:h + 1, :]
                gs.append((cb * jnp.exp(jnp.where(keep, seg, NEG_BIG))).astype(BF16))
            c0 = g * gw + hp * LANES
            xp = xd_b[:, c0:c0 + LANES]
            xstack = jnp.concatenate([jnp.where(lo, xp, zero), jnp.where(lo, zero, xp)], axis=0)
            ydiag = _dot(jnp.concatenate(gs, axis=1), xstack)
            y_ref[:, c0:c0 + LANES] = (ydiag + yoff[:, hp * LANES:(hp + 1) * LANES]).astype(BF16)
        bgt = bg.astype(F32).T.astype(BF16)
        cs = _dot(bgt, xe[:, g * gw:(g + 1) * gw])
        st_scr[g] = st * jnp.exp(totx[:, g * gw:(g + 1) * gw]) + cs

    @pl.when(k == nc - 1)
    def _():
        sfin_ref[...] = st_scr[...]


def ssd_scan(u_xs, u_bc, dt_raw, bias2, aneg2, s0, nb, n):
    q = SSM_CHUNK
    nc = n // q
    gw = SSM_INNER // SSM_GROUPS
    idx = np.arange(q)
    tri = np.stack([idx[:, None] >= idx[None, :], idx[:, None] <= idx[None, :]]).astype(np.float32)
    rep = (np.arange(LANES)[:, None] == np.arange(SSM_INNER)[None, :] // SSM_P).astype(np.float32)

    def chunk(b, d, k):
        return b * nc + k + d * (nc - 1 - 2 * k)

    return pl.pallas_call(
        functools.partial(_ssd_kernel, nc=nc),
        grid=(nb, 2, nc),
        in_specs=[pl.BlockSpec((q, SSM_INNER), lambda b, d, k: (chunk(b, d, k), 0)),
                  pl.BlockSpec((q, 2 * SSM_GROUPS * SSM_STATE), lambda b, d, k: (chunk(b, d, k), 0)),
                  pl.BlockSpec((q, LANES), lambda b, d, k: (chunk(b, d, k), d)),
                  pl.BlockSpec((None, 1, LANES), lambda b, d, k: (d, 0, 0)),
                  pl.BlockSpec((None, 1, LANES), lambda b, d, k: (d, 0, 0)),
                  pl.BlockSpec((None, q, q), lambda b, d, k: (d, 0, 0)),
                  pl.BlockSpec((LANES, SSM_INNER), lambda b, d, k: (0, 0)),
                  pl.BlockSpec((None, None, SSM_GROUPS, SSM_STATE, gw), lambda b, d, k: (b, d, 0, 0, 0))],
        out_specs=[pl.BlockSpec((None, q, SSM_INNER), lambda b, d, k: (d, chunk(b, d, k), 0)),
                   pl.BlockSpec((None, None, SSM_GROUPS, SSM_STATE, gw), lambda b, d, k: (b, d, 0, 0, 0))],
        out_shape=[jax.ShapeDtypeStruct((2, nb * n, SSM_INNER), BF16),
                   jax.ShapeDtypeStruct((nb, 2, SSM_GROUPS, SSM_STATE, gw), F32)],
        scratch_shapes=[pltpu.VMEM((SSM_GROUPS, SSM_STATE, gw), F32)],
        compiler_params=_params(("parallel", "parallel", "arbitrary")),
        name="ssd_scan",
    )(u_xs, u_bc, dt_raw, bias2, aneg2, jnp.asarray(tri, BF16), jnp.asarray(rep, BF16), s0)


def _post_kernel(oa_ref, ow_ref, yf_ref, yb_ref, xs_ref, z_ref, gt_ref, x_ref,
                 dsk_ref, nw_ref, wa_ref, ww_ref, ws_ref, wo_ref, gpost_ref, gate_ref,
                 gpre_ref, sh_ref, sc_ref, wrh_ref, wrl_ref,
                 xo_ref, h_ref, lg_ref):
    y = yf_ref[...].astype(F32) + yb_ref[...].astype(F32) + dsk_ref[...] * xs_ref[...].astype(F32)
    u = y * _silu(z_ref[...].astype(F32))
    gw = SSM_INNER // SSM_GROUPS
    ys = jnp.concatenate([_rms(u[:, g * gw:(g + 1) * gw]) for g in range(SSM_GROUPS)], axis=1)
    ys = (ys * nw_ref[...]).astype(BF16)
    d = x_ref.shape[1]
    ga = jax.nn.sigmoid(gt_ref[:, 0:d].astype(F32))
    gw_ = jax.nn.sigmoid(gt_ref[:, d:2 * d].astype(F32))
    gs = jax.nn.sigmoid(gt_ref[:, 2 * d:3 * d].astype(F32))
    m = ga * _dot(oa_ref[...], wa_ref[...]) + gw_ * _dot(ow_ref[...], ww_ref[...]) + gs * _dot(ys, ws_ref[...])
    ml = _dot(m.astype(BF16), wo_ref[...])
    xn = x_ref[...] + gate_ref[...] * (_rms(ml) * gpost_ref[...])
    xo_ref[...] = xn
    h = (_rms(xn) * gpre_ref[...]) * (1.0 + sc_ref[...]) + sh_ref[...]
    h_ref[...] = h
    hb = h.astype(BF16)
    hl =(h - hb.astype(F32)).astype(BF16)
    lg_ref[...] = _dot_nt(wrh_ref[...], hb) + _dot_nt(wrh_ref[...], hl) + _dot_nt(wrl_ref[...], hb)


def post_mixer(oa, ow, y2, u_xs, p, x, dskip, norm_w, wa, ww, ws, wo, gpost, gate, gpre, shift, scale,
               wr_hi, wr_lo, rows_per_batch):
    t, d = x.shape
    tm = min(256, rows_per_batch)
    tpb = rows_per_batch // tm
    nt = t // tm
    hq = N_HEADS * HEAD_DIM
    row = lambda w, c=0: pl.BlockSpec((tm, w), lambda i: (i, c // w))
    const = lambda shp: pl.BlockSpec(shp, lambda i: (0,) * len(shp))
    mod = pl.BlockSpec((None, 1, d), lambda i: (i // tpb, 0, 0))
    return pl.pallas_call(
        _post_kernel,
        grid=(nt,),
        in_specs=[row(hq), row(hq),
                  pl.BlockSpec((None, tm, SSM_INNER), lambda i: (0, i, 0)),
                  pl.BlockSpec((None, tm, SSM_INNER), lambda i: (1, i, 0)),
                  row(SSM_INNER), row(SSM_INNER, C_Z), row(3 * d, C_GATES), row(d),
                  const((1, SSM_INNER)), const((1, SSM_INNER)),
                  const((hq, d)), const((hq, d)), const((SSM_INNER, d)), const((d, d)),
                  const((1, d)), mod, const((1, d)), mod, mod,
                  const((N_EXPERTS, d)), const((N_EXPERTS, d))],
        out_specs=[row(d), row(d), pl.BlockSpec((N_EXPERTS, tm), lambda i: (0, i))],
        out_shape=[jax.ShapeDtypeStruct((t, d), F32), jax.ShapeDtypeStruct((t, d), F32),
                   jax.ShapeDtypeStruct((N_EXPERTS, t), F32)],
        compiler_params=_params(("parallel",)),
        name="post_mixer",
    )(oa, ow, y2, y2, u_xs, p, p, x, dskip, norm_w, wa, ww, ws, wo, gpost, gate, gpre, shift, scale,
      wr_hi, wr_lo)


def _route_kernel(lg_ref, b_ref, ei_ref, w_ref):
    scores = jax.nn.sigmoid(lg_ref[...])
    sel = scores + b_ref[...]
    tt = sel.shape[1]
    per = N_EXPERTS // N_EXPERT_GROUPS
    r8 = lax.broadcasted_iota(jnp.int32, (per, tt), 0).astype(F32)
    ninf = -jnp.inf

    def argmax_rows(x, rows, nrows):
        m = jnp.max(x, axis=0, keepdims=True)
        idx = jnp.min(jnp.where(x == m, rows, float(nrows)), axis=0, keepdims=True)
        return m, idx

    gscores = []
    for g in range(N_EXPERT_GROUPS):
        blk = sel[g * per:(g + 1) * per, :]
        m1, i1 = argmax_rows(blk, r8, per)
        m2 = jnp.max(jnp.where(r8 == i1, ninf, blk), axis=0, keepdims=True)
        gscores.append(m1 + m2)
    cur = jnp.concatenate(gscores, axis=0)
    rg = lax.broadcasted_iota(jnp.int32, cur.shape, 0).astype(F32)
    chosen = jnp.zeros(cur.shape, F32)
    for _ in range(TOPK_GROUPS):
        _, gi = argmax_rows(cur, rg, N_EXPERT_GROUPS)
        hit = rg == gi
        chosen = jnp.where(hit, 1.0, chosen)
        cur = jnp.where(hit, ninf, cur)
    gmask = jnp.concatenate([jnp.broadcast_to(chosen[g:g + 1, :], (per, tt)) for g in range(N_EXPERT_GROUPS)],
                            axis=0)
    cur = jnp.where(gmask > 0, sel, ninf)
    re = lax.broadcasted_iota(jnp.int32, cur.shape, 0).astype(F32)
    idxs, ws = [], []
    for _ in range(TOP_K):
        _, ei = argmax_rows(cur, re, N_EXPERTS)
        hit = re == ei
        ws.append(jnp.sum(jnp.where(hit, scores, 0.0), axis=0, keepdims=True))
        idxs.append(ei)
        cur = jnp.where(hit, ninf, cur)
    w = jnp.concatenate(ws, axis=0)
    w_ref[...] = w / jnp.sum(w, axis=0, keepdims=True) * ROUTED_SCALE
    ei_ref[...] = jnp.concatenate(idxs, axis=0).astype(jnp.int32)


def route(logits_t, b_router):
    e, t = logits_t.shape
    tt = 512
    return pl.pallas_call(
        _route_kernel,
        grid=(t // tt,),
        in_specs=[pl.BlockSpec((e, tt), lambda i: (0, i)), pl.BlockSpec((e, 1), lambda i: (0, 0))],
        out_specs=[pl.BlockSpec((TOP_K, tt), lambda i: (0, i)), pl.BlockSpec((TOP_K, tt), lambda i: (0, i))],
        out_shape=[jax.ShapeDtypeStruct((TOP_K, t), jnp.int32), jax.ShapeDtypeStruct((TOP_K, t), F32)],
        compiler_params=_params(("parallel",)),
        name="route",
    )(logits_t, b_router.reshape(e, 1))


MOE_ROWS = 512
PLAN_TOKENS = 512


def _moe_geometry(t):
    nblk = -(-(t * TOP_K + N_EXPERTS * (MOE_ROWS - 1)) // MOE_ROWS)
    return nblk, nblk * MOE_ROWS


def _plan_kernel(ei_ref, ut_ref, tril_ref, dest_ref, tab_ref, be_ref, cnt_scr, run_scr):
    ph = pl.program_id(0)
    i = pl.program_id(1)
    ei = ei_ref[...]
    tt = ei.shape[1]
    re = lax.broadcasted_iota(jnp.int32, (N_EXPERTS, tt), 0)
    hits = [re == ei[k:k + 1, :] for k in range(TOP_K)]
    oh = jnp.zeros((N_EXPERTS, tt), F32)
    for k in range(TOP_K):
        oh = oh + jnp.where(hits[k], 1.0, 0.0)

    @pl.when((ph == 0) & (i == 0))
    def _():
        cnt_scr[...] = jnp.zeros(cnt_scr.shape, F32)

    @pl.when(ph == 0)
    def _():
        cnt_scr[...] = cnt_scr[...] + jnp.sum(oh, axis=1, keepdims=True)

    @pl.when((ph == 1) & (i == 0))
    def _():
        cnt = cnt_scr[...]
        padded = ((cnt.astype(jnp.int32) + (MOE_ROWS - 1)) & (-MOE_ROWS)).astype(F32)
        pad_end = _exact_left(tril_ref[...], padded)
        pad_start = pad_end - padded
        run_scr[...] = pad_start
        tab_ref[0] = pad_start
        tab_ref[1] = pad_end
        nbp = be_ref.shape[1]
        blk0 = (lax.broadcasted_iota(jnp.int32, (N_EXPERTS, nbp), 1) * MOE_ROWS).astype(F32)
        be = jnp.sum(jnp.where(pad_end[:, :1] <= blk0, 1.0, 0.0), axis=0, keepdims=True)
        be_ref[...] = jnp.broadcast_to(jnp.minimum(be, N_EXPERTS - 1.0), be_ref.shape).astype(jnp.int32)

    @pl.when(ph == 1)
    def _():
        cin = _dot(oh.astype(BF16), ut_ref[...])
        pos = run_scr[:, :1] + (cin - oh)
        rows = [jnp.sum(jnp.where(hits[k], pos, 0.0), axis=0, keepdims=True) for k in range(TOP_K)]
        dest_ref[...] = jnp.concatenate(rows, axis=0).astype(jnp.int32)
        run_scr[...] = run_scr[...] + cin[:, tt - 1:tt]


def moe_plan(eidx_t):
    k, t = eidx_t.shape
    tt = PLAN_TOKENS
    nt = t // tt
    nblk, _ = _moe_geometry(t)
    nbp = -(-nblk // LANES) * LANES
    ut = jnp.asarray(np.arange(tt)[:, None] <= np.arange(tt)[None, :], BF16)
    tril = jnp.asarray(np.arange(N_EXPERTS)[:, None] >= np.arange(N_EXPERTS)[None, :], BF16)
    return pl.pallas_call(
        _plan_kernel,
        grid=(2, nt),
        in_specs=[pl.BlockSpec((k, tt), lambda p, i: (0, i)),
                  pl.BlockSpec((tt, tt), lambda p, i: (0, 0)),
                  pl.BlockSpec((N_EXPERTS, N_EXPERTS), lambda p, i: (0, 0))],
        out_specs=[pl.BlockSpec((k, tt), lambda p, i: (0, i * p)),
                   pl.BlockSpec((2, N_EXPERTS, LANES), lambda p, i: (0, 0, 0)),
                   pl.BlockSpec((8, nbp), lambda p, i: (0, 0))],
        out_shape=[jax.ShapeDtypeStruct((k, t), jnp.int32),
                   jax.ShapeDtypeStruct((2, N_EXPERTS, LANES), F32),
                   jax.ShapeDtypeStruct((8, nbp), jnp.int32)],
        scratch_shapes=[pltpu.VMEM((N_EXPERTS, LANES), F32), pltpu.VMEM((N_EXPERTS, LANES), F32)],
        compiler_params=_params(("arbitrary", "arbitrary")),
        name="moe_plan",
    )(eidx_t, ut, tril)


def _dispatch_kernel(dest_ref, pstart_ref, pend_ref, h_ref, xs_hbm, zero_scr, sem):
    i = pl.program_id(0)
    tt = h_ref.shape[0]

    @pl.when(i == 0)
    def _():
        zero_scr[...] = jnp.zeros(zero_scr.shape, F32)
        nblk = xs_hbm.shape[0] // MOE_ROWS
        n_used = pend_ref[N_EXPERTS - 1] // MOE_ROWS

        def zero_block(row0):
            return pltpu.make_async_copy(
                zero_scr, xs_hbm.at[pl.ds(pl.multiple_of(row0, MOE_ROWS), MOE_ROWS), :], sem)

        def seg_start(e, c):
            @pl.when(pend_ref[e] > pstart_ref[e])
            def _():
                zero_block(pend_ref[e] - MOE_ROWS).start()
            return c

        def seg_wait(e, c):
            @pl.when(pend_ref[e] > pstart_ref[e])
            def _():
                zero_block(pend_ref[e] - MOE_ROWS).wait()
            return c

        def tail_start(b, c):
            zero_block(b * MOE_ROWS).start()
            return c

        def tail_wait(b, c):
            zero_block(b * MOE_ROWS).wait()
            return c

        lax.fori_loop(0, N_EXPERTS, seg_start, 0)
        lax.fori_loop(n_used, nblk, tail_start, 0)
        lax.fori_loop(0, N_EXPERTS, seg_wait, 0)
        lax.fori_loop(n_used, nblk, tail_wait, 0)

    def issue(t, c):
        for k in range(TOP_K):
            pltpu.make_async_copy(h_ref.at[pl.ds(t, 1), :], xs_hbm.at[pl.ds(dest_ref[k, t], 1), :], sem).start()
        return c

    lax.fori_loop(0, tt, issue, 0)
    for k in range(TOP_K):
        pltpu.make_async_copy(h_ref, xs_hbm.at[pl.ds(0, tt), :], sem).wait()


def moe_dispatch(dest_t, pad_start, pad_end, h):
    t, d = h.shape
    tt = PLAN_TOKENS
    _, cap = _moe_geometry(t)
    smem = pl.BlockSpec(memory_space=pltpu.SMEM)
    return pl.pallas_call(
        _dispatch_kernel,
        grid=(t // tt,),
        in_specs=[pl.BlockSpec((TOP_K, tt), lambda i: (0, i), memory_space=pltpu.SMEM), smem, smem,
                  pl.BlockSpec((tt, d), lambda i: (i, 0))],
        out_specs=pl.BlockSpec(memory_space=pl.ANY),
        out_shape=jax.ShapeDtypeStruct((cap, d), F32),
        scratch_shapes=[pltpu.VMEM((MOE_ROWS, d), F32), pltpu.SemaphoreType.DMA(())],
        compiler_params=_params(("arbitrary",)),
        name="moe_dispatch",
    )(dest_t, pad_start, pad_end, h)


def _expert_kernel(be_ref, nu_ref, x_ref, wg_ref, wu_ref, wd_ref, o_ref):
    used = pl.program_id(0) < nu_ref[0]

    @pl.when(used)
    def _():
        x = x_ref[...].astype(BF16)
        hb = (_silu(_dot(x, wg_ref[...])) * _dot(x, wu_ref[...])).astype(BF16)
        o_ref[...] = _dot(hb, wd_ref[...])

    @pl.when(jnp.logical_not(used))
    def _():
        o_ref[...] = jnp.zeros(o_ref.shape, F32)


def expert_ffn(blk_e, n_used, xs, we_gate, we_up, we_down):
    cap, d = xs.shape
    de = we_gate.shape[2]
    nblk = cap // MOE_ROWS
    blk = lambda i, nu: jnp.minimum(i, nu[0] - 1)
    grid_spec = pltpu.PrefetchScalarGridSpec(
        num_scalar_prefetch=2,
        grid=(nblk,),
        in_specs=[pl.BlockSpec((MOE_ROWS, d), lambda i, be, nu: (blk(i, nu), 0)),
                  pl.BlockSpec((None, d, de), lambda i, be, nu: (be[blk(i, nu)], 0, 0)),
                  pl.BlockSpec((None, d, de), lambda i, be, nu: (be[blk(i, nu)], 0, 0)),
                  pl.BlockSpec((None, de, d), lambda i, be, nu: (be[blk(i, nu)], 0, 0))],
        out_specs=pl.BlockSpec((MOE_ROWS, d), lambda i, be, nu: (i, 0)),
    )
    return pl.pallas_call(
        _expert_kernel,
        grid_spec=grid_spec,
        out_shape=jax.ShapeDtypeStruct((cap, d), F32),
        compiler_params=_params(("arbitrary",)),
        name="expert_ffn",
    )(blk_e, n_used, xs, we_gate, we_up, we_down)


def _ffn_tail_kernel(dest_ref, w_ref, h_ref, x_ref, wg_ref, wu_ref, wd_ref, gpost_ref, gate_ref, ys_hbm,
                     xo_ref, buf, sem):
    tt = h_ref.shape[0]

    def issue(t, c):
        for k in range(TOP_K):
            pltpu.make_async_copy(ys_hbm.at[pl.ds(dest_ref[k, t], 1), :], buf.at[k, pl.ds(t, 1), :], sem).start()
        return c

    lax.fori_loop(0, tt, issue, 0)
    h = h_ref[...].astype(BF16)
    hs = (_silu(_dot(h, wg_ref[...])) * _dot(h, wu_ref[...])).astype(BF16)
    f = _dot(hs, wd_ref[...])
    w = w_ref[...]
    for k in range(TOP_K):
        pltpu.make_async_copy(ys_hbm.at[pl.ds(0, tt), :], buf.at[k], sem).wait()
    for k in range(TOP_K):
        f = f + buf[k] * w[:, k:k + 1]
    xo_ref[...] = x_ref[...] + gate_ref[...] * (_rms(f) * gpost_ref[...])


def ffn_tail(dest_t, wts, h, x, ys, wsg, wsu, wsd, gpost, gate, rows_per_batch):
    t, d = x.shape
    ds = wsg.shape[1]
    tm = min(256, rows_per_batch)
    tpb = rows_per_batch // tm
    row = pl.BlockSpec((tm, d), lambda i: (i, 0))
    const = lambda shp: pl.BlockSpec(shp, lambda i: (0,) * len(shp))
    return pl.pallas_call(
        _ffn_tail_kernel,
        grid=(t // tm,),
        in_specs=[pl.BlockSpec((TOP_K, tm), lambda i: (0, i), memory_space=pltpu.SMEM),
                  pl.BlockSpec((tm, TOP_K), lambda i: (i, 0)),
                  row, row, const((d, ds)), const((d, ds)), const((ds, d)), const((1, d)),
                  pl.BlockSpec((None, 1, d), lambda i: (i // tpb, 0, 0)),
                  pl.BlockSpec(memory_space=pl.ANY)],
        out_specs=row,
        out_shape=jax.ShapeDtypeStruct((t, d), F32),
        scratch_shapes=[pltpu.VMEM((TOP_K, tm, d), F32), pltpu.SemaphoreType.DMA(())],
        compiler_params=_params(("arbitrary",)),
        name="ffn_tail",
    )(dest_t, wts, h, x, wsg, wsu, wsd, gpost, gate, ys)


def _moe_routed(h, logits_t, b_router, we_gate, we_up, we_down):
    eidx_t, wts_t = route(logits_t, b_router)
    dest_t, tabs, blk_e = moe_plan(eidx_t)
    pad_start = tabs[0, :, 0].astype(jnp.int32)
    pad_end = tabs[1, :, 0].astype(jnp.int32)
    n_used = (pad_end[N_EXPERTS - 1:] // MOE_ROWS).astype(jnp.int32)
    xs = moe_dispatch(dest_t, pad_start, pad_end, h)
    ys = expert_ffn(blk_e[0], n_used, xs, we_gate, we_up, we_down)
    return dest_t, wts_t.T, ys


def _reorder_w_in(w):
    qa, ka, va, qw, kw, vw, z, xs, bm, cm, dt, gates = jnp.split(
        w, [512, 640, 768, 1280, 1408, 1536, 2560, 3584, 3840, 4096, 4128], axis=1)
    w_main = jnp.concatenate([qa, qw, z, xs, gates, ka, va, kw, vw, bm, cm], axis=1).astype(BF16)
    pad = jnp.zeros((w.shape[0], LANES - SSM_HEADS), w.dtype)
    w_dt = jnp.concatenate([dt[:, :SSM_HEADS], pad, dt[:, SSM_HEADS:], pad], axis=1).astype(BF16)
    return w_main, w_dt


def _pad_lanes(v):
    return jnp.pad(v, ((0, 0), (0, LANES - v.shape[1])))


def kernel(x, c, ctx, c_ctx, w_ada, b_ada, g_mix_pre, g_mix_post, g_ffn_pre, g_ffn_post, w_in, g_q_a, g_k_a, sink_w, ssm_conv_w, ssm_conv_b, ssm_dt_bias, ssm_a_log, ssm_d, ssm_norm, w_br_a, w_br_w, w_br_s, w_out, w_router, b_router, we_gate, we_up, we_down, ws_gate, ws_up, ws_down):
    nb, n, d = x.shape
    mc = ctx.shape[1]
    depth = w_in.shape[0]
    t_lat, t_ctx = nb * n, nb * mc
    cos, sin = rope_tables(n)
    xl = x.reshape(t_lat, d)
    xc = ctx.reshape(t_ctx, d)
    c8 = jnp.concatenate([c, c_ctx[None, :], jnp.zeros((8 - nb - 1, d), F32)], axis=0)
    zeros_sink = jnp.zeros((N_HEADS,), F32)
    s_zero = jnp.zeros((nb, 2, SSM_GROUPS, SSM_STATE, SSM_INNER // SSM_GROUPS), F32)
    dummy_tab = jnp.zeros((mc, N_HEADS * HEAD_DIM), F32)

    for i in range(depth):
        last = i == depth - 1
        mod = ada_mod(c8, w_ada[i], b_ada[i])
        mod_l = [mod[:nb, k * d:(k + 1) * d].reshape(nb, 1, d) for k in range(6)]
        mod_c = [mod[nb:nb + 1, k * d:(k + 1) * d].reshape(1, 1, d) for k in range(6)]
        w_main, w_dt = _reorder_w_in(w_in[i])
        gq = jnp.tile(g_q_a[i], N_HEADS)[None, :]
        gk = jnp.tile(g_k_a[i], N_KV)[None, :]
        bias2 = _pad_lanes(ssm_dt_bias[i].reshape(2, SSM_HEADS)).reshape(2, 1, LANES)
        aneg2 = _pad_lanes(-jnp.exp(ssm_a_log[i].astype(F32))).reshape(2, 1, LANES)
        dskip = jnp.repeat(ssm_d[i], SSM_P)[None, :]
        norm_w = ssm_norm[i][None, :]
        wa, ww, ws, wo = (w_br_a[i].astype(BF16), w_br_w[i].astype(BF16), w_br_s[i].astype(BF16),
                          w_out[i].astype(BF16))
        wr_t = w_router[i].T
        wr_hi = wr_t.astype(BF16)
        wr_lo = (wr_t - wr_hi.astype(F32)).astype(BF16)
        sink = sink_w[i].astype(F32)

        p_c, dt_c = in_proj(xc, g_mix_pre[i], mod_c[0], mod_c[1], w_main, w_dt, t_ctx)
        qa_c, qw_c, kda_c, vda_c, kdw_c, vdw_c = attn_prep(p_c, dummy_tab, dummy_tab, gq, gk, nb, mc, rope=False)
        uxs_c, ubc_c = ssm_conv(p_c, ssm_conv_w[i], ssm_conv_b[i], nb, mc)
        y_c, s_fin = ssd_scan(uxs_c, ubc_c, dt_c, bias2, aneg2, s_zero, nb, mc)

        p_l, dt_l = in_proj(xl, g_mix_pre[i], mod_l[0], mod_l[1], w_main, w_dt, n)
        qa, qw, kda, vda, kdw, vdw = attn_prep(p_l, cos, sin, gq, gk, nb, n, rope=True)
        m_all = n + mc
        kd_all = jnp.concatenate([kda.reshape(nb, n, -1), kda_c.reshape(nb, mc, -1)], axis=1).reshape(nb * m_all, -1)
        vd_all = jnp.concatenate([vda.reshape(nb, n, -1), vda_c.reshape(nb, mc, -1)], axis=1).reshape(nb * m_all, -1)
        score_bound = (math.sqrt(HEAD_DIM) * jnp.max(jnp.abs(g_q_a[i])) * jnp.max(jnp.abs(g_k_a[i]))).reshape(1)
        oa = flash_attn_bounded(qa, kd_all, vd_all, score_bound.astype(F32), nb, n, m_all)
        ow = window_attn(qw, kdw, vdw, kdw_c, vdw_c, sink, nb, n, mc)
        uxs, ubc = ssm_conv(p_l, ssm_conv_w[i], ssm_conv_b[i], nb, n)
        y_l, _ = ssd_scan(uxs, ubc, dt_l, bias2, aneg2, s_fin, nb, n)
        xl, h_l, lg_l = post_mixer(oa, ow, y_l, uxs, p_l, xl, dskip, norm_w, wa, ww, ws, wo,
                                   g_mix_post[i][None, :], mod_l[2], g_ffn_pre[i][None, :], mod_l[3], mod_l[4],
                                   wr_hi, wr_lo, n)
        wsg, wsu, wsd = ws_gate[i].astype(BF16), ws_up[i].astype(BF16), ws_down[i].astype(BF16)
        weg, weu, wed = we_gate[i].astype(BF16), we_up[i].astype(BF16), we_down[i].astype(BF16)
        if last:
            dest_t, wts, ys = _moe_routed(h_l, lg_l, b_router[i], weg, weu, wed)
            xl = ffn_tail(dest_t, wts, h_l, xl, ys, wsg, wsu, wsd, g_ffn_post[i][None, :], mod_l[5], n)
        else:
            oa_c = flash_attn(qa_c, kda_c, vda_c, zeros_sink, nb, mc, mc, has_sink=False)
            ow_c = flash_attn(qw_c, kdw_c, vdw_c, sink, nb, mc, mc, has_sink=True)
            xc, h_c, lg_c = post_mixer(oa_c, ow_c, y_c, uxs_c, p_c, xc, dskip, norm_w, wa, ww, ws, wo,
                                       g_mix_post[i][None, :], mod_c[2], g_ffn_pre[i][None, :], mod_c[3], mod_c[4],
                                       wr_hi, wr_lo, t_ctx)
            h_all = jnp.concatenate([h_l, h_c], axis=0)
            lg_all = jnp.concatenate([lg_l, lg_c], axis=1)
            dest_t, wts, ys = _moe_routed(h_all, lg_all, b_router[i], weg, weu, wed)
            xl = ffn_tail(dest_t[:, :t_lat], wts[:t_lat], h_l, xl, ys, wsg, wsu, wsd,
                          g_ffn_post[i][None, :], mod_l[5], n)
            xc = ffn_tail(dest_t[:, t_lat:], wts[t_lat:], h_c, xc, ys, wsg, wsu, wsd,
                          g_ffn_post[i][None, :], mod_c[5], t_ctx)
    return xl.reshape(nb, n, d)
```

```python
import functools
import math

import jax
import jax.numpy as jnp
import numpy as np
from jax import lax
from jax.experimental import pallas as pl
from jax.experimental.pallas import tpu as pltpu

F32 = jnp.float32
BF16 = jnp.bfloat16

HEAD_DIM = 64
N_HEADS = 8
N_KV = 2
GRID_W = 64
ROPE_THETA = 10000.0
WINDOW = 128
SSM_HEADS = 16
SSM_P = 64
SSM_INNER = SSM_HEADS * SSM_P
SSM_GROUPS = 2
SSM_STATE = 128
SSM_CHUNK = 128
N_EXPERTS = 64
TOP_K = 8
N_EXPERT_GROUPS = 8
TOPK_GROUPS = 4
ROUTED_SCALE = 2.5
EPS = 1e-6

LANES = 128
HALF = LANES // 2
VMEM_LIMIT = 56 * 1024 * 1024
NEG_BIG = -1e30

C_QA, C_QW, C_Z, C_XS, C_GATES = 0, 512, 1024, 2048, 3072
C_KA, C_VA, C_KW, C_VW, C_BC = 6144, 6272, 6400, 6528, 6656
P_WIDTH = 7168


def _params(sem, vmem=VMEM_LIMIT):
    return pltpu.CompilerParams(dimension_semantics=sem, vmem_limit_bytes=vmem)


def _silu(x):
    return x * jax.nn.sigmoid(x)


SLAB = 8


def _store_rows_as_slabs(ref, x):
    r = x.shape[0]
    for c in range(SLAB):
        ref[pl.ds(c, r, stride=SLAB), :] = x[:, c * LANES:(c + 1) * LANES]


def _load_slabs_as_rows(ref):
    r = ref.shape[0] // SLAB
    return jnp.concatenate([ref[pl.ds(c, r, stride=SLAB), :] for c in range(SLAB)], axis=1)


def _softplus(x):
    return jnp.maximum(x, 0.0) + jnp.log(1.0 + jnp.exp(-jnp.abs(x)))


def _rms(x, eps=EPS):
    return x * lax.rsqrt(jnp.mean(x * x, axis=-1, keepdims=True) + eps)


def _split3(a):
    a1 = a.astype(BF16)
    r1 = a - a1.astype(F32)
    a2 = r1.astype(BF16)
    a3 = (r1 - a2.astype(F32)).astype(BF16)
    return a1, a2, a3


def _dot(a, b):
    return jnp.dot(a, b, preferred_element_type=F32)


def _dot_nt(a, b):
    return lax.dot_general(a, b, (((1,), (1,)), ((), ())), preferred_element_type=F32)


def _exact_right(a, r01):
    a1, a2, a3 = _split3(a)
    return _dot(a1, r01) + _dot(a2, r01) + _dot(a3, r01)


def _exact_left(m01, a):
    a1, a2, a3 = _split3(a)
    return _dot(m01, a1) + _dot(m01, a2) + _dot(m01, a3)


def _ada_kernel(c_ref, w_ref, b_ref, o_ref):
    h = _silu(c_ref[...])
    o_ref[...] = jnp.dot(h, w_ref[...], preferred_element_type=F32,
                         precision=lax.Precision.HIGHEST) + b_ref[...]


def ada_mod(c8, w, b):
    d, n = w.shape
    tn = 1536
    return pl.pallas_call(
        _ada_kernel,
        grid=(n // tn,),
        in_specs=[pl.BlockSpec((8, d), lambda j: (0, 0)),
                  pl.BlockSpec((d, tn), lambda j: (0, j)),
                  pl.BlockSpec((1, tn), lambda j: (0, j))],
        out_specs=pl.BlockSpec((8, tn), lambda j: (0, j)),
        out_shape=jax.ShapeDtypeStruct((8, n), F32),
        compiler_params=_params(("parallel",)),
        name="ada_mod",
    )(c8, w, b.reshape(1, n))


def _inproj_kernel(x_ref, g_ref, sh_ref, sc_ref, w_ref, wdt_ref, o_ref, odt_ref, h_scr):
    @pl.when(pl.program_id(1) == 0)
    def _():
        h = _rms(x_ref[...]) * g_ref[...]
        h = h * (1.0 + sc_ref[...]) + sh_ref[...]
        hb = h.astype(BF16)
        h_scr[...] = hb
        odt_ref[...] = _dot(hb, wdt_ref[...])

    o_ref[...] = _dot(h_scr[...], w_ref[...]).astype(BF16)


def in_proj(x, g, shift, scale, w_main, w_dt, rows_per_batch):
    t, d = x.shape
    n = w_main.shape[1]
    tm = min(512, rows_per_batch)
    tn = n // 2
    tpb = rows_per_batch // tm
    mod_spec = pl.BlockSpec((None, 1, d), lambda i, j: (i // tpb, 0, 0))
    return pl.pallas_call(
        _inproj_kernel,
        grid=(t // tm, n // tn),
        in_specs=[pl.BlockSpec((tm, d), lambda i, j: (i, 0)),
                  pl.BlockSpec((1, d), lambda i, j: (0, 0)),
                  mod_spec, mod_spec,
                  pl.BlockSpec((d, tn), lambda i, j: (0, j)),
                  pl.BlockSpec((d, 2 * LANES), lambda i, j: (0, 0))],
        out_specs=[pl.BlockSpec((tm, tn), lambda i, j: (i, j)),
                   pl.BlockSpec((tm, 2 * LANES), lambda i, j: (i, 0))],
        out_shape=[jax.ShapeDtypeStruct((t, n), BF16),
                   jax.ShapeDtypeStruct((t, 2 * LANES), F32)],
        scratch_shapes=[pltpu.VMEM((tm, d), BF16)],
        compiler_params=_params(("parallel", "arbitrary")),
        name="in_proj",
    )(x, g.reshape(1, d), shift, scale, w_main, w_dt)


def _rope(x, cos, sin):
    w = x.shape[-1]
    lane = lax.broadcasted_iota(jnp.int32, x.shape, 1)
    first = (lane % 32) < 16
    swapped = jnp.where(first, pltpu.roll(x, w - 16, 1), pltpu.roll(x, 16, 1))
    return x * cos + swapped * sin


def _dup_halves(x):
    lane = lax.broadcasted_iota(jnp.int32, x.shape, 1)
    lo = lane < HALF
    r = pltpu.roll(x, HALF, 1)
    return jnp.concatenate([jnp.where(lo, x, r), jnp.where(lo, r, x)], axis=1)


def _prep_kernel(qa_ref, qw_ref, ka_ref, va_ref, kw_ref, vw_ref, cos_ref, sin_ref,
                 gq_ref, gk_ref, bdq_ref, bdk_ref,
                 qa_o, qw_o, kda_o, vda_o, kdw_o, vdw_o, *, rope):
    scale = HEAD_DIM ** -0.5
    inv_hd = 1.0 / HEAD_DIM

    def headnorm(x, g, bd):
        ss = _dot((x * x).astype(BF16), bd) * inv_hd
        return x * lax.rsqrt(ss + EPS) * g

    qa = headnorm(qa_ref[...].astype(F32), gq_ref[...], bdq_ref[...])
    ka = headnorm(ka_ref[...].astype(F32), gk_ref[...], bdk_ref[...])
    qw = qw_ref[...].astype(F32)
    kw = kw_ref[...].astype(F32)
    if rope:
        cos = cos_ref[...]
        sin = sin_ref[...]
        qa = _rope(qa, cos, sin)
        qw = _rope(qw, cos, sin)
        ka = _rope(ka, cos[:, :LANES], sin[:, :LANES])
        kw = _rope(kw, cos[:, :LANES], sin[:, :LANES])
    qa_o[...] = (qa * scale).astype(BF16)
    qw_o[...] = (qw * scale).astype(BF16)
    kda_o[...] = _dup_halves(ka).astype(BF16)
    kdw_o[...] = _dup_halves(kw).astype(BF16)
    vda_o[...] = _dup_halves(va_ref[...].astype(F32)).astype(BF16)
    vdw_o[...] = _dup_halves(vw_ref[...].astype(F32)).astype(BF16)


def attn_prep(p, cos, sin, gq, gk, nb, n, rope):
    t = nb * n
    tm = min(512, n)
    spb = n // tm
    hq = N_HEADS * HEAD_DIM
    hk = N_KV * HEAD_DIM
    bdq = (np.arange(hq)[:, None] // HEAD_DIM == np.arange(hq)[None, :] // HEAD_DIM)
    bdq = jnp.asarray(bdq, BF16)
    bdk = bdq[:hk, :hk]
    qspec = lambda c: pl.BlockSpec((tm, hq), lambda s, b: (b * spb + s, c // hq))
    kspec = lambda c: pl.BlockSpec((tm, hk), lambda s, b: (b * spb + s, c // hk))
    tab = pl.BlockSpec((tm, hq), lambda s, b: (s, 0))
    const = lambda shp: pl.BlockSpec(shp, lambda s, b: (0, 0))
    oq = pl.BlockSpec((tm, hq), lambda s, b: (b * spb + s, 0))
    ok = pl.BlockSpec((tm, 2 * hk), lambda s, b: (b * spb + s, 0))
    return pl.pallas_call(
        functools.partial(_prep_kernel, rope=rope),
        grid=(spb, nb),
        in_specs=[qspec(C_QA), qspec(C_QW), kspec(C_KA), kspec(C_VA), kspec(C_KW), kspec(C_VW),
                  tab, tab, const((1, hq)), const((1, hk)), const((hq, hq)), const((hk, hk))],
        out_specs=[oq, oq, ok, ok, ok, ok],
        out_shape=[jax.ShapeDtypeStruct((t, hq), BF16)] * 2 + [jax.ShapeDtypeStruct((t, 2 * hk), BF16)] * 4,
        compiler_params=_params(("parallel", "arbitrary")),
        name="attn_prep",
    )(p, p, p, p, p, p, cos, sin, gq, gk, bdq, bdk)


def rope_tables(n):
    rows = n // GRID_W
    row = jnp.repeat(jnp.arange(rows, dtype=F32), GRID_W)
    col = jnp.tile(jnp.arange(GRID_W, dtype=F32), rows)
    axis_dim = HEAD_DIM // 2
    inv_freq = ROPE_THETA ** (-jnp.arange(0, axis_dim, 2, dtype=F32) / axis_dim)
    ang_r = row[:, None] * inv_freq[None, :]
    ang_c = col[:, None] * inv_freq[None, :]
    cr, sr, cc, sc = jnp.cos(ang_r), jnp.sin(ang_r), jnp.cos(ang_c), jnp.sin(ang_c)
    cos = jnp.concatenate([cr, cr, cc, cc], axis=1)
    sin = jnp.concatenate([-sr, sr, -sc, sc], axis=1)
    return jnp.tile(cos, (1, N_HEADS)), jnp.tile(sin, (1, N_HEADS))


def _pair_operands(kd, vd):
    lane = lax.broadcasted_iota(jnp.int32, kd.shape, 1)
    lo = lane < HALF
    zero = jnp.zeros_like(kd)
    kmats = (jnp.where(lo, kd, zero), jnp.where(lo, zero, kd))
    vstack = jnp.concatenate([jnp.where(lo, vd, zero), jnp.where(lo, zero, vd)], axis=0)
    return kmats, vstack


def _flash_kernel(sink_ref, q_ref, k_ref, v_ref, o_ref, m_scr, l_scr, acc_scr, *, has_sink, nk):
    ki = pl.program_id(2)
    tq = q_ref.shape[0]

    @pl.when(ki == 0)
    def _():
        m_scr[...] = jnp.full(m_scr.shape, NEG_BIG, F32)
        l_scr[...] = jnp.zeros(l_scr.shape, F32)
        acc_scr[...] = jnp.zeros(acc_scr.shape, F32)

    lane_q = lax.broadcasted_iota(jnp.int32, (tq, LANES), 1)
    lo_q = lane_q < HALF
    pairs_per_kv = N_HEADS // N_KV // 2
    for j in range(N_KV):
        kmats, vstack = _pair_operands(k_ref[:, j * LANES:(j + 1) * LANES],
                                       v_ref[:, j * LANES:(j + 1) * LANES])
        for pp in range(pairs_per_kv):
            hp = j * pairs_per_kv + pp
            qp = q_ref[:, hp * LANES:(hp + 1) * LANES]
            ps, alphas = [], []
            for par in range(2):
                h = 2 * hp + par
                s = _dot_nt(qp, kmats[par])
                m_prev = m_scr[h]
                m_new = jnp.maximum(m_prev, jnp.max(s, axis=1, keepdims=True))
                alpha = jnp.exp(m_prev - m_new)
                p = jnp.exp(s - m_new[:, :1])
                l_scr[h] = alpha * l_scr[h] + jnp.sum(p, axis=1, keepdims=True)
                m_scr[h] = m_new
                ps.append(p.astype(BF16))
                alphas.append(alpha)
            pv = _dot(jnp.concatenate(ps, axis=1), vstack)
            sl = slice(hp * LANES, (hp + 1) * LANES)
            acc_scr[:, sl] = acc_scr[:, sl] * jnp.where(lo_q, alphas[0], alphas[1]) + pv

    @pl.when(ki == nk - 1)
    def _():
        for hp in range(N_HEADS // 2):
            ls = []
            for par in range(2):
                h = 2 * hp + par
                l = l_scr[h]
                if has_sink:
                    l = l + jnp.exp(sink_ref[h] - m_scr[h])
                ls.append(l)
            sl = slice(hp * LANES, (hp + 1) * LANES)
            o_ref[:, sl] = (acc_scr[:, sl] / jnp.where(lo_q, ls[0], ls[1])).astype(BF16)


def _flash_bounded_kernel(c_ref, q_ref, k_ref, v_ref, o_ref, lmin_ref, l_scr, acc_scr, *, nk):
    ki = pl.program_id(2)
    tq = q_ref.shape[0]
    tk = k_ref.shape[0]

    @pl.when(ki == 0)
    def _():
        l_scr[...] = jnp.zeros(l_scr.shape, F32)
        acc_scr[...] = jnp.zeros(acc_scr.shape, F32)

    c = c_ref[0]
    pairs_per_kv = N_HEADS // N_KV // 2
    for j in range(N_KV):
        kmats, vstack = _pair_operands(k_ref[:, j * LANES:(j + 1) * LANES],
                                       v_ref[:, j * LANES:(j + 1) * LANES])
        for pp in range(pairs_per_kv):
            hp = j * pairs_per_kv + pp
            qp = q_ref[:, hp * LANES:(hp + 1) * LANES]
            ps = []
            for par in range(2):
                h = 2 * hp + par
                p = jnp.exp(_dot_nt(qp, kmats[par]) - c)
                part = p[:, 0:LANES]
                for cb in range(1, tk // LANES):
                    part = part + p[:, cb * LANES:(cb + 1) * LANES]
                l_scr[h] = l_scr[h] + part
                ps.append(p.astype(BF16))
            sl = slice(hp * LANES, (hp + 1) * LANES)
            acc_scr[:, sl] = acc_scr[:, sl] + _dot(jnp.concatenate(ps, axis=1), vstack)

    @pl.when(ki == nk - 1)
    def _():
        lo_q = lax.broadcasted_iota(jnp.int32, (tq, LANES), 1) < HALF
        mins = []
        for hp in range(N_HEADS // 2):
            ls = [jnp.sum(l_scr[2 * hp + par], axis=1, keepdims=True) for par in range(2)]
            sl = slice(hp * LANES, (hp + 1) * LANES)
            o_ref[:, sl] = (acc_scr[:, sl] / jnp.where(lo_q, ls[0], ls[1])).astype(BF16)
            mins += [jnp.broadcast_to(jnp.min(l, axis=0, keepdims=True), (1, LANES)) for l in ls]
        lmin_ref[...] = jnp.concatenate(mins, axis=0)


FLASH_MIN_DENOM = 1e-30


def flash_attn_bounded(q, kd, vd, bound, nb, n, m):
    tq = min(512, n)
    tk = _pick_tile(m, (768, 512, 256))
    nq, nk = n // tq, m // tk
    hq = N_HEADS * HEAD_DIM
    o, lmin = pl.pallas_call(
        functools.partial(_flash_bounded_kernel, nk=nk),
        grid=(nb, nq, nk),
        in_specs=[pl.BlockSpec(memory_space=pltpu.SMEM),
                  pl.BlockSpec((tq, hq), lambda b, i, k: (b * nq + i, 0)),
                  pl.BlockSpec((tk, 2 * LANES), lambda b, i, k: (b * nk + k, 0)),
                  pl.BlockSpec((tk, 2 * LANES), lambda b, i, k: (b * nk + k, 0))],
        out_specs=[pl.BlockSpec((tq, hq), lambda b, i, k: (b * nq + i, 0)),
                   pl.BlockSpec((N_HEADS, LANES), lambda b, i, k: (b * nq + i, 0))],
        out_shape=[jax.ShapeDtypeStruct((nb * n, hq), BF16),
                   jax.ShapeDtypeStruct((nb * nq * N_HEADS, LANES), F32)],
        scratch_shapes=[pltpu.VMEM((N_HEADS, tq, LANES), F32),
                        pltpu.VMEM((tq, hq), F32)],
        compiler_params=_params(("parallel", "parallel", "arbitrary")),
        name="flash_attn_bounded",
    )(bound, q, kd, vd)
    ok = jnp.min(lmin) > FLASH_MIN_DENOM
    return lax.cond(ok, lambda: o,
                    lambda: flash_attn(q, kd, vd, jnp.zeros((N_HEADS,), F32), nb, n, m, has_sink=False))


def _pick_tile(m, cands):
    for c in cands:
        if m % c == 0:
            return c
    raise ValueError(f"no tile for {m}")


def flash_attn(q, kd, vd, sink, nb, n, m, has_sink):
    tq = min(512, n)
    tk = _pick_tile(m, (768, 512, 256))
    nq, nk = n // tq, m // tk
    hq = N_HEADS * HEAD_DIM
    return pl.pallas_call(
        functools.partial(_flash_kernel, has_sink=has_sink, nk=nk),
        grid=(nb, nq, nk),
        in_specs=[pl.BlockSpec(memory_space=pltpu.SMEM),
                  pl.BlockSpec((tq, hq), lambda b, i, k: (b * nq + i, 0)),
                  pl.BlockSpec((tk, 2 * LANES), lambda b, i, k: (b * nk + k, 0)),
                  pl.BlockSpec((tk, 2 * LANES), lambda b, i, k: (b * nk + k, 0))],
        out_specs=pl.BlockSpec((tq, hq), lambda b, i, k: (b * nq + i, 0)),
        out_shape=jax.ShapeDtypeStruct((nb * n, hq), BF16),
        scratch_shapes=[pltpu.VMEM((N_HEADS, tq, LANES), F32),
                        pltpu.VMEM((N_HEADS, tq, LANES), F32),
                        pltpu.VMEM((tq, hq), F32)],
        compiler_params=_params(("parallel", "parallel", "arbitrary")),
        name="flash_attn",
    )(sink, q, kd, vd)


def _window_kernel(sink_ref, q_ref, kp_ref, km_ref, kn_ref, vp_ref, vm_ref, vn_ref, kc_ref, vc_ref,
                   o_ref, *, n, tq):
    i = pl.program_id(1)
    span = tq + 2 * WINDOW
    q0 = i * tq
    r = lax.broadcasted_iota(jnp.int32, (tq, span), 0)
    c = lax.broadcasted_iota(jnp.int32, (tq, span), 1)
    kpos = c + (q0 - WINDOW)
    ok = (c >= r) & (c <= r + 2 * WINDOW) & (kpos >= 0) & (kpos < n)
    lane_q = lax.broadcasted_iota(jnp.int32, (tq, LANES), 1)
    lo_q = lane_q < HALF
    kloc = jnp.concatenate([kp_ref[...], km_ref[...], kn_ref[...]], axis=0)
    vloc = jnp.concatenate([vp_ref[...], vm_ref[...], vn_ref[...]], axis=0)
    pairs_per_kv = N_HEADS // N_KV // 2
    for j in range(N_KV):
        js = slice(j * LANES, (j + 1) * LANES)
        kl, vl_stack = _pair_operands(kloc[:, js], vloc[:, js])
        kc, vc_stack = _pair_operands(kc_ref[:, js], vc_ref[:, js])
        for pp in range(pairs_per_kv):
            hp = j * pairs_per_kv + pp
            qp = q_ref[:, hp * LANES:(hp + 1) * LANES]
            pl_, pc_, ls = [], [], []
            for par in range(2):
                h = 2 * hp + par
                s_loc = jnp.where(ok, _dot_nt(qp, kl[par]), NEG_BIG)
                s_ctx = _dot_nt(qp, kc[par])
                snk = sink_ref[h]
                m = jnp.maximum(jnp.maximum(jnp.max(s_loc, axis=1, keepdims=True),
                                            jnp.max(s_ctx, axis=1, keepdims=True)), snk)
                p_loc = jnp.exp(s_loc - m)
                p_ctx = jnp.exp(s_ctx - m)
                l = (jnp.sum(p_loc, axis=1, keepdims=True) + jnp.sum(p_ctx, axis=1, keepdims=True)
                     + jnp.exp(snk - m))
                inv = 1.0 / l
                pl_.append((p_loc * inv).astype(BF16))
                pc_.append((p_ctx * inv).astype(BF16))
            o = _dot(jnp.concatenate(pl_, axis=1), vl_stack) + _dot(jnp.concatenate(pc_, axis=1), vc_stack)
            o_ref[:, hp * LANES:(hp + 1) * LANES] = o.astype(BF16)


def window_attn(q, kd, vd, kdc, vdc, sink, nb, n, mc):
    tq = 2 * WINDOW
    nq = n // tq
    wb = n // WINDOW
    hq = N_HEADS * HEAD_DIM
    prev = pl.BlockSpec((WINDOW, 2 * LANES), lambda b, i: (b * wb + jnp.maximum(2 * i - 1, 0), 0))
    main = pl.BlockSpec((tq, 2 * LANES), lambda b, i: (b * nq + i, 0))
    nxt = pl.BlockSpec((WINDOW, 2 * LANES), lambda b, i: (b * wb + jnp.minimum(2 * i + 2, wb - 1), 0))
    ctx = pl.BlockSpec((mc, 2 * LANES), lambda b, i: (b, 0))
    return pl.pallas_call(
        functools.partial(_window_kernel, n=n, tq=tq),
        grid=(nb, nq),
        in_specs=[pl.BlockSpec(memory_space=pltpu.SMEM),
                  pl.BlockSpec((tq, hq), lambda b, i: (b * nq + i, 0)),
                  prev, main, nxt, prev, main, nxt, ctx, ctx],
        out_specs=pl.BlockSpec((tq, hq), lambda b, i: (b * nq + i, 0)),
        out_shape=jax.ShapeDtypeStruct((nb * n, hq), BF16),
        compiler_params=_params(("parallel", "parallel")),
        name="window_attn",
    )(sink, q, kd, kd, kd, vd, vd, vd, kdc, vdc)


HALO = 16


def _conv_kernel(xm_ref, xp_ref, xn_ref, bm_ref, bp_ref, bn_ref, wx_ref, bx_ref, wb_ref, bb_ref,
                 ox_ref, ob_ref, *, nt):
    i = pl.program_id(1)
    has_prev = jnp.where(i > 0, 1.0, 0.0)
    has_next = jnp.where(i < nt - 1, 1.0, 0.0)

    def conv(m_ref, p_ref, n_ref, w_ref, b_ref, o_ref):
        x = m_ref[...].astype(F32)
        tl = x.shape[0]
        row = lax.broadcasted_iota(jnp.int32, x.shape, 0)
        before = p_ref[...].astype(F32)[HALO - 1:HALO, :] * has_prev
        after = n_ref[...].astype(F32)[0:1, :] * has_next
        xm1 = jnp.where(row == 0, before, pltpu.roll(x, 1, 0))
        xp1 = jnp.where(row == tl - 1, after, pltpu.roll(x, tl - 1, 0))
        w = w_ref[...]
        y = xm1 * w[0:1, :] + x * w[1:2, :] + xp1 * w[2:3, :] + b_ref[...]
        o_ref[...] = _silu(y).astype(BF16)

    conv(xm_ref, xp_ref, xn_ref, wx_ref, bx_ref, ox_ref)
    conv(bm_ref, bp_ref, bn_ref, wb_ref, bb_ref, ob_ref)


def ssm_conv(p, conv_w, conv_b, nb, n):
    tl = min(512, n)
    nt = n // tl
    hb = n // HALO
    hpt = tl // HALO
    cx, cb = SSM_INNER, 2 * SSM_GROUPS * SSM_STATE

    def specs(width, col):
        cblk = col // width
        return (pl.BlockSpec((tl, width), lambda b, i: (b * nt + i, cblk)),
                pl.BlockSpec((HALO, width), lambda b, i: (b * hb + jnp.maximum(i * hpt - 1, 0), cblk)),
                pl.BlockSpec((HALO, width), lambda b, i: (b * hb + jnp.minimum((i + 1) * hpt, hb - 1), cblk)))

    const = lambda shp: pl.BlockSpec(shp, lambda b, i: (0, 0))
    xm, xp, xn = specs(cx, C_XS)
    bm, bp, bn = specs(cb, C_BC)
    return pl.pallas_call(
        functools.partial(_conv_kernel, nt=nt),
        grid=(nb, nt),
        in_specs=[xm, xp, xn, bm, bp, bn, const((3, cx)), const((1, cx)), const((3, cb)), const((1, cb))],
        out_specs=[pl.BlockSpec((tl, cx), lambda b, i: (b * nt + i, 0)),
                   pl.BlockSpec((tl, cb), lambda b, i: (b * nt + i, 0))],
        out_shape=[jax.ShapeDtypeStruct((nb * n, cx), BF16), jax.ShapeDtypeStruct((nb * n, cb), BF16)],
        compiler_params=_params(("parallel", "parallel")),
        name="ssm_conv",
    )(p, p, p, p, p, p, conv_w[:, :cx], conv_b[:cx].reshape(1, cx), conv_w[:, cx:], conv_b[cx:].reshape(1, cb))


def _ssd_kernel(xs_ref, bc_ref, dt_ref, bias_ref, aneg_ref, tri_ref, rep_ref, s0_ref,
                y_ref, sfin_ref, st_scr, *, nc):
    d = pl.program_id(1)
    k = pl.program_id(2)
    q = SSM_CHUNK
    gw = SSM_INNER // SSM_GROUPS

    @pl.when(k == 0)
    def _():
        st_scr[...] = s0_ref[...]

    tri = tri_ref[...]
    rep = rep_ref[...]
    dt = _softplus(dt_ref[...] + bias_ref[...])
    a = dt * aneg_ref[...]
    ac = _exact_left(tri, a)
    act = ac.T
    acx = _exact_right(ac, rep)
    dtx = _exact_right(dt, rep)
    totx = jnp.where(d == 0, acx[q - 1:q, :], acx[0:1, :])
    xd = xs_ref[...].astype(F32) * dtx
    xd_b = xd.astype(BF16)
    xe = (xd * jnp.exp(totx - acx)).astype(BF16)
    ein = jnp.exp(acx)
    keep = tri > 0
    lane = lax.broadcasted_iota(jnp.int32, (q, LANES), 1)
    lo = lane < HALF
    zero = jnp.zeros((q, LANES), BF16)
    hpg = SSM_HEADS // SSM_GROUPS
    for g in range(SSM_GROUPS):
        bg = bc_ref[:, g * SSM_STATE:(g + 1) * SSM_STATE]
        cg = bc_ref[:, (SSM_GROUPS + g) * SSM_STATE:(SSM_GROUPS + g + 1) * SSM_STATE]
        cb = _dot_nt(cg, bg)
        st = st_scr[g]
        yoff = _dot(cg, st.astype(BF16)) * ein[:, g * gw:(g + 1) * gw]
        for hp in range(hpg // 2):
            gs = []
            for par in range(2):
                h = g * hpg + 2 * hp + par
                seg = ac[:, h:h + 1] - act[h:h + 1, :]
                gs.append((cb * jnp.exp(jnp.where(keep, seg, NEG_BIG))).astype(BF16))
            c0 = g * gw + hp * LANES
            xp = xd_b[:, c0:c0 + LANES]
            xstack = jnp.concatenate([jnp.where(lo, xp, zero), jnp.where(lo, zero, xp)], axis=0)
            ydiag = _dot(jnp.concatenate(gs, axis=1), xstack)
            y_ref[:, c0:c0 + LANES] = (ydiag + yoff[:, hp * LANES:(hp + 1) * LANES]).astype(BF16)
        bgt = bg.astype(F32).T.astype(BF16)
        cs = _dot(bgt, xe[:, g * gw:(g + 1) * gw])
        st_scr[g] = st * jnp.exp(totx[:, g * gw:(g + 1) * gw]) + cs

    @pl.when(k == nc - 1)
    def _():
        sfin_ref[...] = st_scr[...]


def ssd_scan(u_xs, u_bc, dt_raw, bias2, aneg2, s0, nb, n):
    q = SSM_CHUNK
    nc = n // q
    gw = SSM_INNER // SSM_GROUPS
    idx = np.arange(q)
    tri = np.stack([idx[:, None] >= idx[None, :], idx[:, None] <= idx[None, :]]).astype(np.float32)
    rep = (np.arange(LANES)[:, None] == np.arange(SSM_INNER)[None, :] // SSM_P).astype(np.float32)

    def chunk(b, d, k):
        return b * nc + k + d * (nc - 1 - 2 * k)

    return pl.pallas_call(
        functools.partial(_ssd_kernel, nc=nc),
        grid=(nb, 2, nc),
        in_specs=[pl.BlockSpec((q, SSM_INNER), lambda b, d, k: (chunk(b, d, k), 0)),
                  pl.BlockSpec((q, 2 * SSM_GROUPS * SSM_STATE), lambda b, d, k: (chunk(b, d, k), 0)),
                  pl.BlockSpec((q, LANES), lambda b, d, k: (chunk(b, d, k), d)),
                  pl.BlockSpec((None, 1, LANES), lambda b, d, k: (d, 0, 0)),
                  pl.BlockSpec((None, 1, LANES), lambda b, d, k: (d, 0, 0)),
                  pl.BlockSpec((None, q, q), lambda b, d, k: (d, 0, 0)),
                  pl.BlockSpec((LANES, SSM_INNER), lambda b, d, k: (0, 0)),
                  pl.BlockSpec((None, None, SSM_GROUPS, SSM_STATE, gw), lambda b, d, k: (b, d, 0, 0, 0))],
        out_specs=[pl.BlockSpec((None, q, SSM_INNER), lambda b, d, k: (d, chunk(b, d, k), 0)),
                   pl.BlockSpec((None, None, SSM_GROUPS, SSM_STATE, gw), lambda b, d, k: (b, d, 0, 0, 0))],
        out_shape=[jax.ShapeDtypeStruct((2, nb * n, SSM_INNER), BF16),
                   jax.ShapeDtypeStruct((nb, 2, SSM_GROUPS, SSM_STATE, gw), F32)],
        scratch_shapes=[pltpu.VMEM((SSM_GROUPS, SSM_STATE, gw), F32)],
        compiler_params=_params(("parallel", "parallel", "arbitrary")),
        name="ssd_scan",
    )(u_xs, u_bc, dt_raw, bias2, aneg2, jnp.asarray(tri, BF16), jnp.asarray(rep, BF16), s0)


def _post_kernel(oa_ref, ow_ref, yf_ref, yb_ref, xs_ref, z_ref, gt_ref, x_ref,
                 dsk_ref, nw_ref, wa_ref, ww_ref, ws_ref, wo_ref, gpost_ref, gate_ref,
                 gpre_ref, sh_ref, sc_ref, wrh_ref, wrl_ref,
                 xo_ref, h_ref, lg_ref):
    y = yf_ref[...].astype(F32) + yb_ref[...].astype(F32) + dsk_ref[...] * xs_ref[...].astype(F32)
    u = y * _silu(z_ref[...].astype(F32))
    gw = SSM_INNER // SSM_GROUPS
    ys = jnp.concatenate([_rms(u[:, g * gw:(g + 1) * gw]) for g in range(SSM_GROUPS)], axis=1)
    ys = (ys * nw_ref[...]).astype(BF16)
    d = x_ref.shape[1]
    ga = jax.nn.sigmoid(gt_ref[:, 0:d].astype(F32))
    gw_ = jax.nn.sigmoid(gt_ref[:, d:2 * d].astype(F32))
    gs = jax.nn.sigmoid(gt_ref[:, 2 * d:3 * d].astype(F32))
    m = ga * _dot(oa_ref[...], wa_ref[...]) + gw_ * _dot(ow_ref[...], ww_ref[...]) + gs * _dot(ys, ws_ref[...])
    ml = _dot(m.astype(BF16), wo_ref[...])
    xn = x_ref[...] + gate_ref[...] * (_rms(ml) * gpost_ref[...])
    xo_ref[...] = xn
    h = (_rms(xn) * gpre_ref[...]) * (1.0 + sc_ref[...]) + sh_ref[...]
    _store_rows_as_slabs(h_ref, h)
    hb = h.astype(BF16)
    hl =(h - hb.astype(F32)).astype(BF16)
    lg_ref[...] = _dot_nt(wrh_ref[...], hb) + _dot_nt(wrh_ref[...], hl) + _dot_nt(wrl_ref[...], hb)


def post_mixer(oa, ow, y2, u_xs, p, x, dskip, norm_w, wa, ww, ws, wo, gpost, gate, gpre, shift, scale,
               wr_hi, wr_lo, rows_per_batch):
    t, d = x.shape
    tm = min(256, rows_per_batch)
    tpb = rows_per_batch // tm
    nt = t // tm
    hq = N_HEADS * HEAD_DIM
    row = lambda w, c=0: pl.BlockSpec((tm, w), lambda i: (i, c // w))
    const = lambda shp: pl.BlockSpec(shp, lambda i: (0,) * len(shp))
    mod = pl.BlockSpec((None, 1, d), lambda i: (i // tpb, 0, 0))
    return pl.pallas_call(
        _post_kernel,
        grid=(nt,),
        in_specs=[row(hq), row(hq),
                  pl.BlockSpec((None, tm, SSM_INNER), lambda i: (0, i, 0)),
                  pl.BlockSpec((None, tm, SSM_INNER), lambda i: (1, i, 0)),
                  row(SSM_INNER), row(SSM_INNER, C_Z), row(3 * d, C_GATES), row(d),
                  const((1, SSM_INNER)), const((1, SSM_INNER)),
                  const((hq, d)), const((hq, d)), const((SSM_INNER, d)), const((d, d)),
                  const((1, d)), mod, const((1, d)), mod, mod,
                  const((N_EXPERTS, d)), const((N_EXPERTS, d))],
        out_specs=[row(d), pl.BlockSpec((tm * SLAB, LANES), lambda i: (i, 0)),
                   pl.BlockSpec((N_EXPERTS, tm), lambda i: (0, i))],
        out_shape=[jax.ShapeDtypeStruct((t, d), F32), jax.ShapeDtypeStruct((t * SLAB, LANES), F32),
                   jax.ShapeDtypeStruct((N_EXPERTS, t), F32)],
        compiler_params=_params(("parallel",)),
        name="post_mixer",
    )(oa, ow, y2, y2, u_xs, p, p, x, dskip, norm_w, wa, ww, ws, wo, gpost, gate, gpre, shift, scale,
      wr_hi, wr_lo)


def _route_kernel(lg_ref, b_ref, ei_ref, w_ref):
    scores = jax.nn.sigmoid(lg_ref[...])
    sel = scores + b_ref[...]
    tt = sel.shape[1]
    per = N_EXPERTS // N_EXPERT_GROUPS
    r8 = lax.broadcasted_iota(jnp.int32, (per, tt), 0).astype(F32)
    ninf = -jnp.inf

    def argmax_rows(x, rows, nrows):
        m = jnp.max(x, axis=0, keepdims=True)
        idx = jnp.min(jnp.where(x == m, rows, float(nrows)), axis=0, keepdims=True)
        return m, idx

    gscores = []
    for g in range(N_EXPERT_GROUPS):
        blk = sel[g * per:(g + 1) * per, :]
        m1, i1 = argmax_rows(blk, r8, per)
        m2 = jnp.max(jnp.where(r8 == i1, ninf, blk), axis=0, keepdims=True)
        gscores.append(m1 + m2)
    cur = jnp.concatenate(gscores, axis=0)
    rg = lax.broadcasted_iota(jnp.int32, cur.shape, 0).astype(F32)
    chosen = jnp.zeros(cur.shape, F32)
    for _ in range(TOPK_GROUPS):
        _, gi = argmax_rows(cur, rg, N_EXPERT_GROUPS)
        hit = rg == gi
        chosen = jnp.where(hit, 1.0, chosen)
        cur = jnp.where(hit, ninf, cur)
    gmask = jnp.concatenate([jnp.broadcast_to(chosen[g:g + 1, :], (per, tt)) for g in range(N_EXPERT_GROUPS)],
                            axis=0)
    cur = jnp.where(gmask > 0, sel, ninf)
    re = lax.broadcasted_iota(jnp.int32, cur.shape, 0).astype(F32)
    idxs, ws = [], []
    for _ in range(TOP_K):
        _, ei = argmax_rows(cur, re, N_EXPERTS)
        hit = re == ei
        ws.append(jnp.sum(jnp.where(hit, scores, 0.0), axis=0, keepdims=True))
        idxs.append(ei)
        cur = jnp.where(hit, ninf, cur)
    w = jnp.concatenate(ws, axis=0)
    w_ref[...] = w / jnp.sum(w, axis=0, keepdims=True) * ROUTED_SCALE
    ei_ref[...] = jnp.concatenate(idxs, axis=0).astype(jnp.int32)


def route(logits_t, b_router):
    e, t = logits_t.shape
    tt = 512
    return pl.pallas_call(
        _route_kernel,
        grid=(t // tt,),
        in_specs=[pl.BlockSpec((e, tt), lambda i: (0, i)), pl.BlockSpec((e, 1), lambda i: (0, 0))],
        out_specs=[pl.BlockSpec((TOP_K, tt), lambda i: (0, i)), pl.BlockSpec((TOP_K, tt), lambda i: (0, i))],
        out_shape=[jax.ShapeDtypeStruct((TOP_K, t), jnp.int32), jax.ShapeDtypeStruct((TOP_K, t), F32)],
        compiler_params=_params(("parallel",)),
        name="route",
    )(logits_t, b_router.reshape(e, 1))


MOE_ROWS = 512
PLAN_TOKENS = 512


def _moe_geometry(t):
    nblk = -(-(t * TOP_K + N_EXPERTS * (MOE_ROWS - 1)) // MOE_ROWS)
    return nblk, nblk * MOE_ROWS


def _plan_kernel(ei_ref, ut_ref, tril_ref, dest_ref, tab_ref, be_ref, cnt_scr, run_scr):
    ph = pl.program_id(0)
    i = pl.program_id(1)
    ei = ei_ref[...]
    tt = ei.shape[1]
    re = lax.broadcasted_iota(jnp.int32, (N_EXPERTS, tt), 0)
    hits = [re == ei[k:k + 1, :] for k in range(TOP_K)]
    oh = jnp.zeros((N_EXPERTS, tt), F32)
    for k in range(TOP_K):
        oh = oh + jnp.where(hits[k], 1.0, 0.0)

    @pl.when((ph == 0) & (i == 0))
    def _():
        cnt_scr[...] = jnp.zeros(cnt_scr.shape, F32)

    @pl.when(ph == 0)
    def _():
        cnt_scr[...] = cnt_scr[...] + jnp.sum(oh, axis=1, keepdims=True)

    @pl.when((ph == 1) & (i == 0))
    def _():
        cnt = cnt_scr[...]
        padded = ((cnt.astype(jnp.int32) + (MOE_ROWS - 1)) & (-MOE_ROWS)).astype(F32)
        pad_end = _exact_left(tril_ref[...], padded)
        pad_start = pad_end - padded
        run_scr[...] = pad_start
        tab_ref[0] = pad_start
        tab_ref[1] = pad_end
        nbp = be_ref.shape[1]
        blk0 = (lax.broadcasted_iota(jnp.int32, (N_EXPERTS, nbp), 1) * MOE_ROWS).astype(F32)
        be = jnp.sum(jnp.where(pad_end[:, :1] <= blk0, 1.0, 0.0), axis=0, keepdims=True)
        be_ref[...] = jnp.broadcast_to(jnp.minimum(be, N_EXPERTS - 1.0), be_ref.shape).astype(jnp.int32)

    @pl.when(ph == 1)
    def _():
        cin = _dot(oh.astype(BF16), ut_ref[...])
        pos = run_scr[:, :1] + (cin - oh)
        rows = [jnp.sum(jnp.where(hits[k], pos, 0.0), axis=0, keepdims=True) for k in range(TOP_K)]
        dest_ref[...] = jnp.concatenate(rows, axis=0).astype(jnp.int32)
        run_scr[...] = run_scr[...] + cin[:, tt - 1:tt]


def moe_plan(eidx_t):
    k, t = eidx_t.shape
    tt = PLAN_TOKENS
    nt = t // tt
    nblk, _ = _moe_geometry(t)
    nbp = -(-nblk // LANES) * LANES
    ut = jnp.asarray(np.arange(tt)[:, None] <= np.arange(tt)[None, :], BF16)
    tril = jnp.asarray(np.arange(N_EXPERTS)[:, None] >= np.arange(N_EXPERTS)[None, :], BF16)
    return pl.pallas_call(
        _plan_kernel,
        grid=(2, nt),
        in_specs=[pl.BlockSpec((k, tt), lambda p, i: (0, i)),
                  pl.BlockSpec((tt, tt), lambda p, i: (0, 0)),
                  pl.BlockSpec((N_EXPERTS, N_EXPERTS), lambda p, i: (0, 0))],
        out_specs=[pl.BlockSpec((k, tt), lambda p, i: (0, i * p)),
                   pl.BlockSpec((2, N_EXPERTS, LANES), lambda p, i: (0, 0, 0)),
                   pl.BlockSpec((8, nbp), lambda p, i: (0, 0))],
        out_shape=[jax.ShapeDtypeStruct((k, t), jnp.int32),
                   jax.ShapeDtypeStruct((2, N_EXPERTS, LANES), F32),
                   jax.ShapeDtypeStruct((8, nbp), jnp.int32)],
        scratch_shapes=[pltpu.VMEM((N_EXPERTS, LANES), F32), pltpu.VMEM((N_EXPERTS, LANES), F32)],
        compiler_params=_params(("arbitrary", "arbitrary")),
        name="moe_plan",
    )(eidx_t, ut, tril)


def _dispatch_kernel(dest_ref, pstart_ref, pend_ref, h_ref, xs_hbm, zero_scr, sem):
    i = pl.program_id(0)
    tt = h_ref.shape[0] // SLAB

    def slab(ref, row, n=1):
        return ref.at[pl.ds(pl.multiple_of(row * SLAB, SLAB), n * SLAB), :]

    @pl.when(i == 0)
    def _():
        zero_scr[...] = jnp.zeros(zero_scr.shape, F32)
        nblk = xs_hbm.shape[0] // (MOE_ROWS * SLAB)
        n_used = pend_ref[N_EXPERTS - 1] // MOE_ROWS

        def zero_block(row0):
            return pltpu.make_async_copy(zero_scr, slab(xs_hbm, row0, MOE_ROWS), sem)

        def seg_start(e, c):
            @pl.when(pend_ref[e] > pstart_ref[e])
            def _():
                zero_block(pend_ref[e] - MOE_ROWS).start()
            return c

        def seg_wait(e, c):
            @pl.when(pend_ref[e] > pstart_ref[e])
            def _():
                zero_block(pend_ref[e] - MOE_ROWS).wait()
            return c

        def tail_start(b, c):
            zero_block(b * MOE_ROWS).start()
            return c

        def tail_wait(b, c):
            zero_block(b * MOE_ROWS).wait()
            return c

        lax.fori_loop(0, N_EXPERTS, seg_start, 0)
        lax.fori_loop(n_used, nblk, tail_start, 0)
        lax.fori_loop(0, N_EXPERTS, seg_wait, 0)
        lax.fori_loop(n_used, nblk, tail_wait, 0)

    def issue(t, c):
        for k in range(TOP_K):
            pltpu.make_async_copy(slab(h_ref, t), slab(xs_hbm, dest_ref[k, t]), sem).start(priority=k % 2)
        return c

    lax.fori_loop(0, tt, issue, 0)
    for k in range(TOP_K):
        pltpu.make_async_copy(h_ref, slab(xs_hbm, 0, tt), sem).wait()


def moe_dispatch(dest_t, pad_start, pad_end, h):
    t = h.shape[0] // SLAB
    tt = PLAN_TOKENS
    _, cap = _moe_geometry(t)
    smem = pl.BlockSpec(memory_space=pltpu.SMEM)
    return pl.pallas_call(
        _dispatch_kernel,
        grid=(t // tt,),
        in_specs=[pl.BlockSpec((TOP_K, tt), lambda i: (0, i), memory_space=pltpu.SMEM), smem, smem,
                  pl.BlockSpec((tt * SLAB, LANES), lambda i: (i, 0))],
        out_specs=pl.BlockSpec(memory_space=pl.ANY),
        out_shape=jax.ShapeDtypeStruct((cap * SLAB, LANES), F32),
        scratch_shapes=[pltpu.VMEM((MOE_ROWS * SLAB, LANES), F32), pltpu.SemaphoreType.DMA(())],
        compiler_params=_params(("arbitrary",)),
        name="moe_dispatch",
    )(dest_t, pad_start, pad_end, h)


def _expert_kernel(be_ref, nu_ref, x_ref, wg_ref, wu_ref, wd_ref, o_ref):
    used = pl.program_id(0) < nu_ref[0]

    @pl.when(used)
    def _():
        x = _load_slabs_as_rows(x_ref).astype(BF16)
        hb = (_silu(_dot(x, wg_ref[...])) * _dot(x, wu_ref[...])).astype(BF16)
        _store_rows_as_slabs(o_ref, _dot(hb, wd_ref[...]))

    @pl.when(jnp.logical_not(used))
    def _():
        o_ref[...] = jnp.zeros(o_ref.shape, F32)


def expert_ffn(blk_e, n_used, xs, we_gate, we_up, we_down):
    d, de = we_gate.shape[1:]
    nblk = xs.shape[0] // (MOE_ROWS * SLAB)
    blk = lambda i, nu: jnp.minimum(i, nu[0] - 1)
    grid_spec = pltpu.PrefetchScalarGridSpec(
        num_scalar_prefetch=2,
        grid=(nblk,),
        in_specs=[pl.BlockSpec((MOE_ROWS * SLAB, LANES), lambda i, be, nu: (blk(i, nu), 0)),
                  pl.BlockSpec((None, d, de), lambda i, be, nu: (be[blk(i, nu)], 0, 0)),
                  pl.BlockSpec((None, d, de), lambda i, be, nu: (be[blk(i, nu)], 0, 0)),
                  pl.BlockSpec((None, de, d), lambda i, be, nu: (be[blk(i, nu)], 0, 0))],
        out_specs=pl.BlockSpec((MOE_ROWS * SLAB, LANES), lambda i, be, nu: (i, 0)),
    )
    return pl.pallas_call(
        _expert_kernel,
        grid_spec=grid_spec,
        out_shape=jax.ShapeDtypeStruct(xs.shape, F32),
        compiler_params=_params(("arbitrary",)),
        name="expert_ffn",
    )(blk_e, n_used, xs, we_gate, we_up, we_down)


def _ffn_tail_kernel(dest_ref, w_ref, h_ref, x_ref, wg_ref, wu_ref, wd_ref, gpost_ref, gate_ref, ys_hbm,
                     xo_ref, buf, sem):
    tt = h_ref.shape[0] // SLAB

    def slab(ref, row, n=1):
        return ref.at[pl.ds(pl.multiple_of(row * SLAB, SLAB), n * SLAB), :]

    def issue(t, c):
        for k in range(TOP_K):
            pltpu.make_async_copy(slab(ys_hbm, dest_ref[k, t]), slab(buf.at[k], t), sem).start(priority=k % 2)
        return c

    lax.fori_loop(0, tt, issue, 0)
    h = _load_slabs_as_rows(h_ref).astype(BF16)
    hs = (_silu(_dot(h, wg_ref[...])) * _dot(h, wu_ref[...])).astype(BF16)
    f = _dot(hs, wd_ref[...])
    w = w_ref[...]
    for k in range(TOP_K):
        pltpu.make_async_copy(slab(ys_hbm, 0, tt), buf.at[k], sem).wait()
    for k in range(TOP_K):
        f = f + _load_slabs_as_rows(buf.at[k]) * w[:, k:k + 1]
    xo_ref[...] = x_ref[...] + gate_ref[...] * (_rms(f) * gpost_ref[...])


def ffn_tail(dest_t, wts, h, x, ys, wsg, wsu, wsd, gpost, gate, rows_per_batch):
    t, d = x.shape
    ds = wsg.shape[1]
    tm = min(256, rows_per_batch)
    tpb = rows_per_batch // tm
    row = pl.BlockSpec((tm, d), lambda i: (i, 0))
    const = lambda shp: pl.BlockSpec(shp, lambda i: (0,) * len(shp))
    return pl.pallas_call(
        _ffn_tail_kernel,
        grid=(t // tm,),
        in_specs=[pl.BlockSpec((TOP_K, tm), lambda i: (0, i), memory_space=pltpu.SMEM),
                  pl.BlockSpec((tm, TOP_K), lambda i: (i, 0)),
                  pl.BlockSpec((tm * SLAB, LANES), lambda i: (i, 0)),
                  row, const((d, ds)), const((d, ds)), const((ds, d)), const((1, d)),
                  pl.BlockSpec((None, 1, d), lambda i: (i // tpb, 0, 0)),
                  pl.BlockSpec(memory_space=pl.ANY)],
        out_specs=row,
        out_shape=jax.ShapeDtypeStruct((t, d), F32),
        scratch_shapes=[pltpu.VMEM((TOP_K, tm * SLAB, LANES), F32), pltpu.SemaphoreType.DMA(())],
        compiler_params=_params(("arbitrary",)),
        name="ffn_tail",
    )(dest_t, wts, h, x, wsg, wsu, wsd, gpost, gate, ys)


def _moe_routed(h, logits_t, b_router, we_gate, we_up, we_down):
    eidx_t, wts_t = route(logits_t, b_router)
    dest_t, tabs, blk_e = moe_plan(eidx_t)
    pad_start = tabs[0, :, 0].astype(jnp.int32)
    pad_end = tabs[1, :, 0].astype(jnp.int32)
    n_used = (pad_end[N_EXPERTS - 1:] // MOE_ROWS).astype(jnp.int32)
    xs = moe_dispatch(dest_t, pad_start, pad_end, h)
    ys = expert_ffn(blk_e[0], n_used, xs, we_gate, we_up, we_down)
    return dest_t, wts_t.T, ys


def _reorder_w_in(w):
    qa, ka, va, qw, kw, vw, z, xs, bm, cm, dt, gates = jnp.split(
        w, [512, 640, 768, 1280, 1408, 1536, 2560, 3584, 3840, 4096, 4128], axis=1)
    w_main = jnp.concatenate([qa, qw, z, xs, gates, ka, va, kw, vw, bm, cm], axis=1).astype(BF16)
    pad = jnp.zeros((w.shape[0], LANES - SSM_HEADS), w.dtype)
    w_dt = jnp.concatenate([dt[:, :SSM_HEADS], pad, dt[:, SSM_HEADS:], pad], axis=1).astype(BF16)
    return w_main, w_dt


def _pad_lanes(v):
    return jnp.pad(v, ((0, 0), (0, LANES - v.shape[1])))


def kernel(x, c, ctx, c_ctx, w_ada, b_ada, g_mix_pre, g_mix_post, g_ffn_pre, g_ffn_post, w_in, g_q_a, g_k_a, sink_w, ssm_conv_w, ssm_conv_b, ssm_dt_bias, ssm_a_log, ssm_d, ssm_norm, w_br_a, w_br_w, w_br_s, w_out, w_router, b_router, we_gate, we_up, we_down, ws_gate, ws_up, ws_down):
    nb, n, d = x.shape
    mc = ctx.shape[1]
    depth = w_in.shape[0]
    t_lat, t_ctx = nb * n, nb * mc
    cos, sin = rope_tables(n)
    xl = x.reshape(t_lat, d)
    xc = ctx.reshape(t_ctx, d)
    c8 = jnp.concatenate([c, c_ctx[None, :], jnp.zeros((8 - nb - 1, d), F32)], axis=0)
    zeros_sink = jnp.zeros((N_HEADS,), F32)
    s_zero = jnp.zeros((nb, 2, SSM_GROUPS, SSM_STATE, SSM_INNER // SSM_GROUPS), F32)
    dummy_tab = jnp.zeros((mc, N_HEADS * HEAD_DIM), F32)

    for i in range(depth):
        last = i == depth - 1
        mod = ada_mod(c8, w_ada[i], b_ada[i])
        mod_l = [mod[:nb, k * d:(k + 1) * d].reshape(nb, 1, d) for k in range(6)]
        mod_c = [mod[nb:nb + 1, k * d:(k + 1) * d].reshape(1, 1, d) for k in range(6)]
        w_main, w_dt = _reorder_w_in(w_in[i])
        gq = jnp.tile(g_q_a[i], N_HEADS)[None, :]
        gk = jnp.tile(g_k_a[i], N_KV)[None, :]
        bias2 = _pad_lanes(ssm_dt_bias[i].reshape(2, SSM_HEADS)).reshape(2, 1, LANES)
        aneg2 = _pad_lanes(-jnp.exp(ssm_a_log[i].astype(F32))).reshape(2, 1, LANES)
        dskip = jnp.repeat(ssm_d[i], SSM_P)[None, :]
        norm_w = ssm_norm[i][None, :]
        wa, ww, ws, wo = (w_br_a[i].astype(BF16), w_br_w[i].astype(BF16), w_br_s[i].astype(BF16),
                          w_out[i].astype(BF16))
        wr_t = w_router[i].T
        wr_hi = wr_t.astype(BF16)
        wr_lo = (wr_t - wr_hi.astype(F32)).astype(BF16)
        sink = sink_w[i].astype(F32)

        p_c, dt_c = in_proj(xc, g_mix_pre[i], mod_c[0], mod_c[1], w_main, w_dt, t_ctx)
        qa_c, qw_c, kda_c, vda_c, kdw_c, vdw_c = attn_prep(p_c, dummy_tab, dummy_tab, gq, gk, nb, mc, rope=False)
        uxs_c, ubc_c = ssm_conv(p_c, ssm_conv_w[i], ssm_conv_b[i], nb, mc)
        y_c, s_fin = ssd_scan(uxs_c, ubc_c, dt_c, bias2, aneg2, s_zero, nb, mc)

        p_l, dt_l = in_proj(xl, g_mix_pre[i], mod_l[0], mod_l[1], w_main, w_dt, n)
        qa, qw, kda, vda, kdw, vdw = attn_prep(p_l, cos, sin, gq, gk, nb, n, rope=True)
        m_all = n + mc
        kd_all = jnp.concatenate([kda.reshape(nb, n, -1), kda_c.reshape(nb, mc, -1)], axis=1).reshape(nb * m_all, -1)
        vd_all = jnp.concatenate([vda.reshape(nb, n, -1), vda_c.reshape(nb, mc, -1)], axis=1).reshape(nb * m_all, -1)
        score_bound = (math.sqrt(HEAD_DIM) * jnp.max(jnp.abs(g_q_a[i])) * jnp.max(jnp.abs(g_k_a[i]))).reshape(1)
        oa = flash_attn_bounded(qa, kd_all, vd_all, score_bound.astype(F32), nb, n, m_all)
        ow = window_attn(qw, kdw, vdw, kdw_c, vdw_c, sink, nb, n, mc)
        uxs, ubc = ssm_conv(p_l, ssm_conv_w[i], ssm_conv_b[i], nb, n)
        y_l, _ = ssd_scan(uxs, ubc, dt_l, bias2, aneg2, s_fin, nb, n)
        xl, h_l, lg_l = post_mixer(oa, ow, y_l, uxs, p_l, xl, dskip, norm_w, wa, ww, ws, wo,
                                   g_mix_post[i][None, :], mod_l[2], g_ffn_pre[i][None, :], mod_l[3], mod_l[4],
                                   wr_hi, wr_lo, n)
        wsg, wsu, wsd = ws_gate[i].astype(BF16), ws_up[i].astype(BF16), ws_down[i].astype(BF16)
        weg, weu, wed = we_gate[i].astype(BF16), we_up[i].astype(BF16), we_down[i].astype(BF16)
        if last:
            dest_t, wts, ys = _moe_routed(h_l, lg_l, b_router[i], weg, weu, wed)
            xl = ffn_tail(dest_t, wts, h_l, xl, ys, wsg, wsu, wsd, g_ffn_post[i][None, :], mod_l[5], n)
        else:
            oa_c = flash_attn(qa_c, kda_c, vda_c, zeros_sink, nb, mc, mc, has_sink=False)
            ow_c = flash_attn(qw_c, kdw_c, vdw_c, sink, nb, mc, mc, has_sink=True)
            xc, h_c, lg_c = post_mixer(oa_c, ow_c, y_c, uxs_c, p_c, xc, dskip, norm_w, wa, ww, ws, wo,
                                       g_mix_post[i][None, :], mod_c[2], g_ffn_pre[i][None, :], mod_c[3], mod_c[4],
                                       wr_hi, wr_lo, t_ctx)
            h_all = jnp.concatenate([h_l, h_c], axis=0)
            lg_all = jnp.concatenate([lg_l, lg_c], axis=1)
            dest_t, wts, ys = _moe_routed(h_all, lg_all, b_router[i], weg, weu, wed)
            xl = ffn_tail(dest_t[:, :t_lat], wts[:t_lat], h_l, xl, ys, wsg, wsu, wsd,
                          g_ffn_post[i][None, :], mod_l[5], n)
            xc = ffn_tail(dest_t[:, t_lat:], wts[t_lat:], h_c, xc, ys, wsg, wsu, wsd,
                          g_ffn_post[i][None, :], mod_c[5], t_ctx)
    return xl.reshape(nb, n, d)
```

```python
import functools
import math

import jax
import jax.numpy as jnp
import numpy as np
from jax import lax
from jax.experimental import pallas as pl
from jax.experimental.pallas import tpu as pltpu

F32 = jnp.float32
BF16 = jnp.bfloat16

HEAD_DIM = 64
N_HEADS = 8
N_KV = 2
GRID_W = 64
ROPE_THETA = 10000.0
WINDOW = 128
SSM_HEADS = 16
SSM_P = 64
SSM_INNER = SSM_HEADS * SSM_P
SSM_GROUPS = 2
SSM_STATE = 128
SSM_CHUNK = 128
N_EXPERTS = 64
TOP_K = 8
N_EXPERT_GROUPS = 8
TOPK_GROUPS = 4
ROUTED_SCALE = 2.5
EPS = 1e-6

LANES = 128
HALF = LANES // 2
VMEM_LIMIT = 56 * 1024 * 1024
NEG_BIG = -1e30

C_QA, C_QW, C_Z, C_XS, C_GATES = 0, 512, 1024, 2048, 3072
C_KA, C_VA, C_KW, C_VW, C_BC = 6144, 6272, 6400, 6528, 6656
P_WIDTH = 7168


def _params(sem, vmem=VMEM_LIMIT):
    return pltpu.CompilerParams(dimension_semantics=sem, vmem_limit_bytes=vmem)


def _silu(x):
    return x * jax.nn.sigmoid(x)


U32 = jnp.uint32
SLAB = 4
HI_MASK = 0xFFFF0000


def _store_rows_as_slabs(ref, x):
    r = x.shape[0]

    def bits(v):
        return pltpu.bitcast(v.astype(BF16).astype(F32), U32)

    for c in range(SLAB):
        lo = bits(x[:, c * LANES:(c + 1) * LANES]) >> 16
        hi = bits(x[:, (SLAB + c) * LANES:(SLAB + c + 1) * LANES]) & jnp.uint32(HI_MASK)
        ref[pl.ds(c, r, stride=SLAB), :] = hi | lo


def _load_slabs_as_rows(ref):
    r = ref.shape[0] // SLAB
    words = [ref[pl.ds(c, r, stride=SLAB), :] for c in range(SLAB)]
    los = [pltpu.bitcast(w << 16, F32) for w in words]
    his = [pltpu.bitcast(w & jnp.uint32(HI_MASK), F32) for w in words]
    return jnp.concatenate(los + his, axis=1)


def _softplus(x):
    return jnp.maximum(x, 0.0) + jnp.log(1.0 + jnp.exp(-jnp.abs(x)))


def _rms(x, eps=EPS):
    return x * lax.rsqrt(jnp.mean(x * x, axis=-1, keepdims=True) + eps)


def _split3(a):
    a1 = a.astype(BF16)
    r1 = a - a1.astype(F32)
    a2 = r1.astype(BF16)
    a3 = (r1 - a2.astype(F32)).astype(BF16)
    return a1, a2, a3


def _dot(a, b):
    return jnp.dot(a, b, preferred_element_type=F32)


def _dot_nt(a, b):
    return lax.dot_general(a, b, (((1,), (1,)), ((), ())), preferred_element_type=F32)


def _exact_right(a, r01):
    a1, a2, a3 = _split3(a)
    return _dot(a1, r01) + _dot(a2, r01) + _dot(a3, r01)


def _exact_left(m01, a):
    a1, a2, a3 = _split3(a)
    return _dot(m01, a1) + _dot(m01, a2) + _dot(m01, a3)


def _ada_kernel(c_ref, w_ref, b_ref, o_ref):
    h = _silu(c_ref[...])
    o_ref[...] = jnp.dot(h, w_ref[...], preferred_element_type=F32,
                         precision=lax.Precision.HIGHEST) + b_ref[...]


def ada_mod(c8, w, b):
    d, n = w.shape
    tn = 1536
    return pl.pallas_call(
        _ada_kernel,
        grid=(n // tn,),
        in_specs=[pl.BlockSpec((8, d), lambda j: (0, 0)),
                  pl.BlockSpec((d, tn), lambda j: (0, j)),
                  pl.BlockSpec((1, tn), lambda j: (0, j))],
        out_specs=pl.BlockSpec((8, tn), lambda j: (0, j)),
        out_shape=jax.ShapeDtypeStruct((8, n), F32),
        compiler_params=_params(("parallel",)),
        name="ada_mod",
    )(c8, w, b.reshape(1, n))


def _inproj_kernel(x_ref, g_ref, sh_ref, sc_ref, w_ref, wdt_ref, o_ref, odt_ref, h_scr):
    @pl.when(pl.program_id(1) == 0)
    def _():
        h = _rms(x_ref[...]) * g_ref[...]
        h = h * (1.0 + sc_ref[...]) + sh_ref[...]
        hb = h.astype(BF16)
        h_scr[...] = hb
        odt_ref[...] = _dot(hb, wdt_ref[...])

    o_ref[...] = _dot(h_scr[...], w_ref[...]).astype(BF16)


def in_proj(x, g, shift, scale, w_main, w_dt, rows_per_batch):
    t, d = x.shape
    n = w_main.shape[1]
    tm = min(512, rows_per_batch)
    tn = n // 2
    tpb = rows_per_batch // tm
    mod_spec = pl.BlockSpec((None, 1, d), lambda i, j: (i // tpb, 0, 0))
    return pl.pallas_call(
        _inproj_kernel,
        grid=(t // tm, n // tn),
        in_specs=[pl.BlockSpec((tm, d), lambda i, j: (i, 0)),
                  pl.BlockSpec((1, d), lambda i, j: (0, 0)),
                  mod_spec, mod_spec,
                  pl.BlockSpec((d, tn), lambda i, j: (0, j)),
                  pl.BlockSpec((d, 2 * LANES), lambda i, j: (0, 0))],
        out_specs=[pl.BlockSpec((tm, tn), lambda i, j: (i, j)),
                   pl.BlockSpec((tm, 2 * LANES), lambda i, j: (i, 0))],
        out_shape=[jax.ShapeDtypeStruct((t, n), BF16),
                   jax.ShapeDtypeStruct((t, 2 * LANES), F32)],
        scratch_shapes=[pltpu.VMEM((tm, d), BF16)],
        compiler_params=_params(("parallel", "arbitrary")),
        name="in_proj",
    )(x, g.reshape(1, d), shift, scale, w_main, w_dt)


def _rope(x, cos, sin):
    w = x.shape[-1]
    lane = lax.broadcasted_iota(jnp.int32, x.shape, 1)
    first = (lane % 32) < 16
    swapped = jnp.where(first, pltpu.roll(x, w - 16, 1), pltpu.roll(x, 16, 1))
    return x * cos + swapped * sin


def _dup_halves(x):
    lane = lax.broadcasted_iota(jnp.int32, x.shape, 1)
    lo = lane < HALF
    r = pltpu.roll(x, HALF, 1)
    return jnp.concatenate([jnp.where(lo, x, r), jnp.where(lo, r, x)], axis=1)


def _prep_kernel(qa_ref, qw_ref, ka_ref, va_ref, kw_ref, vw_ref, cos_ref, sin_ref,
                 gq_ref, gk_ref, bdq_ref, bdk_ref,
                 qa_o, qw_o, kda_o, vda_o, kdw_o, vdw_o, *, rope):
    scale = HEAD_DIM ** -0.5
    inv_hd = 1.0 / HEAD_DIM

    def headnorm(x, g, bd):
        ss = _dot((x * x).astype(BF16), bd) * inv_hd
        return x * lax.rsqrt(ss + EPS) * g

    qa = headnorm(qa_ref[...].astype(F32), gq_ref[...], bdq_ref[...])
    ka = headnorm(ka_ref[...].astype(F32), gk_ref[...], bdk_ref[...])
    qw = qw_ref[...].astype(F32)
    kw = kw_ref[...].astype(F32)
    if rope:
        cos = cos_ref[...]
        sin = sin_ref[...]
        qa = _rope(qa, cos, sin)
        qw = _rope(qw, cos, sin)
        ka = _rope(ka, cos[:, :LANES], sin[:, :LANES])
        kw = _rope(kw, cos[:, :LANES], sin[:, :LANES])
    qa_o[...] = (qa * scale).astype(BF16)
    qw_o[...] = (qw * scale).astype(BF16)
    kda_o[...] = _dup_halves(ka).astype(BF16)
    kdw_o[...] = _dup_halves(kw).astype(BF16)
    vda_o[...] = _dup_halves(va_ref[...].astype(F32)).astype(BF16)
    vdw_o[...] = _dup_halves(vw_ref[...].astype(F32)).astype(BF16)


def attn_prep(p, cos, sin, gq, gk, nb, n, rope):
    t = nb * n
    tm = min(512, n)
    spb = n // tm
    hq = N_HEADS * HEAD_DIM
    hk = N_KV * HEAD_DIM
    bdq = (np.arange(hq)[:, None] // HEAD_DIM == np.arange(hq)[None, :] // HEAD_DIM)
    bdq = jnp.asarray(bdq, BF16)
    bdk = bdq[:hk, :hk]
    qspec = lambda c: pl.BlockSpec((tm, hq), lambda s, b: (b * spb + s, c // hq))
    kspec = lambda c: pl.BlockSpec((tm, hk), lambda s, b: (b * spb + s, c // hk))
    tab = pl.BlockSpec((tm, hq), lambda s, b: (s, 0))
    const = lambda shp: pl.BlockSpec(shp, lambda s, b: (0, 0))
    oq = pl.BlockSpec((tm, hq), lambda s, b: (b * spb + s, 0))
    ok = pl.BlockSpec((tm, 2 * hk), lambda s, b: (b * spb + s, 0))
    return pl.pallas_call(
        functools.partial(_prep_kernel, rope=rope),
        grid=(spb, nb),
        in_specs=[qspec(C_QA), qspec(C_QW), kspec(C_KA), kspec(C_VA), kspec(C_KW), kspec(C_VW),
                  tab, tab, const((1, hq)), const((1, hk)), const((hq, hq)), const((hk, hk))],
        out_specs=[oq, oq, ok, ok, ok, ok],
        out_shape=[jax.ShapeDtypeStruct((t, hq), BF16)] * 2 + [jax.ShapeDtypeStruct((t, 2 * hk), BF16)] * 4,
        compiler_params=_params(("parallel", "arbitrary")),
        name="attn_prep",
    )(p, p, p, p, p, p, cos, sin, gq, gk, bdq, bdk)


def rope_tables(n):
    rows = n // GRID_W
    row = jnp.repeat(jnp.arange(rows, dtype=F32), GRID_W)
    col = jnp.tile(jnp.arange(GRID_W, dtype=F32), rows)
    axis_dim = HEAD_DIM // 2
    inv_freq = ROPE_THETA ** (-jnp.arange(0, axis_dim, 2, dtype=F32) / axis_dim)
    ang_r = row[:, None] * inv_freq[None, :]
    ang_c = col[:, None] * inv_freq[None, :]
    cr, sr, cc, sc = jnp.cos(ang_r), jnp.sin(ang_r), jnp.cos(ang_c), jnp.sin(ang_c)
    cos = jnp.concatenate([cr, cr, cc, cc], axis=1)
    sin = jnp.concatenate([-sr, sr, -sc, sc], axis=1)
    return jnp.tile(cos, (1, N_HEADS)), jnp.tile(sin, (1, N_HEADS))


def _pair_operands(kd, vd):
    lane = lax.broadcasted_iota(jnp.int32, kd.shape, 1)
    lo = lane < HALF
    zero = jnp.zeros_like(kd)
    kmats = (jnp.where(lo, kd, zero), jnp.where(lo, zero, kd))
    vstack = jnp.concatenate([jnp.where(lo, vd, zero), jnp.where(lo, zero, vd)], axis=0)
    return kmats, vstack


def _flash_kernel(sink_ref, q_ref, k_ref, v_ref, o_ref, m_scr, l_scr, acc_scr, *, has_sink, nk):
    ki = pl.program_id(2)
    tq = q_ref.shape[0]

    @pl.when(ki == 0)
    def _():
        m_scr[...] = jnp.full(m_scr.shape, NEG_BIG, F32)
        l_scr[...] = jnp.zeros(l_scr.shape, F32)
        acc_scr[...] = jnp.zeros(acc_scr.shape, F32)

    lane_q = lax.broadcasted_iota(jnp.int32, (tq, LANES), 1)
    lo_q = lane_q < HALF
    pairs_per_kv = N_HEADS // N_KV // 2
    for j in range(N_KV):
        kmats, vstack = _pair_operands(k_ref[:, j * LANES:(j + 1) * LANES],
                                       v_ref[:, j * LANES:(j + 1) * LANES])
        for pp in range(pairs_per_kv):
            hp = j * pairs_per_kv + pp
            qp = q_ref[:, hp * LANES:(hp + 1) * LANES]
            ps, alphas = [], []
            for par in range(2):
                h = 2 * hp + par
                s = _dot_nt(qp, kmats[par])
                m_prev = m_scr[h]
                m_new = jnp.maximum(m_prev, jnp.max(s, axis=1, keepdims=True))
                alpha = jnp.exp(m_prev - m_new)
                p = jnp.exp(s - m_new[:, :1])
                l_scr[h] = alpha * l_scr[h] + jnp.sum(p, axis=1, keepdims=True)
                m_scr[h] = m_new
                ps.append(p.astype(BF16))
                alphas.append(alpha)
            pv = _dot(jnp.concatenate(ps, axis=1), vstack)
            sl = slice(hp * LANES, (hp + 1) * LANES)
            acc_scr[:, sl] = acc_scr[:, sl] * jnp.where(lo_q, alphas[0], alphas[1]) + pv

    @pl.when(ki == nk - 1)
    def _():
        for hp in range(N_HEADS // 2):
            ls = []
            for par in range(2):
                h = 2 * hp + par
                l = l_scr[h]
                if has_sink:
                    l = l + jnp.exp(sink_ref[h] - m_scr[h])
                ls.append(l)
            sl = slice(hp * LANES, (hp + 1) * LANES)
            o_ref[:, sl] = (acc_scr[:, sl] / jnp.where(lo_q, ls[0], ls[1])).astype(BF16)


def _flash_bounded_kernel(c_ref, q_ref, k_ref, v_ref, o_ref, lmin_ref, l_scr, acc_scr, *, nk):
    ki = pl.program_id(2)
    tq = q_ref.shape[0]
    tk = k_ref.shape[0]

    @pl.when(ki == 0)
    def _():
        l_scr[...] = jnp.zeros(l_scr.shape, F32)
        acc_scr[...] = jnp.zeros(acc_scr.shape, F32)

    c = c_ref[0]
    pairs_per_kv = N_HEADS // N_KV // 2
    for j in range(N_KV):
        kmats, vstack = _pair_operands(k_ref[:, j * LANES:(j + 1) * LANES],
                                       v_ref[:, j * LANES:(j + 1) * LANES])
        for pp in range(pairs_per_kv):
            hp = j * pairs_per_kv + pp
            qp = q_ref[:, hp * LANES:(hp + 1) * LANES]
            ps = []
            for par in range(2):
                h = 2 * hp + par
                p = jnp.exp(_dot_nt(qp, kmats[par]) - c)
                part = p[:, 0:LANES]
                for cb in range(1, tk // LANES):
                    part = part + p[:, cb * LANES:(cb + 1) * LANES]
                l_scr[h] = l_scr[h] + part
                ps.append(p.astype(BF16))
            sl = slice(hp * LANES, (hp + 1) * LANES)
            acc_scr[:, sl] = acc_scr[:, sl] + _dot(jnp.concatenate(ps, axis=1), vstack)

    @pl.when(ki == nk - 1)
    def _():
        lo_q = lax.broadcasted_iota(jnp.int32, (tq, LANES), 1) < HALF
        mins = []
        for hp in range(N_HEADS // 2):
            ls = [jnp.sum(l_scr[2 * hp + par], axis=1, keepdims=True) for par in range(2)]
            sl = slice(hp * LANES, (hp + 1) * LANES)
            o_ref[:, sl] = (acc_scr[:, sl] / jnp.where(lo_q, ls[0], ls[1])).astype(BF16)
            mins += [jnp.broadcast_to(jnp.min(l, axis=0, keepdims=True), (1, LANES)) for l in ls]
        lmin_ref[...] = jnp.concatenate(mins, axis=0)


FLASH_MIN_DENOM = 1e-30


def flash_attn_bounded(q, kd, vd, bound, nb, n, m):
    tq = min(512, n)
    tk = _pick_tile(m, (768, 512, 256))
    nq, nk = n // tq, m // tk
    hq = N_HEADS * HEAD_DIM
    o, lmin = pl.pallas_call(
        functools.partial(_flash_bounded_kernel, nk=nk),
        grid=(nb, nq, nk),
        in_specs=[pl.BlockSpec(memory_space=pltpu.SMEM),
                  pl.BlockSpec((tq, hq), lambda b, i, k: (b * nq + i, 0)),
                  pl.BlockSpec((tk, 2 * LANES), lambda b, i, k: (b * nk + k, 0)),
                  pl.BlockSpec((tk, 2 * LANES), lambda b, i, k: (b * nk + k, 0))],
        out_specs=[pl.BlockSpec((tq, hq), lambda b, i, k: (b * nq + i, 0)),
                   pl.BlockSpec((N_HEADS, LANES), lambda b, i, k: (b * nq + i, 0))],
        out_shape=[jax.ShapeDtypeStruct((nb * n, hq), BF16),
                   jax.ShapeDtypeStruct((nb * nq * N_HEADS, LANES), F32)],
        scratch_shapes=[pltpu.VMEM((N_HEADS, tq, LANES), F32),
                        pltpu.VMEM((tq, hq), F32)],
        compiler_params=_params(("parallel", "parallel", "arbitrary")),
        name="flash_attn_bounded",
    )(bound, q, kd, vd)
    ok = jnp.min(lmin) > FLASH_MIN_DENOM
    return lax.cond(ok, lambda: o,
                    lambda: flash_attn(q, kd, vd, jnp.zeros((N_HEADS,), F32), nb, n, m, has_sink=False))


def _pick_tile(m, cands):
    for c in cands:
        if m % c == 0:
            return c
    raise ValueError(f"no tile for {m}")


def flash_attn(q, kd, vd, sink, nb, n, m, has_sink):
    tq = min(512, n)
    tk = _pick_tile(m, (768, 512, 256))
    nq, nk = n // tq, m // tk
    hq = N_HEADS * HEAD_DIM
    return pl.pallas_call(
        functools.partial(_flash_kernel, has_sink=has_sink, nk=nk),
        grid=(nb, nq, nk),
        in_specs=[pl.BlockSpec(memory_space=pltpu.SMEM),
                  pl.BlockSpec((tq, hq), lambda b, i, k: (b * nq + i, 0)),
                  pl.BlockSpec((tk, 2 * LANES), lambda b, i, k: (b * nk + k, 0)),
                  pl.BlockSpec((tk, 2 * LANES), lambda b, i, k: (b * nk + k, 0))],
        out_specs=pl.BlockSpec((tq, hq), lambda b, i, k: (b * nq + i, 0)),
        out_shape=jax.ShapeDtypeStruct((nb * n, hq), BF16),
        scratch_shapes=[pltpu.VMEM((N_HEADS, tq, LANES), F32),
                        pltpu.VMEM((N_HEADS, tq, LANES), F32),
                        pltpu.VMEM((tq, hq), F32)],
        compiler_params=_params(("parallel", "parallel", "arbitrary")),
        name="flash_attn",
    )(sink, q, kd, vd)


def _window_kernel(sink_ref, q_ref, kp_ref, km_ref, kn_ref, vp_ref, vm_ref, vn_ref, kc_ref, vc_ref,
                   o_ref, *, n, tq):
    i = pl.program_id(1)
    span = tq + 2 * WINDOW
    q0 = i * tq
    r = lax.broadcasted_iota(jnp.int32, (tq, span), 0)
    c = lax.broadcasted_iota(jnp.int32, (tq, span), 1)
    kpos = c + (q0 - WINDOW)
    ok = (c >= r) & (c <= r + 2 * WINDOW) & (kpos >= 0) & (kpos < n)
    lane_q = lax.broadcasted_iota(jnp.int32, (tq, LANES), 1)
    lo_q = lane_q < HALF
    kloc = jnp.concatenate([kp_ref[...], km_ref[...], kn_ref[...]], axis=0)
    vloc = jnp.concatenate([vp_ref[...], vm_ref[...], vn_ref[...]], axis=0)
    pairs_per_kv = N_HEADS // N_KV // 2
    for j in range(N_KV):
        js = slice(j * LANES, (j + 1) * LANES)
        kl, vl_stack = _pair_operands(kloc[:, js], vloc[:, js])
        kc, vc_stack = _pair_operands(kc_ref[:, js], vc_ref[:, js])
        for pp in range(pairs_per_kv):
            hp = j * pairs_per_kv + pp
            qp = q_ref[:, hp * LANES:(hp + 1) * LANES]
            pl_, pc_, ls = [], [], []
            for par in range(2):
                h = 2 * hp + par
                s_loc = jnp.where(ok, _dot_nt(qp, kl[par]), NEG_BIG)
                s_ctx = _dot_nt(qp, kc[par])
                snk = sink_ref[h]
                m = jnp.maximum(jnp.maximum(jnp.max(s_loc, axis=1, keepdims=True),
                                            jnp.max(s_ctx, axis=1, keepdims=True)), snk)
                p_loc = jnp.exp(s_loc - m)
                p_ctx = jnp.exp(s_ctx - m)
                l = (jnp.sum(p_loc, axis=1, keepdims=True) + jnp.sum(p_ctx, axis=1, keepdims=True)
                     + jnp.exp(snk - m))
                inv = 1.0 / l
                pl_.append((p_loc * inv).astype(BF16))
                pc_.append((p_ctx * inv).astype(BF16))
            o = _dot(jnp.concatenate(pl_, axis=1), vl_stack) + _dot(jnp.concatenate(pc_, axis=1), vc_stack)
            o_ref[:, hp * LANES:(hp + 1) * LANES] = o.astype(BF16)


def window_attn(q, kd, vd, kdc, vdc, sink, nb, n, mc):
    tq = 2 * WINDOW
    nq = n // tq
    wb = n // WINDOW
    hq = N_HEADS * HEAD_DIM
    prev = pl.BlockSpec((WINDOW, 2 * LANES), lambda b, i: (b * wb + jnp.maximum(2 * i - 1, 0), 0))
    main = pl.BlockSpec((tq, 2 * LANES), lambda b, i: (b * nq + i, 0))
    nxt = pl.BlockSpec((WINDOW, 2 * LANES), lambda b, i: (b * wb + jnp.minimum(2 * i + 2, wb - 1), 0))
    ctx = pl.BlockSpec((mc, 2 * LANES), lambda b, i: (b, 0))
    return pl.pallas_call(
        functools.partial(_window_kernel, n=n, tq=tq),
        grid=(nb, nq),
        in_specs=[pl.BlockSpec(memory_space=pltpu.SMEM),
                  pl.BlockSpec((tq, hq), lambda b, i: (b * nq + i, 0)),
                  prev, main, nxt, prev, main, nxt, ctx, ctx],
        out_specs=pl.BlockSpec((tq, hq), lambda b, i: (b * nq + i, 0)),
        out_shape=jax.ShapeDtypeStruct((nb * n, hq), BF16),
        compiler_params=_params(("parallel", "parallel")),
        name="window_attn",
    )(sink, q, kd, kd, kd, vd, vd, vd, kdc, vdc)


HALO = 16


def _conv_kernel(xm_ref, xp_ref, xn_ref, bm_ref, bp_ref, bn_ref, wx_ref, bx_ref, wb_ref, bb_ref,
                 ox_ref, ob_ref, *, nt):
    i = pl.program_id(1)
    has_prev = jnp.where(i > 0, 1.0, 0.0)
    has_next = jnp.where(i < nt - 1, 1.0, 0.0)

    def conv(m_ref, p_ref, n_ref, w_ref, b_ref, o_ref):
        x = m_ref[...].astype(F32)
        tl = x.shape[0]
        row = lax.broadcasted_iota(jnp.int32, x.shape, 0)
        before = p_ref[...].astype(F32)[HALO - 1:HALO, :] * has_prev
        after = n_ref[...].astype(F32)[0:1, :] * has_next
        xm1 = jnp.where(row == 0, before, pltpu.roll(x, 1, 0))
        xp1 = jnp.where(row == tl - 1, after, pltpu.roll(x, tl - 1, 0))
        w = w_ref[...]
        y = xm1 * w[0:1, :] + x * w[1:2, :] + xp1 * w[2:3, :] + b_ref[...]
        o_ref[...] = _silu(y).astype(BF16)

    conv(xm_ref, xp_ref, xn_ref, wx_ref, bx_ref, ox_ref)
    conv(bm_ref, bp_ref, bn_ref, wb_ref, bb_ref, ob_ref)


def ssm_conv(p, conv_w, conv_b, nb, n):
    tl = min(512, n)
    nt = n // tl
    hb = n // HALO
    hpt = tl // HALO
    cx, cb = SSM_INNER, 2 * SSM_GROUPS * SSM_STATE

    def specs(width, col):
        cblk = col // width
        return (pl.BlockSpec((tl, width), lambda b, i: (b * nt + i, cblk)),
                pl.BlockSpec((HALO, width), lambda b, i: (b * hb + jnp.maximum(i * hpt - 1, 0), cblk)),
                pl.BlockSpec((HALO, width), lambda b, i: (b * hb + jnp.minimum((i + 1) * hpt, hb - 1), cblk)))

    const = lambda shp: pl.BlockSpec(shp, lambda b, i: (0, 0))
    xm, xp, xn = specs(cx, C_XS)
    bm, bp, bn = specs(cb, C_BC)
    return pl.pallas_call(
        functools.partial(_conv_kernel, nt=nt),
        grid=(nb, nt),
        in_specs=[xm, xp, xn, bm, bp, bn, const((3, cx)), const((1, cx)), const((3, cb)), const((1, cb))],
        out_specs=[pl.BlockSpec((tl, cx), lambda b, i: (b * nt + i, 0)),
                   pl.BlockSpec((tl, cb), lambda b, i: (b * nt + i, 0))],
        out_shape=[jax.ShapeDtypeStruct((nb * n, cx), BF16), jax.ShapeDtypeStruct((nb * n, cb), BF16)],
        compiler_params=_params(("parallel", "parallel")),
        name="ssm_conv",
    )(p, p, p, p, p, p, conv_w[:, :cx], conv_b[:cx].reshape(1, cx), conv_w[:, cx:], conv_b[cx:].reshape(1, cb))


def _ssd_kernel(xs_ref, bc_ref, dt_ref, bias_ref, aneg_ref, tri_ref, rep_ref, s0_ref,
                y_ref, sfin_ref, st_scr, *, nc):
    d = pl.program_id(1)
    k = pl.program_id(2)
    q = SSM_CHUNK
    gw = SSM_INNER // SSM_GROUPS

    @pl.when(k == 0)
    def _():
        st_scr[...] = s0_ref[...]

    tri = tri_ref[...]
    rep = rep_ref[...]
    dt = _softplus(dt_ref[...] + bias_ref[...])
    a = dt * aneg_ref[...]
    ac = _exact_left(tri, a)
    act = ac.T
    acx = _exact_right(ac, rep)
    dtx = _exact_right(dt, rep)
    totx = jnp.where(d == 0, acx[q - 1:q, :], acx[0:1, :])
    xd = xs_ref[...].astype(F32) * dtx
    xd_b = xd.astype(BF16)
    xe = (xd * jnp.exp(totx - acx)).astype(BF16)
    ein = jnp.exp(acx)
    keep = tri > 0
    lane = lax.broadcasted_iota(jnp.int32, (q, LANES), 1)
    lo = lane < HALF
    zero = jnp.zeros((q, LANES), BF16)
    hpg = SSM_HEADS // SSM_GROUPS
    for g in range(SSM_GROUPS):
        bg = bc_ref[:, g * SSM_STATE:(g + 1) * SSM_STATE]
        cg = bc_ref[:, (SSM_GROUPS + g) * SSM_STATE:(SSM_GROUPS + g + 1) * SSM_STATE]
        cb = _dot_nt(cg, bg)
        st = st_scr[g]
        yoff = _dot(cg, st.astype(BF16)) * ein[:, g * gw:(g + 1) * gw]
        for hp in range(hpg // 2):
            gs = []
            for par in range(2):
                h = g * hpg + 2 * hp + par
                seg = ac[:, h:h + 1] - act[h:h + 1, :]
                gs.append((cb * jnp.exp(jnp.where(keep, seg, NEG_BIG))).astype(BF16))
            c0 = g * gw + hp * LANES
            xp = xd_b[:, c0:c0 + LANES]
            xstack = jnp.concatenate([jnp.where(lo, xp, zero), jnp.where(lo, zero, xp)], axis=0)
            ydiag = _dot(jnp.concatenate(gs, axis=1), xstack)
            y_ref[:, c0:c0 + LANES] = (ydiag + yoff[:, hp * LANES:(hp + 1) * LANES]).astype(BF16)
        bgt = bg.astype(F32).T.astype(BF16)
        cs = _dot(bgt, xe[:, g * gw:(g + 1) * gw])
        st_scr[g] = st * jnp.exp(totx[:, g * gw:(g + 1) * gw]) + cs

    @pl.when(k == nc - 1)
    def _():
        sfin_ref[...] = st_scr[...]


def ssd_scan(u_xs, u_bc, dt_raw, bias2, aneg2, s0, nb, n):
    q = SSM_CHUNK
    nc = n // q
    gw = SSM_INNER // SSM_GROUPS
    idx = np.arange(q)
    tri = np.stack([idx[:, None] >= idx[None, :], idx[:, None] <= idx[None, :]]).astype(np.float32)
    rep = (np.arange(LANES)[:, None] == np.arange(SSM_INNER)[None, :] // SSM_P).astype(np.float32)

    def chunk(b, d, k):
        return b * nc + k + d * (nc - 1 - 2 * k)

    return pl.pallas_call(
        functools.partial(_ssd_kernel, nc=nc),
        grid=(nb, 2, nc),
        in_specs=[pl.BlockSpec((q, SSM_INNER), lambda b, d, k: (chunk(b, d, k), 0)),
                  pl.BlockSpec((q, 2 * SSM_GROUPS * SSM_STATE), lambda b, d, k: (chunk(b, d, k), 0)),
                  pl.BlockSpec((q, LANES), lambda b, d, k: (chunk(b, d, k), d)),
                  pl.BlockSpec((None, 1, LANES), lambda b, d, k: (d, 0, 0)),
                  pl.BlockSpec((None, 1, LANES), lambda b, d, k: (d, 0, 0)),
                  pl.BlockSpec((None, q, q), lambda b, d, k: (d, 0, 0)),
                  pl.BlockSpec((LANES, SSM_INNER), lambda b, d, k: (0, 0)),
                  pl.BlockSpec((None, None, SSM_GROUPS, SSM_STATE, gw), lambda b, d, k: (b, d, 0, 0, 0))],
        out_specs=[pl.BlockSpec((None, q, SSM_INNER), lambda b, d, k: (d, chunk(b, d, k), 0)),
                   pl.BlockSpec((None, None, SSM_GROUPS, SSM_STATE, gw), lambda b, d, k: (b, d, 0, 0, 0))],
        out_shape=[jax.ShapeDtypeStruct((2, nb * n, SSM_INNER), BF16),
                   jax.ShapeDtypeStruct((nb, 2, SSM_GROUPS, SSM_STATE, gw), F32)],
        scratch_shapes=[pltpu.VMEM((SSM_GROUPS, SSM_STATE, gw), F32)],
        compiler_params=_params(("parallel", "parallel", "arbitrary")),
        name="ssd_scan",
    )(u_xs, u_bc, dt_raw, bias2, aneg2, jnp.asarray(tri, BF16), jnp.asarray(rep, BF16), s0)


def _post_kernel(oa_ref, ow_ref, yf_ref, yb_ref, xs_ref, z_ref, gt_ref, x_ref,
                 dsk_ref, nw_ref, wa_ref, ww_ref, ws_ref, wo_ref, gpost_ref, gate_ref,
                 gpre_ref, sh_ref, sc_ref, wrh_ref, wrl_ref,
                 xo_ref, h_ref, lg_ref):
    y = yf_ref[...].astype(F32) + yb_ref[...].astype(F32) + dsk_ref[...] * xs_ref[...].astype(F32)
    u = y * _silu(z_ref[...].astype(F32))
    gw = SSM_INNER // SSM_GROUPS
    ys = jnp.concatenate([_rms(u[:, g * gw:(g + 1) * gw]) for g in range(SSM_GROUPS)], axis=1)
    ys = (ys * nw_ref[...]).astype(BF16)
    d = x_ref.shape[1]
    ga = jax.nn.sigmoid(gt_ref[:, 0:d].astype(F32))
    gw_ = jax.nn.sigmoid(gt_ref[:, d:2 * d].astype(F32))
    gs = jax.nn.sigmoid(gt_ref[:, 2 * d:3 * d].astype(F32))
    m = ga * _dot(oa_ref[...], wa_ref[...]) + gw_ * _dot(ow_ref[...], ww_ref[...]) + gs * _dot(ys, ws_ref[...])
    ml = _dot(m.astype(BF16), wo_ref[...])
    xn = x_ref[...] + gate_ref[...] * (_rms(ml) * gpost_ref[...])
    xo_ref[...] = xn
    h = (_rms(xn) * gpre_ref[...]) * (1.0 + sc_ref[...]) + sh_ref[...]
    _store_rows_as_slabs(h_ref, h)
    hb = h.astype(BF16)
    hl =(h - hb.astype(F32)).astype(BF16)
    lg_ref[...] = _dot_nt(wrh_ref[...], hb) + _dot_nt(wrh_ref[...], hl) + _dot_nt(wrl_ref[...], hb)


def post_mixer(oa, ow, y2, u_xs, p, x, dskip, norm_w, wa, ww, ws, wo, gpost, gate, gpre, shift, scale,
               wr_hi, wr_lo, rows_per_batch):
    t, d = x.shape
    tm = min(256, rows_per_batch)
    tpb = rows_per_batch // tm
    nt = t // tm
    hq = N_HEADS * HEAD_DIM
    row = lambda w, c=0: pl.BlockSpec((tm, w), lambda i: (i, c // w))
    const = lambda shp: pl.BlockSpec(shp, lambda i: (0,) * len(shp))
    mod = pl.BlockSpec((None, 1, d), lambda i: (i // tpb, 0, 0))
    return pl.pallas_call(
        _post_kernel,
        grid=(nt,),
        in_specs=[row(hq), row(hq),
                  pl.BlockSpec((None, tm, SSM_INNER), lambda i: (0, i, 0)),
                  pl.BlockSpec((None, tm, SSM_INNER), lambda i: (1, i, 0)),
                  row(SSM_INNER), row(SSM_INNER, C_Z), row(3 * d, C_GATES), row(d),
                  const((1, SSM_INNER)), const((1, SSM_INNER)),
                  const((hq, d)), const((hq, d)), const((SSM_INNER, d)), const((d, d)),
                  const((1, d)), mod, const((1, d)), mod, mod,
                  const((N_EXPERTS, d)), const((N_EXPERTS, d))],
        out_specs=[row(d), pl.BlockSpec((tm * SLAB, LANES), lambda i: (i, 0)),
                   pl.BlockSpec((N_EXPERTS, tm), lambda i: (0, i))],
        out_shape=[jax.ShapeDtypeStruct((t, d), F32), jax.ShapeDtypeStruct((t * SLAB, LANES), U32),
                   jax.ShapeDtypeStruct((N_EXPERTS, t), F32)],
        compiler_params=_params(("parallel",)),
        name="post_mixer",
    )(oa, ow, y2, y2, u_xs, p, p, x, dskip, norm_w, wa, ww, ws, wo, gpost, gate, gpre, shift, scale,
      wr_hi, wr_lo)


def _route_kernel(lg_ref, b_ref, ei_ref, w_ref):
    scores = jax.nn.sigmoid(lg_ref[...])
    sel = scores + b_ref[...]
    tt = sel.shape[1]
    per = N_EXPERTS // N_EXPERT_GROUPS
    r8 = lax.broadcasted_iota(jnp.int32, (per, tt), 0).astype(F32)
    ninf = -jnp.inf

    def argmax_rows(x, rows, nrows):
        m = jnp.max(x, axis=0, keepdims=True)
        idx = jnp.min(jnp.where(x == m, rows, float(nrows)), axis=0, keepdims=True)
        return m, idx

    gscores = []
    for g in range(N_EXPERT_GROUPS):
        blk = sel[g * per:(g + 1) * per, :]
        m1, i1 = argmax_rows(blk, r8, per)
        m2 = jnp.max(jnp.where(r8 == i1, ninf, blk), axis=0, keepdims=True)
        gscores.append(m1 + m2)
    cur = jnp.concatenate(gscores, axis=0)
    rg = lax.broadcasted_iota(jnp.int32, cur.shape, 0).astype(F32)
    chosen = jnp.zeros(cur.shape, F32)
    for _ in range(TOPK_GROUPS):
        _, gi = argmax_rows(cur, rg, N_EXPERT_GROUPS)
        hit = rg == gi
        chosen = jnp.where(hit, 1.0, chosen)
        cur = jnp.where(hit, ninf, cur)
    gmask = jnp.concatenate([jnp.broadcast_to(chosen[g:g + 1, :], (per, tt)) for g in range(N_EXPERT_GROUPS)],
                            axis=0)
    cur = jnp.where(gmask > 0, sel, ninf)
    re = lax.broadcasted_iota(jnp.int32, cur.shape, 0).astype(F32)
    idxs, ws = [], []
    for _ in range(TOP_K):
        _, ei = argmax_rows(cur, re, N_EXPERTS)
        hit = re == ei
        ws.append(jnp.sum(jnp.where(hit, scores, 0.0), axis=0, keepdims=True))
        idxs.append(ei)
        cur = jnp.where(hit, ninf, cur)
    w = jnp.concatenate(ws, axis=0)
    w_ref[...] = w / jnp.sum(w, axis=0, keepdims=True) * ROUTED_SCALE
    ei_ref[...] = jnp.concatenate(idxs, axis=0).astype(jnp.int32)


def route(logits_t, b_router):
    e, t = logits_t.shape
    tt = 512
    return pl.pallas_call(
        _route_kernel,
        grid=(t // tt,),
        in_specs=[pl.BlockSpec((e, tt), lambda i: (0, i)), pl.BlockSpec((e, 1), lambda i: (0, 0))],
        out_specs=[pl.BlockSpec((TOP_K, tt), lambda i: (0, i)), pl.BlockSpec((TOP_K, tt), lambda i: (0, i))],
        out_shape=[jax.ShapeDtypeStruct((TOP_K, t), jnp.int32), jax.ShapeDtypeStruct((TOP_K, t), F32)],
        compiler_params=_params(("parallel",)),
        name="route",
    )(logits_t, b_router.reshape(e, 1))


MOE_ROWS = 512
PLAN_TOKENS = 512


def _moe_geometry(t):
    nblk = -(-(t * TOP_K + N_EXPERTS * (MOE_ROWS - 1)) // MOE_ROWS)
    return nblk, nblk * MOE_ROWS


def _plan_kernel(ei_ref, ut_ref, tril_ref, dest_ref, tab_ref, be_ref, cnt_scr, run_scr):
    ph = pl.program_id(0)
    i = pl.program_id(1)
    ei = ei_ref[...]
    tt = ei.shape[1]
    re = lax.broadcasted_iota(jnp.int32, (N_EXPERTS, tt), 0)
    hits = [re == ei[k:k + 1, :] for k in range(TOP_K)]
    oh = jnp.zeros((N_EXPERTS, tt), F32)
    for k in range(TOP_K):
        oh = oh + jnp.where(hits[k], 1.0, 0.0)

    @pl.when((ph == 0) & (i == 0))
    def _():
        cnt_scr[...] = jnp.zeros(cnt_scr.shape, F32)

    @pl.when(ph == 0)
    def _():
        cnt_scr[...] = cnt_scr[...] + jnp.sum(oh, axis=1, keepdims=True)

    @pl.when((ph == 1) & (i == 0))
    def _():
        cnt = cnt_scr[...]
        padded = ((cnt.astype(jnp.int32) + (MOE_ROWS - 1)) & (-MOE_ROWS)).astype(F32)
        pad_end = _exact_left(tril_ref[...], padded)
        pad_start = pad_end - padded
        run_scr[...] = pad_start
        tab_ref[0] = pad_start
        tab_ref[1] = pad_end
        nbp = be_ref.shape[1]
        blk0 = (lax.broadcasted_iota(jnp.int32, (N_EXPERTS, nbp), 1) * MOE_ROWS).astype(F32)
        be = jnp.sum(jnp.where(pad_end[:, :1] <= blk0, 1.0, 0.0), axis=0, keepdims=True)
        be_ref[...] = jnp.broadcast_to(jnp.minimum(be, N_EXPERTS - 1.0), be_ref.shape).astype(jnp.int32)

    @pl.when(ph == 1)
    def _():
        cin = _dot(oh.astype(BF16), ut_ref[...])
        pos = run_scr[:, :1] + (cin - oh)
        rows = [jnp.sum(jnp.where(hits[k], pos, 0.0), axis=0, keepdims=True) for k in range(TOP_K)]
        dest_ref[...] = jnp.concatenate(rows, axis=0).astype(jnp.int32)
        run_scr[...] = run_scr[...] + cin[:, tt - 1:tt]


def moe_plan(eidx_t):
    k, t = eidx_t.shape
    tt = PLAN_TOKENS
    nt = t // tt
    nblk, _ = _moe_geometry(t)
    nbp = -(-nblk // LANES) * LANES
    ut = jnp.asarray(np.arange(tt)[:, None] <= np.arange(tt)[None, :], BF16)
    tril = jnp.asarray(np.arange(N_EXPERTS)[:, None] >= np.arange(N_EXPERTS)[None, :], BF16)
    return pl.pallas_call(
        _plan_kernel,
        grid=(2, nt),
        in_specs=[pl.BlockSpec((k, tt), lambda p, i: (0, i)),
                  pl.BlockSpec((tt, tt), lambda p, i: (0, 0)),
                  pl.BlockSpec((N_EXPERTS, N_EXPERTS), lambda p, i: (0, 0))],
        out_specs=[pl.BlockSpec((k, tt), lambda p, i: (0, i * p)),
                   pl.BlockSpec((2, N_EXPERTS, LANES), lambda p, i: (0, 0, 0)),
                   pl.BlockSpec((8, nbp), lambda p, i: (0, 0))],
        out_shape=[jax.ShapeDtypeStruct((k, t), jnp.int32),
                   jax.ShapeDtypeStruct((2, N_EXPERTS, LANES), F32),
                   jax.ShapeDtypeStruct((8, nbp), jnp.int32)],
        scratch_shapes=[pltpu.VMEM((N_EXPERTS, LANES), F32), pltpu.VMEM((N_EXPERTS, LANES), F32)],
        compiler_params=_params(("arbitrary", "arbitrary")),
        name="moe_plan",
    )(eidx_t, ut, tril)


def _dispatch_kernel(dest_ref, pstart_ref, pend_ref, h_ref, xs_hbm, zero_scr, sem):
    i = pl.program_id(0)
    tt = h_ref.shape[0] // SLAB

    def slab(ref, row, n=1):
        return ref.at[pl.ds(pl.multiple_of(row * SLAB, SLAB), n * SLAB), :]

    @pl.when(i == 0)
    def _():
        zero_scr[...] = jnp.zeros(zero_scr.shape, U32)
        nblk = xs_hbm.shape[0] // (MOE_ROWS * SLAB)
        n_used = pend_ref[N_EXPERTS - 1] // MOE_ROWS

        def zero_block(row0):
            return pltpu.make_async_copy(zero_scr, slab(xs_hbm, row0, MOE_ROWS), sem)

        def seg_start(e, c):
            @pl.when(pend_ref[e] > pstart_ref[e])
            def _():
                zero_block(pend_ref[e] - MOE_ROWS).start()
            return c

        def seg_wait(e, c):
            @pl.when(pend_ref[e] > pstart_ref[e])
            def _():
                zero_block(pend_ref[e] - MOE_ROWS).wait()
            return c

        def tail_start(b, c):
            zero_block(b * MOE_ROWS).start()
            return c

        def tail_wait(b, c):
            zero_block(b * MOE_ROWS).wait()
            return c

        lax.fori_loop(0, N_EXPERTS, seg_start, 0)
        lax.fori_loop(n_used, nblk, tail_start, 0)
        lax.fori_loop(0, N_EXPERTS, seg_wait, 0)
        lax.fori_loop(n_used, nblk, tail_wait, 0)

    def issue(t, c):
        for k in range(TOP_K):
            pltpu.make_async_copy(slab(h_ref, t), slab(xs_hbm, dest_ref[k, t]), sem).start(priority=k % 2)
        return c

    lax.fori_loop(0, tt, issue, 0)
    for k in range(TOP_K):
        pltpu.make_async_copy(h_ref, slab(xs_hbm, 0, tt), sem).wait()


def moe_dispatch(dest_t, pad_start, pad_end, h):
    t = h.shape[0] // SLAB
    tt = PLAN_TOKENS
    _, cap = _moe_geometry(t)
    smem = pl.BlockSpec(memory_space=pltpu.SMEM)
    return pl.pallas_call(
        _dispatch_kernel,
        grid=(t // tt,),
        in_specs=[pl.BlockSpec((TOP_K, tt), lambda i: (0, i), memory_space=pltpu.SMEM), smem, smem,
                  pl.BlockSpec((tt * SLAB, LANES), lambda i: (i, 0))],
        out_specs=pl.BlockSpec(memory_space=pl.ANY),
        out_shape=jax.ShapeDtypeStruct((cap * SLAB, LANES), U32),
        scratch_shapes=[pltpu.VMEM((MOE_ROWS * SLAB, LANES), U32), pltpu.SemaphoreType.DMA(())],
        compiler_params=_params(("arbitrary",)),
        name="moe_dispatch",
    )(dest_t, pad_start, pad_end, h)


def _expert_kernel(be_ref, nu_ref, x_ref, wg_ref, wu_ref, wd_ref, o_ref):
    used = pl.program_id(0) < nu_ref[0]

    @pl.when(used)
    def _():
        x = _load_slabs_as_rows(x_ref).astype(BF16)
        hg = _dot(x, wg_ref[...].astype(BF16))
        hu = _dot(x, wu_ref[...].astype(BF16))
        hb = (_silu(hg) * hu).astype(BF16)
        _store_rows_as_slabs(o_ref, _dot(hb, wd_ref[...].astype(BF16)))

    @pl.when(jnp.logical_not(used))
    def _():
        o_ref[...] = jnp.zeros(o_ref.shape, U32)


def expert_ffn(blk_e, n_used, xs, we_gate, we_up, we_down):
    d, de = we_gate.shape[1:]
    nblk = xs.shape[0] // (MOE_ROWS * SLAB)
    blk = lambda i, nu: jnp.minimum(i, nu[0] - 1)
    grid_spec = pltpu.PrefetchScalarGridSpec(
        num_scalar_prefetch=2,
        grid=(nblk,),
        in_specs=[pl.BlockSpec((MOE_ROWS * SLAB, LANES), lambda i, be, nu: (blk(i, nu), 0)),
                  pl.BlockSpec((None, d, de), lambda i, be, nu: (be[blk(i, nu)], 0, 0)),
                  pl.BlockSpec((None, d, de), lambda i, be, nu: (be[blk(i, nu)], 0, 0)),
                  pl.BlockSpec((None, de, d), lambda i, be, nu: (be[blk(i, nu)], 0, 0))],
        out_specs=pl.BlockSpec((MOE_ROWS * SLAB, LANES), lambda i, be, nu: (i, 0)),
    )
    return pl.pallas_call(
        _expert_kernel,
        grid_spec=grid_spec,
        out_shape=jax.ShapeDtypeStruct(xs.shape, U32),
        compiler_params=_params(("arbitrary",)),
        name="expert_ffn",
    )(blk_e, n_used, xs, we_gate, we_up, we_down)


def _ffn_tail_kernel(dest_ref, dnext_ref, w_ref, h_ref, x_ref, wg_ref, wu_ref, wd_ref, gpost_ref, gate_ref, ys_hbm,
                     xo_ref, buf, sem, *, nt):
    i = pl.program_id(0)
    tt = h_ref.shape[0] // SLAB
    cur = i % 2

    def slab(ref, row, n=1):
        return ref.at[pl.ds(pl.multiple_of(row * SLAB, SLAB), n * SLAB), :]

    def gather(idx_ref, slot):
        def issue(t, c):
            for k in range(TOP_K):
                pltpu.make_async_copy(slab(ys_hbm, idx_ref[k, t]), slab(buf.at[slot, k], t),
                                      sem.at[slot]).start(priority=k % 2)
            return c
        lax.fori_loop(0, tt, issue, 0)

    @pl.when(i == 0)
    def _():
        gather(dest_ref, 0)

    @pl.when(i + 1 < nt)
    def _():
        gather(dnext_ref, 1 - cur)

    h = _load_slabs_as_rows(h_ref).astype(BF16)
    hs = (_silu(_dot(h, wg_ref[...])) * _dot(h, wu_ref[...])).astype(BF16)
    f = _dot(hs, wd_ref[...])
    w = w_ref[...]
    for k in range(TOP_K):
        pltpu.make_async_copy(slab(ys_hbm, 0, tt), buf.at[cur, k], sem.at[cur]).wait()
    for k in range(TOP_K):
        f = f + _load_slabs_as_rows(buf.at[cur, k]) * w[:, k:k + 1]
    xo_ref[...] = x_ref[...] + gate_ref[...] * (_rms(f) * gpost_ref[...])


def ffn_tail(dest_t, wts, h, x, ys, wsg, wsu, wsd, gpost, gate, rows_per_batch):
    t, d = x.shape
    ds = wsg.shape[1]
    tm = min(256, rows_per_batch)
    tpb = rows_per_batch // tm
    nt = t // tm
    row = pl.BlockSpec((tm, d), lambda i: (i, 0))
    const = lambda shp: pl.BlockSpec(shp, lambda i: (0,) * len(shp))
    return pl.pallas_call(
        functools.partial(_ffn_tail_kernel, nt=nt),
        grid=(nt,),
        in_specs=[pl.BlockSpec((TOP_K, tm), lambda i: (0, i), memory_space=pltpu.SMEM),
                  pl.BlockSpec((TOP_K, tm), lambda i: (0, jnp.minimum(i + 1, nt - 1)), memory_space=pltpu.SMEM),
                  pl.BlockSpec((tm, TOP_K), lambda i: (i, 0)),
                  pl.BlockSpec((tm * SLAB, LANES), lambda i: (i, 0)),
                  row, const((d, ds)), const((d, ds)), const((ds, d)), const((1, d)),
                  pl.BlockSpec((None, 1, d), lambda i: (i // tpb, 0, 0)),
                  pl.BlockSpec(memory_space=pl.ANY)],
        out_specs=row,
        out_shape=jax.ShapeDtypeStruct((t, d), F32),
        scratch_shapes=[pltpu.VMEM((2, TOP_K, tm * SLAB, LANES), U32), pltpu.SemaphoreType.DMA((2,))],
        compiler_params=_params(("arbitrary",)),
        name="ffn_tail",
    )(dest_t, dest_t, wts, h, x, wsg, wsu, wsd, gpost, gate, ys)


def _moe_routed(h, logits_t, b_router, we_gate, we_up, we_down):
    eidx_t, wts_t = route(logits_t, b_router)
    dest_t, tabs, blk_e = moe_plan(eidx_t)
    pad_start = tabs[0, :, 0].astype(jnp.int32)
    pad_end = tabs[1, :, 0].astype(jnp.int32)
    n_used = (pad_end[N_EXPERTS - 1:] // MOE_ROWS).astype(jnp.int32)
    xs = moe_dispatch(dest_t, pad_start, pad_end, h)
    ys = expert_ffn(blk_e[0], n_used, xs, we_gate, we_up, we_down)
    return dest_t, wts_t.T, ys


def _reorder_w_in(w):
    qa, ka, va, qw, kw, vw, z, xs, bm, cm, dt, gates = jnp.split(
        w, [512, 640, 768, 1280, 1408, 1536, 2560, 3584, 3840, 4096, 4128], axis=1)
    w_main = jnp.concatenate([qa, qw, z, xs, gates, ka, va, kw, vw, bm, cm], axis=1).astype(BF16)
    pad = jnp.zeros((w.shape[0], LANES - SSM_HEADS), w.dtype)
    w_dt = jnp.concatenate([dt[:, :SSM_HEADS], pad, dt[:, SSM_HEADS:], pad], axis=1).astype(BF16)
    return w_main, w_dt


def _pad_lanes(v):
    return jnp.pad(v, ((0, 0), (0, LANES - v.shape[1])))


def kernel(x, c, ctx, c_ctx, w_ada, b_ada, g_mix_pre, g_mix_post, g_ffn_pre, g_ffn_post, w_in, g_q_a, g_k_a, sink_w, ssm_conv_w, ssm_conv_b, ssm_dt_bias, ssm_a_log, ssm_d, ssm_norm, w_br_a, w_br_w, w_br_s, w_out, w_router, b_router, we_gate, we_up, we_down, ws_gate, ws_up, ws_down):
    nb, n, d = x.shape
    mc = ctx.shape[1]
    depth = w_in.shape[0]
    t_lat, t_ctx = nb * n, nb * mc
    cos, sin = rope_tables(n)
    xl = x.reshape(t_lat, d)
    xc = ctx.reshape(t_ctx, d)
    c8 = jnp.concatenate([c, c_ctx[None, :], jnp.zeros((8 - nb - 1, d), F32)], axis=0)
    zeros_sink = jnp.zeros((N_HEADS,), F32)
    s_zero = jnp.zeros((nb, 2, SSM_GROUPS, SSM_STATE, SSM_INNER // SSM_GROUPS), F32)
    dummy_tab = jnp.zeros((mc, N_HEADS * HEAD_DIM), F32)

    for i in range(depth):
        last = i == depth - 1
        mod = ada_mod(c8, w_ada[i], b_ada[i])
        mod_l = [mod[:nb, k * d:(k + 1) * d].reshape(nb, 1, d) for k in range(6)]
        mod_c = [mod[nb:nb + 1, k * d:(k + 1) * d].reshape(1, 1, d) for k in range(6)]
        w_main, w_dt = _reorder_w_in(w_in[i])
        gq = jnp.tile(g_q_a[i], N_HEADS)[None, :]
        gk = jnp.tile(g_k_a[i], N_KV)[None, :]
        bias2 = _pad_lanes(ssm_dt_bias[i].reshape(2, SSM_HEADS)).reshape(2, 1, LANES)
        aneg2 = _pad_lanes(-jnp.exp(ssm_a_log[i].astype(F32))).reshape(2, 1, LANES)
        dskip = jnp.repeat(ssm_d[i], SSM_P)[None, :]
        norm_w = ssm_norm[i][None, :]
        wa, ww, ws, wo = (w_br_a[i].astype(BF16), w_br_w[i].astype(BF16), w_br_s[i].astype(BF16),
                          w_out[i].astype(BF16))
        wr_t = w_router[i].T
        wr_hi = wr_t.astype(BF16)
        wr_lo = (wr_t - wr_hi.astype(F32)).astype(BF16)
        sink = sink_w[i].astype(F32)

        p_c, dt_c = in_proj(xc, g_mix_pre[i], mod_c[0], mod_c[1], w_main, w_dt, t_ctx)
        qa_c, qw_c, kda_c, vda_c, kdw_c, vdw_c = attn_prep(p_c, dummy_tab, dummy_tab, gq, gk, nb, mc, rope=False)
        uxs_c, ubc_c = ssm_conv(p_c, ssm_conv_w[i], ssm_conv_b[i], nb, mc)
        y_c, s_fin = ssd_scan(uxs_c, ubc_c, dt_c, bias2, aneg2, s_zero, nb, mc)

        p_l, dt_l = in_proj(xl, g_mix_pre[i], mod_l[0], mod_l[1], w_main, w_dt, n)
        qa, qw, kda, vda, kdw, vdw = attn_prep(p_l, cos, sin, gq, gk, nb, n, rope=True)
        m_all = n + mc
        kd_all = jnp.concatenate([kda.reshape(nb, n, -1), kda_c.reshape(nb, mc, -1)], axis=1).reshape(nb * m_all, -1)
        vd_all = jnp.concatenate([vda.reshape(nb, n, -1), vda_c.reshape(nb, mc, -1)], axis=1).reshape(nb * m_all, -1)
        score_bound = (math.sqrt(HEAD_DIM) * jnp.max(jnp.abs(g_q_a[i])) * jnp.max(jnp.abs(g_k_a[i]))).reshape(1)
        oa = flash_attn_bounded(qa, kd_all, vd_all, score_bound.astype(F32), nb, n, m_all)
        ow = window_attn(qw, kdw, vdw, kdw_c, vdw_c, sink, nb, n, mc)
        uxs, ubc = ssm_conv(p_l, ssm_conv_w[i], ssm_conv_b[i], nb, n)
        y_l, _ = ssd_scan(uxs, ubc, dt_l, bias2, aneg2, s_fin, nb, n)
        xl, h_l, lg_l = post_mixer(oa, ow, y_l, uxs, p_l, xl, dskip, norm_w, wa, ww, ws, wo,
                                   g_mix_post[i][None, :], mod_l[2], g_ffn_pre[i][None, :], mod_l[3], mod_l[4],
                                   wr_hi, wr_lo, n)
        wsg, wsu, wsd = ws_gate[i].astype(BF16), ws_up[i].astype(BF16), ws_down[i].astype(BF16)
        weg, weu, wed = we_gate[i], we_up[i], we_down[i]
        if last:
            dest_t, wts, ys = _moe_routed(h_l, lg_l, b_router[i], weg, weu, wed)
            xl = ffn_tail(dest_t, wts, h_l, xl, ys, wsg, wsu, wsd, g_ffn_post[i][None, :], mod_l[5], n)
        else:
            oa_c = flash_attn(qa_c, kda_c, vda_c, zeros_sink, nb, mc, mc, has_sink=False)
            ow_c = flash_attn(qw_c, kdw_c, vdw_c, sink, nb, mc, mc, has_sink=True)
            xc, h_c, lg_c = post_mixer(oa_c, ow_c, y_c, uxs_c, p_c, xc, dskip, norm_w, wa, ww, ws, wo,
                                       g_mix_post[i][None, :], mod_c[2], g_ffn_pre[i][None, :], mod_c[3], mod_c[4],
                                       wr_hi, wr_lo, t_ctx)
            h_all = jnp.concatenate([h_l, h_c], axis=0)
            lg_all = jnp.concatenate([lg_l, lg_c], axis=1)
            dest_t, wts, ys = _moe_routed(h_all, lg_all, b_router[i], weg, weu, wed)
            xl = ffn_tail(dest_t[:, :t_lat], wts[:t_lat], h_l, xl, ys, wsg, wsu, wsd,
                          g_ffn_post[i][None, :], mod_l[5], n)
            xc = ffn_tail(dest_t[:, t_lat:], wts[t_lat:], h_c, xc, ys, wsg, wsu, wsd,
                          g_ffn_post[i][None, :], mod_c[5], t_ctx)
    return xl.reshape(nb, n, d)
```

```python
import functools
import math

import jax
import jax.numpy as jnp
import numpy as np
from jax import lax
from jax.experimental import pallas as pl
from jax.experimental.pallas import tpu as pltpu

F32 = jnp.float32
BF16 = jnp.bfloat16

HEAD_DIM = 64
N_HEADS = 8
N_KV = 2
GRID_W = 64
ROPE_THETA = 10000.0
WINDOW = 128
SSM_HEADS = 16
SSM_P = 64
SSM_INNER = SSM_HEADS * SSM_P
SSM_GROUPS = 2
SSM_STATE = 128
SSM_CHUNK = 128
N_EXPERTS = 64
TOP_K = 8
N_EXPERT_GROUPS = 8
TOPK_GROUPS = 4
ROUTED_SCALE = 2.5
EPS = 1e-6

LANES = 128
HALF = LANES // 2
VMEM_LIMIT = 56 * 1024 * 1024
NEG_BIG = -1e30

C_QA, C_QW, C_Z, C_XS, C_GATES = 0, 512, 1024, 2048, 3072
C_KA, C_VA, C_KW, C_VW, C_BC = 6144, 6272, 6400, 6528, 6656
P_WIDTH = 7168


def _params(sem, vmem=VMEM_LIMIT):
    return pltpu.CompilerParams(dimension_semantics=sem, vmem_limit_bytes=vmem)


def _silu(x):
    return x * jax.nn.sigmoid(x)


U32 = jnp.uint32
SLAB = 4
HI_MASK = 0xFFFF0000


def _store_rows_as_slabs(ref, x):
    r = x.shape[0]

    def bits(v):
        return pltpu.bitcast(v.astype(BF16).astype(F32), U32)

    for c in range(SLAB):
        lo = bits(x[:, c * LANES:(c + 1) * LANES]) >> 16
        hi = bits(x[:, (SLAB + c) * LANES:(SLAB + c + 1) * LANES]) & jnp.uint32(HI_MASK)
        ref[pl.ds(c, r, stride=SLAB), :] = hi | lo


def _load_slabs_as_rows(ref):
    r = ref.shape[0] // SLAB
    words = [ref[pl.ds(c, r, stride=SLAB), :] for c in range(SLAB)]
    los = [pltpu.bitcast(w << 16, F32) for w in words]
    his = [pltpu.bitcast(w & jnp.uint32(HI_MASK), F32) for w in words]
    return jnp.concatenate(los + his, axis=1)


def _softplus(x):
    return jnp.maximum(x, 0.0) + jnp.log(1.0 + jnp.exp(-jnp.abs(x)))


def _rms(x, eps=EPS):
    return x * lax.rsqrt(jnp.mean(x * x, axis=-1, keepdims=True) + eps)


def _split3(a):
    a1 = a.astype(BF16)
    r1 = a - a1.astype(F32)
    a2 = r1.astype(BF16)
    a3 = (r1 - a2.astype(F32)).astype(BF16)
    return a1, a2, a3


def _dot(a, b):
    return jnp.dot(a, b, preferred_element_type=F32)


def _dot_nt(a, b):
    return lax.dot_general(a, b, (((1,), (1,)), ((), ())), preferred_element_type=F32)


def _exact_right(a, r01, pieces=3):
    return sum(_dot(p, r01) for p in _split3(a)[:pieces])


def _exact_left(m01, a, pieces=3):
    return sum(_dot(m01, p) for p in _split3(a)[:pieces])


def _ada_kernel(c_ref, w_ref, b_ref, o_ref):
    h = _silu(c_ref[...])
    o_ref[...] = jnp.dot(h, w_ref[...], preferred_element_type=F32,
                         precision=lax.Precision.HIGHEST) + b_ref[...]


def ada_mod(c8, w, b):
    d, n = w.shape
    tn = 1536
    return pl.pallas_call(
        _ada_kernel,
        grid=(n // tn,),
        in_specs=[pl.BlockSpec((8, d), lambda j: (0, 0)),
                  pl.BlockSpec((d, tn), lambda j: (0, j)),
                  pl.BlockSpec((1, tn), lambda j: (0, j))],
        out_specs=pl.BlockSpec((8, tn), lambda j: (0, j)),
        out_shape=jax.ShapeDtypeStruct((8, n), F32),
        compiler_params=_params(("parallel",)),
        name="ada_mod",
    )(c8, w, b.reshape(1, n))


def _inproj_kernel(x_ref, g_ref, sh_ref, sc_ref, w_ref, wdt_ref, o_ref, odt_ref, h_scr):
    @pl.when(pl.program_id(1) == 0)
    def _():
        h = _rms(x_ref[...]) * g_ref[...]
        h = h * (1.0 + sc_ref[...]) + sh_ref[...]
        hb = h.astype(BF16)
        h_scr[...] = hb
        odt_ref[...] = _dot(hb, wdt_ref[...])

    o_ref[...] = _dot(h_scr[...], w_ref[...]).astype(BF16)


def in_proj(x, g, shift, scale, w_main, w_dt, rows_per_batch):
    t, d = x.shape
    n = w_main.shape[1]
    tm = min(1024, rows_per_batch)
    tn = n // 4
    tpb = rows_per_batch // tm
    mod_spec = pl.BlockSpec((None, 1, d), lambda i, j: (i // tpb, 0, 0))
    return pl.pallas_call(
        _inproj_kernel,
        grid=(t // tm, n // tn),
        in_specs=[pl.BlockSpec((tm, d), lambda i, j: (i, 0)),
                  pl.BlockSpec((1, d), lambda i, j: (0, 0)),
                  mod_spec, mod_spec,
                  pl.BlockSpec((d, tn), lambda i, j: (0, j)),
                  pl.BlockSpec((d, 2 * LANES), lambda i, j: (0, 0))],
        out_specs=[pl.BlockSpec((tm, tn), lambda i, j: (i, j)),
                   pl.BlockSpec((tm, 2 * LANES), lambda i, j: (i, 0))],
        out_shape=[jax.ShapeDtypeStruct((t, n), BF16),
                   jax.ShapeDtypeStruct((t, 2 * LANES), F32)],
        scratch_shapes=[pltpu.VMEM((tm, d), BF16)],
        compiler_params=_params(("parallel", "arbitrary")),
        name="in_proj",
    )(x, g.reshape(1, d), shift, scale, w_main, w_dt)


def _rope(x, cos, sin):
    w = x.shape[-1]
    lane = lax.broadcasted_iota(jnp.int32, x.shape, 1)
    first = (lane % 32) < 16
    swapped = jnp.where(first, pltpu.roll(x, w - 16, 1), pltpu.roll(x, 16, 1))
    return x * cos + swapped * sin


def _dup_halves(x):
    lane = lax.broadcasted_iota(jnp.int32, x.shape, 1)
    lo = lane < HALF
    r = pltpu.roll(x, HALF, 1)
    return jnp.concatenate([jnp.where(lo, x, r), jnp.where(lo, r, x)], axis=1)


def _prep_kernel(qa_ref, qw_ref, ka_ref, va_ref, kw_ref, vw_ref, cos_ref, sin_ref,
                 gq_ref, gk_ref, bdq_ref, bdk_ref,
                 qa_o, qw_o, kda_o, vda_o, kdw_o, vdw_o, *, rope):
    scale = HEAD_DIM ** -0.5
    inv_hd = 1.0 / HEAD_DIM

    def headnorm(x, g, bd):
        ss = _dot((x * x).astype(BF16), bd) * inv_hd
        return x * lax.rsqrt(ss + EPS) * g

    qa = headnorm(qa_ref[...].astype(F32), gq_ref[...], bdq_ref[...])
    ka = headnorm(ka_ref[...].astype(F32), gk_ref[...], bdk_ref[...])
    qw = qw_ref[...].astype(F32)
    kw = kw_ref[...].astype(F32)
    if rope:
        cos = cos_ref[...]
        sin = sin_ref[...]
        qa = _rope(qa, cos, sin)
        qw = _rope(qw, cos, sin)
        ka = _rope(ka, cos[:, :LANES], sin[:, :LANES])
        kw = _rope(kw, cos[:, :LANES], sin[:, :LANES])
    qa_o[...] = (qa * scale).astype(BF16)
    qw_o[...] = (qw * scale).astype(BF16)
    kda_o[...] = _dup_halves(ka).astype(BF16)
    kdw_o[...] = _dup_halves(kw).astype(BF16)
    vda_o[...] = _dup_halves(va_ref[...].astype(F32)).astype(BF16)
    vdw_o[...] = _dup_halves(vw_ref[...].astype(F32)).astype(BF16)


def attn_prep(p, cos, sin, gq, gk, nb, n, rope):
    t = nb * n
    tm = min(512, n)
    spb = n // tm
    hq = N_HEADS * HEAD_DIM
    hk = N_KV * HEAD_DIM
    bdq = (np.arange(hq)[:, None] // HEAD_DIM == np.arange(hq)[None, :] // HEAD_DIM)
    bdq = jnp.asarray(bdq, BF16)
    bdk = bdq[:hk, :hk]
    qspec = lambda c: pl.BlockSpec((tm, hq), lambda s, b: (b * spb + s, c // hq))
    kspec = lambda c: pl.BlockSpec((tm, hk), lambda s, b: (b * spb + s, c // hk))
    tab = pl.BlockSpec((tm, hq), lambda s, b: (s, 0))
    const = lambda shp: pl.BlockSpec(shp, lambda s, b: (0, 0))
    oq = pl.BlockSpec((tm, hq), lambda s, b: (b * spb + s, 0))
    ok = pl.BlockSpec((tm, 2 * hk), lambda s, b: (b * spb + s, 0))
    return pl.pallas_call(
        functools.partial(_prep_kernel, rope=rope),
        grid=(spb, nb),
        in_specs=[qspec(C_QA), qspec(C_QW), kspec(C_KA), kspec(C_VA), kspec(C_KW), kspec(C_VW),
                  tab, tab, const((1, hq)), const((1, hk)), const((hq, hq)), const((hk, hk))],
        out_specs=[oq, oq, ok, ok, ok, ok],
        out_shape=[jax.ShapeDtypeStruct((t, hq), BF16)] * 2 + [jax.ShapeDtypeStruct((t, 2 * hk), BF16)] * 4,
        compiler_params=_params(("parallel", "arbitrary")),
        name="attn_prep",
    )(p, p, p, p, p, p, cos, sin, gq, gk, bdq, bdk)


def rope_tables(n):
    rows = n // GRID_W
    row = jnp.repeat(jnp.arange(rows, dtype=F32), GRID_W)
    col = jnp.tile(jnp.arange(GRID_W, dtype=F32), rows)
    axis_dim = HEAD_DIM // 2
    inv_freq = ROPE_THETA ** (-jnp.arange(0, axis_dim, 2, dtype=F32) / axis_dim)
    ang_r = row[:, None] * inv_freq[None, :]
    ang_c = col[:, None] * inv_freq[None, :]
    cr, sr, cc, sc = jnp.cos(ang_r), jnp.sin(ang_r), jnp.cos(ang_c), jnp.sin(ang_c)
    cos = jnp.concatenate([cr, cr, cc, cc], axis=1)
    sin = jnp.concatenate([-sr, sr, -sc, sc], axis=1)
    return jnp.tile(cos, (1, N_HEADS)), jnp.tile(sin, (1, N_HEADS))


def _pair_operands(kd, vd):
    lane = lax.broadcasted_iota(jnp.int32, kd.shape, 1)
    lo = lane < HALF
    zero = jnp.zeros_like(kd)
    kmats = (jnp.where(lo, kd, zero), jnp.where(lo, zero, kd))
    vstack = jnp.concatenate([jnp.where(lo, vd, zero), jnp.where(lo, zero, vd)], axis=0)
    return kmats, vstack


def _flash_kernel(sink_ref, q_ref, k_ref, v_ref, o_ref, m_scr, l_scr, acc_scr, *, has_sink, nk):
    ki = pl.program_id(2)
    tq = q_ref.shape[0]

    @pl.when(ki == 0)
    def _():
        m_scr[...] = jnp.full(m_scr.shape, NEG_BIG, F32)
        l_scr[...] = jnp.zeros(l_scr.shape, F32)
        acc_scr[...] = jnp.zeros(acc_scr.shape, F32)

    lane_q = lax.broadcasted_iota(jnp.int32, (tq, LANES), 1)
    lo_q = lane_q < HALF
    pairs_per_kv = N_HEADS // N_KV // 2
    for j in range(N_KV):
        kmats, vstack = _pair_operands(k_ref[:, j * LANES:(j + 1) * LANES],
                                       v_ref[:, j * LANES:(j + 1) * LANES])
        for pp in range(pairs_per_kv):
            hp = j * pairs_per_kv + pp
            qp = q_ref[:, hp * LANES:(hp + 1) * LANES]
            ps, alphas = [], []
            for par in range(2):
                h = 2 * hp + par
                s = _dot_nt(qp, kmats[par])
                m_prev = m_scr[h]
                m_new = jnp.maximum(m_prev, jnp.max(s, axis=1, keepdims=True))
                alpha = jnp.exp(m_prev - m_new)
                p = jnp.exp(s - m_new[:, :1])
                l_scr[h] = alpha * l_scr[h] + jnp.sum(p, axis=1, keepdims=True)
                m_scr[h] = m_new
                ps.append(p.astype(BF16))
                alphas.append(alpha)
            pv = _dot(jnp.concatenate(ps, axis=1), vstack)
            sl = slice(hp * LANES, (hp + 1) * LANES)
            acc_scr[:, sl] = acc_scr[:, sl] * jnp.where(lo_q, alphas[0], alphas[1]) + pv

    @pl.when(ki == nk - 1)
    def _():
        for hp in range(N_HEADS // 2):
            ls = []
            for par in range(2):
                h = 2 * hp + par
                l = l_scr[h]
                if has_sink:
                    l = l + jnp.exp(sink_ref[h] - m_scr[h])
                ls.append(l)
            sl = slice(hp * LANES, (hp + 1) * LANES)
            o_ref[:, sl] = (acc_scr[:, sl] / jnp.where(lo_q, ls[0], ls[1])).astype(BF16)


def _flash_bounded_kernel(c_ref, q_ref, k_ref, v_ref, o_ref, lmin_ref, l_scr, acc_scr, *, nk):
    ki = pl.program_id(2)
    tq = q_ref.shape[0]
    tk = k_ref.shape[0]

    @pl.when(ki == 0)
    def _():
        l_scr[...] = jnp.zeros(l_scr.shape, F32)
        acc_scr[...] = jnp.zeros(acc_scr.shape, F32)

    c = c_ref[0]
    pairs_per_kv = N_HEADS // N_KV // 2
    for j in range(N_KV):
        kmats, vstack = _pair_operands(k_ref[:, j * LANES:(j + 1) * LANES],
                                       v_ref[:, j * LANES:(j + 1) * LANES])
        for pp in range(pairs_per_kv):
            hp = j * pairs_per_kv + pp
            qp = q_ref[:, hp * LANES:(hp + 1) * LANES]
            ps = []
            for par in range(2):
                h = 2 * hp + par
                p = jnp.exp(_dot_nt(qp, kmats[par]) - c)
                part = p[:, 0:LANES]
                for cb in range(1, tk // LANES):
                    part = part + p[:, cb * LANES:(cb + 1) * LANES]
                l_scr[h] = l_scr[h] + part
                ps.append(p.astype(BF16))
            sl = slice(hp * LANES, (hp + 1) * LANES)
            acc_scr[:, sl] = acc_scr[:, sl] + _dot(jnp.concatenate(ps, axis=1), vstack)

    @pl.when(ki == nk - 1)
    def _():
        lo_q = lax.broadcasted_iota(jnp.int32, (tq, LANES), 1) < HALF
        mins = []
        for hp in range(N_HEADS // 2):
            ls = [jnp.sum(l_scr[2 * hp + par], axis=1, keepdims=True) for par in range(2)]
            sl = slice(hp * LANES, (hp + 1) * LANES)
            o_ref[:, sl] = (acc_scr[:, sl] / jnp.where(lo_q, ls[0], ls[1])).astype(BF16)
            mins += [jnp.broadcast_to(jnp.min(l, axis=0, keepdims=True), (1, LANES)) for l in ls]
        lmin_ref[...] = jnp.concatenate(mins, axis=0)


FLASH_MIN_DENOM = 1e-30


def flash_attn_bounded(q, kd, vd, bound, nb, n, m):
    tq = min(512, n)
    tk = _pick_tile(m, (2816, 768, 512, 256))
    nq, nk = n // tq, m // tk
    hq = N_HEADS * HEAD_DIM
    o, lmin = pl.pallas_call(
        functools.partial(_flash_bounded_kernel, nk=nk),
        grid=(nb, nq, nk),
        in_specs=[pl.BlockSpec(memory_space=pltpu.SMEM),
                  pl.BlockSpec((tq, hq), lambda b, i, k: (b * nq + i, 0)),
                  pl.BlockSpec((tk, 2 * LANES), lambda b, i, k: (b * nk + k, 0)),
                  pl.BlockSpec((tk, 2 * LANES), lambda b, i, k: (b * nk + k, 0))],
        out_specs=[pl.BlockSpec((tq, hq), lambda b, i, k: (b * nq + i, 0)),
                   pl.BlockSpec((N_HEADS, LANES), lambda b, i, k: (b * nq + i, 0))],
        out_shape=[jax.ShapeDtypeStruct((nb * n, hq), BF16),
                   jax.ShapeDtypeStruct((nb * nq * N_HEADS, LANES), F32)],
        scratch_shapes=[pltpu.VMEM((N_HEADS, tq, LANES), F32),
                        pltpu.VMEM((tq, hq), F32)],
        compiler_params=_params(("parallel", "parallel", "arbitrary")),
        name="flash_attn_bounded",
    )(bound, q, kd, vd)
    ok = jnp.min(lmin) > FLASH_MIN_DENOM
    return lax.cond(ok, lambda: o,
                    lambda: flash_attn(q, kd, vd, jnp.zeros((N_HEADS,), F32), nb, n, m, has_sink=False))


def _pick_tile(m, cands):
    for c in cands:
        if m % c == 0:
            return c
    raise ValueError(f"no tile for {m}")


def flash_attn(q, kd, vd, sink, nb, n, m, has_sink):
    tq = min(512, n)
    tk = _pick_tile(m, (768, 512, 256))
    nq, nk = n // tq, m // tk
    hq = N_HEADS * HEAD_DIM
    return pl.pallas_call(
        functools.partial(_flash_kernel, has_sink=has_sink, nk=nk),
        grid=(nb, nq, nk),
        in_specs=[pl.BlockSpec(memory_space=pltpu.SMEM),
                  pl.BlockSpec((tq, hq), lambda b, i, k: (b * nq + i, 0)),
                  pl.BlockSpec((tk, 2 * LANES), lambda b, i, k: (b * nk + k, 0)),
                  pl.BlockSpec((tk, 2 * LANES), lambda b, i, k: (b * nk + k, 0))],
        out_specs=pl.BlockSpec((tq, hq), lambda b, i, k: (b * nq + i, 0)),
        out_shape=jax.ShapeDtypeStruct((nb * n, hq), BF16),
        scratch_shapes=[pltpu.VMEM((N_HEADS, tq, LANES), F32),
                        pltpu.VMEM((N_HEADS, tq, LANES), F32),
                        pltpu.VMEM((tq, hq), F32)],
        compiler_params=_params(("parallel", "parallel", "arbitrary")),
        name="flash_attn",
    )(sink, q, kd, vd)


def _window_kernel(sink_ref, q_ref, kp_ref, km_ref, kn_ref, vp_ref, vm_ref, vn_ref, kc_ref, vc_ref,
                   o_ref, *, n, tq):
    i = pl.program_id(1)
    span = tq + 2 * WINDOW
    q0 = i * tq
    r = lax.broadcasted_iota(jnp.int32, (tq, span), 0)
    c = lax.broadcasted_iota(jnp.int32, (tq, span), 1)
    kpos = c + (q0 - WINDOW)
    ok = (c >= r) & (c <= r + 2 * WINDOW) & (kpos >= 0) & (kpos < n)
    lane_q = lax.broadcasted_iota(jnp.int32, (tq, LANES), 1)
    lo_q = lane_q < HALF
    kloc = jnp.concatenate([kp_ref[...], km_ref[...], kn_ref[...]], axis=0)
    vloc = jnp.concatenate([vp_ref[...], vm_ref[...], vn_ref[...]], axis=0)
    pairs_per_kv = N_HEADS // N_KV // 2
    for j in range(N_KV):
        js = slice(j * LANES, (j + 1) * LANES)
        kl, vl_stack = _pair_operands(kloc[:, js], vloc[:, js])
        kc, vc_stack = _pair_operands(kc_ref[:, js], vc_ref[:, js])
        for pp in range(pairs_per_kv):
            hp = j * pairs_per_kv + pp
            qp = q_ref[:, hp * LANES:(hp + 1) * LANES]
            pl_, pc_, ls = [], [], []
            for par in range(2):
                h = 2 * hp + par
                s_loc = jnp.where(ok, _dot_nt(qp, kl[par]), NEG_BIG)
                s_ctx = _dot_nt(qp, kc[par])
                snk = sink_ref[h]
                m = jnp.maximum(jnp.maximum(jnp.max(s_loc, axis=1, keepdims=True),
                                            jnp.max(s_ctx, axis=1, keepdims=True)), snk)
                p_loc = jnp.exp(s_loc - m)
                p_ctx = jnp.exp(s_ctx - m)
                l = (jnp.sum(p_loc, axis=1, keepdims=True) + jnp.sum(p_ctx, axis=1, keepdims=True)
                     + jnp.exp(snk - m))
                inv = 1.0 / l
                pl_.append((p_loc * inv).astype(BF16))
                pc_.append((p_ctx * inv).astype(BF16))
            o = _dot(jnp.concatenate(pl_, axis=1), vl_stack) + _dot(jnp.concatenate(pc_, axis=1), vc_stack)
            o_ref[:, hp * LANES:(hp + 1) * LANES] = o.astype(BF16)


def window_attn(q, kd, vd, kdc, vdc, sink, nb, n, mc):
    tq = 2 * WINDOW
    nq = n // tq
    wb = n // WINDOW
    hq = N_HEADS * HEAD_DIM
    prev = pl.BlockSpec((WINDOW, 2 * LANES), lambda b, i: (b * wb + jnp.maximum(2 * i - 1, 0), 0))
    main = pl.BlockSpec((tq, 2 * LANES), lambda b, i: (b * nq + i, 0))
    nxt = pl.BlockSpec((WINDOW, 2 * LANES), lambda b, i: (b * wb + jnp.minimum(2 * i + 2, wb - 1), 0))
    ctx = pl.BlockSpec((mc, 2 * LANES), lambda b, i: (b, 0))
    return pl.pallas_call(
        functools.partial(_window_kernel, n=n, tq=tq),
        grid=(nb, nq),
        in_specs=[pl.BlockSpec(memory_space=pltpu.SMEM),
                  pl.BlockSpec((tq, hq), lambda b, i: (b * nq + i, 0)),
                  prev, main, nxt, prev, main, nxt, ctx, ctx],
        out_specs=pl.BlockSpec((tq, hq), lambda b, i: (b * nq + i, 0)),
        out_shape=jax.ShapeDtypeStruct((nb * n, hq), BF16),
        compiler_params=_params(("parallel", "parallel")),
        name="window_attn",
    )(sink, q, kd, kd, kd, vd, vd, vd, kdc, vdc)


HALO = 16


def _conv_kernel(xm_ref, xp_ref, xn_ref, bm_ref, bp_ref, bn_ref, wx_ref, bx_ref, wb_ref, bb_ref,
                 ox_ref, ob_ref, *, nt):
    i = pl.program_id(1)
    has_prev = jnp.where(i > 0, 1.0, 0.0)
    has_next = jnp.where(i < nt - 1, 1.0, 0.0)

    def conv(m_ref, p_ref, n_ref, w_ref, b_ref, o_ref):
        x = m_ref[...].astype(F32)
        tl = x.shape[0]
        row = lax.broadcasted_iota(jnp.int32, x.shape, 0)
        before = p_ref[...].astype(F32)[HALO - 1:HALO, :] * has_prev
        after = n_ref[...].astype(F32)[0:1, :] * has_next
        xm1 = jnp.where(row == 0, before, pltpu.roll(x, 1, 0))
        xp1 = jnp.where(row == tl - 1, after, pltpu.roll(x, tl - 1, 0))
        w = w_ref[...]
        y = xm1 * w[0:1, :] + x * w[1:2, :] + xp1 * w[2:3, :] + b_ref[...]
        o_ref[...] = _silu(y).astype(BF16)

    conv(xm_ref, xp_ref, xn_ref, wx_ref, bx_ref, ox_ref)
    conv(bm_ref, bp_ref, bn_ref, wb_ref, bb_ref, ob_ref)


def ssm_conv(p, conv_w, conv_b, nb, n):
    tl = min(512, n)
    nt = n // tl
    hb = n // HALO
    hpt = tl // HALO
    cx, cb = SSM_INNER, 2 * SSM_GROUPS * SSM_STATE

    def specs(width, col):
        cblk = col // width
        return (pl.BlockSpec((tl, width), lambda b, i: (b * nt + i, cblk)),
                pl.BlockSpec((HALO, width), lambda b, i: (b * hb + jnp.maximum(i * hpt - 1, 0), cblk)),
                pl.BlockSpec((HALO, width), lambda b, i: (b * hb + jnp.minimum((i + 1) * hpt, hb - 1), cblk)))

    const = lambda shp: pl.BlockSpec(shp, lambda b, i: (0, 0))
    xm, xp, xn = specs(cx, C_XS)
    bm, bp, bn = specs(cb, C_BC)
    return pl.pallas_call(
        functools.partial(_conv_kernel, nt=nt),
        grid=(nb, nt),
        in_specs=[xm, xp, xn, bm, bp, bn, const((3, cx)), const((1, cx)), const((3, cb)), const((1, cb))],
        out_specs=[pl.BlockSpec((tl, cx), lambda b, i: (b * nt + i, 0)),
                   pl.BlockSpec((tl, cb), lambda b, i: (b * nt + i, 0))],
        out_shape=[jax.ShapeDtypeStruct((nb * n, cx), BF16), jax.ShapeDtypeStruct((nb * n, cb), BF16)],
        compiler_params=_params(("parallel", "parallel")),
        name="ssm_conv",
    )(p, p, p, p, p, p, conv_w[:, :cx], conv_b[:cx].reshape(1, cx), conv_w[:, cx:], conv_b[cx:].reshape(1, cb))


def _ssd_chunk(d, r0, xs_ref, bc_ref, dt_ref, bias, aneg, tri, rep, st_scr, y_ref):
    q = SSM_CHUNK
    rows = slice(r0, r0 + q)
    gw = SSM_INNER // SSM_GROUPS
    dt = _softplus(dt_ref[rows, :] + bias)
    a = dt * aneg
    ac = _exact_left(tri, a, pieces=2)
    act = ac.T
    acx = _exact_right(ac, rep, pieces=2)
    dtx = _dot(dt.astype(BF16), rep)
    totx = acx[q - 1:q, :] if d == 0 else acx[0:1, :]
    xd = xs_ref[rows, :].astype(F32) * dtx
    xd_b = xd.astype(BF16)
    xe = (xd * jnp.exp(totx - acx)).astype(BF16)
    ein = jnp.exp(acx)
    keep = tri > 0
    lane = lax.broadcasted_iota(jnp.int32, (q, LANES), 1)
    lo = lane < HALF
    zero = jnp.zeros((q, LANES), BF16)
    hpg = SSM_HEADS // SSM_GROUPS
    for g in range(SSM_GROUPS):
        bg = bc_ref[rows, g * SSM_STATE:(g + 1) * SSM_STATE]
        cg = bc_ref[rows, (SSM_GROUPS + g) * SSM_STATE:(SSM_GROUPS + g + 1) * SSM_STATE]
        cb = _dot_nt(cg, bg)
        st = st_scr[d, g]
        yoff = _dot(cg, st.astype(BF16)) * ein[:, g * gw:(g + 1) * gw]
        for hp in range(hpg // 2):
            gs = []
            for par in range(2):
                h = g * hpg + 2 * hp + par
                seg = ac[:, h:h + 1] - act[h:h + 1, :]
                gs.append((cb * jnp.exp(jnp.where(keep, seg, NEG_BIG))).astype(BF16))
            c0 = g * gw + hp * LANES
            xp = xd_b[:, c0:c0 + LANES]
            xstack = jnp.concatenate([jnp.where(lo, xp, zero), jnp.where(lo, zero, xp)], axis=0)
            ydiag = _dot(jnp.concatenate(gs, axis=1), xstack)
            y_ref[rows, c0:c0 + LANES] = (ydiag + yoff[:, hp * LANES:(hp + 1) * LANES]).astype(BF16)
        bgt = bg.astype(F32).T.astype(BF16)
        cs = _dot(bgt, xe[:, g * gw:(g + 1) * gw])
        st_scr[d, g] = st * jnp.exp(totx[:, g * gw:(g + 1) * gw]) + cs


def _ssd_kernel(xf_ref, bcf_ref, dtf_ref, xb_ref, bcb_ref, dtb_ref, bias_ref, aneg_ref, tri_ref, rep_ref,
                s0_ref, yf_ref, yb_ref, sfin_ref, st_scr, *, nsteps):
    k = pl.program_id(1)

    @pl.when(k == 0)
    def _():
        st_scr[...] = s0_ref[...]

    rep = rep_ref[...]
    for c in range(SSD_CHUNKS_PER_STEP):
        _ssd_chunk(0, c * SSM_CHUNK, xf_ref, bcf_ref, dtf_ref, bias_ref[0], aneg_ref[0], tri_ref[0], rep,
                   st_scr, yf_ref)
        _ssd_chunk(1, (SSD_CHUNKS_PER_STEP - 1 - c) * SSM_CHUNK, xb_ref, bcb_ref, dtb_ref, bias_ref[1],
                   aneg_ref[1], tri_ref[1], rep, st_scr, yb_ref)

    @pl.when(k == nsteps - 1)
    def _():
        sfin_ref[...] = st_scr[...]


SSD_CHUNKS_PER_STEP = 2


def ssd_scan(u_xs, u_bc, dt_raw, bias2, aneg2, s0, nb, n):
    q = SSM_CHUNK
    nc = n // q
    gw = SSM_INNER // SSM_GROUPS
    idx = np.arange(q)
    tri = np.stack([idx[:, None] >= idx[None, :], idx[:, None] <= idx[None, :]]).astype(np.float32)
    rep = (np.arange(LANES)[:, None] == np.arange(SSM_INNER)[None, :] // SSM_P).astype(np.float32)

    ns = nc // SSD_CHUNKS_PER_STEP
    rb = q * SSD_CHUNKS_PER_STEP
    fwd = lambda b, k: b * ns + k
    bwd = lambda b, k: b * ns + (ns - 1 - k)
    cbc = 2 * SSM_GROUPS * SSM_STATE
    state = pl.BlockSpec((None, 2, SSM_GROUPS, SSM_STATE, gw), lambda b, k: (b, 0, 0, 0, 0))
    const = lambda shp: pl.BlockSpec(shp, lambda b, k: (0,) * len(shp))
    yf, yb, sfin = pl.pallas_call(
        functools.partial(_ssd_kernel, nsteps=ns),
        grid=(nb, ns),
        in_specs=[pl.BlockSpec((rb, SSM_INNER), lambda b, k: (fwd(b, k), 0)),
                  pl.BlockSpec((rb, cbc), lambda b, k: (fwd(b, k), 0)),
                  pl.BlockSpec((rb, LANES), lambda b, k: (fwd(b, k), 0)),
                  pl.BlockSpec((rb, SSM_INNER), lambda b, k: (bwd(b, k), 0)),
                  pl.BlockSpec((rb, cbc), lambda b, k: (bwd(b, k), 0)),
                  pl.BlockSpec((rb, LANES), lambda b, k: (bwd(b, k), 1)),
                  const((2, 1, LANES)), const((2, 1, LANES)), const((2, q, q)), const((LANES, SSM_INNER)),
                  state],
        out_specs=[pl.BlockSpec((rb, SSM_INNER), lambda b, k: (fwd(b, k), 0)),
                   pl.BlockSpec((rb, SSM_INNER), lambda b, k: (bwd(b, k), 0)),
                   state],
        out_shape=[jax.ShapeDtypeStruct((nb * n, SSM_INNER), BF16),
                   jax.ShapeDtypeStruct((nb * n, SSM_INNER), BF16),
                   jax.ShapeDtypeStruct((nb, 2, SSM_GROUPS, SSM_STATE, gw), F32)],
        scratch_shapes=[pltpu.VMEM((2, SSM_GROUPS, SSM_STATE, gw), F32)],
        compiler_params=_params(("parallel", "arbitrary")),
        name="ssd_scan",
    )(u_xs, u_bc, dt_raw, u_xs, u_bc, dt_raw, bias2, aneg2, jnp.asarray(tri, BF16), jnp.asarray(rep, BF16), s0)
    return (yf, yb), sfin


def _post_kernel(oa_ref, ow_ref, yf_ref, yb_ref, xs_ref, z_ref, gt_ref, x_ref,
                 dsk_ref, nw_ref, wa_ref, ww_ref, ws_ref, wo_ref, gpost_ref, gate_ref,
                 gpre_ref, sh_ref, sc_ref, wrh_ref, wrl_ref,
                 xo_ref, h_ref, lg_ref):
    y = yf_ref[...].astype(F32) + yb_ref[...].astype(F32) + dsk_ref[...] * xs_ref[...].astype(F32)
    u = y * _silu(z_ref[...].astype(F32))
    gw = SSM_INNER // SSM_GROUPS
    ys = jnp.concatenate([_rms(u[:, g * gw:(g + 1) * gw]) for g in range(SSM_GROUPS)], axis=1)
    ys = (ys * nw_ref[...]).astype(BF16)
    d = x_ref.shape[1]
    ga = jax.nn.sigmoid(gt_ref[:, 0:d].astype(F32))
    gw_ = jax.nn.sigmoid(gt_ref[:, d:2 * d].astype(F32))
    gs = jax.nn.sigmoid(gt_ref[:, 2 * d:3 * d].astype(F32))
    m = ga * _dot(oa_ref[...], wa_ref[...]) + gw_ * _dot(ow_ref[...], ww_ref[...]) + gs * _dot(ys, ws_ref[...])
    ml = _dot(m.astype(BF16), wo_ref[...])
    xn = x_ref[...] + gate_ref[...] * (_rms(ml) * gpost_ref[...])
    xo_ref[...] = xn
    h = (_rms(xn) * gpre_ref[...]) * (1.0 + sc_ref[...]) + sh_ref[...]
    _store_rows_as_slabs(h_ref, h)
    hb = h.astype(BF16)
    hl =(h - hb.astype(F32)).astype(BF16)
    lg_ref[...] = _dot_nt(wrh_ref[...], hb) + _dot_nt(wrh_ref[...], hl) + _dot_nt(wrl_ref[...], hb)


def post_mixer(oa, ow, y2, u_xs, p, x, dskip, norm_w, wa, ww, ws, wo, gpost, gate, gpre, shift, scale,
               wr_hi, wr_lo, rows_per_batch):
    t, d = x.shape
    tm = min(256, rows_per_batch)
    tpb = rows_per_batch // tm
    nt = t // tm
    hq = N_HEADS * HEAD_DIM
    row = lambda w, c=0: pl.BlockSpec((tm, w), lambda i: (i, c // w))
    const = lambda shp: pl.BlockSpec(shp, lambda i: (0,) * len(shp))
    mod = pl.BlockSpec((None, 1, d), lambda i: (i // tpb, 0, 0))
    return pl.pallas_call(
        _post_kernel,
        grid=(nt,),
        in_specs=[row(hq), row(hq), row(SSM_INNER), row(SSM_INNER),
                  row(SSM_INNER), row(SSM_INNER, C_Z), row(3 * d, C_GATES), row(d),
                  const((1, SSM_INNER)), const((1, SSM_INNER)),
                  const((hq, d)), const((hq, d)), const((SSM_INNER, d)), const((d, d)),
                  const((1, d)), mod, const((1, d)), mod, mod,
                  const((N_EXPERTS, d)), const((N_EXPERTS, d))],
        out_specs=[row(d), pl.BlockSpec((tm * SLAB, LANES), lambda i: (i, 0)),
                   pl.BlockSpec((N_EXPERTS, tm), lambda i: (0, i))],
        out_shape=[jax.ShapeDtypeStruct((t, d), F32), jax.ShapeDtypeStruct((t * SLAB, LANES), U32),
                   jax.ShapeDtypeStruct((N_EXPERTS, t), F32)],
        compiler_params=_params(("parallel",)),
        name="post_mixer",
    )(oa, ow, y2[0], y2[1], u_xs, p, p, x, dskip, norm_w, wa, ww, ws, wo, gpost, gate, gpre, shift, scale,
      wr_hi, wr_lo)


def _route_kernel(lg_ref, b_ref, ei_ref, w_ref):
    scores = jax.nn.sigmoid(lg_ref[...])
    sel = scores + b_ref[...]
    tt = sel.shape[1]
    per = N_EXPERTS // N_EXPERT_GROUPS
    r8 = lax.broadcasted_iota(jnp.int32, (per, tt), 0).astype(F32)
    ninf = -jnp.inf

    def argmax_rows(x, rows, nrows):
        m = jnp.max(x, axis=0, keepdims=True)
        idx = jnp.min(jnp.where(x == m, rows, float(nrows)), axis=0, keepdims=True)
        return m, idx

    gscores = []
    for g in range(N_EXPERT_GROUPS):
        blk = sel[g * per:(g + 1) * per, :]
        m1, i1 = argmax_rows(blk, r8, per)
        m2 = jnp.max(jnp.where(r8 == i1, ninf, blk), axis=0, keepdims=True)
        gscores.append(m1 + m2)
    cur = jnp.concatenate(gscores, axis=0)
    rg = lax.broadcasted_iota(jnp.int32, cur.shape, 0).astype(F32)
    chosen = jnp.zeros(cur.shape, F32)
    for _ in range(TOPK_GROUPS):
        _, gi = argmax_rows(cur, rg, N_EXPERT_GROUPS)
        hit = rg == gi
        chosen = jnp.where(hit, 1.0, chosen)
        cur = jnp.where(hit, ninf, cur)
    gmask = jnp.concatenate([jnp.broadcast_to(chosen[g:g + 1, :], (per, tt)) for g in range(N_EXPERT_GROUPS)],
                            axis=0)
    cur = jnp.where(gmask > 0, sel, ninf)
    re = lax.broadcasted_iota(jnp.int32, cur.shape, 0).astype(F32)
    idxs, ws = [], []
    for _ in range(TOP_K):
        _, ei = argmax_rows(cur, re, N_EXPERTS)
        hit = re == ei
        ws.append(jnp.sum(jnp.where(hit, scores, 0.0), axis=0, keepdims=True))
        idxs.append(ei)
        cur = jnp.where(hit, ninf, cur)
    w = jnp.concatenate(ws, axis=0)
    w_ref[...] = w / jnp.sum(w, axis=0, keepdims=True) * ROUTED_SCALE
    ei_ref[...] = jnp.concatenate(idxs, axis=0).astype(jnp.int32)


def route(logits_t, b_router):
    e, t = logits_t.shape
    tt = 512
    return pl.pallas_call(
        _route_kernel,
        grid=(t // tt,),
        in_specs=[pl.BlockSpec((e, tt), lambda i: (0, i)), pl.BlockSpec((e, 1), lambda i: (0, 0))],
        out_specs=[pl.BlockSpec((TOP_K, tt), lambda i: (0, i)), pl.BlockSpec((TOP_K, tt), lambda i: (0, i))],
        out_shape=[jax.ShapeDtypeStruct((TOP_K, t), jnp.int32), jax.ShapeDtypeStruct((TOP_K, t), F32)],
        compiler_params=_params(("parallel",)),
        name="route",
    )(logits_t, b_router.reshape(e, 1))


MOE_ROWS = 512
PLAN_TOKENS = 512


def _moe_geometry(t):
    nblk = -(-(t * TOP_K + N_EXPERTS * (MOE_ROWS - 1)) // MOE_ROWS)
    return nblk, nblk * MOE_ROWS


def _plan_kernel(ei_ref, ut_ref, tril_ref, dest_ref, tab_ref, be_ref, cnt_scr, run_scr):
    ph = pl.program_id(0)
    i = pl.program_id(1)
    ei = ei_ref[...]
    tt = ei.shape[1]
    re = lax.broadcasted_iota(jnp.int32, (N_EXPERTS, tt), 0)
    hits = [re == ei[k:k + 1, :] for k in range(TOP_K)]
    oh = jnp.zeros((N_EXPERTS, tt), F32)
    for k in range(TOP_K):
        oh = oh + jnp.where(hits[k], 1.0, 0.0)

    @pl.when((ph == 0) & (i == 0))
    def _():
        cnt_scr[...] = jnp.zeros(cnt_scr.shape, F32)

    @pl.when(ph == 0)
    def _():
        cnt_scr[...] = cnt_scr[...] + jnp.sum(oh, axis=1, keepdims=True)

    @pl.when((ph == 1) & (i == 0))
    def _():
        cnt = cnt_scr[...]
        padded = ((cnt.astype(jnp.int32) + (MOE_ROWS - 1)) & (-MOE_ROWS)).astype(F32)
        pad_end = _exact_left(tril_ref[...], padded)
        pad_start = pad_end - padded
        run_scr[...] = pad_start
        tab_ref[0] = pad_start
        tab_ref[1] = pad_end
        nbp = be_ref.shape[1]
        blk0 = (lax.broadcasted_iota(jnp.int32, (N_EXPERTS, nbp), 1) * MOE_ROWS).astype(F32)
        be = jnp.sum(jnp.where(pad_end[:, :1] <= blk0, 1.0, 0.0), axis=0, keepdims=True)
        be_ref[...] = jnp.broadcast_to(jnp.minimum(be, N_EXPERTS - 1.0), be_ref.shape).astype(jnp.int32)

    @pl.when(ph == 1)
    def _():
        cin = _dot(oh.astype(BF16), ut_ref[...])
        pos = run_scr[:, :1] + (cin - oh)
        rows = [jnp.sum(jnp.where(hits[k], pos, 0.0), axis=0, keepdims=True) for k in range(TOP_K)]
        dest_ref[...] = jnp.concatenate(rows, axis=0).astype(jnp.int32)
        run_scr[...] = run_scr[...] + cin[:, tt - 1:tt]


def moe_plan(eidx_t):
    k, t = eidx_t.shape
    tt = PLAN_TOKENS
    nt = t // tt
    nblk, _ = _moe_geometry(t)
    nbp = -(-nblk // LANES) * LANES
    ut = jnp.asarray(np.arange(tt)[:, None] <= np.arange(tt)[None, :], BF16)
    tril = jnp.asarray(np.arange(N_EXPERTS)[:, None] >= np.arange(N_EXPERTS)[None, :], BF16)
    return pl.pallas_call(
        _plan_kernel,
        grid=(2, nt),
        in_specs=[pl.BlockSpec((k, tt), lambda p, i: (0, i)),
                  pl.BlockSpec((tt, tt), lambda p, i: (0, 0)),
                  pl.BlockSpec((N_EXPERTS, N_EXPERTS), lambda p, i: (0, 0))],
        out_specs=[pl.BlockSpec((k, tt), lambda p, i: (0, i * p)),
                   pl.BlockSpec((2, N_EXPERTS, LANES), lambda p, i: (0, 0, 0)),
                   pl.BlockSpec((8, nbp), lambda p, i: (0, 0))],
        out_shape=[jax.ShapeDtypeStruct((k, t), jnp.int32),
                   jax.ShapeDtypeStruct((2, N_EXPERTS, LANES), F32),
                   jax.ShapeDtypeStruct((8, nbp), jnp.int32)],
        scratch_shapes=[pltpu.VMEM((N_EXPERTS, LANES), F32), pltpu.VMEM((N_EXPERTS, LANES), F32)],
        compiler_params=_params(("arbitrary", "arbitrary")),
        name="moe_plan",
    )(eidx_t, ut, tril)


def _dispatch_kernel(dest_ref, pstart_ref, pend_ref, h_ref, xs_hbm, zero_scr, sem):
    i = pl.program_id(0)
    tt = h_ref.shape[0] // SLAB

    def slab(ref, row, n=1):
        return ref.at[pl.ds(pl.multiple_of(row * SLAB, SLAB), n * SLAB), :]

    @pl.when(i == 0)
    def _():
        zero_scr[...] = jnp.zeros(zero_scr.shape, U32)
        nblk = xs_hbm.shape[0] // (MOE_ROWS * SLAB)
        n_used = pend_ref[N_EXPERTS - 1] // MOE_ROWS

        def zero_block(row0):
            return pltpu.make_async_copy(zero_scr, slab(xs_hbm, row0, MOE_ROWS), sem)

        def seg_start(e, c):
            @pl.when(pend_ref[e] > pstart_ref[e])
            def _():
                zero_block(pend_ref[e] - MOE_ROWS).start()
            return c

        def seg_wait(e, c):
            @pl.when(pend_ref[e] > pstart_ref[e])
            def _():
                zero_block(pend_ref[e] - MOE_ROWS).wait()
            return c

        def tail_start(b, c):
            zero_block(b * MOE_ROWS).start()
            return c

        def tail_wait(b, c):
            zero_block(b * MOE_ROWS).wait()
            return c

        lax.fori_loop(0, N_EXPERTS, seg_start, 0)
        lax.fori_loop(n_used, nblk, tail_start, 0)
        lax.fori_loop(0, N_EXPERTS, seg_wait, 0)
        lax.fori_loop(n_used, nblk, tail_wait, 0)

    def issue(t, c):
        for k in range(TOP_K):
            pltpu.make_async_copy(slab(h_ref, t), slab(xs_hbm, dest_ref[k, t]), sem).start(priority=k % 2)
        return c

    lax.fori_loop(0, tt, issue, 0)
    for k in range(TOP_K):
        pltpu.make_async_copy(h_ref, slab(xs_hbm, 0, tt), sem).wait()


def moe_dispatch(dest_t, pad_start, pad_end, h):
    t = h.shape[0] // SLAB
    tt = PLAN_TOKENS
    _, cap = _moe_geometry(t)
    smem = pl.BlockSpec(memory_space=pltpu.SMEM)
    return pl.pallas_call(
        _dispatch_kernel,
        grid=(t // tt,),
        in_specs=[pl.BlockSpec((TOP_K, tt), lambda i: (0, i), memory_space=pltpu.SMEM), smem, smem,
                  pl.BlockSpec((tt * SLAB, LANES), lambda i: (i, 0))],
        out_specs=pl.BlockSpec(memory_space=pl.ANY),
        out_shape=jax.ShapeDtypeStruct((cap * SLAB, LANES), U32),
        scratch_shapes=[pltpu.VMEM((MOE_ROWS * SLAB, LANES), U32), pltpu.SemaphoreType.DMA(())],
        compiler_params=_params(("arbitrary",)),
        name="moe_dispatch",
    )(dest_t, pad_start, pad_end, h)


def _expert_kernel(be_ref, nu_ref, x_ref, wg_ref, wu_ref, wd_ref, o_ref):
    used = pl.program_id(0) < nu_ref[0]

    @pl.when(used)
    def _():
        x = _load_slabs_as_rows(x_ref).astype(BF16)
        hg = _dot(x, wg_ref[...].astype(BF16))
        hu = _dot(x, wu_ref[...].astype(BF16))
        hb = (_silu(hg) * hu).astype(BF16)
        _store_rows_as_slabs(o_ref, _dot(hb, wd_ref[...].astype(BF16)))

    @pl.when(jnp.logical_not(used))
    def _():
        o_ref[...] = jnp.zeros(o_ref.shape, U32)


def expert_ffn(blk_e, n_used, xs, we_gate, we_up, we_down):
    d, de = we_gate.shape[1:]
    nblk = xs.shape[0] // (MOE_ROWS * SLAB)
    blk = lambda i, nu: jnp.minimum(i, nu[0] - 1)
    grid_spec = pltpu.PrefetchScalarGridSpec(
        num_scalar_prefetch=2,
        grid=(nblk,),
        in_specs=[pl.BlockSpec((MOE_ROWS * SLAB, LANES), lambda i, be, nu: (blk(i, nu), 0)),
                  pl.BlockSpec((None, d, de), lambda i, be, nu: (be[blk(i, nu)], 0, 0)),
                  pl.BlockSpec((None, d, de), lambda i, be, nu: (be[blk(i, nu)], 0, 0)),
                  pl.BlockSpec((None, de, d), lambda i, be, nu: (be[blk(i, nu)], 0, 0))],
        out_specs=pl.BlockSpec((MOE_ROWS * SLAB, LANES), lambda i, be, nu: (i, 0)),
    )
    return pl.pallas_call(
        _expert_kernel,
        grid_spec=grid_spec,
        out_shape=jax.ShapeDtypeStruct(xs.shape, U32),
        compiler_params=_params(("arbitrary",)),
        name="expert_ffn",
    )(blk_e, n_used, xs, we_gate, we_up, we_down)


def _ffn_tail_kernel(dest_ref, dnext_ref, w_ref, h_ref, x_ref, wg_ref, wu_ref, wd_ref, gpost_ref, gate_ref, ys_hbm,
                     xo_ref, buf, sem, *, nt):
    i = pl.program_id(0)
    tt = h_ref.shape[0] // SLAB
    cur = i % 2

    def slab(ref, row, n=1):
        return ref.at[pl.ds(pl.multiple_of(row * SLAB, SLAB), n * SLAB), :]

    def gather(idx_ref, slot):
        def issue(t, c):
            for k in range(TOP_K):
                pltpu.make_async_copy(slab(ys_hbm, idx_ref[k, t]), slab(buf.at[slot, k], t),
                                      sem.at[slot]).start(priority=k % 2)
            return c
        lax.fori_loop(0, tt, issue, 0)

    @pl.when(i == 0)
    def _():
        gather(dest_ref, 0)

    @pl.when(i + 1 < nt)
    def _():
        gather(dnext_ref, 1 - cur)

    h = _load_slabs_as_rows(h_ref).astype(BF16)
    hs = (_silu(_dot(h, wg_ref[...])) * _dot(h, wu_ref[...])).astype(BF16)
    f = _dot(hs, wd_ref[...])
    w = w_ref[...]
    for k in range(TOP_K):
        pltpu.make_async_copy(slab(ys_hbm, 0, tt), buf.at[cur, k], sem.at[cur]).wait()
    for k in range(TOP_K):
        f = f + _load_slabs_as_rows(buf.at[cur, k]) * w[:, k:k + 1]
    xo_ref[...] = x_ref[...] + gate_ref[...] * (_rms(f) * gpost_ref[...])


def ffn_tail(dest_t, wts, h, x, ys, wsg, wsu, wsd, gpost, gate, rows_per_batch):
    t, d = x.shape
    ds = wsg.shape[1]
    tm = min(256, rows_per_batch)
    tpb = rows_per_batch // tm
    nt = t // tm
    row = pl.BlockSpec((tm, d), lambda i: (i, 0))
    const = lambda shp: pl.BlockSpec(shp, lambda i: (0,) * len(shp))
    return pl.pallas_call(
        functools.partial(_ffn_tail_kernel, nt=nt),
        grid=(nt,),
        in_specs=[pl.BlockSpec((TOP_K, tm), lambda i: (0, i), memory_space=pltpu.SMEM),
                  pl.BlockSpec((TOP_K, tm), lambda i: (0, jnp.minimum(i + 1, nt - 1)), memory_space=pltpu.SMEM),
                  pl.BlockSpec((tm, TOP_K), lambda i: (i, 0)),
                  pl.BlockSpec((tm * SLAB, LANES), lambda i: (i, 0)),
                  row, const((d, ds)), const((d, ds)), const((ds, d)), const((1, d)),
                  pl.BlockSpec((None, 1, d), lambda i: (i // tpb, 0, 0)),
                  pl.BlockSpec(memory_space=pl.ANY)],
        out_specs=row,
        out_shape=jax.ShapeDtypeStruct((t, d), F32),
        scratch_shapes=[pltpu.VMEM((2, TOP_K, tm * SLAB, LANES), U32), pltpu.SemaphoreType.DMA((2,))],
        compiler_params=_params(("arbitrary",)),
        name="ffn_tail",
    )(dest_t, dest_t, wts, h, x, wsg, wsu, wsd, gpost, gate, ys)


def _moe_routed(h, logits_t, b_router, we_gate, we_up, we_down):
    eidx_t, wts_t = route(logits_t, b_router)
    dest_t, tabs, blk_e = moe_plan(eidx_t)
    pad_start = tabs[0, :, 0].astype(jnp.int32)
    pad_end = tabs[1, :, 0].astype(jnp.int32)
    n_used = (pad_end[N_EXPERTS - 1:] // MOE_ROWS).astype(jnp.int32)
    xs = moe_dispatch(dest_t, pad_start, pad_end, h)
    ys = expert_ffn(blk_e[0], n_used, xs, we_gate, we_up, we_down)
    return dest_t, wts_t.T, ys


def _reorder_w_in(w):
    qa, ka, va, qw, kw, vw, z, xs, bm, cm, dt, gates = jnp.split(
        w, [512, 640, 768, 1280, 1408, 1536, 2560, 3584, 3840, 4096, 4128], axis=1)
    w_main = jnp.concatenate([qa, qw, z, xs, gates, ka, va, kw, vw, bm, cm], axis=1).astype(BF16)
    pad = jnp.zeros((w.shape[0], LANES - SSM_HEADS), w.dtype)
    w_dt = jnp.concatenate([dt[:, :SSM_HEADS], pad, dt[:, SSM_HEADS:], pad], axis=1).astype(BF16)
    return w_main, w_dt


def _pad_lanes(v):
    return jnp.pad(v, ((0, 0), (0, LANES - v.shape[1])))


def kernel(x, c, ctx, c_ctx, w_ada, b_ada, g_mix_pre, g_mix_post, g_ffn_pre, g_ffn_post, w_in, g_q_a, g_k_a, sink_w, ssm_conv_w, ssm_conv_b, ssm_dt_bias, ssm_a_log, ssm_d, ssm_norm, w_br_a, w_br_w, w_br_s, w_out, w_router, b_router, we_gate, we_up, we_down, ws_gate, ws_up, ws_down):
    nb, n, d = x.shape
    mc = ctx.shape[1]
    depth = w_in.shape[0]
    t_lat, t_ctx = nb * n, nb * mc
    cos, sin = rope_tables(n)
    xl = x.reshape(t_lat, d)
    xc = ctx.reshape(t_ctx, d)
    c8 = jnp.concatenate([c, c_ctx[None, :], jnp.zeros((8 - nb - 1, d), F32)], axis=0)
    zeros_sink = jnp.zeros((N_HEADS,), F32)
    s_zero = jnp.zeros((nb, 2, SSM_GROUPS, SSM_STATE, SSM_INNER // SSM_GROUPS), F32)
    dummy_tab = jnp.zeros((mc, N_HEADS * HEAD_DIM), F32)

    for i in range(depth):
        last = i == depth - 1
        mod = ada_mod(c8, w_ada[i], b_ada[i])
        mod_l = [mod[:nb, k * d:(k + 1) * d].reshape(nb, 1, d) for k in range(6)]
        mod_c = [mod[nb:nb + 1, k * d:(k + 1) * d].reshape(1, 1, d) for k in range(6)]
        w_main, w_dt = _reorder_w_in(w_in[i])
        gq = jnp.tile(g_q_a[i], N_HEADS)[None, :]
        gk = jnp.tile(g_k_a[i], N_KV)[None, :]
        bias2 = _pad_lanes(ssm_dt_bias[i].reshape(2, SSM_HEADS)).reshape(2, 1, LANES)
        aneg2 = _pad_lanes(-jnp.exp(ssm_a_log[i].astype(F32))).reshape(2, 1, LANES)
        dskip = jnp.repeat(ssm_d[i], SSM_P)[None, :]
        norm_w = ssm_norm[i][None, :]
        wa, ww, ws, wo = (w_br_a[i].astype(BF16), w_br_w[i].astype(BF16), w_br_s[i].astype(BF16),
                          w_out[i].astype(BF16))
        wr_t = w_router[i].T
        wr_hi = wr_t.astype(BF16)
        wr_lo = (wr_t - wr_hi.astype(F32)).astype(BF16)
        sink = sink_w[i].astype(F32)

        p_c, dt_c = in_proj(xc, g_mix_pre[i], mod_c[0], mod_c[1], w_main, w_dt, t_ctx)
        qa_c, qw_c, kda_c, vda_c, kdw_c, vdw_c = attn_prep(p_c, dummy_tab, dummy_tab, gq, gk, nb, mc, rope=False)
        uxs_c, ubc_c = ssm_conv(p_c, ssm_conv_w[i], ssm_conv_b[i], nb, mc)
        y_c, s_fin = ssd_scan(uxs_c, ubc_c, dt_c, bias2, aneg2, s_zero, nb, mc)

        p_l, dt_l = in_proj(xl, g_mix_pre[i], mod_l[0], mod_l[1], w_main, w_dt, n)
        qa, qw, kda, vda, kdw, vdw = attn_prep(p_l, cos, sin, gq, gk, nb, n, rope=True)
        m_all = n + mc
        kd_all = jnp.concatenate([kda.reshape(nb, n, -1), kda_c.reshape(nb, mc, -1)], axis=1).reshape(nb * m_all, -1)
        vd_all = jnp.concatenate([vda.reshape(nb, n, -1), vda_c.reshape(nb, mc, -1)], axis=1).reshape(nb * m_all, -1)
        score_bound = (math.sqrt(HEAD_DIM) * jnp.max(jnp.abs(g_q_a[i])) * jnp.max(jnp.abs(g_k_a[i]))).reshape(1)
        oa = flash_attn_bounded(qa, kd_all, vd_all, score_bound.astype(F32), nb, n, m_all)
        ow = window_attn(qw, kdw, vdw, kdw_c, vdw_c, sink, nb, n, mc)
        uxs, ubc = ssm_conv(p_l, ssm_conv_w[i], ssm_conv_b[i], nb, n)
        y_l, _ = ssd_scan(uxs, ubc, dt_l, bias2, aneg2, s_fin, nb, n)
        xl, h_l, lg_l = post_mixer(oa, ow, y_l, uxs, p_l, xl, dskip, norm_w, wa, ww, ws, wo,
                                   g_mix_post[i][None, :], mod_l[2], g_ffn_pre[i][None, :], mod_l[3], mod_l[4],
                                   wr_hi, wr_lo, n)
        wsg, wsu, wsd = ws_gate[i].astype(BF16), ws_up[i].astype(BF16), ws_down[i].astype(BF16)
        weg, weu, wed = we_gate[i], we_up[i], we_down[i]
        if last:
            dest_t, wts, ys = _moe_routed(h_l, lg_l, b_router[i], weg, weu, wed)
            xl = ffn_tail(dest_t, wts, h_l, xl, ys, wsg, wsu, wsd, g_ffn_post[i][None, :], mod_l[5], n)
        else:
            oa_c = flash_attn(qa_c, kda_c, vda_c, zeros_sink, nb, mc, mc, has_sink=False)
            ow_c = flash_attn(qw_c, kdw_c, vdw_c, sink, nb, mc, mc, has_sink=True)
            xc, h_c, lg_c = post_mixer(oa_c, ow_c, y_c, uxs_c, p_c, xc, dskip, norm_w, wa, ww, ws, wo,
                                       g_mix_post[i][None, :], mod_c[2], g_ffn_pre[i][None, :], mod_c[3], mod_c[4],
                                       wr_hi, wr_lo, t_ctx)
            h_all = jnp.concatenate([h_l, h_c], axis=0)
            lg_all = jnp.concatenate([lg_l, lg_c], axis=1)
            dest_t, wts, ys = _moe_routed(h_all, lg_all, b_router[i], weg, weu, wed)
            xl = ffn_tail(dest_t[:, :t_lat], wts[:t_lat], h_l, xl, ys, wsg, wsu, wsd,
                          g_ffn_post[i][None, :], mod_l[5], n)
            xc = ffn_tail(dest_t[:, t_lat:], wts[t_lat:], h_c, xc, ys, wsg, wsu, wsd,
                          g_ffn_post[i][None, :], mod_c[5], t_ctx)
    return xl.reshape(nb, n, d)
```

```python
import functools
import math

import jax
import jax.numpy as jnp
import numpy as np
from jax import lax
from jax.experimental import pallas as pl
from jax.experimental.pallas import tpu as pltpu

F32 = jnp.float32
BF16 = jnp.bfloat16

HEAD_DIM = 64
N_HEADS = 8
N_KV = 2
GRID_W = 64
ROPE_THETA = 10000.0
WINDOW = 128
SSM_HEADS = 16
SSM_P = 64
SSM_INNER = SSM_HEADS * SSM_P
SSM_GROUPS = 2
SSM_STATE = 128
SSM_CHUNK = 128
N_EXPERTS = 64
TOP_K = 8
N_EXPERT_GROUPS = 8
TOPK_GROUPS = 4
ROUTED_SCALE = 2.5
EPS = 1e-6

LANES = 128
HALF = LANES // 2
VMEM_LIMIT = 56 * 1024 * 1024
NEG_BIG = -1e30

C_QA, C_QW, C_Z, C_XS, C_GATES = 0, 512, 1024, 2048, 3072
C_KA, C_VA, C_KW, C_VW, C_BC = 6144, 6272, 6400, 6528, 6656
P_WIDTH = 7168


def _params(sem, vmem=VMEM_LIMIT):
    return pltpu.CompilerParams(dimension_semantics=sem, vmem_limit_bytes=vmem)


def _silu(x):
    return x * jax.nn.sigmoid(x)


U32 = jnp.uint32
SLAB = 4
HI_MASK = 0xFFFF0000


def _store_rows_as_slabs(ref, x):
    r = x.shape[0]

    def bits(v):
        return pltpu.bitcast(v.astype(BF16).astype(F32), U32)

    for c in range(SLAB):
        lo = bits(x[:, c * LANES:(c + 1) * LANES]) >> 16
        hi = bits(x[:, (SLAB + c) * LANES:(SLAB + c + 1) * LANES]) & jnp.uint32(HI_MASK)
        ref[pl.ds(c, r, stride=SLAB), :] = hi | lo


def _load_slabs_as_rows(ref):
    r = ref.shape[0] // SLAB
    words = [ref[pl.ds(c, r, stride=SLAB), :] for c in range(SLAB)]
    los = [pltpu.bitcast(w << 16, F32) for w in words]
    his = [pltpu.bitcast(w & jnp.uint32(HI_MASK), F32) for w in words]
    return jnp.concatenate(los + his, axis=1)


def _softplus(x):
    return jnp.maximum(x, 0.0) + jnp.log(1.0 + jnp.exp(-jnp.abs(x)))


def _rms(x, eps=EPS):
    return x * lax.rsqrt(jnp.mean(x * x, axis=-1, keepdims=True) + eps)


def _split3(a):
    a1 = a.astype(BF16)
    r1 = a - a1.astype(F32)
    a2 = r1.astype(BF16)
    a3 = (r1 - a2.astype(F32)).astype(BF16)
    return a1, a2, a3


def _dot(a, b):
    return jnp.dot(a, b, preferred_element_type=F32)


def _dot_nt(a, b):
    return lax.dot_general(a, b, (((1,), (1,)), ((), ())), preferred_element_type=F32)


def _exact_right(a, r01, pieces=3):
    return sum(_dot(p, r01) for p in _split3(a)[:pieces])


def _exact_left(m01, a, pieces=3):
    return sum(_dot(m01, p) for p in _split3(a)[:pieces])


def _ada_kernel(c_ref, w_ref, b_ref, o_ref):
    h = _silu(c_ref[...])
    o_ref[...] = jnp.dot(h, w_ref[...], preferred_element_type=F32,
                         precision=lax.Precision.HIGHEST) + b_ref[...]


def ada_mod(c8, w_all, b, layer):
    _, d, n = w_all.shape
    tn = 1536
    return pl.pallas_call(
        _ada_kernel,
        grid=(n // tn,),
        in_specs=[pl.BlockSpec((8, d), lambda j: (0, 0)),
                  pl.BlockSpec((None, d, tn), lambda j: (layer, 0, j)),
                  pl.BlockSpec((1, tn), lambda j: (0, j))],
        out_specs=pl.BlockSpec((8, tn), lambda j: (0, j)),
        out_shape=jax.ShapeDtypeStruct((8, n), F32),
        compiler_params=_params(("parallel",)),
        name="ada_mod",
    )(c8, w_all, b.reshape(1, n))


def _inproj_kernel(x_ref, g_ref, sh_ref, sc_ref, w_ref, wdt_ref, o_ref, odt_ref, h_scr):
    @pl.when(pl.program_id(1) == 0)
    def _():
        h = _rms(x_ref[...]) * g_ref[...]
        h = h * (1.0 + sc_ref[...]) + sh_ref[...]
        hb = h.astype(BF16)
        h_scr[...] = hb
        odt_ref[...] = _dot(hb, wdt_ref[...])

    o_ref[...] = _dot(h_scr[...], w_ref[...]).astype(BF16)


def in_proj(x, g, shift, scale, w_main, w_dt, rows_per_batch):
    t, d = x.shape
    n = w_main.shape[1]
    tm = min(1024, rows_per_batch)
    tn = n // 4
    tpb = rows_per_batch // tm
    mod_spec = pl.BlockSpec((None, 1, d), lambda i, j: (i // tpb, 0, 0))
    return pl.pallas_call(
        _inproj_kernel,
        grid=(t // tm, n // tn),
        in_specs=[pl.BlockSpec((tm, d), lambda i, j: (i, 0)),
                  pl.BlockSpec((1, d), lambda i, j: (0, 0)),
                  mod_spec, mod_spec,
                  pl.BlockSpec((d, tn), lambda i, j: (0, j)),
                  pl.BlockSpec((d, 2 * LANES), lambda i, j: (0, 0))],
        out_specs=[pl.BlockSpec((tm, tn), lambda i, j: (i, j)),
                   pl.BlockSpec((tm, 2 * LANES), lambda i, j: (i, 0))],
        out_shape=[jax.ShapeDtypeStruct((t, n), BF16),
                   jax.ShapeDtypeStruct((t, 2 * LANES), F32)],
        scratch_shapes=[pltpu.VMEM((tm, d), BF16)],
        compiler_params=_params(("parallel", "arbitrary")),
        name="in_proj",
    )(x, g.reshape(1, d), shift, scale, w_main, w_dt)


def _rope(x, cos, sin):
    w = x.shape[-1]
    lane = lax.broadcasted_iota(jnp.int32, x.shape, 1)
    first = (lane % 32) < 16
    swapped = jnp.where(first, pltpu.roll(x, w - 16, 1), pltpu.roll(x, 16, 1))
    return x * cos + swapped * sin


def _dup_halves(x):
    lane = lax.broadcasted_iota(jnp.int32, x.shape, 1)
    lo = lane < HALF
    r = pltpu.roll(x, HALF, 1)
    return jnp.concatenate([jnp.where(lo, x, r), jnp.where(lo, r, x)], axis=1)


def _prep_kernel(qa_ref, qw_ref, ka_ref, va_ref, kw_ref, vw_ref, cos_ref, sin_ref,
                 gq_ref, gk_ref, bdq_ref, bdk_ref,
                 qa_o, qw_o, kda_o, vda_o, kdw_o, vdw_o, *, rope):
    scale = HEAD_DIM ** -0.5
    inv_hd = 1.0 / HEAD_DIM

    def headnorm(x, g, bd):
        ss = _dot((x * x).astype(BF16), bd) * inv_hd
        return x * lax.rsqrt(ss + EPS) * g

    qa = headnorm(qa_ref[...].astype(F32), gq_ref[...], bdq_ref[...])
    ka = headnorm(ka_ref[...].astype(F32), gk_ref[...], bdk_ref[...])
    qw = qw_ref[...].astype(F32)
    kw = kw_ref[...].astype(F32)
    if rope:
        cos = cos_ref[...]
        sin = sin_ref[...]
        qa = _rope(qa, cos, sin)
        qw = _rope(qw, cos, sin)
        ka = _rope(ka, cos[:, :LANES], sin[:, :LANES])
        kw = _rope(kw, cos[:, :LANES], sin[:, :LANES])
    qa_o[...] = (qa * scale).astype(BF16)
    qw_o[...] = (qw * scale).astype(BF16)
    kda_o[...] = _dup_halves(ka).astype(BF16)
    kdw_o[...] = _dup_halves(kw).astype(BF16)
    vda_o[...] = _dup_halves(va_ref[...].astype(F32)).astype(BF16)
    vdw_o[...] = _dup_halves(vw_ref[...].astype(F32)).astype(BF16)


def attn_prep(p, cos, sin, gq, gk, nb, n, rope):
    t = nb * n
    tm = min(512, n)
    spb = n // tm
    hq = N_HEADS * HEAD_DIM
    hk = N_KV * HEAD_DIM
    bdq = (np.arange(hq)[:, None] // HEAD_DIM == np.arange(hq)[None, :] // HEAD_DIM)
    bdq = jnp.asarray(bdq, BF16)
    bdk = bdq[:hk, :hk]
    qspec = lambda c: pl.BlockSpec((tm, hq), lambda s, b: (b * spb + s, c // hq))
    kspec = lambda c: pl.BlockSpec((tm, hk), lambda s, b: (b * spb + s, c // hk))
    tab = pl.BlockSpec((tm, hq), lambda s, b: (s, 0))
    const = lambda shp: pl.BlockSpec(shp, lambda s, b: (0, 0))
    oq = pl.BlockSpec((tm, hq), lambda s, b: (b * spb + s, 0))
    ok = pl.BlockSpec((tm, 2 * hk), lambda s, b: (b * spb + s, 0))
    return pl.pallas_call(
        functools.partial(_prep_kernel, rope=rope),
        grid=(spb, nb),
        in_specs=[qspec(C_QA), qspec(C_QW), kspec(C_KA), kspec(C_VA), kspec(C_KW), kspec(C_VW),
                  tab, tab, const((1, hq)), const((1, hk)), const((hq, hq)), const((hk, hk))],
        out_specs=[oq, oq, ok, ok, ok, ok],
        out_shape=[jax.ShapeDtypeStruct((t, hq), BF16)] * 2 + [jax.ShapeDtypeStruct((t, 2 * hk), BF16)] * 4,
        compiler_params=_params(("parallel", "arbitrary")),
        name="attn_prep",
    )(p, p, p, p, p, p, cos, sin, gq, gk, bdq, bdk)


def rope_tables(n):
    rows = n // GRID_W
    row = jnp.repeat(jnp.arange(rows, dtype=F32), GRID_W)
    col = jnp.tile(jnp.arange(GRID_W, dtype=F32), rows)
    axis_dim = HEAD_DIM // 2
    inv_freq = ROPE_THETA ** (-jnp.arange(0, axis_dim, 2, dtype=F32) / axis_dim)
    ang_r = row[:, None] * inv_freq[None, :]
    ang_c = col[:, None] * inv_freq[None, :]
    cr, sr, cc, sc = jnp.cos(ang_r), jnp.sin(ang_r), jnp.cos(ang_c), jnp.sin(ang_c)
    cos = jnp.concatenate([cr, cr, cc, cc], axis=1)
    sin = jnp.concatenate([-sr, sr, -sc, sc], axis=1)
    return jnp.tile(cos, (1, N_HEADS)), jnp.tile(sin, (1, N_HEADS))


def _pair_operands(kd, vd):
    lane = lax.broadcasted_iota(jnp.int32, kd.shape, 1)
    lo = lane < HALF
    zero = jnp.zeros_like(kd)
    kmats = (jnp.where(lo, kd, zero), jnp.where(lo, zero, kd))
    vstack = jnp.concatenate([jnp.where(lo, vd, zero), jnp.where(lo, zero, vd)], axis=0)
    return kmats, vstack


def _flash_kernel(sink_ref, q_ref, k_ref, v_ref, o_ref, m_scr, l_scr, acc_scr, *, has_sink, nk):
    ki = pl.program_id(2)
    tq = q_ref.shape[0]

    @pl.when(ki == 0)
    def _():
        m_scr[...] = jnp.full(m_scr.shape, NEG_BIG, F32)
        l_scr[...] = jnp.zeros(l_scr.shape, F32)
        acc_scr[...] = jnp.zeros(acc_scr.shape, F32)

    lane_q = lax.broadcasted_iota(jnp.int32, (tq, LANES), 1)
    lo_q = lane_q < HALF
    pairs_per_kv = N_HEADS // N_KV // 2
    for j in range(N_KV):
        kmats, vstack = _pair_operands(k_ref[:, j * LANES:(j + 1) * LANES],
                                       v_ref[:, j * LANES:(j + 1) * LANES])
        for pp in range(pairs_per_kv):
            hp = j * pairs_per_kv + pp
            qp = q_ref[:, hp * LANES:(hp + 1) * LANES]
            ps, alphas = [], []
            for par in range(2):
                h = 2 * hp + par
                s = _dot_nt(qp, kmats[par])
                m_prev = m_scr[h]
                m_new = jnp.maximum(m_prev, jnp.max(s, axis=1, keepdims=True))
                alpha = jnp.exp(m_prev - m_new)
                p = jnp.exp(s - m_new[:, :1])
                l_scr[h] = alpha * l_scr[h] + jnp.sum(p, axis=1, keepdims=True)
                m_scr[h] = m_new
                ps.append(p.astype(BF16))
                alphas.append(alpha)
            pv = _dot(jnp.concatenate(ps, axis=1), vstack)
            sl = slice(hp * LANES, (hp + 1) * LANES)
            acc_scr[:, sl] = acc_scr[:, sl] * jnp.where(lo_q, alphas[0], alphas[1]) + pv

    @pl.when(ki == nk - 1)
    def _():
        for hp in range(N_HEADS // 2):
            ls = []
            for par in range(2):
                h = 2 * hp + par
                l = l_scr[h]
                if has_sink:
                    l = l + jnp.exp(sink_ref[h] - m_scr[h])
                ls.append(l)
            sl = slice(hp * LANES, (hp + 1) * LANES)
            o_ref[:, sl] = (acc_scr[:, sl] / jnp.where(lo_q, ls[0], ls[1])).astype(BF16)


def _flash_bounded_kernel(c_ref, q_ref, k_ref, v_ref, o_ref, lmin_ref, l_scr, acc_scr, *, nk):
    ki = pl.program_id(2)
    tq = q_ref.shape[0]
    tk = k_ref.shape[0]

    @pl.when(ki == 0)
    def _():
        l_scr[...] = jnp.zeros(l_scr.shape, F32)
        acc_scr[...] = jnp.zeros(acc_scr.shape, F32)

    c = c_ref[0]
    pairs_per_kv = N_HEADS // N_KV // 2
    for j in range(N_KV):
        kmats, vstack = _pair_operands(k_ref[:, j * LANES:(j + 1) * LANES],
                                       v_ref[:, j * LANES:(j + 1) * LANES])
        for pp in range(pairs_per_kv):
            hp = j * pairs_per_kv + pp
            qp = q_ref[:, hp * LANES:(hp + 1) * LANES]
            ps = []
            for par in range(2):
                h = 2 * hp + par
                p = jnp.exp(_dot_nt(qp, kmats[par]) - c)
                part = p[:, 0:LANES]
                for cb in range(1, tk // LANES):
                    part = part + p[:, cb * LANES:(cb + 1) * LANES]
                l_scr[h] = l_scr[h] + part
                ps.append(p.astype(BF16))
            sl = slice(hp * LANES, (hp + 1) * LANES)
            acc_scr[:, sl] = acc_scr[:, sl] + _dot(jnp.concatenate(ps, axis=1), vstack)

    @pl.when(ki == nk - 1)
    def _():
        lo_q = lax.broadcasted_iota(jnp.int32, (tq, LANES), 1) < HALF
        mins = []
        for hp in range(N_HEADS // 2):
            ls = [jnp.sum(l_scr[2 * hp + par], axis=1, keepdims=True) for par in range(2)]
            sl = slice(hp * LANES, (hp + 1) * LANES)
            o_ref[:, sl] = (acc_scr[:, sl] / jnp.where(lo_q, ls[0], ls[1])).astype(BF16)
            mins += [jnp.broadcast_to(jnp.min(l, axis=0, keepdims=True), (1, LANES)) for l in ls]
        lmin_ref[...] = jnp.concatenate(mins, axis=0)


FLASH_MIN_DENOM = 1e-30


def flash_attn_bounded(q, kd, vd, bound, nb, n, m):
    tq = min(1024, n)
    tk = _pick_tile(m, (768, 512, 256))
    nq, nk = n // tq, m // tk
    hq = N_HEADS * HEAD_DIM
    o, lmin = pl.pallas_call(
        functools.partial(_flash_bounded_kernel, nk=nk),
        grid=(nb, nq, nk),
        in_specs=[pl.BlockSpec(memory_space=pltpu.SMEM),
                  pl.BlockSpec((tq, hq), lambda b, i, k: (b * nq + i, 0)),
                  pl.BlockSpec((tk, 2 * LANES), lambda b, i, k: (b * nk + k, 0)),
                  pl.BlockSpec((tk, 2 * LANES), lambda b, i, k: (b * nk + k, 0))],
        out_specs=[pl.BlockSpec((tq, hq), lambda b, i, k: (b * nq + i, 0)),
                   pl.BlockSpec((N_HEADS, LANES), lambda b, i, k: (b * nq + i, 0))],
        out_shape=[jax.ShapeDtypeStruct((nb * n, hq), BF16),
                   jax.ShapeDtypeStruct((nb * nq * N_HEADS, LANES), F32)],
        scratch_shapes=[pltpu.VMEM((N_HEADS, tq, LANES), F32),
                        pltpu.VMEM((tq, hq), F32)],
        compiler_params=_params(("parallel", "parallel", "arbitrary")),
        name="flash_attn_bounded",
    )(bound, q, kd, vd)
    ok = jnp.min(lmin) > FLASH_MIN_DENOM
    return lax.cond(ok, lambda: o,
                    lambda: flash_attn(q, kd, vd, jnp.zeros((N_HEADS,), F32), nb, n, m, has_sink=False))


def _pick_tile(m, cands):
    for c in cands:
        if m % c == 0:
            return c
    raise ValueError(f"no tile for {m}")


def flash_attn(q, kd, vd, sink, nb, n, m, has_sink):
    tq = min(512, n)
    tk = _pick_tile(m, (768, 512, 256))
    nq, nk = n // tq, m // tk
    hq = N_HEADS * HEAD_DIM
    return pl.pallas_call(
        functools.partial(_flash_kernel, has_sink=has_sink, nk=nk),
        grid=(nb, nq, nk),
        in_specs=[pl.BlockSpec(memory_space=pltpu.SMEM),
                  pl.BlockSpec((tq, hq), lambda b, i, k: (b * nq + i, 0)),
                  pl.BlockSpec((tk, 2 * LANES), lambda b, i, k: (b * nk + k, 0)),
                  pl.BlockSpec((tk, 2 * LANES), lambda b, i, k: (b * nk + k, 0))],
        out_specs=pl.BlockSpec((tq, hq), lambda b, i, k: (b * nq + i, 0)),
        out_shape=jax.ShapeDtypeStruct((nb * n, hq), BF16),
        scratch_shapes=[pltpu.VMEM((N_HEADS, tq, LANES), F32),
                        pltpu.VMEM((N_HEADS, tq, LANES), F32),
                        pltpu.VMEM((tq, hq), F32)],
        compiler_params=_params(("parallel", "parallel", "arbitrary")),
        name="flash_attn",
    )(sink, q, kd, vd)


def _window_kernel(sink_ref, q_ref, kp_ref, km_ref, kn_ref, vp_ref, vm_ref, vn_ref, kc_ref, vc_ref,
                   o_ref, *, n, tq):
    i = pl.program_id(1)
    span = tq + 2 * WINDOW
    q0 = i * tq
    r = lax.broadcasted_iota(jnp.int32, (tq, span), 0)
    c = lax.broadcasted_iota(jnp.int32, (tq, span), 1)
    kpos = c + (q0 - WINDOW)
    ok = (c >= r) & (c <= r + 2 * WINDOW) & (kpos >= 0) & (kpos < n)
    lane_q = lax.broadcasted_iota(jnp.int32, (tq, LANES), 1)
    lo_q = lane_q < HALF
    k_all = jnp.concatenate([kp_ref[...], km_ref[...], kn_ref[...], kc_ref[...]], axis=0)
    v_all = jnp.concatenate([vp_ref[...], vm_ref[...], vn_ref[...], vc_ref[...]], axis=0)
    bias = jnp.concatenate([jnp.where(ok, 0.0, NEG_BIG), jnp.zeros((tq, kc_ref.shape[0]), F32)], axis=1)
    pairs_per_kv = N_HEADS // N_KV // 2
    for j in range(N_KV):
        js = slice(j * LANES, (j + 1) * LANES)
        kmats, vstack = _pair_operands(k_all[:, js], v_all[:, js])
        for pp in range(pairs_per_kv):
            hp = j * pairs_per_kv + pp
            qp = q_ref[:, hp * LANES:(hp + 1) * LANES]
            ps, ls = [], []
            for par in range(2):
                h = 2 * hp + par
                s = _dot_nt(qp, kmats[par]) + bias
                snk = sink_ref[h]
                m = jnp.maximum(jnp.max(s, axis=1, keepdims=True), snk)
                p = jnp.exp(s - m)
                ls.append(jnp.sum(p, axis=1, keepdims=True) + jnp.exp(snk - m))
                ps.append(p.astype(BF16))
            o = _dot(jnp.concatenate(ps, axis=1), vstack)
            o_ref[:, hp * LANES:(hp + 1) * LANES] = (o / jnp.where(lo_q, ls[0], ls[1])).astype(BF16)


def window_attn(q, kd, vd, kdc, vdc, sink, nb, n, mc):
    tq = 2 * WINDOW
    nq = n // tq
    wb = n // WINDOW
    hq = N_HEADS * HEAD_DIM
    prev = pl.BlockSpec((WINDOW, 2 * LANES), lambda b, i: (b * wb + jnp.maximum(2 * i - 1, 0), 0))
    main = pl.BlockSpec((tq, 2 * LANES), lambda b, i: (b * nq + i, 0))
    nxt = pl.BlockSpec((WINDOW, 2 * LANES), lambda b, i: (b * wb + jnp.minimum(2 * i + 2, wb - 1), 0))
    ctx = pl.BlockSpec((mc, 2 * LANES), lambda b, i: (b, 0))
    return pl.pallas_call(
        functools.partial(_window_kernel, n=n, tq=tq),
        grid=(nb, nq),
        in_specs=[pl.BlockSpec(memory_space=pltpu.SMEM),
                  pl.BlockSpec((tq, hq), lambda b, i: (b * nq + i, 0)),
                  prev, main, nxt, prev, main, nxt, ctx, ctx],
        out_specs=pl.BlockSpec((tq, hq), lambda b, i: (b * nq + i, 0)),
        out_shape=jax.ShapeDtypeStruct((nb * n, hq), BF16),
        compiler_params=_params(("parallel", "parallel")),
        name="window_attn",
    )(sink, q, kd, kd, kd, vd, vd, vd, kdc, vdc)


HALO = 16


def _conv_kernel(xm_ref, xp_ref, xn_ref, bm_ref, bp_ref, bn_ref, wx_ref, bx_ref, wb_ref, bb_ref,
                 ox_ref, ob_ref, *, nt):
    i = pl.program_id(1)
    has_prev = jnp.where(i > 0, 1.0, 0.0)
    has_next = jnp.where(i < nt - 1, 1.0, 0.0)

    def conv(m_ref, p_ref, n_ref, w_ref, b_ref, o_ref):
        x = m_ref[...].astype(F32)
        tl = x.shape[0]
        row = lax.broadcasted_iota(jnp.int32, x.shape, 0)
        before = p_ref[...].astype(F32)[HALO - 1:HALO, :] * has_prev
        after = n_ref[...].astype(F32)[0:1, :] * has_next
        xm1 = jnp.where(row == 0, before, pltpu.roll(x, 1, 0))
        xp1 = jnp.where(row == tl - 1, after, pltpu.roll(x, tl - 1, 0))
        w = w_ref[...]
        y = xm1 * w[0:1, :] + x * w[1:2, :] + xp1 * w[2:3, :] + b_ref[...]
        o_ref[...] = _silu(y).astype(BF16)

    conv(xm_ref, xp_ref, xn_ref, wx_ref, bx_ref, ox_ref)
    conv(bm_ref, bp_ref, bn_ref, wb_ref, bb_ref, ob_ref)


def ssm_conv(p, conv_w, conv_b, nb, n):
    tl = min(512, n)
    nt = n // tl
    hb = n // HALO
    hpt = tl // HALO
    cx, cb = SSM_INNER, 2 * SSM_GROUPS * SSM_STATE

    def specs(width, col):
        cblk = col // width
        return (pl.BlockSpec((tl, width), lambda b, i: (b * nt + i, cblk)),
                pl.BlockSpec((HALO, width), lambda b, i: (b * hb + jnp.maximum(i * hpt - 1, 0), cblk)),
                pl.BlockSpec((HALO, width), lambda b, i: (b * hb + jnp.minimum((i + 1) * hpt, hb - 1), cblk)))

    const = lambda shp: pl.BlockSpec(shp, lambda b, i: (0, 0))
    xm, xp, xn = specs(cx, C_XS)
    bm, bp, bn = specs(cb, C_BC)
    return pl.pallas_call(
        functools.partial(_conv_kernel, nt=nt),
        grid=(nb, nt),
        in_specs=[xm, xp, xn, bm, bp, bn, const((3, cx)), const((1, cx)), const((3, cb)), const((1, cb))],
        out_specs=[pl.BlockSpec((tl, cx), lambda b, i: (b * nt + i, 0)),
                   pl.BlockSpec((tl, cb), lambda b, i: (b * nt + i, 0))],
        out_shape=[jax.ShapeDtypeStruct((nb * n, cx), BF16), jax.ShapeDtypeStruct((nb * n, cb), BF16)],
        compiler_params=_params(("parallel", "parallel")),
        name="ssm_conv",
    )(p, p, p, p, p, p, conv_w[:, :cx], conv_b[:cx].reshape(1, cx), conv_w[:, cx:], conv_b[cx:].reshape(1, cb))


def _ssd_chunk(d, r0, xs_ref, bc_ref, dt_ref, bias, aneg, tri, rep, st_scr, y_ref):
    q = SSM_CHUNK
    rows = slice(r0, r0 + q)
    gw = SSM_INNER // SSM_GROUPS
    dt = _softplus(dt_ref[rows, :] + bias)
    a = dt * aneg
    ac = _exact_left(tri, a, pieces=2)
    act = ac.T
    acx = _exact_right(ac, rep, pieces=2)
    dtx = _dot(dt.astype(BF16), rep)
    totx = acx[q - 1:q, :] if d == 0 else acx[0:1, :]
    xd = xs_ref[rows, :].astype(F32) * dtx
    xd_b = xd.astype(BF16)
    xe = (xd * jnp.exp(totx - acx)).astype(BF16)
    ein = jnp.exp(acx)
    keep = tri > 0
    lane = lax.broadcasted_iota(jnp.int32, (q, LANES), 1)
    lo = lane < HALF
    zero = jnp.zeros((q, LANES), BF16)
    hpg = SSM_HEADS // SSM_GROUPS
    for g in range(SSM_GROUPS):
        bg = bc_ref[rows, g * SSM_STATE:(g + 1) * SSM_STATE]
        cg = bc_ref[rows, (SSM_GROUPS + g) * SSM_STATE:(SSM_GROUPS + g + 1) * SSM_STATE]
        cb = _dot_nt(cg, bg)
        st = st_scr[d, g]
        yoff = _dot(cg, st.astype(BF16)) * ein[:, g * gw:(g + 1) * gw]
        for hp in range(hpg // 2):
            gs = []
            for par in range(2):
                h = g * hpg + 2 * hp + par
                seg = ac[:, h:h + 1] - act[h:h + 1, :]
                gs.append((cb * jnp.exp(jnp.where(keep, seg, NEG_BIG))).astype(BF16))
            c0 = g * gw + hp * LANES
            xp = xd_b[:, c0:c0 + LANES]
            xstack = jnp.concatenate([jnp.where(lo, xp, zero), jnp.where(lo, zero, xp)], axis=0)
            ydiag = _dot(jnp.concatenate(gs, axis=1), xstack)
            y_ref[rows, c0:c0 + LANES] = (ydiag + yoff[:, hp * LANES:(hp + 1) * LANES]).astype(BF16)
        bgt = bg.astype(F32).T.astype(BF16)
        cs = _dot(bgt, xe[:, g * gw:(g + 1) * gw])
        st_scr[d, g] = st * jnp.exp(totx[:, g * gw:(g + 1) * gw]) + cs


def _ssd_kernel(xf_ref, bcf_ref, dtf_ref, xb_ref, bcb_ref, dtb_ref, bias_ref, aneg_ref, tri_ref, rep_ref,
                s0_ref, yf_ref, yb_ref, sfin_ref, st_scr, *, nsteps):
    k = pl.program_id(1)

    @pl.when(k == 0)
    def _():
        st_scr[...] = s0_ref[...]

    rep = rep_ref[...]
    for c in range(SSD_CHUNKS_PER_STEP):
        _ssd_chunk(0, c * SSM_CHUNK, xf_ref, bcf_ref, dtf_ref, bias_ref[0], aneg_ref[0], tri_ref[0], rep,
                   st_scr, yf_ref)
        _ssd_chunk(1, (SSD_CHUNKS_PER_STEP - 1 - c) * SSM_CHUNK, xb_ref, bcb_ref, dtb_ref, bias_ref[1],
                   aneg_ref[1], tri_ref[1], rep, st_scr, yb_ref)

    @pl.when(k == nsteps - 1)
    def _():
        sfin_ref[...] = st_scr[...]


SSD_CHUNKS_PER_STEP = 2


def ssd_scan(u_xs, u_bc, dt_raw, bias2, aneg2, s0, nb, n):
    q = SSM_CHUNK
    nc = n // q
    gw = SSM_INNER // SSM_GROUPS
    idx = np.arange(q)
    tri = np.stack([idx[:, None] >= idx[None, :], idx[:, None] <= idx[None, :]]).astype(np.float32)
    rep = (np.arange(LANES)[:, None] == np.arange(SSM_INNER)[None, :] // SSM_P).astype(np.float32)

    ns = nc // SSD_CHUNKS_PER_STEP
    rb = q * SSD_CHUNKS_PER_STEP
    fwd = lambda b, k: b * ns + k
    bwd = lambda b, k: b * ns + (ns - 1 - k)
    cbc = 2 * SSM_GROUPS * SSM_STATE
    state = pl.BlockSpec((None, 2, SSM_GROUPS, SSM_STATE, gw), lambda b, k: (b, 0, 0, 0, 0))
    const = lambda shp: pl.BlockSpec(shp, lambda b, k: (0,) * len(shp))
    yf, yb, sfin = pl.pallas_call(
        functools.partial(_ssd_kernel, nsteps=ns),
        grid=(nb, ns),
        in_specs=[pl.BlockSpec((rb, SSM_INNER), lambda b, k: (fwd(b, k), 0)),
                  pl.BlockSpec((rb, cbc), lambda b, k: (fwd(b, k), 0)),
                  pl.BlockSpec((rb, LANES), lambda b, k: (fwd(b, k), 0)),
                  pl.BlockSpec((rb, SSM_INNER), lambda b, k: (bwd(b, k), 0)),
                  pl.BlockSpec((rb, cbc), lambda b, k: (bwd(b, k), 0)),
                  pl.BlockSpec((rb, LANES), lambda b, k: (bwd(b, k), 1)),
                  const((2, 1, LANES)), const((2, 1, LANES)), const((2, q, q)), const((LANES, SSM_INNER)),
                  state],
        out_specs=[pl.BlockSpec((rb, SSM_INNER), lambda b, k: (fwd(b, k), 0)),
                   pl.BlockSpec((rb, SSM_INNER), lambda b, k: (bwd(b, k), 0)),
                   state],
        out_shape=[jax.ShapeDtypeStruct((nb * n, SSM_INNER), BF16),
                   jax.ShapeDtypeStruct((nb * n, SSM_INNER), BF16),
                   jax.ShapeDtypeStruct((nb, 2, SSM_GROUPS, SSM_STATE, gw), F32)],
        scratch_shapes=[pltpu.VMEM((2, SSM_GROUPS, SSM_STATE, gw), F32)],
        compiler_params=_params(("parallel", "arbitrary")),
        name="ssd_scan",
    )(u_xs, u_bc, dt_raw, u_xs, u_bc, dt_raw, bias2, aneg2, jnp.asarray(tri, BF16), jnp.asarray(rep, BF16), s0)
    return (yf, yb), sfin


def _post_kernel(oa_ref, ow_ref, yf_ref, yb_ref, xs_ref, z_ref, gt_ref, x_ref,
                 dsk_ref, nw_ref, wa_ref, ww_ref, ws_ref, wo_ref, gpost_ref, gate_ref,
                 gpre_ref, sh_ref, sc_ref, wrh_ref, wrl_ref,
                 xo_ref, h_ref, lg_ref):
    y = yf_ref[...].astype(F32) + yb_ref[...].astype(F32) + dsk_ref[...] * xs_ref[...].astype(F32)
    u = y * _silu(z_ref[...].astype(F32))
    gw = SSM_INNER // SSM_GROUPS
    ys = jnp.concatenate([_rms(u[:, g * gw:(g + 1) * gw]) for g in range(SSM_GROUPS)], axis=1)
    ys = (ys * nw_ref[...]).astype(BF16)
    d = x_ref.shape[1]
    ga = jax.nn.sigmoid(gt_ref[:, 0:d].astype(F32))
    gw_ = jax.nn.sigmoid(gt_ref[:, d:2 * d].astype(F32))
    gs = jax.nn.sigmoid(gt_ref[:, 2 * d:3 * d].astype(F32))
    m = ga * _dot(oa_ref[...], wa_ref[...]) + gw_ * _dot(ow_ref[...], ww_ref[...]) + gs * _dot(ys, ws_ref[...])
    ml = _dot(m.astype(BF16), wo_ref[...])
    xn = x_ref[...] + gate_ref[...] * (_rms(ml) * gpost_ref[...])
    xo_ref[...] = xn
    h = (_rms(xn) * gpre_ref[...]) * (1.0 + sc_ref[...]) + sh_ref[...]
    _store_rows_as_slabs(h_ref, h)
    hb = h.astype(BF16)
    hl =(h - hb.astype(F32)).astype(BF16)
    lg_ref[...] = _dot_nt(wrh_ref[...], hb) + _dot_nt(wrh_ref[...], hl) + _dot_nt(wrl_ref[...], hb)


def post_mixer(oa, ow, y2, u_xs, p, x, dskip, norm_w, wa, ww, ws, wo, gpost, gate, gpre, shift, scale,
               wr_hi, wr_lo, rows_per_batch):
    t, d = x.shape
    tm = min(256, rows_per_batch)
    tpb = rows_per_batch // tm
    nt = t // tm
    hq = N_HEADS * HEAD_DIM
    row = lambda w, c=0: pl.BlockSpec((tm, w), lambda i: (i, c // w))
    const = lambda shp: pl.BlockSpec(shp, lambda i: (0,) * len(shp))
    mod = pl.BlockSpec((None, 1, d), lambda i: (i // tpb, 0, 0))
    return pl.pallas_call(
        _post_kernel,
        grid=(nt,),
        in_specs=[row(hq), row(hq), row(SSM_INNER), row(SSM_INNER),
                  row(SSM_INNER), row(SSM_INNER, C_Z), row(3 * d, C_GATES), row(d),
                  const((1, SSM_INNER)), const((1, SSM_INNER)),
                  const((hq, d)), const((hq, d)), const((SSM_INNER, d)), const((d, d)),
                  const((1, d)), mod, const((1, d)), mod, mod,
                  const((N_EXPERTS, d)), const((N_EXPERTS, d))],
        out_specs=[row(d), pl.BlockSpec((tm * SLAB, LANES), lambda i: (i, 0)),
                   pl.BlockSpec((N_EXPERTS, tm), lambda i: (0, i))],
        out_shape=[jax.ShapeDtypeStruct((t, d), F32), jax.ShapeDtypeStruct((t * SLAB, LANES), U32),
                   jax.ShapeDtypeStruct((N_EXPERTS, t), F32)],
        compiler_params=_params(("parallel",)),
        name="post_mixer",
    )(oa, ow, y2[0], y2[1], u_xs, p, p, x, dskip, norm_w, wa, ww, ws, wo, gpost, gate, gpre, shift, scale,
      wr_hi, wr_lo)


def _route_kernel(lg_ref, b_ref, ei_ref, w_ref):
    scores = jax.nn.sigmoid(lg_ref[...])
    sel = scores + b_ref[...]
    tt = sel.shape[1]
    per = N_EXPERTS // N_EXPERT_GROUPS
    r8 = lax.broadcasted_iota(jnp.int32, (per, tt), 0).astype(F32)
    ninf = -jnp.inf

    def argmax_rows(x, rows, nrows):
        m = jnp.max(x, axis=0, keepdims=True)
        idx = jnp.min(jnp.where(x == m, rows, float(nrows)), axis=0, keepdims=True)
        return m, idx

    gscores = []
    for g in range(N_EXPERT_GROUPS):
        blk = sel[g * per:(g + 1) * per, :]
        m1, i1 = argmax_rows(blk, r8, per)
        m2 = jnp.max(jnp.where(r8 == i1, ninf, blk), axis=0, keepdims=True)
        gscores.append(m1 + m2)
    cur = jnp.concatenate(gscores, axis=0)
    rg = lax.broadcasted_iota(jnp.int32, cur.shape, 0).astype(F32)
    chosen = jnp.zeros(cur.shape, F32)
    for _ in range(TOPK_GROUPS):
        _, gi = argmax_rows(cur, rg, N_EXPERT_GROUPS)
        hit = rg == gi
        chosen = jnp.where(hit, 1.0, chosen)
        cur = jnp.where(hit, ninf, cur)
    gmask = jnp.concatenate([jnp.broadcast_to(chosen[g:g + 1, :], (per, tt)) for g in range(N_EXPERT_GROUPS)],
                            axis=0)
    cur = jnp.where(gmask > 0, sel, ninf)
    re = lax.broadcasted_iota(jnp.int32, cur.shape, 0).astype(F32)
    idxs, ws = [], []
    for _ in range(TOP_K):
        _, ei = argmax_rows(cur, re, N_EXPERTS)
        hit = re == ei
        ws.append(jnp.sum(jnp.where(hit, scores, 0.0), axis=0, keepdims=True))
        idxs.append(ei)
        cur = jnp.where(hit, ninf, cur)
    w = jnp.concatenate(ws, axis=0)
    w_ref[...] = w / jnp.sum(w, axis=0, keepdims=True) * ROUTED_SCALE
    ei_ref[...] = jnp.concatenate(idxs, axis=0).astype(jnp.int32)


def route(logits_t, b_router):
    e, t = logits_t.shape
    tt = 512
    return pl.pallas_call(
        _route_kernel,
        grid=(t // tt,),
        in_specs=[pl.BlockSpec((e, tt), lambda i: (0, i)), pl.BlockSpec((e, 1), lambda i: (0, 0))],
        out_specs=[pl.BlockSpec((TOP_K, tt), lambda i: (0, i)), pl.BlockSpec((TOP_K, tt), lambda i: (0, i))],
        out_shape=[jax.ShapeDtypeStruct((TOP_K, t), jnp.int32), jax.ShapeDtypeStruct((TOP_K, t), F32)],
        compiler_params=_params(("parallel",)),
        name="route",
    )(logits_t, b_router.reshape(e, 1))


MOE_ROWS = 512
PLAN_TOKENS = 512


def _moe_geometry(t):
    nblk = -(-(t * TOP_K + N_EXPERTS * (MOE_ROWS - 1)) // MOE_ROWS)
    return nblk, nblk * MOE_ROWS


def _plan_kernel(ei_ref, ut_ref, tril_ref, dest_ref, tab_ref, be_ref, cnt_scr, run_scr):
    ph = pl.program_id(0)
    i = pl.program_id(1)
    ei = ei_ref[...]
    tt = ei.shape[1]
    re = lax.broadcasted_iota(jnp.int32, (N_EXPERTS, tt), 0)
    hits = [re == ei[k:k + 1, :] for k in range(TOP_K)]
    oh = jnp.zeros((N_EXPERTS, tt), F32)
    for k in range(TOP_K):
        oh = oh + jnp.where(hits[k], 1.0, 0.0)

    @pl.when((ph == 0) & (i == 0))
    def _():
        cnt_scr[...] = jnp.zeros(cnt_scr.shape, F32)

    @pl.when(ph == 0)
    def _():
        cnt_scr[...] = cnt_scr[...] + jnp.sum(oh, axis=1, keepdims=True)

    @pl.when((ph == 1) & (i == 0))
    def _():
        cnt = cnt_scr[...]
        padded = ((cnt.astype(jnp.int32) + (MOE_ROWS - 1)) & (-MOE_ROWS)).astype(F32)
        pad_end = _exact_left(tril_ref[...], padded)
        pad_start = pad_end - padded
        run_scr[...] = pad_start
        tab_ref[0] = pad_start
        tab_ref[1] = pad_end
        nbp = be_ref.shape[1]
        blk0 = (lax.broadcasted_iota(jnp.int32, (N_EXPERTS, nbp), 1) * MOE_ROWS).astype(F32)
        be = jnp.sum(jnp.where(pad_end[:, :1] <= blk0, 1.0, 0.0), axis=0, keepdims=True)
        be_ref[...] = jnp.broadcast_to(jnp.minimum(be, N_EXPERTS - 1.0), be_ref.shape).astype(jnp.int32)

    @pl.when(ph == 1)
    def _():
        cin = _dot(oh.astype(BF16), ut_ref[...])
        pos = run_scr[:, :1] + (cin - oh)
        rows = [jnp.sum(jnp.where(hits[k], pos, 0.0), axis=0, keepdims=True) for k in range(TOP_K)]
        dest_ref[...] = jnp.concatenate(rows, axis=0).astype(jnp.int32)
        run_scr[...] = run_scr[...] + cin[:, tt - 1:tt]


def moe_plan(eidx_t):
    k, t = eidx_t.shape
    tt = PLAN_TOKENS
    nt = t // tt
    nblk, _ = _moe_geometry(t)
    nbp = -(-nblk // LANES) * LANES
    ut = jnp.asarray(np.arange(tt)[:, None] <= np.arange(tt)[None, :], BF16)
    tril = jnp.asarray(np.arange(N_EXPERTS)[:, None] >= np.arange(N_EXPERTS)[None, :], BF16)
    return pl.pallas_call(
        _plan_kernel,
        grid=(2, nt),
        in_specs=[pl.BlockSpec((k, tt), lambda p, i: (0, i)),
                  pl.BlockSpec((tt, tt), lambda p, i: (0, 0)),
                  pl.BlockSpec((N_EXPERTS, N_EXPERTS), lambda p, i: (0, 0))],
        out_specs=[pl.BlockSpec((k, tt), lambda p, i: (0, i * p)),
                   pl.BlockSpec((2, N_EXPERTS, LANES), lambda p, i: (0, 0, 0)),
                   pl.BlockSpec((8, nbp), lambda p, i: (0, 0))],
        out_shape=[jax.ShapeDtypeStruct((k, t), jnp.int32),
                   jax.ShapeDtypeStruct((2, N_EXPERTS, LANES), F32),
                   jax.ShapeDtypeStruct((8, nbp), jnp.int32)],
        scratch_shapes=[pltpu.VMEM((N_EXPERTS, LANES), F32), pltpu.VMEM((N_EXPERTS, LANES), F32)],
        compiler_params=_params(("arbitrary", "arbitrary")),
        name="moe_plan",
    )(eidx_t, ut, tril)


def _dispatch_kernel(dest_ref, pstart_ref, pend_ref, h_ref, xs_hbm, zero_scr, sem):
    i = pl.program_id(0)
    tt = h_ref.shape[0] // SLAB

    def slab(ref, row, n=1):
        return ref.at[pl.ds(pl.multiple_of(row * SLAB, SLAB), n * SLAB), :]

    @pl.when(i == 0)
    def _():
        zero_scr[...] = jnp.zeros(zero_scr.shape, U32)
        nblk = xs_hbm.shape[0] // (MOE_ROWS * SLAB)
        n_used = pend_ref[N_EXPERTS - 1] // MOE_ROWS

        def zero_block(row0):
            return pltpu.make_async_copy(zero_scr, slab(xs_hbm, row0, MOE_ROWS), sem)

        def seg_start(e, c):
            @pl.when(pend_ref[e] > pstart_ref[e])
            def _():
                zero_block(pend_ref[e] - MOE_ROWS).start()
            return c

        def seg_wait(e, c):
            @pl.when(pend_ref[e] > pstart_ref[e])
            def _():
                zero_block(pend_ref[e] - MOE_ROWS).wait()
            return c

        def tail_start(b, c):
            zero_block(b * MOE_ROWS).start()
            return c

        def tail_wait(b, c):
            zero_block(b * MOE_ROWS).wait()
            return c

        lax.fori_loop(0, N_EXPERTS, seg_start, 0)
        lax.fori_loop(n_used, nblk, tail_start, 0)
        lax.fori_loop(0, N_EXPERTS, seg_wait, 0)
        lax.fori_loop(n_used, nblk, tail_wait, 0)

    def issue(t, c):
        for k in range(TOP_K):
            pltpu.make_async_copy(slab(h_ref, t), slab(xs_hbm, dest_ref[k, t]), sem).start(priority=k % 2)
        return c

    lax.fori_loop(0, tt, issue, 0)
    for k in range(TOP_K):
        pltpu.make_async_copy(h_ref, slab(xs_hbm, 0, tt), sem).wait()


def moe_dispatch(dest_t, pad_start, pad_end, h):
    t = h.shape[0] // SLAB
    tt = PLAN_TOKENS
    _, cap = _moe_geometry(t)
    smem = pl.BlockSpec(memory_space=pltpu.SMEM)
    return pl.pallas_call(
        _dispatch_kernel,
        grid=(t // tt,),
        in_specs=[pl.BlockSpec((TOP_K, tt), lambda i: (0, i), memory_space=pltpu.SMEM), smem, smem,
                  pl.BlockSpec((tt * SLAB, LANES), lambda i: (i, 0))],
        out_specs=pl.BlockSpec(memory_space=pl.ANY),
        out_shape=jax.ShapeDtypeStruct((cap * SLAB, LANES), U32),
        scratch_shapes=[pltpu.VMEM((MOE_ROWS * SLAB, LANES), U32), pltpu.SemaphoreType.DMA(())],
        compiler_params=_params(("arbitrary",)),
        name="moe_dispatch",
    )(dest_t, pad_start, pad_end, h)


def _expert_kernel(be_ref, nu_ref, x_ref, wg_ref, wu_ref, wd_ref, o_ref):
    used = pl.program_id(0) < nu_ref[0]

    @pl.when(used)
    def _():
        x = _load_slabs_as_rows(x_ref).astype(BF16)
        hg = _dot(x, wg_ref[...].astype(BF16))
        hu = _dot(x, wu_ref[...].astype(BF16))
        hb = (_silu(hg) * hu).astype(BF16)
        _store_rows_as_slabs(o_ref, _dot(hb, wd_ref[...].astype(BF16)))

    @pl.when(jnp.logical_not(used))
    def _():
        o_ref[...] = jnp.zeros(o_ref.shape, U32)


def expert_ffn(blk_e, n_used, xs, we_gate, we_up, we_down, layer):
    d, de = we_gate.shape[2:]
    nblk = xs.shape[0] // (MOE_ROWS * SLAB)
    blk = lambda i, nu: jnp.minimum(i, nu[0] - 1)
    grid_spec = pltpu.PrefetchScalarGridSpec(
        num_scalar_prefetch=2,
        grid=(nblk,),
        in_specs=[pl.BlockSpec((MOE_ROWS * SLAB, LANES), lambda i, be, nu: (blk(i, nu), 0)),
                  pl.BlockSpec((None, None, d, de), lambda i, be, nu: (layer, be[blk(i, nu)], 0, 0)),
                  pl.BlockSpec((None, None, d, de), lambda i, be, nu: (layer, be[blk(i, nu)], 0, 0)),
                  pl.BlockSpec((None, None, de, d), lambda i, be, nu: (layer, be[blk(i, nu)], 0, 0))],
        out_specs=pl.BlockSpec((MOE_ROWS * SLAB, LANES), lambda i, be, nu: (i, 0)),
    )
    return pl.pallas_call(
        _expert_kernel,
        grid_spec=grid_spec,
        out_shape=jax.ShapeDtypeStruct(xs.shape, U32),
        compiler_params=_params(("arbitrary",)),
        name="expert_ffn",
    )(blk_e, n_used, xs, we_gate, we_up, we_down)


def _ffn_tail_kernel(dest_ref, dnext_ref, w_ref, h_ref, x_ref, wg_ref, wu_ref, wd_ref, gpost_ref, gate_ref, ys_hbm,
                     xo_ref, buf, sem, *, nt):
    i = pl.program_id(0)
    tt = h_ref.shape[0] // SLAB
    cur = i % 2

    def slab(ref, row, n=1):
        return ref.at[pl.ds(pl.multiple_of(row * SLAB, SLAB), n * SLAB), :]

    def gather(idx_ref, slot):
        def issue(t, c):
            for k in range(TOP_K):
                pltpu.make_async_copy(slab(ys_hbm, idx_ref[k, t]), slab(buf.at[slot, k], t),
                                      sem.at[slot]).start(priority=k % 2)
            return c
        lax.fori_loop(0, tt, issue, 0)

    @pl.when(i == 0)
    def _():
        gather(dest_ref, 0)

    @pl.when(i + 1 < nt)
    def _():
        gather(dnext_ref, 1 - cur)

    h = _load_slabs_as_rows(h_ref).astype(BF16)
    hs = (_silu(_dot(h, wg_ref[...])) * _dot(h, wu_ref[...])).astype(BF16)
    f = _dot(hs, wd_ref[...])
    w = w_ref[...]
    for k in range(TOP_K):
        pltpu.make_async_copy(slab(ys_hbm, 0, tt), buf.at[cur, k], sem.at[cur]).wait()
    for k in range(TOP_K):
        f = f + _load_slabs_as_rows(buf.at[cur, k]) * w[:, k:k + 1]
    xo_ref[...] = x_ref[...] + gate_ref[...] * (_rms(f) * gpost_ref[...])


def ffn_tail(dest_t, wts, h, x, ys, wsg, wsu, wsd, gpost, gate, rows_per_batch):
    t, d = x.shape
    ds = wsg.shape[1]
    tm = min(256, rows_per_batch)
    tpb = rows_per_batch // tm
    nt = t // tm
    row = pl.BlockSpec((tm, d), lambda i: (i, 0))
    const = lambda shp: pl.BlockSpec(shp, lambda i: (0,) * len(shp))
    return pl.pallas_call(
        functools.partial(_ffn_tail_kernel, nt=nt),
        grid=(nt,),
        in_specs=[pl.BlockSpec((TOP_K, tm), lambda i: (0, i), memory_space=pltpu.SMEM),
                  pl.BlockSpec((TOP_K, tm), lambda i: (0, jnp.minimum(i + 1, nt - 1)), memory_space=pltpu.SMEM),
                  pl.BlockSpec((tm, TOP_K), lambda i: (i, 0)),
                  pl.BlockSpec((tm * SLAB, LANES), lambda i: (i, 0)),
                  row, const((d, ds)), const((d, ds)), const((ds, d)), const((1, d)),
                  pl.BlockSpec((None, 1, d), lambda i: (i // tpb, 0, 0)),
                  pl.BlockSpec(memory_space=pl.ANY)],
        out_specs=row,
        out_shape=jax.ShapeDtypeStruct((t, d), F32),
        scratch_shapes=[pltpu.VMEM((2, TOP_K, tm * SLAB, LANES), U32), pltpu.SemaphoreType.DMA((2,))],
        compiler_params=_params(("arbitrary",)),
        name="ffn_tail",
    )(dest_t, dest_t, wts, h, x, wsg, wsu, wsd, gpost, gate, ys)


def _moe_routed(h, logits_t, b_router, we_gate, we_up, we_down, layer):
    eidx_t, wts_t = route(logits_t, b_router)
    dest_t, tabs, blk_e = moe_plan(eidx_t)
    pad_start = tabs[0, :, 0].astype(jnp.int32)
    pad_end = tabs[1, :, 0].astype(jnp.int32)
    n_used = (pad_end[N_EXPERTS - 1:] // MOE_ROWS).astype(jnp.int32)
    xs = moe_dispatch(dest_t, pad_start, pad_end, h)
    ys = expert_ffn(blk_e[0], n_used, xs, we_gate, we_up, we_down, layer)
    return dest_t, wts_t.T, ys


def _reorder_w_in(w):
    qa, ka, va, qw, kw, vw, z, xs, bm, cm, dt, gates = jnp.split(
        w, [512, 640, 768, 1280, 1408, 1536, 2560, 3584, 3840, 4096, 4128], axis=1)
    w_main = jnp.concatenate([qa, qw, z, xs, gates, ka, va, kw, vw, bm, cm], axis=1).astype(BF16)
    pad = jnp.zeros((w.shape[0], LANES - SSM_HEADS), w.dtype)
    w_dt = jnp.concatenate([dt[:, :SSM_HEADS], pad, dt[:, SSM_HEADS:], pad], axis=1).astype(BF16)
    return w_main, w_dt


def _pad_lanes(v):
    return jnp.pad(v, ((0, 0), (0, LANES - v.shape[1])))


def kernel(x, c, ctx, c_ctx, w_ada, b_ada, g_mix_pre, g_mix_post, g_ffn_pre, g_ffn_post, w_in, g_q_a, g_k_a, sink_w, ssm_conv_w, ssm_conv_b, ssm_dt_bias, ssm_a_log, ssm_d, ssm_norm, w_br_a, w_br_w, w_br_s, w_out, w_router, b_router, we_gate, we_up, we_down, ws_gate, ws_up, ws_down):
    nb, n, d = x.shape
    mc = ctx.shape[1]
    depth = w_in.shape[0]
    t_lat, t_ctx = nb * n, nb * mc
    cos, sin = rope_tables(n)
    xl = x.reshape(t_lat, d)
    xc = ctx.reshape(t_ctx, d)
    c8 = jnp.concatenate([c, c_ctx[None, :], jnp.zeros((8 - nb - 1, d), F32)], axis=0)
    zeros_sink = jnp.zeros((N_HEADS,), F32)
    s_zero = jnp.zeros((nb, 2, SSM_GROUPS, SSM_STATE, SSM_INNER // SSM_GROUPS), F32)
    dummy_tab = jnp.zeros((mc, N_HEADS * HEAD_DIM), F32)

    for i in range(depth):
        last = i == depth - 1
        mod = ada_mod(c8, w_ada, b_ada[i], i)
        mod_l = [mod[:nb, k * d:(k + 1) * d].reshape(nb, 1, d) for k in range(6)]
        mod_c = [mod[nb:nb + 1, k * d:(k + 1) * d].reshape(1, 1, d) for k in range(6)]
        w_main, w_dt = _reorder_w_in(w_in[i])
        gq = jnp.tile(g_q_a[i], N_HEADS)[None, :]
        gk = jnp.tile(g_k_a[i], N_KV)[None, :]
        bias2 = _pad_lanes(ssm_dt_bias[i].reshape(2, SSM_HEADS)).reshape(2, 1, LANES)
        aneg2 = _pad_lanes(-jnp.exp(ssm_a_log[i].astype(F32))).reshape(2, 1, LANES)
        dskip = jnp.repeat(ssm_d[i], SSM_P)[None, :]
        norm_w = ssm_norm[i][None, :]
        wa, ww, ws, wo = (w_br_a[i].astype(BF16), w_br_w[i].astype(BF16), w_br_s[i].astype(BF16),
                          w_out[i].astype(BF16))
        wr_t = w_router[i].T
        wr_hi = wr_t.astype(BF16)
        wr_lo = (wr_t - wr_hi.astype(F32)).astype(BF16)
        sink = sink_w[i].astype(F32)

        p_c, dt_c = in_proj(xc, g_mix_pre[i], mod_c[0], mod_c[1], w_main, w_dt, t_ctx)
        qa_c, qw_c, kda_c, vda_c, kdw_c, vdw_c = attn_prep(p_c, dummy_tab, dummy_tab, gq, gk, nb, mc, rope=False)
        uxs_c, ubc_c = ssm_conv(p_c, ssm_conv_w[i], ssm_conv_b[i], nb, mc)
        y_c, s_fin = ssd_scan(uxs_c, ubc_c, dt_c, bias2, aneg2, s_zero, nb, mc)

        p_l, dt_l = in_proj(xl, g_mix_pre[i], mod_l[0], mod_l[1], w_main, w_dt, n)
        qa, qw, kda, vda, kdw, vdw = attn_prep(p_l, cos, sin, gq, gk, nb, n, rope=True)
        m_all = n + mc
        kd_all = jnp.concatenate([kda.reshape(nb, n, -1), kda_c.reshape(nb, mc, -1)], axis=1).reshape(nb * m_all, -1)
        vd_all = jnp.concatenate([vda.reshape(nb, n, -1), vda_c.reshape(nb, mc, -1)], axis=1).reshape(nb * m_all, -1)
        score_bound = (math.sqrt(HEAD_DIM) * jnp.max(jnp.abs(g_q_a[i])) * jnp.max(jnp.abs(g_k_a[i]))).reshape(1)
        oa = flash_attn_bounded(qa, kd_all, vd_all, score_bound.astype(F32), nb, n, m_all)
        ow = window_attn(qw, kdw, vdw, kdw_c, vdw_c, sink, nb, n, mc)
        uxs, ubc = ssm_conv(p_l, ssm_conv_w[i], ssm_conv_b[i], nb, n)
        y_l, _ = ssd_scan(uxs, ubc, dt_l, bias2, aneg2, s_fin, nb, n)
        xl, h_l, lg_l = post_mixer(oa, ow, y_l, uxs, p_l, xl, dskip, norm_w, wa, ww, ws, wo,
                                   g_mix_post[i][None, :], mod_l[2], g_ffn_pre[i][None, :], mod_l[3], mod_l[4],
                                   wr_hi, wr_lo, n)
        wsg, wsu, wsd = ws_gate[i].astype(BF16), ws_up[i].astype(BF16), ws_down[i].astype(BF16)
        moe_w = (we_gate, we_up, we_down, i)
        if last:
            dest_t, wts, ys = _moe_routed(h_l, lg_l, b_router[i], *moe_w)
            xl = ffn_tail(dest_t, wts, h_l, xl, ys, wsg, wsu, wsd, g_ffn_post[i][None, :], mod_l[5], n)
        else:
            oa_c = flash_attn(qa_c, kda_c, vda_c, zeros_sink, nb, mc, mc, has_sink=False)
            ow_c = flash_attn(qw_c, kdw_c, vdw_c, sink, nb, mc, mc, has_sink=True)
            xc, h_c, lg_c = post_mixer(oa_c, ow_c, y_c, uxs_c, p_c, xc, dskip, norm_w, wa, ww, ws, wo,
                                       g_mix_post[i][None, :], mod_c[2], g_ffn_pre[i][None, :], mod_c[3], mod_c[4],
                                       wr_hi, wr_lo, t_ctx)
            h_all = jnp.concatenate([h_l, h_c], axis=0)
            lg_all = jnp.concatenate([lg_l, lg_c], axis=1)
            dest_t, wts, ys = _moe_routed(h_all, lg_all, b_router[i], *moe_w)
            xl = ffn_tail(dest_t[:, :t_lat], wts[:t_lat], h_l, xl, ys, wsg, wsu, wsd,
                          g_ffn_post[i][None, :], mod_l[5], n)
            xc = ffn_tail(dest_t[:, t_lat:], wts[t_lat:], h_c, xc, ys, wsg, wsu, wsd,
                          g_ffn_post[i][None, :], mod_c[5], t_ctx)
    return xl.reshape(nb, n, d)
```

```python
import functools
import math

import jax
import jax.numpy as jnp
import numpy as np
from jax import lax
from jax.experimental import pallas as pl
from jax.experimental.pallas import tpu as pltpu

F32 = jnp.float32
BF16 = jnp.bfloat16

HEAD_DIM = 64
N_HEADS = 8
N_KV = 2
GRID_W = 64
ROPE_THETA = 10000.0
WINDOW = 128
SSM_HEADS = 16
SSM_P = 64
SSM_INNER = SSM_HEADS * SSM_P
SSM_GROUPS = 2
SSM_STATE = 128
SSM_CHUNK = 128
N_EXPERTS = 64
TOP_K = 8
N_EXPERT_GROUPS = 8
TOPK_GROUPS = 4
ROUTED_SCALE = 2.5
EPS = 1e-6

LANES = 128
HALF = LANES // 2
VMEM_LIMIT = 56 * 1024 * 1024
NEG_BIG = -1e30

C_QA, C_QW, C_Z, C_XS, C_GATES = 0, 512, 1024, 2048, 3072
C_KA, C_VA, C_KW, C_VW, C_BC = 6144, 6272, 6400, 6528, 6656
P_WIDTH = 7168


def _params(sem, vmem=VMEM_LIMIT):
    return pltpu.CompilerParams(dimension_semantics=sem, vmem_limit_bytes=vmem)


def _silu(x):
    return x * jax.nn.sigmoid(x)


U32 = jnp.uint32
SLAB = 4
HI_MASK = 0xFFFF0000


def _store_rows_as_slabs(ref, x):
    r = x.shape[0]

    def bits(v):
        return pltpu.bitcast(v.astype(BF16).astype(F32), U32)

    for c in range(SLAB):
        lo = bits(x[:, c * LANES:(c + 1) * LANES]) >> 16
        hi = bits(x[:, (SLAB + c) * LANES:(SLAB + c + 1) * LANES]) & jnp.uint32(HI_MASK)
        ref[pl.ds(c, r, stride=SLAB), :] = hi | lo


def _load_slabs_as_rows(ref, r0=0, r=None):
    r = ref.shape[0] // SLAB if r is None else r
    words = [ref[pl.ds(r0 * SLAB + c, r, stride=SLAB), :] for c in range(SLAB)]
    los = [pltpu.bitcast(w << 16, F32) for w in words]
    his = [pltpu.bitcast(w & jnp.uint32(HI_MASK), F32) for w in words]
    return jnp.concatenate(los + his, axis=1)


def _softplus(x):
    return jnp.maximum(x, 0.0) + jnp.log(1.0 + jnp.exp(-jnp.abs(x)))


def _rms(x, eps=EPS):
    return x * lax.rsqrt(jnp.mean(x * x, axis=-1, keepdims=True) + eps)


def _split3(a):
    a1 = a.astype(BF16)
    r1 = a - a1.astype(F32)
    a2 = r1.astype(BF16)
    a3 = (r1 - a2.astype(F32)).astype(BF16)
    return a1, a2, a3


def _dot(a, b):
    return jnp.dot(a, b, preferred_element_type=F32)


def _dot_nt(a, b):
    return lax.dot_general(a, b, (((1,), (1,)), ((), ())), preferred_element_type=F32)


def _exact_right(a, r01, pieces=3):
    return sum(_dot(p, r01) for p in _split3(a)[:pieces])


def _exact_left(m01, a, pieces=3):
    return sum(_dot(m01, p) for p in _split3(a)[:pieces])


def _ada_kernel(c_ref, w_ref, b_ref, o_ref):
    h = _silu(c_ref[...])
    o_ref[...] = jnp.dot(h, w_ref[...], preferred_element_type=F32,
                         precision=lax.Precision.HIGHEST) + b_ref[...]


def ada_mod(c8, w_all, b, layer):
    _, d, n = w_all.shape
    tn = 1536
    return pl.pallas_call(
        _ada_kernel,
        grid=(n // tn,),
        in_specs=[pl.BlockSpec((8, d), lambda j: (0, 0)),
                  pl.BlockSpec((None, d, tn), lambda j: (layer, 0, j)),
                  pl.BlockSpec((1, tn), lambda j: (0, j))],
        out_specs=pl.BlockSpec((8, tn), lambda j: (0, j)),
        out_shape=jax.ShapeDtypeStruct((8, n), F32),
        compiler_params=_params(("parallel",)),
        name="ada_mod",
    )(c8, w_all, b.reshape(1, n))


def _inproj_kernel(x_ref, g_ref, sh_ref, sc_ref, w_ref, wdt_ref, o_ref, odt_ref, h_scr):
    @pl.when(pl.program_id(1) == 0)
    def _():
        h = _rms(x_ref[...]) * g_ref[...]
        h = h * (1.0 + sc_ref[...]) + sh_ref[...]
        hb = h.astype(BF16)
        h_scr[...] = hb
        odt_ref[...] = _dot(hb, wdt_ref[...])

    o_ref[...] = _dot(h_scr[...], w_ref[...]).astype(BF16)


def in_proj(x, g, shift, scale, w_main, w_dt, rows_per_batch):
    t, d = x.shape
    n = w_main.shape[1]
    tm = min(1024, rows_per_batch)
    tn = n // 4
    tpb = rows_per_batch // tm
    mod_spec = pl.BlockSpec((None, 1, d), lambda i, j: (i // tpb, 0, 0))
    return pl.pallas_call(
        _inproj_kernel,
        grid=(t // tm, n // tn),
        in_specs=[pl.BlockSpec((tm, d), lambda i, j: (i, 0)),
                  pl.BlockSpec((1, d), lambda i, j: (0, 0)),
                  mod_spec, mod_spec,
                  pl.BlockSpec((d, tn), lambda i, j: (0, j)),
                  pl.BlockSpec((d, 2 * LANES), lambda i, j: (0, 0))],
        out_specs=[pl.BlockSpec((tm, tn), lambda i, j: (i, j)),
                   pl.BlockSpec((tm, 2 * LANES), lambda i, j: (i, 0))],
        out_shape=[jax.ShapeDtypeStruct((t, n), BF16),
                   jax.ShapeDtypeStruct((t, 2 * LANES), F32)],
        scratch_shapes=[pltpu.VMEM((tm, d), BF16)],
        compiler_params=_params(("parallel", "arbitrary")),
        name="in_proj",
    )(x, g.reshape(1, d), shift, scale, w_main, w_dt)


def _rope(x, cos, sin):
    w = x.shape[-1]
    lane = lax.broadcasted_iota(jnp.int32, x.shape, 1)
    first = (lane % 32) < 16
    swapped = jnp.where(first, pltpu.roll(x, w - 16, 1), pltpu.roll(x, 16, 1))
    return x * cos + swapped * sin


def _dup_halves(x):
    lane = lax.broadcasted_iota(jnp.int32, x.shape, 1)
    lo = lane < HALF
    r = pltpu.roll(x, HALF, 1)
    return jnp.concatenate([jnp.where(lo, x, r), jnp.where(lo, r, x)], axis=1)


def _prep_kernel(qa_ref, qw_ref, ka_ref, va_ref, kw_ref, vw_ref, cos_ref, sin_ref,
                 gq_ref, gk_ref, bdq_ref, bdk_ref,
                 qa_o, qw_o, kda_o, vda_o, kdw_o, vdw_o, *, rope):
    scale = HEAD_DIM ** -0.5
    inv_hd = 1.0 / HEAD_DIM

    def headnorm(x, g, bd):
        ss = _dot((x * x).astype(BF16), bd) * inv_hd
        return x * lax.rsqrt(ss + EPS) * g

    qa = headnorm(qa_ref[...].astype(F32), gq_ref[...], bdq_ref[...])
    ka = headnorm(ka_ref[...].astype(F32), gk_ref[...], bdk_ref[...])
    qw = qw_ref[...].astype(F32)
    kw = kw_ref[...].astype(F32)
    if rope:
        cos = cos_ref[...]
        sin = sin_ref[...]
        qa = _rope(qa, cos, sin)
        qw = _rope(qw, cos, sin)
        ka = _rope(ka, cos[:, :LANES], sin[:, :LANES])
        kw = _rope(kw, cos[:, :LANES], sin[:, :LANES])
    qa_o[...] = (qa * scale).astype(BF16)
    qw_o[...] = (qw * scale).astype(BF16)
    kda_o[...] = _dup_halves(ka).astype(BF16)
    kdw_o[...] = _dup_halves(kw).astype(BF16)
    vda_o[...] = _dup_halves(va_ref[...].astype(F32)).astype(BF16)
    vdw_o[...] = _dup_halves(vw_ref[...].astype(F32)).astype(BF16)


def attn_prep(p, cos, sin, gq, gk, nb, n, rope):
    t = nb * n
    tm = min(512, n)
    spb = n // tm
    hq = N_HEADS * HEAD_DIM
    hk = N_KV * HEAD_DIM
    bdq = (np.arange(hq)[:, None] // HEAD_DIM == np.arange(hq)[None, :] // HEAD_DIM)
    bdq = jnp.asarray(bdq, BF16)
    bdk = bdq[:hk, :hk]
    qspec = lambda c: pl.BlockSpec((tm, hq), lambda s, b: (b * spb + s, c // hq))
    kspec = lambda c: pl.BlockSpec((tm, hk), lambda s, b: (b * spb + s, c // hk))
    tab = pl.BlockSpec((tm, hq), lambda s, b: (s, 0))
    const = lambda shp: pl.BlockSpec(shp, lambda s, b: (0, 0))
    oq = pl.BlockSpec((tm, hq), lambda s, b: (b * spb + s, 0))
    ok = pl.BlockSpec((tm, 2 * hk), lambda s, b: (b * spb + s, 0))
    return pl.pallas_call(
        functools.partial(_prep_kernel, rope=rope),
        grid=(spb, nb),
        in_specs=[qspec(C_QA), qspec(C_QW), kspec(C_KA), kspec(C_VA), kspec(C_KW), kspec(C_VW),
                  tab, tab, const((1, hq)), const((1, hk)), const((hq, hq)), const((hk, hk))],
        out_specs=[oq, oq, ok, ok, ok, ok],
        out_shape=[jax.ShapeDtypeStruct((t, hq), BF16)] * 2 + [jax.ShapeDtypeStruct((t, 2 * hk), BF16)] * 4,
        compiler_params=_params(("parallel", "arbitrary")),
        name="attn_prep",
    )(p, p, p, p, p, p, cos, sin, gq, gk, bdq, bdk)


def rope_tables(n):
    rows = n // GRID_W
    row = jnp.repeat(jnp.arange(rows, dtype=F32), GRID_W)
    col = jnp.tile(jnp.arange(GRID_W, dtype=F32), rows)
    axis_dim = HEAD_DIM // 2
    inv_freq = ROPE_THETA ** (-jnp.arange(0, axis_dim, 2, dtype=F32) / axis_dim)
    ang_r = row[:, None] * inv_freq[None, :]
    ang_c = col[:, None] * inv_freq[None, :]
    cr, sr, cc, sc = jnp.cos(ang_r), jnp.sin(ang_r), jnp.cos(ang_c), jnp.sin(ang_c)
    cos = jnp.concatenate([cr, cr, cc, cc], axis=1)
    sin = jnp.concatenate([-sr, sr, -sc, sc], axis=1)
    return jnp.tile(cos, (1, N_HEADS)), jnp.tile(sin, (1, N_HEADS))


def _pair_operands(kd, vd):
    lane = lax.broadcasted_iota(jnp.int32, kd.shape, 1)
    lo = lane < HALF
    zero = jnp.zeros_like(kd)
    kmats = (jnp.where(lo, kd, zero), jnp.where(lo, zero, kd))
    vstack = jnp.concatenate([jnp.where(lo, vd, zero), jnp.where(lo, zero, vd)], axis=0)
    return kmats, vstack


def _flash_kernel(sink_ref, q_ref, k_ref, v_ref, o_ref, m_scr, l_scr, acc_scr, *, has_sink, nk):
    ki = pl.program_id(2)
    tq = q_ref.shape[0]

    @pl.when(ki == 0)
    def _():
        m_scr[...] = jnp.full(m_scr.shape, NEG_BIG, F32)
        l_scr[...] = jnp.zeros(l_scr.shape, F32)
        acc_scr[...] = jnp.zeros(acc_scr.shape, F32)

    lane_q = lax.broadcasted_iota(jnp.int32, (tq, LANES), 1)
    lo_q = lane_q < HALF
    pairs_per_kv = N_HEADS // N_KV // 2
    for j in range(N_KV):
        kmats, vstack = _pair_operands(k_ref[:, j * LANES:(j + 1) * LANES],
                                       v_ref[:, j * LANES:(j + 1) * LANES])
        for pp in range(pairs_per_kv):
            hp = j * pairs_per_kv + pp
            qp = q_ref[:, hp * LANES:(hp + 1) * LANES]
            ps, alphas = [], []
            for par in range(2):
                h = 2 * hp + par
                s = _dot_nt(qp, kmats[par])
                m_prev = m_scr[h]
                m_new = jnp.maximum(m_prev, jnp.max(s, axis=1, keepdims=True))
                alpha = jnp.exp(m_prev - m_new)
                p = jnp.exp(s - m_new[:, :1])
                l_scr[h] = alpha * l_scr[h] + jnp.sum(p, axis=1, keepdims=True)
                m_scr[h] = m_new
                ps.append(p.astype(BF16))
                alphas.append(alpha)
            pv = _dot(jnp.concatenate(ps, axis=1), vstack)
            sl = slice(hp * LANES, (hp + 1) * LANES)
            acc_scr[:, sl] = acc_scr[:, sl] * jnp.where(lo_q, alphas[0], alphas[1]) + pv

    @pl.when(ki == nk - 1)
    def _():
        for hp in range(N_HEADS // 2):
            ls = []
            for par in range(2):
                h = 2 * hp + par
                l = l_scr[h]
                if has_sink:
                    l = l + jnp.exp(sink_ref[h] - m_scr[h])
                ls.append(l)
            sl = slice(hp * LANES, (hp + 1) * LANES)
            o_ref[:, sl] = (acc_scr[:, sl] / jnp.where(lo_q, ls[0], ls[1])).astype(BF16)


def _flash_bounded_kernel(c_ref, q_ref, k_ref, v_ref, o_ref, lmin_ref, l_scr, acc_scr, *, nk):
    ki = pl.program_id(2)
    tq = q_ref.shape[0]
    tk = k_ref.shape[0]

    @pl.when(ki == 0)
    def _():
        l_scr[...] = jnp.zeros(l_scr.shape, F32)
        acc_scr[...] = jnp.zeros(acc_scr.shape, F32)

    c = c_ref[0]
    pairs_per_kv = N_HEADS // N_KV // 2
    for j in range(N_KV):
        kmats, vstack = _pair_operands(k_ref[:, j * LANES:(j + 1) * LANES],
                                       v_ref[:, j * LANES:(j + 1) * LANES])
        for pp in range(pairs_per_kv):
            hp = j * pairs_per_kv + pp
            qp = q_ref[:, hp * LANES:(hp + 1) * LANES]
            ps = []
            for par in range(2):
                h = 2 * hp + par
                p = jnp.exp(_dot_nt(qp, kmats[par]) - c)
                part = p[:, 0:LANES]
                for cb in range(1, tk // LANES):
                    part = part + p[:, cb * LANES:(cb + 1) * LANES]
                l_scr[h] = l_scr[h] + part
                ps.append(p.astype(BF16))
            sl = slice(hp * LANES, (hp + 1) * LANES)
            acc_scr[:, sl] = acc_scr[:, sl] + _dot(jnp.concatenate(ps, axis=1), vstack)

    @pl.when(ki == nk - 1)
    def _():
        lo_q = lax.broadcasted_iota(jnp.int32, (tq, LANES), 1) < HALF
        mins = []
        for hp in range(N_HEADS // 2):
            ls = [jnp.sum(l_scr[2 * hp + par], axis=1, keepdims=True) for par in range(2)]
            sl = slice(hp * LANES, (hp + 1) * LANES)
            o_ref[:, sl] = (acc_scr[:, sl] / jnp.where(lo_q, ls[0], ls[1])).astype(BF16)
            mins += [jnp.broadcast_to(jnp.min(l, axis=0, keepdims=True), (1, LANES)) for l in ls]
        lmin_ref[...] = jnp.concatenate(mins, axis=0)


FLASH_MIN_DENOM = 1e-30


def flash_attn_bounded(q, kd, vd, bound, nb, n, m):
    tq = min(1024, n)
    tk = _pick_tile(m, (768, 512, 256))
    nq, nk = n // tq, m // tk
    hq = N_HEADS * HEAD_DIM
    o, lmin = pl.pallas_call(
        functools.partial(_flash_bounded_kernel, nk=nk),
        grid=(nb, nq, nk),
        in_specs=[pl.BlockSpec(memory_space=pltpu.SMEM),
                  pl.BlockSpec((tq, hq), lambda b, i, k: (b * nq + i, 0)),
                  pl.BlockSpec((tk, 2 * LANES), lambda b, i, k: (b * nk + k, 0)),
                  pl.BlockSpec((tk, 2 * LANES), lambda b, i, k: (b * nk + k, 0))],
        out_specs=[pl.BlockSpec((tq, hq), lambda b, i, k: (b * nq + i, 0)),
                   pl.BlockSpec((N_HEADS, LANES), lambda b, i, k: (b * nq + i, 0))],
        out_shape=[jax.ShapeDtypeStruct((nb * n, hq), BF16),
                   jax.ShapeDtypeStruct((nb * nq * N_HEADS, LANES), F32)],
        scratch_shapes=[pltpu.VMEM((N_HEADS, tq, LANES), F32),
                        pltpu.VMEM((tq, hq), F32)],
        compiler_params=_params(("parallel", "parallel", "arbitrary")),
        name="flash_attn_bounded",
    )(bound, q, kd, vd)
    ok = jnp.min(lmin) > FLASH_MIN_DENOM
    return lax.cond(ok, lambda: o,
                    lambda: flash_attn(q, kd, vd, jnp.zeros((N_HEADS,), F32), nb, n, m, has_sink=False))


def _pick_tile(m, cands):
    for c in cands:
        if m % c == 0:
            return c
    raise ValueError(f"no tile for {m}")


def flash_attn(q, kd, vd, sink, nb, n, m, has_sink):
    tq = min(512, n)
    tk = _pick_tile(m, (768, 512, 256))
    nq, nk = n // tq, m // tk
    hq = N_HEADS * HEAD_DIM
    return pl.pallas_call(
        functools.partial(_flash_kernel, has_sink=has_sink, nk=nk),
        grid=(nb, nq, nk),
        in_specs=[pl.BlockSpec(memory_space=pltpu.SMEM),
                  pl.BlockSpec((tq, hq), lambda b, i, k: (b * nq + i, 0)),
                  pl.BlockSpec((tk, 2 * LANES), lambda b, i, k: (b * nk + k, 0)),
                  pl.BlockSpec((tk, 2 * LANES), lambda b, i, k: (b * nk + k, 0))],
        out_specs=pl.BlockSpec((tq, hq), lambda b, i, k: (b * nq + i, 0)),
        out_shape=jax.ShapeDtypeStruct((nb * n, hq), BF16),
        scratch_shapes=[pltpu.VMEM((N_HEADS, tq, LANES), F32),
                        pltpu.VMEM((N_HEADS, tq, LANES), F32),
                        pltpu.VMEM((tq, hq), F32)],
        compiler_params=_params(("parallel", "parallel", "arbitrary")),
        name="flash_attn",
    )(sink, q, kd, vd)


def _window_kernel(sink_ref, q_ref, kp_ref, km_ref, kn_ref, vp_ref, vm_ref, vn_ref, kc_ref, vc_ref,
                   o_ref, *, n, tq):
    i = pl.program_id(1)
    span = tq + 2 * WINDOW
    q0 = i * tq
    r = lax.broadcasted_iota(jnp.int32, (tq, span), 0)
    c = lax.broadcasted_iota(jnp.int32, (tq, span), 1)
    kpos = c + (q0 - WINDOW)
    ok = (c >= r) & (c <= r + 2 * WINDOW) & (kpos >= 0) & (kpos < n)
    lane_q = lax.broadcasted_iota(jnp.int32, (tq, LANES), 1)
    lo_q = lane_q < HALF
    k_all = jnp.concatenate([kp_ref[...], km_ref[...], kn_ref[...], kc_ref[...]], axis=0)
    v_all = jnp.concatenate([vp_ref[...], vm_ref[...], vn_ref[...], vc_ref[...]], axis=0)
    bias = jnp.concatenate([jnp.where(ok, 0.0, NEG_BIG), jnp.zeros((tq, kc_ref.shape[0]), F32)], axis=1)
    pairs_per_kv = N_HEADS // N_KV // 2
    for j in range(N_KV):
        js = slice(j * LANES, (j + 1) * LANES)
        kmats, vstack = _pair_operands(k_all[:, js], v_all[:, js])
        for pp in range(pairs_per_kv):
            hp = j * pairs_per_kv + pp
            qp = q_ref[:, hp * LANES:(hp + 1) * LANES]
            ps, ls = [], []
            for par in range(2):
                h = 2 * hp + par
                s = _dot_nt(qp, kmats[par]) + bias
                snk = sink_ref[h]
                m = jnp.maximum(jnp.max(s, axis=1, keepdims=True), snk)
                p = jnp.exp(s - m)
                ls.append(jnp.sum(p, axis=1, keepdims=True) + jnp.exp(snk - m))
                ps.append(p.astype(BF16))
            o = _dot(jnp.concatenate(ps, axis=1), vstack)
            o_ref[:, hp * LANES:(hp + 1) * LANES] = (o / jnp.where(lo_q, ls[0], ls[1])).astype(BF16)


def window_attn(q, kd, vd, kdc, vdc, sink, nb, n, mc):
    tq = 2 * WINDOW
    nq = n // tq
    wb = n // WINDOW
    hq = N_HEADS * HEAD_DIM
    prev = pl.BlockSpec((WINDOW, 2 * LANES), lambda b, i: (b * wb + jnp.maximum(2 * i - 1, 0), 0))
    main = pl.BlockSpec((tq, 2 * LANES), lambda b, i: (b * nq + i, 0))
    nxt = pl.BlockSpec((WINDOW, 2 * LANES), lambda b, i: (b * wb + jnp.minimum(2 * i + 2, wb - 1), 0))
    ctx = pl.BlockSpec((mc, 2 * LANES), lambda b, i: (b, 0))
    return pl.pallas_call(
        functools.partial(_window_kernel, n=n, tq=tq),
        grid=(nb, nq),
        in_specs=[pl.BlockSpec(memory_space=pltpu.SMEM),
                  pl.BlockSpec((tq, hq), lambda b, i: (b * nq + i, 0)),
                  prev, main, nxt, prev, main, nxt, ctx, ctx],
        out_specs=pl.BlockSpec((tq, hq), lambda b, i: (b * nq + i, 0)),
        out_shape=jax.ShapeDtypeStruct((nb * n, hq), BF16),
        compiler_params=_params(("parallel", "parallel")),
        name="window_attn",
    )(sink, q, kd, kd, kd, vd, vd, vd, kdc, vdc)


HALO = 16


def _conv_kernel(xm_ref, xp_ref, xn_ref, bm_ref, bp_ref, bn_ref, wx_ref, bx_ref, wb_ref, bb_ref,
                 ox_ref, ob_ref, *, nt):
    i = pl.program_id(1)
    has_prev = jnp.where(i > 0, 1.0, 0.0)
    has_next = jnp.where(i < nt - 1, 1.0, 0.0)

    def conv(m_ref, p_ref, n_ref, w_ref, b_ref, o_ref):
        x = m_ref[...].astype(F32)
        tl = x.shape[0]
        row = lax.broadcasted_iota(jnp.int32, x.shape, 0)
        before = p_ref[...].astype(F32)[HALO - 1:HALO, :] * has_prev
        after = n_ref[...].astype(F32)[0:1, :] * has_next
        xm1 = jnp.where(row == 0, before, pltpu.roll(x, 1, 0))
        xp1 = jnp.where(row == tl - 1, after, pltpu.roll(x, tl - 1, 0))
        w = w_ref[...]
        y = xm1 * w[0:1, :] + x * w[1:2, :] + xp1 * w[2:3, :] + b_ref[...]
        o_ref[...] = _silu(y).astype(BF16)

    conv(xm_ref, xp_ref, xn_ref, wx_ref, bx_ref, ox_ref)
    conv(bm_ref, bp_ref, bn_ref, wb_ref, bb_ref, ob_ref)


def ssm_conv(p, conv_w, conv_b, nb, n):
    tl = min(512, n)
    nt = n // tl
    hb = n // HALO
    hpt = tl // HALO
    cx, cb = SSM_INNER, 2 * SSM_GROUPS * SSM_STATE

    def specs(width, col):
        cblk = col // width
        return (pl.BlockSpec((tl, width), lambda b, i: (b * nt + i, cblk)),
                pl.BlockSpec((HALO, width), lambda b, i: (b * hb + jnp.maximum(i * hpt - 1, 0), cblk)),
                pl.BlockSpec((HALO, width), lambda b, i: (b * hb + jnp.minimum((i + 1) * hpt, hb - 1), cblk)))

    const = lambda shp: pl.BlockSpec(shp, lambda b, i: (0, 0))
    xm, xp, xn = specs(cx, C_XS)
    bm, bp, bn = specs(cb, C_BC)
    return pl.pallas_call(
        functools.partial(_conv_kernel, nt=nt),
        grid=(nb, nt),
        in_specs=[xm, xp, xn, bm, bp, bn, const((3, cx)), const((1, cx)), const((3, cb)), const((1, cb))],
        out_specs=[pl.BlockSpec((tl, cx), lambda b, i: (b * nt + i, 0)),
                   pl.BlockSpec((tl, cb), lambda b, i: (b * nt + i, 0))],
        out_shape=[jax.ShapeDtypeStruct((nb * n, cx), BF16), jax.ShapeDtypeStruct((nb * n, cb), BF16)],
        compiler_params=_params(("parallel", "parallel")),
        name="ssm_conv",
    )(p, p, p, p, p, p, conv_w[:, :cx], conv_b[:cx].reshape(1, cx), conv_w[:, cx:], conv_b[cx:].reshape(1, cb))


def _ssd_chunk(d, r0, xs_ref, bc_ref, dt_ref, bias, aneg, tri, rep, st_scr, y_ref):
    q = SSM_CHUNK
    rows = slice(r0, r0 + q)
    gw = SSM_INNER // SSM_GROUPS
    dt = _softplus(dt_ref[rows, :] + bias)
    a = dt * aneg
    ac = _exact_left(tri, a, pieces=2)
    act = ac.T
    acx = _exact_right(ac, rep, pieces=2)
    dtx = _dot(dt.astype(BF16), rep)
    totx = acx[q - 1:q, :] if d == 0 else acx[0:1, :]
    xd = xs_ref[rows, :].astype(F32) * dtx
    xd_b = xd.astype(BF16)
    xe = (xd * jnp.exp(totx - acx)).astype(BF16)
    ein = jnp.exp(acx)
    keep = tri > 0
    lane = lax.broadcasted_iota(jnp.int32, (q, LANES), 1)
    lo = lane < HALF
    zero = jnp.zeros((q, LANES), BF16)
    hpg = SSM_HEADS // SSM_GROUPS
    for g in range(SSM_GROUPS):
        bg = bc_ref[rows, g * SSM_STATE:(g + 1) * SSM_STATE]
        cg = bc_ref[rows, (SSM_GROUPS + g) * SSM_STATE:(SSM_GROUPS + g + 1) * SSM_STATE]
        cb = _dot_nt(cg, bg)
        st = st_scr[d, g]
        yoff = _dot(cg, st.astype(BF16)) * ein[:, g * gw:(g + 1) * gw]
        for hp in range(hpg // 2):
            gs = []
            for par in range(2):
                h = g * hpg + 2 * hp + par
                seg = ac[:, h:h + 1] - act[h:h + 1, :]
                gs.append((cb * jnp.exp(jnp.where(keep, seg, NEG_BIG))).astype(BF16))
            c0 = g * gw + hp * LANES
            xp = xd_b[:, c0:c0 + LANES]
            xstack = jnp.concatenate([jnp.where(lo, xp, zero), jnp.where(lo, zero, xp)], axis=0)
            ydiag = _dot(jnp.concatenate(gs, axis=1), xstack)
            y_ref[rows, c0:c0 + LANES] = (ydiag + yoff[:, hp * LANES:(hp + 1) * LANES]).astype(BF16)
        bgt = bg.astype(F32).T.astype(BF16)
        cs = _dot(bgt, xe[:, g * gw:(g + 1) * gw])
        st_scr[d, g] = st * jnp.exp(totx[:, g * gw:(g + 1) * gw]) + cs


def _ssd_kernel(xf_ref, bcf_ref, dtf_ref, xb_ref, bcb_ref, dtb_ref, bias_ref, aneg_ref, tri_ref, rep_ref,
                s0_ref, yf_ref, yb_ref, sfin_ref, st_scr, *, nsteps):
    k = pl.program_id(1)

    @pl.when(k == 0)
    def _():
        st_scr[...] = s0_ref[...]

    rep = rep_ref[...]
    for c in range(SSD_CHUNKS_PER_STEP):
        _ssd_chunk(0, c * SSM_CHUNK, xf_ref, bcf_ref, dtf_ref, bias_ref[0], aneg_ref[0], tri_ref[0], rep,
                   st_scr, yf_ref)
        _ssd_chunk(1, (SSD_CHUNKS_PER_STEP - 1 - c) * SSM_CHUNK, xb_ref, bcb_ref, dtb_ref, bias_ref[1],
                   aneg_ref[1], tri_ref[1], rep, st_scr, yb_ref)

    @pl.when(k == nsteps - 1)
    def _():
        sfin_ref[...] = st_scr[...]


SSD_CHUNKS_PER_STEP = 2


def ssd_scan(u_xs, u_bc, dt_raw, bias2, aneg2, s0, nb, n):
    q = SSM_CHUNK
    nc = n // q
    gw = SSM_INNER // SSM_GROUPS
    idx = np.arange(q)
    tri = np.stack([idx[:, None] >= idx[None, :], idx[:, None] <= idx[None, :]]).astype(np.float32)
    rep = (np.arange(LANES)[:, None] == np.arange(SSM_INNER)[None, :] // SSM_P).astype(np.float32)

    ns = nc // SSD_CHUNKS_PER_STEP
    rb = q * SSD_CHUNKS_PER_STEP
    fwd = lambda b, k: b * ns + k
    bwd = lambda b, k: b * ns + (ns - 1 - k)
    cbc = 2 * SSM_GROUPS * SSM_STATE
    state = pl.BlockSpec((None, 2, SSM_GROUPS, SSM_STATE, gw), lambda b, k: (b, 0, 0, 0, 0))
    const = lambda shp: pl.BlockSpec(shp, lambda b, k: (0,) * len(shp))
    yf, yb, sfin = pl.pallas_call(
        functools.partial(_ssd_kernel, nsteps=ns),
        grid=(nb, ns),
        in_specs=[pl.BlockSpec((rb, SSM_INNER), lambda b, k: (fwd(b, k), 0)),
                  pl.BlockSpec((rb, cbc), lambda b, k: (fwd(b, k), 0)),
                  pl.BlockSpec((rb, LANES), lambda b, k: (fwd(b, k), 0)),
                  pl.BlockSpec((rb, SSM_INNER), lambda b, k: (bwd(b, k), 0)),
                  pl.BlockSpec((rb, cbc), lambda b, k: (bwd(b, k), 0)),
                  pl.BlockSpec((rb, LANES), lambda b, k: (bwd(b, k), 1)),
                  const((2, 1, LANES)), const((2, 1, LANES)), const((2, q, q)), const((LANES, SSM_INNER)),
                  state],
        out_specs=[pl.BlockSpec((rb, SSM_INNER), lambda b, k: (fwd(b, k), 0)),
                   pl.BlockSpec((rb, SSM_INNER), lambda b, k: (bwd(b, k), 0)),
                   state],
        out_shape=[jax.ShapeDtypeStruct((nb * n, SSM_INNER), BF16),
                   jax.ShapeDtypeStruct((nb * n, SSM_INNER), BF16),
                   jax.ShapeDtypeStruct((nb, 2, SSM_GROUPS, SSM_STATE, gw), F32)],
        scratch_shapes=[pltpu.VMEM((2, SSM_GROUPS, SSM_STATE, gw), F32)],
        compiler_params=_params(("parallel", "arbitrary")),
        name="ssd_scan",
    )(u_xs, u_bc, dt_raw, u_xs, u_bc, dt_raw, bias2, aneg2, jnp.asarray(tri, BF16), jnp.asarray(rep, BF16), s0)
    return (yf, yb), sfin


def _post_kernel(oa_ref, ow_ref, yf_ref, yb_ref, xs_ref, z_ref, gt_ref, x_ref,
                 dsk_ref, nw_ref, wa_ref, ww_ref, ws_ref, wo_ref, gpost_ref, gate_ref,
                 gpre_ref, sh_ref, sc_ref, wrh_ref, wrl_ref,
                 xo_ref, h_ref, lg_ref):
    y = yf_ref[...].astype(F32) + yb_ref[...].astype(F32) + dsk_ref[...] * xs_ref[...].astype(F32)
    u = y * _silu(z_ref[...].astype(F32))
    gw = SSM_INNER // SSM_GROUPS
    ys = jnp.concatenate([_rms(u[:, g * gw:(g + 1) * gw]) for g in range(SSM_GROUPS)], axis=1)
    ys = (ys * nw_ref[...]).astype(BF16)
    d = x_ref.shape[1]
    ga = jax.nn.sigmoid(gt_ref[:, 0:d].astype(F32))
    gw_ = jax.nn.sigmoid(gt_ref[:, d:2 * d].astype(F32))
    gs = jax.nn.sigmoid(gt_ref[:, 2 * d:3 * d].astype(F32))
    m = ga * _dot(oa_ref[...], wa_ref[...]) + gw_ * _dot(ow_ref[...], ww_ref[...]) + gs * _dot(ys, ws_ref[...])
    ml = _dot(m.astype(BF16), wo_ref[...])
    xn = x_ref[...] + gate_ref[...] * (_rms(ml) * gpost_ref[...])
    xo_ref[...] = xn
    h = (_rms(xn) * gpre_ref[...]) * (1.0 + sc_ref[...]) + sh_ref[...]
    _store_rows_as_slabs(h_ref, h)
    hb = h.astype(BF16)
    hl =(h - hb.astype(F32)).astype(BF16)
    lg_ref[...] = _dot_nt(wrh_ref[...], hb) + _dot_nt(wrh_ref[...], hl) + _dot_nt(wrl_ref[...], hb)


def post_mixer(oa, ow, y2, u_xs, p, x, dskip, norm_w, wa, ww, ws, wo, gpost, gate, gpre, shift, scale,
               wr_hi, wr_lo, rows_per_batch):
    t, d = x.shape
    tm = min(256, rows_per_batch)
    tpb = rows_per_batch // tm
    nt = t // tm
    hq = N_HEADS * HEAD_DIM
    row = lambda w, c=0: pl.BlockSpec((tm, w), lambda i: (i, c // w))
    const = lambda shp: pl.BlockSpec(shp, lambda i: (0,) * len(shp))
    mod = pl.BlockSpec((None, 1, d), lambda i: (i // tpb, 0, 0))
    return pl.pallas_call(
        _post_kernel,
        grid=(nt,),
        in_specs=[row(hq), row(hq), row(SSM_INNER), row(SSM_INNER),
                  row(SSM_INNER), row(SSM_INNER, C_Z), row(3 * d, C_GATES), row(d),
                  const((1, SSM_INNER)), const((1, SSM_INNER)),
                  const((hq, d)), const((hq, d)), const((SSM_INNER, d)), const((d, d)),
                  const((1, d)), mod, const((1, d)), mod, mod,
                  const((N_EXPERTS, d)), const((N_EXPERTS, d))],
        out_specs=[row(d), pl.BlockSpec((tm * SLAB, LANES), lambda i: (i, 0)),
                   pl.BlockSpec((N_EXPERTS, tm), lambda i: (0, i))],
        out_shape=[jax.ShapeDtypeStruct((t, d), F32), jax.ShapeDtypeStruct((t * SLAB, LANES), U32),
                   jax.ShapeDtypeStruct((N_EXPERTS, t), F32)],
        compiler_params=_params(("parallel",)),
        name="post_mixer",
    )(oa, ow, y2[0], y2[1], u_xs, p, p, x, dskip, norm_w, wa, ww, ws, wo, gpost, gate, gpre, shift, scale,
      wr_hi, wr_lo)


def _route_kernel(lg_ref, b_ref, ei_ref, w_ref):
    scores = jax.nn.sigmoid(lg_ref[...])
    sel = scores + b_ref[...]
    tt = sel.shape[1]
    per = N_EXPERTS // N_EXPERT_GROUPS
    r8 = lax.broadcasted_iota(jnp.int32, (per, tt), 0).astype(F32)
    ninf = -jnp.inf

    def argmax_rows(x, rows, nrows):
        m = jnp.max(x, axis=0, keepdims=True)
        idx = jnp.min(jnp.where(x == m, rows, float(nrows)), axis=0, keepdims=True)
        return m, idx

    gscores = []
    for g in range(N_EXPERT_GROUPS):
        blk = sel[g * per:(g + 1) * per, :]
        m1, i1 = argmax_rows(blk, r8, per)
        m2 = jnp.max(jnp.where(r8 == i1, ninf, blk), axis=0, keepdims=True)
        gscores.append(m1 + m2)
    cur = jnp.concatenate(gscores, axis=0)
    rg = lax.broadcasted_iota(jnp.int32, cur.shape, 0).astype(F32)
    chosen = jnp.zeros(cur.shape, F32)
    for _ in range(TOPK_GROUPS):
        _, gi = argmax_rows(cur, rg, N_EXPERT_GROUPS)
        hit = rg == gi
        chosen = jnp.where(hit, 1.0, chosen)
        cur = jnp.where(hit, ninf, cur)
    gmask = jnp.concatenate([jnp.broadcast_to(chosen[g:g + 1, :], (per, tt)) for g in range(N_EXPERT_GROUPS)],
                            axis=0)
    cur = jnp.where(gmask > 0, sel, ninf)
    re = lax.broadcasted_iota(jnp.int32, cur.shape, 0).astype(F32)
    idxs, ws = [], []
    for _ in range(TOP_K):
        _, ei = argmax_rows(cur, re, N_EXPERTS)
        hit = re == ei
        ws.append(jnp.sum(jnp.where(hit, scores, 0.0), axis=0, keepdims=True))
        idxs.append(ei)
        cur = jnp.where(hit, ninf, cur)
    w = jnp.concatenate(ws, axis=0)
    w_ref[...] = w / jnp.sum(w, axis=0, keepdims=True) * ROUTED_SCALE
    ei_ref[...] = jnp.concatenate(idxs, axis=0).astype(jnp.int32)


def route(logits_t, b_router):
    e, t = logits_t.shape
    tt = 512
    return pl.pallas_call(
        _route_kernel,
        grid=(t // tt,),
        in_specs=[pl.BlockSpec((e, tt), lambda i: (0, i)), pl.BlockSpec((e, 1), lambda i: (0, 0))],
        out_specs=[pl.BlockSpec((TOP_K, tt), lambda i: (0, i)), pl.BlockSpec((TOP_K, tt), lambda i: (0, i))],
        out_shape=[jax.ShapeDtypeStruct((TOP_K, t), jnp.int32), jax.ShapeDtypeStruct((TOP_K, t), F32)],
        compiler_params=_params(("parallel",)),
        name="route",
    )(logits_t, b_router.reshape(e, 1))


MOE_ROWS = 512
PLAN_TOKENS = 512


def _moe_geometry(t):
    nblk = -(-(t * TOP_K + N_EXPERTS * (MOE_ROWS - 1)) // MOE_ROWS)
    return nblk, nblk * MOE_ROWS


def _plan_kernel(ei_ref, ut_ref, tril_ref, dest_ref, tab_ref, be_ref, cnt_scr, run_scr):
    ph = pl.program_id(0)
    i = pl.program_id(1)
    ei = ei_ref[...]
    tt = ei.shape[1]
    re = lax.broadcasted_iota(jnp.int32, (N_EXPERTS, tt), 0)
    hits = [re == ei[k:k + 1, :] for k in range(TOP_K)]
    oh = jnp.zeros((N_EXPERTS, tt), F32)
    for k in range(TOP_K):
        oh = oh + jnp.where(hits[k], 1.0, 0.0)

    @pl.when((ph == 0) & (i == 0))
    def _():
        cnt_scr[...] = jnp.zeros(cnt_scr.shape, F32)

    @pl.when(ph == 0)
    def _():
        cnt_scr[...] = cnt_scr[...] + jnp.sum(oh, axis=1, keepdims=True)

    @pl.when((ph == 1) & (i == 0))
    def _():
        cnt = cnt_scr[...]
        padded = ((cnt.astype(jnp.int32) + (MOE_ROWS - 1)) & (-MOE_ROWS)).astype(F32)
        pad_end = _exact_left(tril_ref[...], padded)
        pad_start = pad_end - padded
        run_scr[...] = pad_start
        tab_ref[0] = pad_start
        tab_ref[1] = pad_end
        nbp = be_ref.shape[1]
        blk0 = (lax.broadcasted_iota(jnp.int32, (N_EXPERTS, nbp), 1) * MOE_ROWS).astype(F32)
        be = jnp.sum(jnp.where(pad_end[:, :1] <= blk0, 1.0, 0.0), axis=0, keepdims=True)
        be_ref[...] = jnp.broadcast_to(jnp.minimum(be, N_EXPERTS - 1.0), be_ref.shape).astype(jnp.int32)

    @pl.when(ph == 1)
    def _():
        cin = _dot(oh.astype(BF16), ut_ref[...])
        pos = run_scr[:, :1] + (cin - oh)
        rows = [jnp.sum(jnp.where(hits[k], pos, 0.0), axis=0, keepdims=True) for k in range(TOP_K)]
        dest_ref[...] = jnp.concatenate(rows, axis=0).astype(jnp.int32)
        run_scr[...] = run_scr[...] + cin[:, tt - 1:tt]


def moe_plan(eidx_t):
    k, t = eidx_t.shape
    tt = PLAN_TOKENS
    nt = t // tt
    nblk, _ = _moe_geometry(t)
    nbp = -(-nblk // LANES) * LANES
    ut = jnp.asarray(np.arange(tt)[:, None] <= np.arange(tt)[None, :], BF16)
    tril = jnp.asarray(np.arange(N_EXPERTS)[:, None] >= np.arange(N_EXPERTS)[None, :], BF16)
    return pl.pallas_call(
        _plan_kernel,
        grid=(2, nt),
        in_specs=[pl.BlockSpec((k, tt), lambda p, i: (0, i)),
                  pl.BlockSpec((tt, tt), lambda p, i: (0, 0)),
                  pl.BlockSpec((N_EXPERTS, N_EXPERTS), lambda p, i: (0, 0))],
        out_specs=[pl.BlockSpec((k, tt), lambda p, i: (0, i * p)),
                   pl.BlockSpec((2, N_EXPERTS, LANES), lambda p, i: (0, 0, 0)),
                   pl.BlockSpec((8, nbp), lambda p, i: (0, 0))],
        out_shape=[jax.ShapeDtypeStruct((k, t), jnp.int32),
                   jax.ShapeDtypeStruct((2, N_EXPERTS, LANES), F32),
                   jax.ShapeDtypeStruct((8, nbp), jnp.int32)],
        scratch_shapes=[pltpu.VMEM((N_EXPERTS, LANES), F32), pltpu.VMEM((N_EXPERTS, LANES), F32)],
        compiler_params=_params(("arbitrary", "arbitrary")),
        name="moe_plan",
    )(eidx_t, ut, tril)


def _dispatch_kernel(dest_ref, pstart_ref, pend_ref, h_ref, xs_hbm, zero_scr, sem):
    i = pl.program_id(0)
    tt = h_ref.shape[0] // SLAB

    def slab(ref, row, n=1):
        return ref.at[pl.ds(pl.multiple_of(row * SLAB, SLAB), n * SLAB), :]

    @pl.when(i == 0)
    def _():
        zero_scr[...] = jnp.zeros(zero_scr.shape, U32)
        nblk = xs_hbm.shape[0] // (MOE_ROWS * SLAB)
        n_used = pend_ref[N_EXPERTS - 1] // MOE_ROWS

        def zero_block(row0):
            return pltpu.make_async_copy(zero_scr, slab(xs_hbm, row0, MOE_ROWS), sem)

        def seg_start(e, c):
            @pl.when(pend_ref[e] > pstart_ref[e])
            def _():
                zero_block(pend_ref[e] - MOE_ROWS).start()
            return c

        def seg_wait(e, c):
            @pl.when(pend_ref[e] > pstart_ref[e])
            def _():
                zero_block(pend_ref[e] - MOE_ROWS).wait()
            return c

        def tail_start(b, c):
            zero_block(b * MOE_ROWS).start()
            return c

        def tail_wait(b, c):
            zero_block(b * MOE_ROWS).wait()
            return c

        lax.fori_loop(0, N_EXPERTS, seg_start, 0)
        lax.fori_loop(n_used, nblk, tail_start, 0)
        lax.fori_loop(0, N_EXPERTS, seg_wait, 0)
        lax.fori_loop(n_used, nblk, tail_wait, 0)

    def issue(t, c):
        for k in range(TOP_K):
            pltpu.make_async_copy(slab(h_ref, t), slab(xs_hbm, dest_ref[k, t]), sem).start(priority=k % 2)
        return c

    lax.fori_loop(0, tt, issue, 0)
    for k in range(TOP_K):
        pltpu.make_async_copy(h_ref, slab(xs_hbm, 0, tt), sem).wait()


def moe_dispatch(dest_t, pad_start, pad_end, h):
    t = h.shape[0] // SLAB
    tt = PLAN_TOKENS
    _, cap = _moe_geometry(t)
    smem = pl.BlockSpec(memory_space=pltpu.SMEM)
    return pl.pallas_call(
        _dispatch_kernel,
        grid=(t // tt,),
        in_specs=[pl.BlockSpec((TOP_K, tt), lambda i: (0, i), memory_space=pltpu.SMEM), smem, smem,
                  pl.BlockSpec((tt * SLAB, LANES), lambda i: (i, 0))],
        out_specs=pl.BlockSpec(memory_space=pl.ANY),
        out_shape=jax.ShapeDtypeStruct((cap * SLAB, LANES), U32),
        scratch_shapes=[pltpu.VMEM((MOE_ROWS * SLAB, LANES), U32), pltpu.SemaphoreType.DMA(())],
        compiler_params=_params(("arbitrary",)),
        name="moe_dispatch",
    )(dest_t, pad_start, pad_end, h)


def _expert_kernel(be_ref, nu_ref, x_ref, wg_ref, wu_ref, wd_ref, o_ref):
    used = pl.program_id(0) < nu_ref[0]

    @pl.when(used)
    def _():
        x = _load_slabs_as_rows(x_ref).astype(BF16)
        hg = _dot(x, wg_ref[...].astype(BF16))
        hu = _dot(x, wu_ref[...].astype(BF16))
        hb = (_silu(hg) * hu).astype(BF16)
        _store_rows_as_slabs(o_ref, _dot(hb, wd_ref[...].astype(BF16)))

    @pl.when(jnp.logical_not(used))
    def _():
        o_ref[...] = jnp.zeros(o_ref.shape, U32)


def expert_ffn(blk_e, n_used, xs, we_gate, we_up, we_down, layer):
    d, de = we_gate.shape[2:]
    nblk = xs.shape[0] // (MOE_ROWS * SLAB)
    blk = lambda i, nu: jnp.minimum(i, nu[0] - 1)
    grid_spec = pltpu.PrefetchScalarGridSpec(
        num_scalar_prefetch=2,
        grid=(nblk,),
        in_specs=[pl.BlockSpec((MOE_ROWS * SLAB, LANES), lambda i, be, nu: (blk(i, nu), 0)),
                  pl.BlockSpec((None, None, d, de), lambda i, be, nu: (layer, be[blk(i, nu)], 0, 0)),
                  pl.BlockSpec((None, None, d, de), lambda i, be, nu: (layer, be[blk(i, nu)], 0, 0)),
                  pl.BlockSpec((None, None, de, d), lambda i, be, nu: (layer, be[blk(i, nu)], 0, 0))],
        out_specs=pl.BlockSpec((MOE_ROWS * SLAB, LANES), lambda i, be, nu: (i, 0)),
    )
    return pl.pallas_call(
        _expert_kernel,
        grid_spec=grid_spec,
        out_shape=jax.ShapeDtypeStruct(xs.shape, U32),
        compiler_params=_params(("arbitrary",)),
        name="expert_ffn",
    )(blk_e, n_used, xs, we_gate, we_up, we_down)


def _ffn_tail_kernel(dest_ref, dnext_ref, w_ref, h_ref, x_ref, wg_ref, wu_ref, wd_ref, gpost_ref, gate_ref, ys_hbm,
                     xo_ref, buf, sem, *, nt):
    i = pl.program_id(0)
    tt = h_ref.shape[0] // SLAB
    cur = i % 2

    def slab(ref, row, n=1):
        return ref.at[pl.ds(pl.multiple_of(row * SLAB, SLAB), n * SLAB), :]

    def gather(idx_ref, slot, t0, t1):
        def issue(t, c):
            for k in range(TOP_K):
                pltpu.make_async_copy(slab(ys_hbm, idx_ref[k, t]), slab(buf.at[slot, k], t),
                                      sem.at[slot]).start(priority=k % 2)
            return c
        lax.fori_loop(t0, t1, issue, 0)

    @pl.when(i == 0)
    def _():
        gather(dest_ref, 0, 0, tt)

    for k in range(TOP_K):
        pltpu.make_async_copy(slab(ys_hbm, 0, tt), buf.at[cur, k], sem.at[cur]).wait()

    rc = tt // TAIL_PIECES
    for piece in range(TAIL_PIECES):
        r0 = piece * rc

        @pl.when(i + 1 < nt)
        def _():
            gather(dnext_ref, 1 - cur, r0, r0 + rc)

        h = _load_slabs_as_rows(h_ref, r0, rc).astype(BF16)
        hs = (_silu(_dot(h, wg_ref[...])) * _dot(h, wu_ref[...])).astype(BF16)
        f = _dot(hs, wd_ref[...])
        w = w_ref[r0:r0 + rc, :]
        for k in range(TOP_K):
            f = f + _load_slabs_as_rows(buf.at[cur, k], r0, rc) * w[:, k:k + 1]
        xo_ref[r0:r0 + rc, :] = x_ref[r0:r0 + rc, :] + gate_ref[...] * (_rms(f) * gpost_ref[...])


TAIL_PIECES = 4


def ffn_tail(dest_t, wts, h, x, ys, wsg, wsu, wsd, gpost, gate, rows_per_batch):
    t, d = x.shape
    ds = wsg.shape[1]
    tm = min(256, rows_per_batch)
    tpb = rows_per_batch // tm
    nt = t // tm
    row = pl.BlockSpec((tm, d), lambda i: (i, 0))
    const = lambda shp: pl.BlockSpec(shp, lambda i: (0,) * len(shp))
    return pl.pallas_call(
        functools.partial(_ffn_tail_kernel, nt=nt),
        grid=(nt,),
        in_specs=[pl.BlockSpec((TOP_K, tm), lambda i: (0, i), memory_space=pltpu.SMEM),
                  pl.BlockSpec((TOP_K, tm), lambda i: (0, jnp.minimum(i + 1, nt - 1)), memory_space=pltpu.SMEM),
                  pl.BlockSpec((tm, TOP_K), lambda i: (i, 0)),
                  pl.BlockSpec((tm * SLAB, LANES), lambda i: (i, 0)),
                  row, const((d, ds)), const((d, ds)), const((ds, d)), const((1, d)),
                  pl.BlockSpec((None, 1, d), lambda i: (i // tpb, 0, 0)),
                  pl.BlockSpec(memory_space=pl.ANY)],
        out_specs=row,
        out_shape=jax.ShapeDtypeStruct((t, d), F32),
        scratch_shapes=[pltpu.VMEM((2, TOP_K, tm * SLAB, LANES), U32), pltpu.SemaphoreType.DMA((2,))],
        compiler_params=_params(("arbitrary",)),
        name="ffn_tail",
    )(dest_t, dest_t, wts, h, x, wsg, wsu, wsd, gpost, gate, ys)


def _moe_routed(h, logits_t, b_router, we_gate, we_up, we_down, layer):
    eidx_t, wts_t = route(logits_t, b_router)
    dest_t, tabs, blk_e = moe_plan(eidx_t)
    pad_start = tabs[0, :, 0].astype(jnp.int32)
    pad_end = tabs[1, :, 0].astype(jnp.int32)
    n_used = (pad_end[N_EXPERTS - 1:] // MOE_ROWS).astype(jnp.int32)
    xs = moe_dispatch(dest_t, pad_start, pad_end, h)
    ys = expert_ffn(blk_e[0], n_used, xs, we_gate, we_up, we_down, layer)
    return dest_t, wts_t.T, ys


def _reorder_w_in(w):
    qa, ka, va, qw, kw, vw, z, xs, bm, cm, dt, gates = jnp.split(
        w, [512, 640, 768, 1280, 1408, 1536, 2560, 3584, 3840, 4096, 4128], axis=1)
    w_main = jnp.concatenate([qa, qw, z, xs, gates, ka, va, kw, vw, bm, cm], axis=1).astype(BF16)
    pad = jnp.zeros((w.shape[0], LANES - SSM_HEADS), w.dtype)
    w_dt = jnp.concatenate([dt[:, :SSM_HEADS], pad, dt[:, SSM_HEADS:], pad], axis=1).astype(BF16)
    return w_main, w_dt


def _pad_lanes(v):
    return jnp.pad(v, ((0, 0), (0, LANES - v.shape[1])))


def kernel(x, c, ctx, c_ctx, w_ada, b_ada, g_mix_pre, g_mix_post, g_ffn_pre, g_ffn_post, w_in, g_q_a, g_k_a, sink_w, ssm_conv_w, ssm_conv_b, ssm_dt_bias, ssm_a_log, ssm_d, ssm_norm, w_br_a, w_br_w, w_br_s, w_out, w_router, b_router, we_gate, we_up, we_down, ws_gate, ws_up, ws_down):
    nb, n, d = x.shape
    mc = ctx.shape[1]
    depth = w_in.shape[0]
    t_lat, t_ctx = nb * n, nb * mc
    cos, sin = rope_tables(n)
    xl = x.reshape(t_lat, d)
    xc = ctx.reshape(t_ctx, d)
    c8 = jnp.concatenate([c, c_ctx[None, :], jnp.zeros((8 - nb - 1, d), F32)], axis=0)
    zeros_sink = jnp.zeros((N_HEADS,), F32)
    s_zero = jnp.zeros((nb, 2, SSM_GROUPS, SSM_STATE, SSM_INNER // SSM_GROUPS), F32)
    dummy_tab = jnp.zeros((mc, N_HEADS * HEAD_DIM), F32)

    for i in range(depth):
        last = i == depth - 1
        mod = ada_mod(c8, w_ada, b_ada[i], i)
        mod_l = [mod[:nb, k * d:(k + 1) * d].reshape(nb, 1, d) for k in range(6)]
        mod_c = [mod[nb:nb + 1, k * d:(k + 1) * d].reshape(1, 1, d) for k in range(6)]
        w_main, w_dt = _reorder_w_in(w_in[i])
        gq = jnp.tile(g_q_a[i], N_HEADS)[None, :]
        gk = jnp.tile(g_k_a[i], N_KV)[None, :]
        bias2 = _pad_lanes(ssm_dt_bias[i].reshape(2, SSM_HEADS)).reshape(2, 1, LANES)
        aneg2 = _pad_lanes(-jnp.exp(ssm_a_log[i].astype(F32))).reshape(2, 1, LANES)
        dskip = jnp.repeat(ssm_d[i], SSM_P)[None, :]
        norm_w = ssm_norm[i][None, :]
        wa, ww, ws, wo = (w_br_a[i].astype(BF16), w_br_w[i].astype(BF16), w_br_s[i].astype(BF16),
                          w_out[i].astype(BF16))
        wr_t = w_router[i].T
        wr_hi = wr_t.astype(BF16)
        wr_lo = (wr_t - wr_hi.astype(F32)).astype(BF16)
        sink = sink_w[i].astype(F32)

        p_c, dt_c = in_proj(xc, g_mix_pre[i], mod_c[0], mod_c[1], w_main, w_dt, t_ctx)
        qa_c, qw_c, kda_c, vda_c, kdw_c, vdw_c = attn_prep(p_c, dummy_tab, dummy_tab, gq, gk, nb, mc, rope=False)
        uxs_c, ubc_c = ssm_conv(p_c, ssm_conv_w[i], ssm_conv_b[i], nb, mc)
        y_c, s_fin = ssd_scan(uxs_c, ubc_c, dt_c, bias2, aneg2, s_zero, nb, mc)

        p_l, dt_l = in_proj(xl, g_mix_pre[i], mod_l[0], mod_l[1], w_main, w_dt, n)
        qa, qw, kda, vda, kdw, vdw = attn_prep(p_l, cos, sin, gq, gk, nb, n, rope=True)
        m_all = n + mc
        kd_all = jnp.concatenate([kda.reshape(nb, n, -1), kda_c.reshape(nb, mc, -1)], axis=1).reshape(nb * m_all, -1)
        vd_all = jnp.concatenate([vda.reshape(nb, n, -1), vda_c.reshape(nb, mc, -1)], axis=1).reshape(nb * m_all, -1)
        score_bound = (math.sqrt(HEAD_DIM) * jnp.max(jnp.abs(g_q_a[i])) * jnp.max(jnp.abs(g_k_a[i]))).reshape(1)
        oa = flash_attn_bounded(qa, kd_all, vd_all, score_bound.astype(F32), nb, n, m_all)
        ow = window_attn(qw, kdw, vdw, kdw_c, vdw_c, sink, nb, n, mc)
        uxs, ubc = ssm_conv(p_l, ssm_conv_w[i], ssm_conv_b[i], nb, n)
        y_l, _ = ssd_scan(uxs, ubc, dt_l, bias2, aneg2, s_fin, nb, n)
        xl, h_l, lg_l = post_mixer(oa, ow, y_l, uxs, p_l, xl, dskip, norm_w, wa, ww, ws, wo,
                                   g_mix_post[i][None, :], mod_l[2], g_ffn_pre[i][None, :], mod_l[3], mod_l[4],
                                   wr_hi, wr_lo, n)
        wsg, wsu, wsd = ws_gate[i].astype(BF16), ws_up[i].astype(BF16), ws_down[i].astype(BF16)
        moe_w = (we_gate, we_up, we_down, i)
        if last:
            dest_t, wts, ys = _moe_routed(h_l, lg_l, b_router[i], *moe_w)
            xl = ffn_tail(dest_t, wts, h_l, xl, ys, wsg, wsu, wsd, g_ffn_post[i][None, :], mod_l[5], n)
        else:
            oa_c = flash_attn(qa_c, kda_c, vda_c, zeros_sink, nb, mc, mc, has_sink=False)
            ow_c = flash_attn(qw_c, kdw_c, vdw_c, sink, nb, mc, mc, has_sink=True)
            xc, h_c, lg_c = post_mixer(oa_c, ow_c, y_c, uxs_c, p_c, xc, dskip, norm_w, wa, ww, ws, wo,
                                       g_mix_post[i][None, :], mod_c[2], g_ffn_pre[i][None, :], mod_c[3], mod_c[4],
                                       wr_hi, wr_lo, t_ctx)
            h_all = jnp.concatenate([h_l, h_c], axis=0)
            lg_all = jnp.concatenate([lg_l, lg_c], axis=1)
            dest_t, wts, ys = _moe_routed(h_all, lg_all, b_router[i], *moe_w)
            xl = ffn_tail(dest_t[:, :t_lat], wts[:t_lat], h_l, xl, ys, wsg, wsu, wsd,
                          g_ffn_post[i][None, :], mod_l[5], n)
            xc = ffn_tail(dest_t[:, t_lat:], wts[t_lat:], h_c, xc, ys, wsg, wsu, wsd,
                          g_ffn_post[i][None, :], mod_c[5], t_ctx)
    return xl.reshape(nb, n, d)
```

```python
import functools
import math

import jax
import jax.numpy as jnp
import numpy as np
from jax import lax
from jax.experimental import pallas as pl
from jax.experimental.pallas import tpu as pltpu

F32 = jnp.float32
BF16 = jnp.bfloat16

HEAD_DIM = 64
N_HEADS = 8
N_KV = 2
GRID_W = 64
ROPE_THETA = 10000.0
WINDOW = 128
SSM_HEADS = 16
SSM_P = 64
SSM_INNER = SSM_HEADS * SSM_P
SSM_GROUPS = 2
SSM_STATE = 128
SSM_CHUNK = 128
N_EXPERTS = 64
TOP_K = 8
N_EXPERT_GROUPS = 8
TOPK_GROUPS = 4
ROUTED_SCALE = 2.5
EPS = 1e-6

LANES = 128
HALF = LANES // 2
VMEM_LIMIT = 56 * 1024 * 1024
NEG_BIG = -1e30

C_QA, C_QW, C_Z, C_XS, C_GATES = 0, 512, 1024, 2048, 3072
C_KA, C_VA, C_KW, C_VW, C_BC = 6144, 6272, 6400, 6528, 6656
P_WIDTH = 7168


def _params(sem, vmem=VMEM_LIMIT):
    return pltpu.CompilerParams(dimension_semantics=sem, vmem_limit_bytes=vmem)


_sigmoid = jax.nn.sigmoid


def _silu(x):
    return x * _sigmoid(x)


U32 = jnp.uint32
SLAB = 4
HI_MASK = 0xFFFF0000


def _store_rows_as_slabs(ref, x):
    r = x.shape[0]

    def bits(v):
        return pltpu.bitcast(v.astype(BF16).astype(F32), U32)

    for c in range(SLAB):
        lo = bits(x[:, c * LANES:(c + 1) * LANES]) >> 16
        hi = bits(x[:, (SLAB + c) * LANES:(SLAB + c + 1) * LANES]) & jnp.uint32(HI_MASK)
        ref[pl.ds(c, r, stride=SLAB), :] = hi | lo


def _load_slabs_as_rows(ref):
    r = ref.shape[0] // SLAB
    words = [ref[pl.ds(c, r, stride=SLAB), :] for c in range(SLAB)]
    los = [pltpu.bitcast(w << 16, F32) for w in words]
    his = [pltpu.bitcast(w & jnp.uint32(HI_MASK), F32) for w in words]
    return jnp.concatenate(los + his, axis=1)


def _softplus(x):
    return jnp.maximum(x, 0.0) + jnp.log(1.0 + jnp.exp(-jnp.abs(x)))


def _rms(x, eps=EPS):
    return x * lax.rsqrt(jnp.mean(x * x, axis=-1, keepdims=True) + eps)


def _split3(a):
    a1 = a.astype(BF16)
    r1 = a - a1.astype(F32)
    a2 = r1.astype(BF16)
    a3 = (r1 - a2.astype(F32)).astype(BF16)
    return a1, a2, a3


def _dot(a, b):
    return jnp.dot(a, b, preferred_element_type=F32)


def _dot_nt(a, b):
    return lax.dot_general(a, b, (((1,), (1,)), ((), ())), preferred_element_type=F32)


def _exact_right(a, r01, pieces=3):
    return sum(_dot(p, r01) for p in _split3(a)[:pieces])


def _exact_left(m01, a, pieces=3):
    return sum(_dot(m01, p) for p in _split3(a)[:pieces])


def _ada_kernel(c_ref, w_ref, b_ref, o_ref):
    h = _silu(c_ref[...])
    o_ref[...] = jnp.dot(h, w_ref[...], preferred_element_type=F32,
                         precision=lax.Precision.HIGHEST) + b_ref[...]


def ada_mod(c8, w_all, b, layer):
    _, d, n = w_all.shape
    tn = 1536
    return pl.pallas_call(
        _ada_kernel,
        grid=(n // tn,),
        in_specs=[pl.BlockSpec((8, d), lambda j: (0, 0)),
                  pl.BlockSpec((None, d, tn), lambda j: (layer, 0, j)),
                  pl.BlockSpec((1, tn), lambda j: (0, j))],
        out_specs=pl.BlockSpec((8, tn), lambda j: (0, j)),
        out_shape=jax.ShapeDtypeStruct((8, n), F32),
        compiler_params=_params(("parallel",)),
        name="ada_mod",
    )(c8, w_all, b.reshape(1, n))


def _inproj_kernel(x_ref, g_ref, sh_ref, sc_ref, w_ref, wdt_ref, o_ref, odt_ref, h_scr):
    @pl.when(pl.program_id(1) == 0)
    def _():
        h = _rms(x_ref[...]) * g_ref[...]
        h = h * (1.0 + sc_ref[...]) + sh_ref[...]
        hb = h.astype(BF16)
        h_scr[...] = hb
        odt_ref[...] = _dot(hb, wdt_ref[...])

    o_ref[...] = _dot(h_scr[...], w_ref[...]).astype(BF16)


def in_proj(x, g, shift, scale, w_main, w_dt, rows_per_batch):
    t, d = x.shape
    n = w_main.shape[1]
    tm = min(1024, rows_per_batch)
    tn = n // 4
    tpb = rows_per_batch // tm
    mod_spec = pl.BlockSpec((None, 1, d), lambda i, j: (i // tpb, 0, 0))
    return pl.pallas_call(
        _inproj_kernel,
        grid=(t // tm, n // tn),
        in_specs=[pl.BlockSpec((tm, d), lambda i, j: (i, 0)),
                  pl.BlockSpec((1, d), lambda i, j: (0, 0)),
                  mod_spec, mod_spec,
                  pl.BlockSpec((d, tn), lambda i, j: (0, j)),
                  pl.BlockSpec((d, 2 * LANES), lambda i, j: (0, 0))],
        out_specs=[pl.BlockSpec((tm, tn), lambda i, j: (i, j)),
                   pl.BlockSpec((tm, 2 * LANES), lambda i, j: (i, 0))],
        out_shape=[jax.ShapeDtypeStruct((t, n), BF16),
                   jax.ShapeDtypeStruct((t, 2 * LANES), F32)],
        scratch_shapes=[pltpu.VMEM((tm, d), BF16)],
        compiler_params=_params(("parallel", "arbitrary")),
        name="in_proj",
    )(x, g.reshape(1, d), shift, scale, w_main, w_dt)


def _rope(x, cos, sin):
    w = x.shape[-1]
    lane = lax.broadcasted_iota(jnp.int32, x.shape, 1)
    first = (lane % 32) < 16
    swapped = jnp.where(first, pltpu.roll(x, w - 16, 1), pltpu.roll(x, 16, 1))
    return x * cos + swapped * sin


def _dup_halves(x):
    lane = lax.broadcasted_iota(jnp.int32, x.shape, 1)
    lo = lane < HALF
    r = pltpu.roll(x, HALF, 1)
    return jnp.concatenate([jnp.where(lo, x, r), jnp.where(lo, r, x)], axis=1)


def _prep_kernel(qa_ref, qw_ref, ka_ref, va_ref, kw_ref, vw_ref, cos_ref, sin_ref,
                 gq_ref, gk_ref, bdq_ref, bdk_ref,
                 qa_o, qw_o, kda_o, vda_o, kdw_o, vdw_o, *, rope):
    scale = HEAD_DIM ** -0.5
    inv_hd = 1.0 / HEAD_DIM

    def headnorm(x, g, bd):
        ss = _dot((x * x).astype(BF16), bd) * inv_hd
        return x * lax.rsqrt(ss + EPS) * g

    qa = headnorm(qa_ref[...].astype(F32), gq_ref[...], bdq_ref[...])
    ka = headnorm(ka_ref[...].astype(F32), gk_ref[...], bdk_ref[...])
    qw = qw_ref[...].astype(F32)
    kw = kw_ref[...].astype(F32)
    if rope:
        cos = cos_ref[...]
        sin = sin_ref[...]
        qa = _rope(qa, cos, sin)
        qw = _rope(qw, cos, sin)
        ka = _rope(ka, cos[:, :LANES], sin[:, :LANES])
        kw = _rope(kw, cos[:, :LANES], sin[:, :LANES])
    qa_o[...] = (qa * scale).astype(BF16)
    qw_o[...] = (qw * scale).astype(BF16)
    kda_o[...] = _dup_halves(ka).astype(BF16)
    kdw_o[...] = _dup_halves(kw).astype(BF16)
    vda_o[...] = _dup_halves(va_ref[...].astype(F32)).astype(BF16)
    vdw_o[...] = _dup_halves(vw_ref[...].astype(F32)).astype(BF16)


def attn_prep(p, cos, sin, gq, gk, nb, n, rope):
    t = nb * n
    tm = min(512, n)
    spb = n // tm
    hq = N_HEADS * HEAD_DIM
    hk = N_KV * HEAD_DIM
    bdq = (np.arange(hq)[:, None] // HEAD_DIM == np.arange(hq)[None, :] // HEAD_DIM)
    bdq = jnp.asarray(bdq, BF16)
    bdk = bdq[:hk, :hk]
    qspec = lambda c: pl.BlockSpec((tm, hq), lambda s, b: (b * spb + s, c // hq))
    kspec = lambda c: pl.BlockSpec((tm, hk), lambda s, b: (b * spb + s, c // hk))
    tab = pl.BlockSpec((tm, hq), lambda s, b: (s, 0))
    const = lambda shp: pl.BlockSpec(shp, lambda s, b: (0, 0))
    oq = pl.BlockSpec((tm, hq), lambda s, b: (b * spb + s, 0))
    ok = pl.BlockSpec((tm, 2 * hk), lambda s, b: (b * spb + s, 0))
    return pl.pallas_call(
        functools.partial(_prep_kernel, rope=rope),
        grid=(spb, nb),
        in_specs=[qspec(C_QA), qspec(C_QW), kspec(C_KA), kspec(C_VA), kspec(C_KW), kspec(C_VW),
                  tab, tab, const((1, hq)), const((1, hk)), const((hq, hq)), const((hk, hk))],
        out_specs=[oq, oq, ok, ok, ok, ok],
        out_shape=[jax.ShapeDtypeStruct((t, hq), BF16)] * 2 + [jax.ShapeDtypeStruct((t, 2 * hk), BF16)] * 4,
        compiler_params=_params(("parallel", "arbitrary")),
        name="attn_prep",
    )(p, p, p, p, p, p, cos, sin, gq, gk, bdq, bdk)


def rope_tables(n):
    rows = n // GRID_W
    row = jnp.repeat(jnp.arange(rows, dtype=F32), GRID_W)
    col = jnp.tile(jnp.arange(GRID_W, dtype=F32), rows)
    axis_dim = HEAD_DIM // 2
    inv_freq = ROPE_THETA ** (-jnp.arange(0, axis_dim, 2, dtype=F32) / axis_dim)
    ang_r = row[:, None] * inv_freq[None, :]
    ang_c = col[:, None] * inv_freq[None, :]
    cr, sr, cc, sc = jnp.cos(ang_r), jnp.sin(ang_r), jnp.cos(ang_c), jnp.sin(ang_c)
    cos = jnp.concatenate([cr, cr, cc, cc], axis=1)
    sin = jnp.concatenate([-sr, sr, -sc, sc], axis=1)
    return jnp.tile(cos, (1, N_HEADS)), jnp.tile(sin, (1, N_HEADS))


def _pair_operands(kd, vd):
    lane = lax.broadcasted_iota(jnp.int32, kd.shape, 1)
    lo = lane < HALF
    zero = jnp.zeros_like(kd)
    kmats = (jnp.where(lo, kd, zero), jnp.where(lo, zero, kd))
    vstack = jnp.concatenate([jnp.where(lo, vd, zero), jnp.where(lo, zero, vd)], axis=0)
    return kmats, vstack


def _flash_kernel(sink_ref, q_ref, k_ref, v_ref, o_ref, m_scr, l_scr, acc_scr, *, has_sink, nk):
    ki = pl.program_id(2)
    tq = q_ref.shape[0]

    @pl.when(ki == 0)
    def _():
        m_scr[...] = jnp.full(m_scr.shape, NEG_BIG, F32)
        l_scr[...] = jnp.zeros(l_scr.shape, F32)
        acc_scr[...] = jnp.zeros(acc_scr.shape, F32)

    lane_q = lax.broadcasted_iota(jnp.int32, (tq, LANES), 1)
    lo_q = lane_q < HALF
    pairs_per_kv = N_HEADS // N_KV // 2
    for j in range(N_KV):
        kmats, vstack = _pair_operands(k_ref[:, j * LANES:(j + 1) * LANES],
                                       v_ref[:, j * LANES:(j + 1) * LANES])
        for pp in range(pairs_per_kv):
            hp = j * pairs_per_kv + pp
            qp = q_ref[:, hp * LANES:(hp + 1) * LANES]
            ps, alphas = [], []
            for par in range(2):
                h = 2 * hp + par
                s = _dot_nt(qp, kmats[par])
                m_prev = m_scr[h]
                m_new = jnp.maximum(m_prev, jnp.max(s, axis=1, keepdims=True))
                alpha = jnp.exp(m_prev - m_new)
                p = jnp.exp(s - m_new[:, :1])
                l_scr[h] = alpha * l_scr[h] + jnp.sum(p, axis=1, keepdims=True)
                m_scr[h] = m_new
                ps.append(p.astype(BF16))
                alphas.append(alpha)
            pv = _dot(jnp.concatenate(ps, axis=1), vstack)
            sl = slice(hp * LANES, (hp + 1) * LANES)
            acc_scr[:, sl] = acc_scr[:, sl] * jnp.where(lo_q, alphas[0], alphas[1]) + pv

    @pl.when(ki == nk - 1)
    def _():
        for hp in range(N_HEADS // 2):
            ls = []
            for par in range(2):
                h = 2 * hp + par
                l = l_scr[h]
                if has_sink:
                    l = l + jnp.exp(sink_ref[h] - m_scr[h])
                ls.append(l)
            sl = slice(hp * LANES, (hp + 1) * LANES)
            o_ref[:, sl] = (acc_scr[:, sl] / jnp.where(lo_q, ls[0], ls[1])).astype(BF16)


def _flash_bounded_kernel(c_ref, q_ref, k_ref, v_ref, o_ref, lmin_ref, l_scr, acc_scr, *, nk):
    ki = pl.program_id(2)
    tq = q_ref.shape[0]
    tk = k_ref.shape[0]

    @pl.when(ki == 0)
    def _():
        l_scr[...] = jnp.zeros(l_scr.shape, F32)
        acc_scr[...] = jnp.zeros(acc_scr.shape, F32)

    c = c_ref[0]
    pairs_per_kv = N_HEADS // N_KV // 2
    operands = [_pair_operands(k_ref[:, j * LANES:(j + 1) * LANES], v_ref[:, j * LANES:(j + 1) * LANES])
                for j in range(N_KV)]
    rc = min(FLASH_ROW_CHUNK, tq)

    def row_chunk(ri, carry):
        rows = pl.ds(pl.multiple_of(ri * rc, rc), rc)
        for j in range(N_KV):
            kmats, vstack = operands[j]
            for pp in range(pairs_per_kv):
                hp = j * pairs_per_kv + pp
                sl = slice(hp * LANES, (hp + 1) * LANES)
                qp = q_ref[rows, sl]
                ps = []
                for par in range(2):
                    h = 2 * hp + par
                    p = jnp.exp(_dot_nt(qp, kmats[par]) - c)
                    part = p[:, 0:LANES]
                    for cb in range(1, tk // LANES):
                        part = part + p[:, cb * LANES:(cb + 1) * LANES]
                    l_scr[h, rows, :] = l_scr[h, rows, :] + part
                    ps.append(p.astype(BF16))
                acc_scr[rows, sl] = acc_scr[rows, sl] + _dot(jnp.concatenate(ps, axis=1), vstack)
        return carry

    lax.fori_loop(0, tq // rc, row_chunk, 0)

    @pl.when(ki == nk - 1)
    def _():
        lo_q = lax.broadcasted_iota(jnp.int32, (tq, LANES), 1) < HALF
        mins = []
        for hp in range(N_HEADS // 2):
            ls = [jnp.sum(l_scr[2 * hp + par], axis=1, keepdims=True) for par in range(2)]
            sl = slice(hp * LANES, (hp + 1) * LANES)
            o_ref[:, sl] = (acc_scr[:, sl] / jnp.where(lo_q, ls[0], ls[1])).astype(BF16)
            mins += [jnp.broadcast_to(jnp.min(l, axis=0, keepdims=True), (1, LANES)) for l in ls]
        lmin_ref[...] = jnp.concatenate(mins, axis=0)


FLASH_MIN_DENOM = 1e-30
FLASH_ROW_CHUNK = 256


def flash_attn_bounded(q, kd, vd, bound, nb, n, m):
    tq = min(1024, n)
    tk = _pick_tile(m, (768, 512, 256))
    nq, nk = n // tq, m // tk
    hq = N_HEADS * HEAD_DIM
    o, lmin = pl.pallas_call(
        functools.partial(_flash_bounded_kernel, nk=nk),
        grid=(nb, nq, nk),
        in_specs=[pl.BlockSpec(memory_space=pltpu.SMEM),
                  pl.BlockSpec((tq, hq), lambda b, i, k: (b * nq + i, 0)),
                  pl.BlockSpec((tk, 2 * LANES), lambda b, i, k: (b * nk + k, 0)),
                  pl.BlockSpec((tk, 2 * LANES), lambda b, i, k: (b * nk + k, 0))],
        out_specs=[pl.BlockSpec((tq, hq), lambda b, i, k: (b * nq + i, 0)),
                   pl.BlockSpec((N_HEADS, LANES), lambda b, i, k: (b * nq + i, 0))],
        out_shape=[jax.ShapeDtypeStruct((nb * n, hq), BF16),
                   jax.ShapeDtypeStruct((nb * nq * N_HEADS, LANES), F32)],
        scratch_shapes=[pltpu.VMEM((N_HEADS, tq, LANES), F32),
                        pltpu.VMEM((tq, hq), F32)],
        compiler_params=_params(("parallel", "parallel", "arbitrary")),
        name="flash_attn_bounded",
    )(bound, q, kd, vd)
    ok = jnp.min(lmin) > FLASH_MIN_DENOM
    return lax.cond(ok, lambda: o,
                    lambda: flash_attn(q, kd, vd, jnp.zeros((N_HEADS,), F32), nb, n, m, has_sink=False))


def _pick_tile(m, cands):
    for c in cands:
        if m % c == 0:
            return c
    raise ValueError(f"no tile for {m}")


def flash_attn(q, kd, vd, sink, nb, n, m, has_sink):
    tq = min(512, n)
    tk = _pick_tile(m, (768, 512, 256))
    nq, nk = n // tq, m // tk
    hq = N_HEADS * HEAD_DIM
    return pl.pallas_call(
        functools.partial(_flash_kernel, has_sink=has_sink, nk=nk),
        grid=(nb, nq, nk),
        in_specs=[pl.BlockSpec(memory_space=pltpu.SMEM),
                  pl.BlockSpec((tq, hq), lambda b, i, k: (b * nq + i, 0)),
                  pl.BlockSpec((tk, 2 * LANES), lambda b, i, k: (b * nk + k, 0)),
                  pl.BlockSpec((tk, 2 * LANES), lambda b, i, k: (b * nk + k, 0))],
        out_specs=pl.BlockSpec((tq, hq), lambda b, i, k: (b * nq + i, 0)),
        out_shape=jax.ShapeDtypeStruct((nb * n, hq), BF16),
        scratch_shapes=[pltpu.VMEM((N_HEADS, tq, LANES), F32),
                        pltpu.VMEM((N_HEADS, tq, LANES), F32),
                        pltpu.VMEM((tq, hq), F32)],
        compiler_params=_params(("parallel", "parallel", "arbitrary")),
        name="flash_attn",
    )(sink, q, kd, vd)


def _window_kernel(sink_ref, q_ref, kp_ref, km_ref, kn_ref, vp_ref, vm_ref, vn_ref, kc_ref, vc_ref,
                   o_ref, *, n, tq):
    i = pl.program_id(1)
    span = tq + 2 * WINDOW
    q0 = i * tq
    r = lax.broadcasted_iota(jnp.int32, (tq, span), 0)
    c = lax.broadcasted_iota(jnp.int32, (tq, span), 1)
    kpos = c + (q0 - WINDOW)
    ok = (c >= r) & (c <= r + 2 * WINDOW) & (kpos >= 0) & (kpos < n)
    lane_q = lax.broadcasted_iota(jnp.int32, (tq, LANES), 1)
    lo_q = lane_q < HALF
    k_all = jnp.concatenate([kp_ref[...], km_ref[...], kn_ref[...], kc_ref[...]], axis=0)
    v_all = jnp.concatenate([vp_ref[...], vm_ref[...], vn_ref[...], vc_ref[...]], axis=0)
    bias = jnp.concatenate([jnp.where(ok, 0.0, NEG_BIG), jnp.zeros((tq, kc_ref.shape[0]), F32)], axis=1)
    pairs_per_kv = N_HEADS // N_KV // 2
    for j in range(N_KV):
        js = slice(j * LANES, (j + 1) * LANES)
        kmats, vstack = _pair_operands(k_all[:, js], v_all[:, js])
        for pp in range(pairs_per_kv):
            hp = j * pairs_per_kv + pp
            qp = q_ref[:, hp * LANES:(hp + 1) * LANES]
            ps, ls = [], []
            for par in range(2):
                h = 2 * hp + par
                s = _dot_nt(qp, kmats[par]) + bias
                snk = sink_ref[h]
                m = jnp.maximum(jnp.max(s, axis=1, keepdims=True), snk)
                p = jnp.exp(s - m)
                ls.append(jnp.sum(p, axis=1, keepdims=True) + jnp.exp(snk - m))
                ps.append(p.astype(BF16))
            o = _dot(jnp.concatenate(ps, axis=1), vstack)
            o_ref[:, hp * LANES:(hp + 1) * LANES] = (o / jnp.where(lo_q, ls[0], ls[1])).astype(BF16)


def window_attn(q, kd, vd, kdc, vdc, sink, nb, n, mc):
    tq = 2 * WINDOW
    nq = n // tq
    wb = n // WINDOW
    hq = N_HEADS * HEAD_DIM
    prev = pl.BlockSpec((WINDOW, 2 * LANES), lambda b, i: (b * wb + jnp.maximum(2 * i - 1, 0), 0))
    main = pl.BlockSpec((tq, 2 * LANES), lambda b, i: (b * nq + i, 0))
    nxt = pl.BlockSpec((WINDOW, 2 * LANES), lambda b, i: (b * wb + jnp.minimum(2 * i + 2, wb - 1), 0))
    ctx = pl.BlockSpec((mc, 2 * LANES), lambda b, i: (b, 0))
    return pl.pallas_call(
        functools.partial(_window_kernel, n=n, tq=tq),
        grid=(nb, nq),
        in_specs=[pl.BlockSpec(memory_space=pltpu.SMEM),
                  pl.BlockSpec((tq, hq), lambda b, i: (b * nq + i, 0)),
                  prev, main, nxt, prev, main, nxt, ctx, ctx],
        out_specs=pl.BlockSpec((tq, hq), lambda b, i: (b * nq + i, 0)),
        out_shape=jax.ShapeDtypeStruct((nb * n, hq), BF16),
        compiler_params=_params(("parallel", "parallel")),
        name="window_attn",
    )(sink, q, kd, kd, kd, vd, vd, vd, kdc, vdc)


HALO = 16


def _conv_kernel(xm_ref, xp_ref, xn_ref, bm_ref, bp_ref, bn_ref, wx_ref, bx_ref, wb_ref, bb_ref,
                 ox_ref, ob_ref, *, nt):
    i = pl.program_id(1)
    has_prev = jnp.where(i > 0, 1.0, 0.0)
    has_next = jnp.where(i < nt - 1, 1.0, 0.0)

    def conv(m_ref, p_ref, n_ref, w_ref, b_ref, o_ref):
        x = m_ref[...].astype(F32)
        tl = x.shape[0]
        row = lax.broadcasted_iota(jnp.int32, x.shape, 0)
        before = p_ref[...].astype(F32)[HALO - 1:HALO, :] * has_prev
        after = n_ref[...].astype(F32)[0:1, :] * has_next
        xm1 = jnp.where(row == 0, before, pltpu.roll(x, 1, 0))
        xp1 = jnp.where(row == tl - 1, after, pltpu.roll(x, tl - 1, 0))
        w = w_ref[...]
        y = xm1 * w[0:1, :] + x * w[1:2, :] + xp1 * w[2:3, :] + b_ref[...]
        o_ref[...] = _silu(y).astype(BF16)

    conv(xm_ref, xp_ref, xn_ref, wx_ref, bx_ref, ox_ref)
    conv(bm_ref, bp_ref, bn_ref, wb_ref, bb_ref, ob_ref)


def ssm_conv(p, conv_w, conv_b, nb, n):
    tl = min(512, n)
    nt = n // tl
    hb = n // HALO
    hpt = tl // HALO
    cx, cb = SSM_INNER, 2 * SSM_GROUPS * SSM_STATE

    def specs(width, col):
        cblk = col // width
        return (pl.BlockSpec((tl, width), lambda b, i: (b * nt + i, cblk)),
                pl.BlockSpec((HALO, width), lambda b, i: (b * hb + jnp.maximum(i * hpt - 1, 0), cblk)),
                pl.BlockSpec((HALO, width), lambda b, i: (b * hb + jnp.minimum((i + 1) * hpt, hb - 1), cblk)))

    const = lambda shp: pl.BlockSpec(shp, lambda b, i: (0, 0))
    xm, xp, xn = specs(cx, C_XS)
    bm, bp, bn = specs(cb, C_BC)
    return pl.pallas_call(
        functools.partial(_conv_kernel, nt=nt),
        grid=(nb, nt),
        in_specs=[xm, xp, xn, bm, bp, bn, const((3, cx)), const((1, cx)), const((3, cb)), const((1, cb))],
        out_specs=[pl.BlockSpec((tl, cx), lambda b, i: (b * nt + i, 0)),
                   pl.BlockSpec((tl, cb), lambda b, i: (b * nt + i, 0))],
        out_shape=[jax.ShapeDtypeStruct((nb * n, cx), BF16), jax.ShapeDtypeStruct((nb * n, cb), BF16)],
        compiler_params=_params(("parallel", "parallel")),
        name="ssm_conv",
    )(p, p, p, p, p, p, conv_w[:, :cx], conv_b[:cx].reshape(1, cx), conv_w[:, cx:], conv_b[cx:].reshape(1, cb))


def _ssd_chunk(d, r0, xs_ref, bc_ref, dt_ref, bias, aneg, tri, rep, st_scr, y_ref):
    q = SSM_CHUNK
    rows = slice(r0, r0 + q)
    gw = SSM_INNER // SSM_GROUPS
    dt = _softplus(dt_ref[rows, :] + bias)
    a = dt * aneg
    ac = _exact_left(tri, a, pieces=2)
    act = ac.T
    acx = _exact_right(ac, rep, pieces=2)
    dtx = _dot(dt.astype(BF16), rep)
    totx = acx[q - 1:q, :] if d == 0 else acx[0:1, :]
    xd = xs_ref[rows, :].astype(F32) * dtx
    xd_b = xd.astype(BF16)
    xe = (xd * jnp.exp(totx - acx)).astype(BF16)
    ein = jnp.exp(acx)
    keep = tri > 0
    lane = lax.broadcasted_iota(jnp.int32, (q, LANES), 1)
    lo = lane < HALF
    zero = jnp.zeros((q, LANES), BF16)
    hpg = SSM_HEADS // SSM_GROUPS
    for g in range(SSM_GROUPS):
        bg = bc_ref[rows, g * SSM_STATE:(g + 1) * SSM_STATE]
        cg = bc_ref[rows, (SSM_GROUPS + g) * SSM_STATE:(SSM_GROUPS + g + 1) * SSM_STATE]
        cb = _dot_nt(cg, bg)
        st = st_scr[d, g]
        yoff = _dot(cg, st.astype(BF16)) * ein[:, g * gw:(g + 1) * gw]
        for hp in range(hpg // 2):
            gs = []
            for par in range(2):
                h = g * hpg + 2 * hp + par
                seg = ac[:, h:h + 1] - act[h:h + 1, :]
                gs.append((cb * jnp.exp(jnp.where(keep, seg, NEG_BIG))).astype(BF16))
            c0 = g * gw + hp * LANES
            xp = xd_b[:, c0:c0 + LANES]
            xstack = jnp.concatenate([jnp.where(lo, xp, zero), jnp.where(lo, zero, xp)], axis=0)
            ydiag = _dot(jnp.concatenate(gs, axis=1), xstack)
            y_ref[rows, c0:c0 + LANES] = (ydiag + yoff[:, hp * LANES:(hp + 1) * LANES]).astype(BF16)
        bgt = bg.astype(F32).T.astype(BF16)
        cs = _dot(bgt, xe[:, g * gw:(g + 1) * gw])
        st_scr[d, g] = st * jnp.exp(totx[:, g * gw:(g + 1) * gw]) + cs


def _ssd_kernel(xf_ref, bcf_ref, dtf_ref, xb_ref, bcb_ref, dtb_ref, bias_ref, aneg_ref, tri_ref, rep_ref,
                s0_ref, yf_ref, yb_ref, sfin_ref, st_scr, *, nsteps):
    k = pl.program_id(1)

    @pl.when(k == 0)
    def _():
        st_scr[...] = s0_ref[...]

    rep = rep_ref[...]
    for c in range(SSD_CHUNKS_PER_STEP):
        _ssd_chunk(0, c * SSM_CHUNK, xf_ref, bcf_ref, dtf_ref, bias_ref[0], aneg_ref[0], tri_ref[0], rep,
                   st_scr, yf_ref)
        _ssd_chunk(1, (SSD_CHUNKS_PER_STEP - 1 - c) * SSM_CHUNK, xb_ref, bcb_ref, dtb_ref, bias_ref[1],
                   aneg_ref[1], tri_ref[1], rep, st_scr, yb_ref)

    @pl.when(k == nsteps - 1)
    def _():
        sfin_ref[...] = st_scr[...]


SSD_CHUNKS_PER_STEP = 2


def ssd_scan(u_xs, u_bc, dt_raw, bias2, aneg2, s0, nb, n):
    q = SSM_CHUNK
    nc = n // q
    gw = SSM_INNER // SSM_GROUPS
    idx = np.arange(q)
    tri = np.stack([idx[:, None] >= idx[None, :], idx[:, None] <= idx[None, :]]).astype(np.float32)
    rep = (np.arange(LANES)[:, None] == np.arange(SSM_INNER)[None, :] // SSM_P).astype(np.float32)

    ns = nc // SSD_CHUNKS_PER_STEP
    rb = q * SSD_CHUNKS_PER_STEP
    fwd = lambda b, k: b * ns + k
    bwd = lambda b, k: b * ns + (ns - 1 - k)
    cbc = 2 * SSM_GROUPS * SSM_STATE
    state = pl.BlockSpec((None, 2, SSM_GROUPS, SSM_STATE, gw), lambda b, k: (b, 0, 0, 0, 0))
    const = lambda shp: pl.BlockSpec(shp, lambda b, k: (0,) * len(shp))
    yf, yb, sfin = pl.pallas_call(
        functools.partial(_ssd_kernel, nsteps=ns),
        grid=(nb, ns),
        in_specs=[pl.BlockSpec((rb, SSM_INNER), lambda b, k: (fwd(b, k), 0)),
                  pl.BlockSpec((rb, cbc), lambda b, k: (fwd(b, k), 0)),
                  pl.BlockSpec((rb, LANES), lambda b, k: (fwd(b, k), 0)),
                  pl.BlockSpec((rb, SSM_INNER), lambda b, k: (bwd(b, k), 0)),
                  pl.BlockSpec((rb, cbc), lambda b, k: (bwd(b, k), 0)),
                  pl.BlockSpec((rb, LANES), lambda b, k: (bwd(b, k), 1)),
                  const((2, 1, LANES)), const((2, 1, LANES)), const((2, q, q)), const((LANES, SSM_INNER)),
                  state],
        out_specs=[pl.BlockSpec((rb, SSM_INNER), lambda b, k: (fwd(b, k), 0)),
                   pl.BlockSpec((rb, SSM_INNER), lambda b, k: (bwd(b, k), 0)),
                   state],
        out_shape=[jax.ShapeDtypeStruct((nb * n, SSM_INNER), BF16),
                   jax.ShapeDtypeStruct((nb * n, SSM_INNER), BF16),
                   jax.ShapeDtypeStruct((nb, 2, SSM_GROUPS, SSM_STATE, gw), F32)],
        scratch_shapes=[pltpu.VMEM((2, SSM_GROUPS, SSM_STATE, gw), F32)],
        compiler_params=_params(("parallel", "arbitrary")),
        name="ssd_scan",
    )(u_xs, u_bc, dt_raw, u_xs, u_bc, dt_raw, bias2, aneg2, jnp.asarray(tri, BF16), jnp.asarray(rep, BF16), s0)
    return (yf, yb), sfin


def _post_kernel(oa_ref, ow_ref, yf_ref, yb_ref, xs_ref, z_ref, gt_ref, x_ref,
                 dsk_ref, nw_ref, wa_ref, ww_ref, ws_ref, wo_ref, gpost_ref, gate_ref,
                 gpre_ref, sh_ref, sc_ref, wrh_ref, wrl_ref,
                 xo_ref, h_ref, lg_ref):
    y = yf_ref[...].astype(F32) + yb_ref[...].astype(F32) + dsk_ref[...] * xs_ref[...].astype(F32)
    u = y * _silu(z_ref[...].astype(F32))
    gw = SSM_INNER // SSM_GROUPS
    ys = jnp.concatenate([_rms(u[:, g * gw:(g + 1) * gw]) for g in range(SSM_GROUPS)], axis=1)
    ys = (ys * nw_ref[...]).astype(BF16)
    d = x_ref.shape[1]
    ga = _sigmoid(gt_ref[:, 0:d].astype(F32))
    gw_ = _sigmoid(gt_ref[:, d:2 * d].astype(F32))
    gs = _sigmoid(gt_ref[:, 2 * d:3 * d].astype(F32))
    m = ga * _dot(oa_ref[...], wa_ref[...]) + gw_ * _dot(ow_ref[...], ww_ref[...]) + gs * _dot(ys, ws_ref[...])
    ml = _dot(m.astype(BF16), wo_ref[...])
    xn = x_ref[...] + gate_ref[...] * (_rms(ml) * gpost_ref[...])
    xo_ref[...] = xn
    h = (_rms(xn) * gpre_ref[...]) * (1.0 + sc_ref[...]) + sh_ref[...]
    _store_rows_as_slabs(h_ref, h)
    hb = h.astype(BF16)
    hl =(h - hb.astype(F32)).astype(BF16)
    lg_ref[...] = _dot_nt(wrh_ref[...], hb) + _dot_nt(wrh_ref[...], hl) + _dot_nt(wrl_ref[...], hb)


def post_mixer(oa, ow, y2, u_xs, p, x, dskip, norm_w, wa, ww, ws, wo, gpost, gate, gpre, shift, scale,
               wr_hi, wr_lo, rows_per_batch):
    t, d = x.shape
    tm = min(256, rows_per_batch)
    tpb = rows_per_batch // tm
    nt = t // tm
    hq = N_HEADS * HEAD_DIM
    row = lambda w, c=0: pl.BlockSpec((tm, w), lambda i: (i, c // w))
    const = lambda shp: pl.BlockSpec(shp, lambda i: (0,) * len(shp))
    mod = pl.BlockSpec((None, 1, d), lambda i: (i // tpb, 0, 0))
    return pl.pallas_call(
        _post_kernel,
        grid=(nt,),
        in_specs=[row(hq), row(hq), row(SSM_INNER), row(SSM_INNER),
                  row(SSM_INNER), row(SSM_INNER, C_Z), row(3 * d, C_GATES), row(d),
                  const((1, SSM_INNER)), const((1, SSM_INNER)),
                  const((hq, d)), const((hq, d)), const((SSM_INNER, d)), const((d, d)),
                  const((1, d)), mod, const((1, d)), mod, mod,
                  const((N_EXPERTS, d)), const((N_EXPERTS, d))],
        out_specs=[row(d), pl.BlockSpec((tm * SLAB, LANES), lambda i: (i, 0)),
                   pl.BlockSpec((N_EXPERTS, tm), lambda i: (0, i))],
        out_shape=[jax.ShapeDtypeStruct((t, d), F32), jax.ShapeDtypeStruct((t * SLAB, LANES), U32),
                   jax.ShapeDtypeStruct((N_EXPERTS, t), F32)],
        compiler_params=_params(("parallel",)),
        name="post_mixer",
    )(oa, ow, y2[0], y2[1], u_xs, p, p, x, dskip, norm_w, wa, ww, ws, wo, gpost, gate, gpre, shift, scale,
      wr_hi, wr_lo)


def _route_kernel(lg_ref, b_ref, ei_ref, w_ref):
    scores = _sigmoid(lg_ref[...])
    sel = scores + b_ref[...]
    tt = sel.shape[1]
    per = N_EXPERTS // N_EXPERT_GROUPS
    r8 = lax.broadcasted_iota(jnp.int32, (per, tt), 0).astype(F32)
    ninf = -jnp.inf

    def argmax_rows(x, rows, nrows):
        m = jnp.max(x, axis=0, keepdims=True)
        idx = jnp.min(jnp.where(x == m, rows, float(nrows)), axis=0, keepdims=True)
        return m, idx

    gscores = []
    for g in range(N_EXPERT_GROUPS):
        blk = sel[g * per:(g + 1) * per, :]
        m1, i1 = argmax_rows(blk, r8, per)
        m2 = jnp.max(jnp.where(r8 == i1, ninf, blk), axis=0, keepdims=True)
        gscores.append(m1 + m2)
    cur = jnp.concatenate(gscores, axis=0)
    rg = lax.broadcasted_iota(jnp.int32, cur.shape, 0).astype(F32)
    chosen = jnp.zeros(cur.shape, F32)
    for _ in range(TOPK_GROUPS):
        _, gi = argmax_rows(cur, rg, N_EXPERT_GROUPS)
        hit = rg == gi
        chosen = jnp.where(hit, 1.0, chosen)
        cur = jnp.where(hit, ninf, cur)
    gmask = jnp.concatenate([jnp.broadcast_to(chosen[g:g + 1, :], (per, tt)) for g in range(N_EXPERT_GROUPS)],
                            axis=0)
    cur = jnp.where(gmask > 0, sel, ninf)
    re = lax.broadcasted_iota(jnp.int32, cur.shape, 0).astype(F32)
    idxs, ws = [], []
    for _ in range(TOP_K):
        _, ei = argmax_rows(cur, re, N_EXPERTS)
        hit = re == ei
        ws.append(jnp.sum(jnp.where(hit, scores, 0.0), axis=0, keepdims=True))
        idxs.append(ei)
        cur = jnp.where(hit, ninf, cur)
    w = jnp.concatenate(ws, axis=0)
    w_ref[...] = w / jnp.sum(w, axis=0, keepdims=True) * ROUTED_SCALE
    ei_ref[...] = jnp.concatenate(idxs, axis=0).astype(jnp.int32)


def route(logits_t, b_router):
    e, t = logits_t.shape
    tt = 512
    return pl.pallas_call(
        _route_kernel,
        grid=(t // tt,),
        in_specs=[pl.BlockSpec((e, tt), lambda i: (0, i)), pl.BlockSpec((e, 1), lambda i: (0, 0))],
        out_specs=[pl.BlockSpec((TOP_K, tt), lambda i: (0, i)), pl.BlockSpec((TOP_K, tt), lambda i: (0, i))],
        out_shape=[jax.ShapeDtypeStruct((TOP_K, t), jnp.int32), jax.ShapeDtypeStruct((TOP_K, t), F32)],
        compiler_params=_params(("parallel",)),
        name="route",
    )(logits_t, b_router.reshape(e, 1))


MOE_ROWS = 512
PLAN_TOKENS = 512


def _moe_geometry(t):
    nblk = -(-(t * TOP_K + N_EXPERTS * (MOE_ROWS - 1)) // MOE_ROWS)
    return nblk, nblk * MOE_ROWS


def _plan_kernel(ei_ref, ut_ref, tril_ref, dest_ref, tab_ref, be_ref, cnt_scr, run_scr):
    ph = pl.program_id(0)
    i = pl.program_id(1)
    ei = ei_ref[...]
    tt = ei.shape[1]
    re = lax.broadcasted_iota(jnp.int32, (N_EXPERTS, tt), 0)
    hits = [re == ei[k:k + 1, :] for k in range(TOP_K)]
    oh = jnp.zeros((N_EXPERTS, tt), F32)
    for k in range(TOP_K):
        oh = oh + jnp.where(hits[k], 1.0, 0.0)

    @pl.when((ph == 0) & (i == 0))
    def _():
        cnt_scr[...] = jnp.zeros(cnt_scr.shape, F32)

    @pl.when(ph == 0)
    def _():
        cnt_scr[...] = cnt_scr[...] + jnp.sum(oh, axis=1, keepdims=True)

    @pl.when((ph == 1) & (i == 0))
    def _():
        cnt = cnt_scr[...]
        padded = ((cnt.astype(jnp.int32) + (MOE_ROWS - 1)) & (-MOE_ROWS)).astype(F32)
        pad_end = _exact_left(tril_ref[...], padded)
        pad_start = pad_end - padded
        run_scr[...] = pad_start
        tab_ref[0] = pad_start
        tab_ref[1] = pad_end
        nbp = be_ref.shape[1]
        blk0 = (lax.broadcasted_iota(jnp.int32, (N_EXPERTS, nbp), 1) * MOE_ROWS).astype(F32)
        be = jnp.sum(jnp.where(pad_end[:, :1] <= blk0, 1.0, 0.0), axis=0, keepdims=True)
        be_ref[...] = jnp.broadcast_to(jnp.minimum(be, N_EXPERTS - 1.0), be_ref.shape).astype(jnp.int32)

    @pl.when(ph == 1)
    def _():
        cin = _dot(oh.astype(BF16), ut_ref[...])
        pos = run_scr[:, :1] + (cin - oh)
        rows = [jnp.sum(jnp.where(hits[k], pos, 0.0), axis=0, keepdims=True) for k in range(TOP_K)]
        dest_ref[...] = jnp.concatenate(rows, axis=0).astype(jnp.int32)
        run_scr[...] = run_scr[...] + cin[:, tt - 1:tt]


def moe_plan(eidx_t):
    k, t = eidx_t.shape
    tt = PLAN_TOKENS
    nt = t // tt
    nblk, _ = _moe_geometry(t)
    nbp = -(-nblk // LANES) * LANES
    ut = jnp.asarray(np.arange(tt)[:, None] <= np.arange(tt)[None, :], BF16)
    tril = jnp.asarray(np.arange(N_EXPERTS)[:, None] >= np.arange(N_EXPERTS)[None, :], BF16)
    return pl.pallas_call(
        _plan_kernel,
        grid=(2, nt),
        in_specs=[pl.BlockSpec((k, tt), lambda p, i: (0, i)),
                  pl.BlockSpec((tt, tt), lambda p, i: (0, 0)),
                  pl.BlockSpec((N_EXPERTS, N_EXPERTS), lambda p, i: (0, 0))],
        out_specs=[pl.BlockSpec((k, tt), lambda p, i: (0, i * p)),
                   pl.BlockSpec((2, N_EXPERTS, LANES), lambda p, i: (0, 0, 0)),
                   pl.BlockSpec((8, nbp), lambda p, i: (0, 0))],
        out_shape=[jax.ShapeDtypeStruct((k, t), jnp.int32),
                   jax.ShapeDtypeStruct((2, N_EXPERTS, LANES), F32),
                   jax.ShapeDtypeStruct((8, nbp), jnp.int32)],
        scratch_shapes=[pltpu.VMEM((N_EXPERTS, LANES), F32), pltpu.VMEM((N_EXPERTS, LANES), F32)],
        compiler_params=_params(("arbitrary", "arbitrary")),
        name="moe_plan",
    )(eidx_t, ut, tril)


def _dispatch_kernel(dest_ref, pstart_ref, pend_ref, h_ref, xs_hbm, zero_scr, sem):
    i = pl.program_id(0)
    tt = h_ref.shape[0] // SLAB

    def slab(ref, row, n=1):
        return ref.at[pl.ds(pl.multiple_of(row * SLAB, SLAB), n * SLAB), :]

    @pl.when(i == 0)
    def _():
        zero_scr[...] = jnp.zeros(zero_scr.shape, U32)
        nblk = xs_hbm.shape[0] // (MOE_ROWS * SLAB)
        n_used = pend_ref[N_EXPERTS - 1] // MOE_ROWS

        def zero_block(row0):
            return pltpu.make_async_copy(zero_scr, slab(xs_hbm, row0, MOE_ROWS), sem)

        def seg_start(e, c):
            @pl.when(pend_ref[e] > pstart_ref[e])
            def _():
                zero_block(pend_ref[e] - MOE_ROWS).start()
            return c

        def seg_wait(e, c):
            @pl.when(pend_ref[e] > pstart_ref[e])
            def _():
                zero_block(pend_ref[e] - MOE_ROWS).wait()
            return c

        def tail_start(b, c):
            zero_block(b * MOE_ROWS).start()
            return c

        def tail_wait(b, c):
            zero_block(b * MOE_ROWS).wait()
            return c

        lax.fori_loop(0, N_EXPERTS, seg_start, 0)
        lax.fori_loop(n_used, nblk, tail_start, 0)
        lax.fori_loop(0, N_EXPERTS, seg_wait, 0)
        lax.fori_loop(n_used, nblk, tail_wait, 0)

    def issue(t, c):
        for k in range(TOP_K):
            pltpu.make_async_copy(slab(h_ref, t), slab(xs_hbm, dest_ref[k, t]), sem).start(priority=k % 2)
        return c

    lax.fori_loop(0, tt, issue, 0)
    for k in range(TOP_K):
        pltpu.make_async_copy(h_ref, slab(xs_hbm, 0, tt), sem).wait()


def moe_dispatch(dest_t, pad_start, pad_end, h):
    t = h.shape[0] // SLAB
    tt = PLAN_TOKENS
    _, cap = _moe_geometry(t)
    smem = pl.BlockSpec(memory_space=pltpu.SMEM)
    return pl.pallas_call(
        _dispatch_kernel,
        grid=(t // tt,),
        in_specs=[pl.BlockSpec((TOP_K, tt), lambda i: (0, i), memory_space=pltpu.SMEM), smem, smem,
                  pl.BlockSpec((tt * SLAB, LANES), lambda i: (i, 0))],
        out_specs=pl.BlockSpec(memory_space=pl.ANY),
        out_shape=jax.ShapeDtypeStruct((cap * SLAB, LANES), U32),
        scratch_shapes=[pltpu.VMEM((MOE_ROWS * SLAB, LANES), U32), pltpu.SemaphoreType.DMA(())],
        compiler_params=_params(("arbitrary",)),
        name="moe_dispatch",
    )(dest_t, pad_start, pad_end, h)


def _expert_kernel(be_ref, nu_ref, x_ref, wg_ref, wu_ref, wd_ref, o_ref):
    used = pl.program_id(0) < nu_ref[0]

    @pl.when(used)
    def _():
        x = _load_slabs_as_rows(x_ref).astype(BF16)
        hg = _dot(x, wg_ref[...].astype(BF16))
        hu = _dot(x, wu_ref[...].astype(BF16))
        hb = (_silu(hg) * hu).astype(BF16)
        _store_rows_as_slabs(o_ref, _dot(hb, wd_ref[...].astype(BF16)))

    @pl.when(jnp.logical_not(used))
    def _():
        o_ref[...] = jnp.zeros(o_ref.shape, U32)


def expert_ffn(blk_e, n_used, xs, we_gate, we_up, we_down, layer):
    d, de = we_gate.shape[2:]
    nblk = xs.shape[0] // (MOE_ROWS * SLAB)
    blk = lambda i, nu: jnp.minimum(i, nu[0] - 1)
    grid_spec = pltpu.PrefetchScalarGridSpec(
        num_scalar_prefetch=2,
        grid=(nblk,),
        in_specs=[pl.BlockSpec((MOE_ROWS * SLAB, LANES), lambda i, be, nu: (blk(i, nu), 0)),
                  pl.BlockSpec((None, None, d, de), lambda i, be, nu: (layer, be[blk(i, nu)], 0, 0)),
                  pl.BlockSpec((None, None, d, de), lambda i, be, nu: (layer, be[blk(i, nu)], 0, 0)),
                  pl.BlockSpec((None, None, de, d), lambda i, be, nu: (layer, be[blk(i, nu)], 0, 0))],
        out_specs=pl.BlockSpec((MOE_ROWS * SLAB, LANES), lambda i, be, nu: (i, 0)),
    )
    return pl.pallas_call(
        _expert_kernel,
        grid_spec=grid_spec,
        out_shape=jax.ShapeDtypeStruct(xs.shape, U32),
        compiler_params=_params(("arbitrary",)),
        name="expert_ffn",
    )(blk_e, n_used, xs, we_gate, we_up, we_down)


def _ffn_tail_kernel(dest_ref, dnext_ref, w_ref, h_ref, x_ref, wg_ref, wu_ref, wd_ref, gpost_ref, gate_ref, ys_hbm,
                     xo_ref, buf, sem, *, nt):
    i = pl.program_id(0)
    tt = h_ref.shape[0] // SLAB
    cur = i % 2

    def slab(ref, row, n=1):
        return ref.at[pl.ds(pl.multiple_of(row * SLAB, SLAB), n * SLAB), :]

    def gather(idx_ref, slot):
        def issue(t, c):
            for k in range(TOP_K):
                pltpu.make_async_copy(slab(ys_hbm, idx_ref[k, t]), slab(buf.at[slot, k], t),
                                      sem.at[slot]).start(priority=k % 2)
            return c
        lax.fori_loop(0, tt, issue, 0)

    @pl.when(i == 0)
    def _():
        gather(dest_ref, 0)

    @pl.when(i + 1 < nt)
    def _():
        gather(dnext_ref, 1 - cur)

    h = _load_slabs_as_rows(h_ref).astype(BF16)
    hs = (_silu(_dot(h, wg_ref[...])) * _dot(h, wu_ref[...])).astype(BF16)
    f = _dot(hs, wd_ref[...])
    w = w_ref[...]
    for k in range(TOP_K):
        pltpu.make_async_copy(slab(ys_hbm, 0, tt), buf.at[cur, k], sem.at[cur]).wait()
    for k in range(TOP_K):
        f = f + _load_slabs_as_rows(buf.at[cur, k]) * w[:, k:k + 1]
    xo_ref[...] = x_ref[...] + gate_ref[...] * (_rms(f) * gpost_ref[...])


def ffn_tail(dest_t, wts, h, x, ys, wsg, wsu, wsd, gpost, gate, rows_per_batch):
    t, d = x.shape
    ds = wsg.shape[1]
    tm = min(256, rows_per_batch)
    tpb = rows_per_batch // tm
    nt = t // tm
    row = pl.BlockSpec((tm, d), lambda i: (i, 0))
    const = lambda shp: pl.BlockSpec(shp, lambda i: (0,) * len(shp))
    return pl.pallas_call(
        functools.partial(_ffn_tail_kernel, nt=nt),
        grid=(nt,),
        in_specs=[pl.BlockSpec((TOP_K, tm), lambda i: (0, i), memory_space=pltpu.SMEM),
                  pl.BlockSpec((TOP_K, tm), lambda i: (0, jnp.minimum(i + 1, nt - 1)), memory_space=pltpu.SMEM),
                  pl.BlockSpec((tm, TOP_K), lambda i: (i, 0)),
                  pl.BlockSpec((tm * SLAB, LANES), lambda i: (i, 0)),
                  row, const((d, ds)), const((d, ds)), const((ds, d)), const((1, d)),
                  pl.BlockSpec((None, 1, d), lambda i: (i // tpb, 0, 0)),
                  pl.BlockSpec(memory_space=pl.ANY)],
        out_specs=row,
        out_shape=jax.ShapeDtypeStruct((t, d), F32),
        scratch_shapes=[pltpu.VMEM((2, TOP_K, tm * SLAB, LANES), U32), pltpu.SemaphoreType.DMA((2,))],
        compiler_params=_params(("arbitrary",)),
        name="ffn_tail",
    )(dest_t, dest_t, wts, h, x, wsg, wsu, wsd, gpost, gate, ys)


def _moe_routed(h, logits_t, b_router, we_gate, we_up, we_down, layer):
    eidx_t, wts_t = route(logits_t, b_router)
    dest_t, tabs, blk_e = moe_plan(eidx_t)
    pad_start = tabs[0, :, 0].astype(jnp.int32)
    pad_end = tabs[1, :, 0].astype(jnp.int32)
    n_used = (pad_end[N_EXPERTS - 1:] // MOE_ROWS).astype(jnp.int32)
    xs = moe_dispatch(dest_t, pad_start, pad_end, h)
    ys = expert_ffn(blk_e[0], n_used, xs, we_gate, we_up, we_down, layer)
    return dest_t, wts_t.T, ys


def _reorder_w_in(w):
    qa, ka, va, qw, kw, vw, z, xs, bm, cm, dt, gates = jnp.split(
        w, [512, 640, 768, 1280, 1408, 1536, 2560, 3584, 3840, 4096, 4128], axis=1)
    w_main = jnp.concatenate([qa, qw, z, xs, gates, ka, va, kw, vw, bm, cm], axis=1).astype(BF16)
    pad = jnp.zeros((w.shape[0], LANES - SSM_HEADS), w.dtype)
    w_dt = jnp.concatenate([dt[:, :SSM_HEADS], pad, dt[:, SSM_HEADS:], pad], axis=1).astype(BF16)
    return w_main, w_dt


def _pad_lanes(v):
    return jnp.pad(v, ((0, 0), (0, LANES - v.shape[1])))


def kernel(x, c, ctx, c_ctx, w_ada, b_ada, g_mix_pre, g_mix_post, g_ffn_pre, g_ffn_post, w_in, g_q_a, g_k_a, sink_w, ssm_conv_w, ssm_conv_b, ssm_dt_bias, ssm_a_log, ssm_d, ssm_norm, w_br_a, w_br_w, w_br_s, w_out, w_router, b_router, we_gate, we_up, we_down, ws_gate, ws_up, ws_down):
    nb, n, d = x.shape
    mc = ctx.shape[1]
    depth = w_in.shape[0]
    t_lat, t_ctx = nb * n, nb * mc
    cos, sin = rope_tables(n)
    xl = x.reshape(t_lat, d)
    xc = ctx.reshape(t_ctx, d)
    c8 = jnp.concatenate([c, c_ctx[None, :], jnp.zeros((8 - nb - 1, d), F32)], axis=0)
    zeros_sink = jnp.zeros((N_HEADS,), F32)
    s_zero = jnp.zeros((nb, 2, SSM_GROUPS, SSM_STATE, SSM_INNER // SSM_GROUPS), F32)
    dummy_tab = jnp.zeros((mc, N_HEADS * HEAD_DIM), F32)

    for i in range(depth):
        last = i == depth - 1
        mod = ada_mod(c8, w_ada, b_ada[i], i)
        mod_l = [mod[:nb, k * d:(k + 1) * d].reshape(nb, 1, d) for k in range(6)]
        mod_c = [mod[nb:nb + 1, k * d:(k + 1) * d].reshape(1, 1, d) for k in range(6)]
        w_main, w_dt = _reorder_w_in(w_in[i])
        gq = jnp.tile(g_q_a[i], N_HEADS)[None, :]
        gk = jnp.tile(g_k_a[i], N_KV)[None, :]
        bias2 = _pad_lanes(ssm_dt_bias[i].reshape(2, SSM_HEADS)).reshape(2, 1, LANES)
        aneg2 = _pad_lanes(-jnp.exp(ssm_a_log[i].astype(F32))).reshape(2, 1, LANES)
        dskip = jnp.repeat(ssm_d[i], SSM_P)[None, :]
        norm_w = ssm_norm[i][None, :]
        wa, ww, ws, wo = (w_br_a[i].astype(BF16), w_br_w[i].astype(BF16), w_br_s[i].astype(BF16),
                          w_out[i].astype(BF16))
        wr_t = w_router[i].T
        wr_hi = wr_t.astype(BF16)
        wr_lo = (wr_t - wr_hi.astype(F32)).astype(BF16)
        sink = sink_w[i].astype(F32)

        p_c, dt_c = in_proj(xc, g_mix_pre[i], mod_c[0], mod_c[1], w_main, w_dt, t_ctx)
        qa_c, qw_c, kda_c, vda_c, kdw_c, vdw_c = attn_prep(p_c, dummy_tab, dummy_tab, gq, gk, nb, mc, rope=False)
        uxs_c, ubc_c = ssm_conv(p_c, ssm_conv_w[i], ssm_conv_b[i], nb, mc)
        y_c, s_fin = ssd_scan(uxs_c, ubc_c, dt_c, bias2, aneg2, s_zero, nb, mc)

        p_l, dt_l = in_proj(xl, g_mix_pre[i], mod_l[0], mod_l[1], w_main, w_dt, n)
        qa, qw, kda, vda, kdw, vdw = attn_prep(p_l, cos, sin, gq, gk, nb, n, rope=True)
        m_all = n + mc
        kd_all = jnp.concatenate([kda.reshape(nb, n, -1), kda_c.reshape(nb, mc, -1)], axis=1).reshape(nb * m_all, -1)
        vd_all = jnp.concatenate([vda.reshape(nb, n, -1), vda_c.reshape(nb, mc, -1)], axis=1).reshape(nb * m_all, -1)
        score_bound = (math.sqrt(HEAD_DIM) * jnp.max(jnp.abs(g_q_a[i])) * jnp.max(jnp.abs(g_k_a[i]))).reshape(1)
        oa = flash_attn_bounded(qa, kd_all, vd_all, score_bound.astype(F32), nb, n, m_all)
        ow = window_attn(qw, kdw, vdw, kdw_c, vdw_c, sink, nb, n, mc)
        uxs, ubc = ssm_conv(p_l, ssm_conv_w[i], ssm_conv_b[i], nb, n)
        y_l, _ = ssd_scan(uxs, ubc, dt_l, bias2, aneg2, s_fin, nb, n)
        xl, h_l, lg_l = post_mixer(oa, ow, y_l, uxs, p_l, xl, dskip, norm_w, wa, ww, ws, wo,
                                   g_mix_post[i][None, :], mod_l[2], g_ffn_pre[i][None, :], mod_l[3], mod_l[4],
                                   wr_hi, wr_lo, n)
        wsg, wsu, wsd = ws_gate[i].astype(BF16), ws_up[i].astype(BF16), ws_down[i].astype(BF16)
        moe_w = (we_gate, we_up, we_down, i)
        if last:
            dest_t, wts, ys = _moe_routed(h_l, lg_l, b_router[i], *moe_w)
            xl = ffn_tail(dest_t, wts, h_l, xl, ys, wsg, wsu, wsd, g_ffn_post[i][None, :], mod_l[5], n)
        else:
            oa_c = flash_attn(qa_c, kda_c, vda_c, zeros_sink, nb, mc, mc, has_sink=False)
            ow_c = flash_attn(qw_c, kdw_c, vdw_c, sink, nb, mc, mc, has_sink=True)
            xc, h_c, lg_c = post_mixer(oa_c, ow_c, y_c, uxs_c, p_c, xc, dskip, norm_w, wa, ww, ws, wo,
                                       g_mix_post[i][None, :], mod_c[2], g_ffn_pre[i][None, :], mod_c[3], mod_c[4],
                                       wr_hi, wr_lo, t_ctx)
            h_all = jnp.concatenate([h_l, h_c], axis=0)
            lg_all = jnp.concatenate([lg_l, lg_c], axis=1)
            dest_t, wts, ys = _moe_routed(h_all, lg_all, b_router[i], *moe_w)
            xl = ffn_tail(dest_t[:, :t_lat], wts[:t_lat], h_l, xl, ys, wsg, wsu, wsd,
                          g_ffn_post[i][None, :], mod_l[5], n)
            xc = ffn_tail(dest_t[:, t_lat:], wts[t_lat:], h_c, xc, ys, wsg, wsu, wsd,
                          g_ffn_post[i][None, :], mod_c[5], t_ctx)
    return xl.reshape(nb, n, d)
```

```python
import functools
import math

import jax
import jax.numpy as jnp
import numpy as np
from jax import lax
from jax.experimental import pallas as pl
from jax.experimental.pallas import tpu as pltpu

F32 = jnp.float32
BF16 = jnp.bfloat16

HEAD_DIM = 64
N_HEADS = 8
N_KV = 2
GRID_W = 64
ROPE_THETA = 10000.0
WINDOW = 128
SSM_HEADS = 16
SSM_P = 64
SSM_INNER = SSM_HEADS * SSM_P
SSM_GROUPS = 2
SSM_STATE = 128
SSM_CHUNK = 128
N_EXPERTS = 64
TOP_K = 8
N_EXPERT_GROUPS = 8
TOPK_GROUPS = 4
ROUTED_SCALE = 2.5
EPS = 1e-6

LANES = 128
HALF = LANES // 2
VMEM_LIMIT = 56 * 1024 * 1024
NEG_BIG = -1e30

C_QA, C_QW, C_Z, C_XS, C_GATES = 0, 512, 1024, 2048, 3072
C_KA, C_VA, C_KW, C_VW, C_BC = 6144, 6272, 6400, 6528, 6656
P_WIDTH = 7168


def _params(sem, vmem=VMEM_LIMIT):
    return pltpu.CompilerParams(dimension_semantics=sem, vmem_limit_bytes=vmem)


def _silu(x):
    return x * jax.nn.sigmoid(x)


U32 = jnp.uint32
SLAB = 4
HI_MASK = 0xFFFF0000


def _store_rows_as_slabs(ref, x):
    r = x.shape[0]

    def bits(v):
        return pltpu.bitcast(v.astype(BF16).astype(F32), U32)

    for c in range(SLAB):
        lo = bits(x[:, c * LANES:(c + 1) * LANES]) >> 16
        hi = bits(x[:, (SLAB + c) * LANES:(SLAB + c + 1) * LANES]) & jnp.uint32(HI_MASK)
        ref[pl.ds(c, r, stride=SLAB), :] = hi | lo


def _load_slabs_as_rows(ref):
    r = ref.shape[0] // SLAB
    words = [ref[pl.ds(c, r, stride=SLAB), :] for c in range(SLAB)]
    los = [pltpu.bitcast(w << 16, F32) for w in words]
    his = [pltpu.bitcast(w & jnp.uint32(HI_MASK), F32) for w in words]
    return jnp.concatenate(los + his, axis=1)


def _softplus(x):
    return jnp.maximum(x, 0.0) + jnp.log(1.0 + jnp.exp(-jnp.abs(x)))


def _rms(x, eps=EPS):
    return x * lax.rsqrt(jnp.mean(x * x, axis=-1, keepdims=True) + eps)


def _split3(a):
    a1 = a.astype(BF16)
    r1 = a - a1.astype(F32)
    a2 = r1.astype(BF16)
    a3 = (r1 - a2.astype(F32)).astype(BF16)
    return a1, a2, a3


def _dot(a, b):
    return jnp.dot(a, b, preferred_element_type=F32)


def _dot_nt(a, b):
    return lax.dot_general(a, b, (((1,), (1,)), ((), ())), preferred_element_type=F32)


def _exact_right(a, r01, pieces=3):
    return sum(_dot(p, r01) for p in _split3(a)[:pieces])


def _exact_left(m01, a, pieces=3):
    return sum(_dot(m01, p) for p in _split3(a)[:pieces])


def _ada_kernel(c_ref, w_ref, b_ref, o_ref):
    h = _silu(c_ref[...])
    o_ref[...] = jnp.dot(h, w_ref[...], preferred_element_type=F32,
                         precision=lax.Precision.HIGHEST) + b_ref[...]


def ada_mod(c8, w_all, b, layer):
    _, d, n = w_all.shape
    tn = 1536
    return pl.pallas_call(
        _ada_kernel,
        grid=(n // tn,),
        in_specs=[pl.BlockSpec((8, d), lambda j: (0, 0)),
                  pl.BlockSpec((None, d, tn), lambda j: (layer, 0, j)),
                  pl.BlockSpec((1, tn), lambda j: (0, j))],
        out_specs=pl.BlockSpec((8, tn), lambda j: (0, j)),
        out_shape=jax.ShapeDtypeStruct((8, n), F32),
        compiler_params=_params(("parallel",)),
        name="ada_mod",
    )(c8, w_all, b.reshape(1, n))


def _inproj_kernel(x_ref, g_ref, sh_ref, sc_ref, w_ref, wdt_ref, o_ref, odt_ref, h_scr):
    @pl.when(pl.program_id(1) == 0)
    def _():
        h = _rms(x_ref[...]) * g_ref[...]
        h = h * (1.0 + sc_ref[...]) + sh_ref[...]
        hb = h.astype(BF16)
        h_scr[...] = hb
        odt_ref[...] = _dot(hb, wdt_ref[...])

    o_ref[...] = _dot(h_scr[...], w_ref[...]).astype(BF16)


def in_proj(x, g, shift, scale, w_main, w_dt, rows_per_batch):
    t, d = x.shape
    n = w_main.shape[1]
    tm = min(1024, rows_per_batch)
    tn = n // 4
    tpb = rows_per_batch // tm
    mod_spec = pl.BlockSpec((None, 1, d), lambda i, j: (i // tpb, 0, 0))
    return pl.pallas_call(
        _inproj_kernel,
        grid=(t // tm, n // tn),
        in_specs=[pl.BlockSpec((tm, d), lambda i, j: (i, 0)),
                  pl.BlockSpec((1, d), lambda i, j: (0, 0)),
                  mod_spec, mod_spec,
                  pl.BlockSpec((d, tn), lambda i, j: (0, j)),
                  pl.BlockSpec((d, 2 * LANES), lambda i, j: (0, 0))],
        out_specs=[pl.BlockSpec((tm, tn), lambda i, j: (i, j)),
                   pl.BlockSpec((tm, 2 * LANES), lambda i, j: (i, 0))],
        out_shape=[jax.ShapeDtypeStruct((t, n), BF16),
                   jax.ShapeDtypeStruct((t, 2 * LANES), F32)],
        scratch_shapes=[pltpu.VMEM((tm, d), BF16)],
        compiler_params=_params(("parallel", "arbitrary")),
        name="in_proj",
    )(x, g.reshape(1, d), shift, scale, w_main, w_dt)


def _rope(x, cos, sin):
    w = x.shape[-1]
    lane = lax.broadcasted_iota(jnp.int32, x.shape, 1)
    first = (lane % 32) < 16
    swapped = jnp.where(first, pltpu.roll(x, w - 16, 1), pltpu.roll(x, 16, 1))
    return x * cos + swapped * sin


def _dup_halves(x):
    lane = lax.broadcasted_iota(jnp.int32, x.shape, 1)
    lo = lane < HALF
    r = pltpu.roll(x, HALF, 1)
    return jnp.concatenate([jnp.where(lo, x, r), jnp.where(lo, r, x)], axis=1)


def _prep_kernel(qa_ref, qw_ref, ka_ref, va_ref, kw_ref, vw_ref, cos_ref, sin_ref,
                 gq_ref, gk_ref, bdq_ref, bdk_ref,
                 qa_o, qw_o, kda_o, vda_o, kdw_o, vdw_o, *, rope):
    scale = HEAD_DIM ** -0.5
    inv_hd = 1.0 / HEAD_DIM

    def headnorm(x, g, bd):
        ss = _dot((x * x).astype(BF16), bd) * inv_hd
        return x * lax.rsqrt(ss + EPS) * g

    qa = headnorm(qa_ref[...].astype(F32), gq_ref[...], bdq_ref[...])
    ka = headnorm(ka_ref[...].astype(F32), gk_ref[...], bdk_ref[...])
    qw = qw_ref[...].astype(F32)
    kw = kw_ref[...].astype(F32)
    if rope:
        cos = cos_ref[...]
        sin = sin_ref[...]
        qa = _rope(qa, cos, sin)
        qw = _rope(qw, cos, sin)
        ka = _rope(ka, cos[:, :LANES], sin[:, :LANES])
        kw = _rope(kw, cos[:, :LANES], sin[:, :LANES])
    qa_o[...] = (qa * scale).astype(BF16)
    qw_o[...] = (qw * scale).astype(BF16)
    kda_o[...] = _dup_halves(ka).astype(BF16)
    kdw_o[...] = _dup_halves(kw).astype(BF16)
    vda_o[...] = _dup_halves(va_ref[...].astype(F32)).astype(BF16)
    vdw_o[...] = _dup_halves(vw_ref[...].astype(F32)).astype(BF16)


def attn_prep(p, cos, sin, gq, gk, nb, n, rope):
    t = nb * n
    tm = min(512, n)
    spb = n // tm
    hq = N_HEADS * HEAD_DIM
    hk = N_KV * HEAD_DIM
    bdq = (np.arange(hq)[:, None] // HEAD_DIM == np.arange(hq)[None, :] // HEAD_DIM)
    bdq = jnp.asarray(bdq, BF16)
    bdk = bdq[:hk, :hk]
    qspec = lambda c: pl.BlockSpec((tm, hq), lambda s, b: (b * spb + s, c // hq))
    kspec = lambda c: pl.BlockSpec((tm, hk), lambda s, b: (b * spb + s, c // hk))
    tab = pl.BlockSpec((tm, hq), lambda s, b: (s, 0))
    const = lambda shp: pl.BlockSpec(shp, lambda s, b: (0, 0))
    oq = pl.BlockSpec((tm, hq), lambda s, b: (b * spb + s, 0))
    ok = pl.BlockSpec((tm, 2 * hk), lambda s, b: (b * spb + s, 0))
    return pl.pallas_call(
        functools.partial(_prep_kernel, rope=rope),
        grid=(spb, nb),
        in_specs=[qspec(C_QA), qspec(C_QW), kspec(C_KA), kspec(C_VA), kspec(C_KW), kspec(C_VW),
                  tab, tab, const((1, hq)), const((1, hk)), const((hq, hq)), const((hk, hk))],
        out_specs=[oq, oq, ok, ok, ok, ok],
        out_shape=[jax.ShapeDtypeStruct((t, hq), BF16)] * 2 + [jax.ShapeDtypeStruct((t, 2 * hk), BF16)] * 4,
        compiler_params=_params(("parallel", "arbitrary")),
        name="attn_prep",
    )(p, p, p, p, p, p, cos, sin, gq, gk, bdq, bdk)


def rope_tables(n):
    rows = n // GRID_W
    row = jnp.repeat(jnp.arange(rows, dtype=F32), GRID_W)
    col = jnp.tile(jnp.arange(GRID_W, dtype=F32), rows)
    axis_dim = HEAD_DIM // 2
    inv_freq = ROPE_THETA ** (-jnp.arange(0, axis_dim, 2, dtype=F32) / axis_dim)
    ang_r = row[:, None] * inv_freq[None, :]
    ang_c = col[:, None] * inv_freq[None, :]
    cr, sr, cc, sc = jnp.cos(ang_r), jnp.sin(ang_r), jnp.cos(ang_c), jnp.sin(ang_c)
    cos = jnp.concatenate([cr, cr, cc, cc], axis=1)
    sin = jnp.concatenate([-sr, sr, -sc, sc], axis=1)
    return jnp.tile(cos, (1, N_HEADS)), jnp.tile(sin, (1, N_HEADS))


def _pair_operands(kd, vd):
    lane = lax.broadcasted_iota(jnp.int32, kd.shape, 1)
    lo = lane < HALF
    zero = jnp.zeros_like(kd)
    kmats = (jnp.where(lo, kd, zero), jnp.where(lo, zero, kd))
    vstack = jnp.concatenate([jnp.where(lo, vd, zero), jnp.where(lo, zero, vd)], axis=0)
    return kmats, vstack


def _flash_kernel(sink_ref, q_ref, k_ref, v_ref, o_ref, m_scr, l_scr, acc_scr, *, has_sink, nk):
    ki = pl.program_id(2)
    tq = q_ref.shape[0]

    @pl.when(ki == 0)
    def _():
        m_scr[...] = jnp.full(m_scr.shape, NEG_BIG, F32)
        l_scr[...] = jnp.zeros(l_scr.shape, F32)
        acc_scr[...] = jnp.zeros(acc_scr.shape, F32)

    lane_q = lax.broadcasted_iota(jnp.int32, (tq, LANES), 1)
    lo_q = lane_q < HALF
    pairs_per_kv = N_HEADS // N_KV // 2
    for j in range(N_KV):
        kmats, vstack = _pair_operands(k_ref[:, j * LANES:(j + 1) * LANES],
                                       v_ref[:, j * LANES:(j + 1) * LANES])
        for pp in range(pairs_per_kv):
            hp = j * pairs_per_kv + pp
            qp = q_ref[:, hp * LANES:(hp + 1) * LANES]
            ps, alphas = [], []
            for par in range(2):
                h = 2 * hp + par
                s = _dot_nt(qp, kmats[par])
                m_prev = m_scr[h]
                m_new = jnp.maximum(m_prev, jnp.max(s, axis=1, keepdims=True))
                alpha = jnp.exp(m_prev - m_new)
                p = jnp.exp(s - m_new[:, :1])
                l_scr[h] = alpha * l_scr[h] + jnp.sum(p, axis=1, keepdims=True)
                m_scr[h] = m_new
                ps.append(p.astype(BF16))
                alphas.append(alpha)
            pv = _dot(jnp.concatenate(ps, axis=1), vstack)
            sl = slice(hp * LANES, (hp + 1) * LANES)
            acc_scr[:, sl] = acc_scr[:, sl] * jnp.where(lo_q, alphas[0], alphas[1]) + pv

    @pl.when(ki == nk - 1)
    def _():
        for hp in range(N_HEADS // 2):
            ls = []
            for par in range(2):
                h = 2 * hp + par
                l = l_scr[h]
                if has_sink:
                    l = l + jnp.exp(sink_ref[h] - m_scr[h])
                ls.append(l)
            sl = slice(hp * LANES, (hp + 1) * LANES)
            o_ref[:, sl] = (acc_scr[:, sl] / jnp.where(lo_q, ls[0], ls[1])).astype(BF16)


def _flash_bounded_kernel(c_ref, q_ref, k_ref, v_ref, o_ref, lmin_ref, l_scr, acc_scr, *, nk):
    ki = pl.program_id(2)
    tq = q_ref.shape[0]
    tk = k_ref.shape[0]

    @pl.when(ki == 0)
    def _():
        l_scr[...] = jnp.zeros(l_scr.shape, F32)
        acc_scr[...] = jnp.zeros(acc_scr.shape, F32)

    c = c_ref[0]
    pairs_per_kv = N_HEADS // N_KV // 2
    for j in range(N_KV):
        kmats, vstack = _pair_operands(k_ref[:, j * LANES:(j + 1) * LANES],
                                       v_ref[:, j * LANES:(j + 1) * LANES])
        for pp in range(pairs_per_kv):
            hp = j * pairs_per_kv + pp
            qp = q_ref[:, hp * LANES:(hp + 1) * LANES]
            ps = []
            for par in range(2):
                h = 2 * hp + par
                p = jnp.exp(_dot_nt(qp, kmats[par]) - c)
                part = p[:, 0:LANES]
                for cb in range(1, tk // LANES):
                    part = part + p[:, cb * LANES:(cb + 1) * LANES]
                l_scr[h] = l_scr[h] + part
                ps.append(p.astype(BF16))
            sl = slice(hp * LANES, (hp + 1) * LANES)
            acc_scr[:, sl] = acc_scr[:, sl] + _dot(jnp.concatenate(ps, axis=1), vstack)

    @pl.when(ki == nk - 1)
    def _():
        lo_q = lax.broadcasted_iota(jnp.int32, (tq, LANES), 1) < HALF
        mins = []
        for hp in range(N_HEADS // 2):
            ls = [jnp.sum(l_scr[2 * hp + par], axis=1, keepdims=True) for par in range(2)]
            sl = slice(hp * LANES, (hp + 1) * LANES)
            o_ref[:, sl] = (acc_scr[:, sl] / jnp.where(lo_q, ls[0], ls[1])).astype(BF16)
            mins += [jnp.broadcast_to(jnp.min(l, axis=0, keepdims=True), (1, LANES)) for l in ls]
        lmin_ref[...] = jnp.concatenate(mins, axis=0)


FLASH_MIN_DENOM = 1e-30


def flash_attn_bounded(q, kd, vd, bound, nb, n, m):
    tq = min(2048, n)
    tk = _pick_tile(m, (768, 512, 256))
    nq, nk = n // tq, m // tk
    hq = N_HEADS * HEAD_DIM
    o, lmin = pl.pallas_call(
        functools.partial(_flash_bounded_kernel, nk=nk),
        grid=(nb, nq, nk),
        in_specs=[pl.BlockSpec(memory_space=pltpu.SMEM),
                  pl.BlockSpec((tq, hq), lambda b, i, k: (b * nq + i, 0)),
                  pl.BlockSpec((tk, 2 * LANES), lambda b, i, k: (b * nk + k, 0)),
                  pl.BlockSpec((tk, 2 * LANES), lambda b, i, k: (b * nk + k, 0))],
        out_specs=[pl.BlockSpec((tq, hq), lambda b, i, k: (b * nq + i, 0)),
                   pl.BlockSpec((N_HEADS, LANES), lambda b, i, k: (b * nq + i, 0))],
        out_shape=[jax.ShapeDtypeStruct((nb * n, hq), BF16),
                   jax.ShapeDtypeStruct((nb * nq * N_HEADS, LANES), F32)],
        scratch_shapes=[pltpu.VMEM((N_HEADS, tq, LANES), F32),
                        pltpu.VMEM((tq, hq), F32)],
        compiler_params=_params(("parallel", "parallel", "arbitrary")),
        name="flash_attn_bounded",
    )(bound, q, kd, vd)
    ok = jnp.min(lmin) > FLASH_MIN_DENOM
    return lax.cond(ok, lambda: o,
                    lambda: flash_attn(q, kd, vd, jnp.zeros((N_HEADS,), F32), nb, n, m, has_sink=False))


def _pick_tile(m, cands):
    for c in cands:
        if m % c == 0:
            return c
    raise ValueError(f"no tile for {m}")


def flash_attn(q, kd, vd, sink, nb, n, m, has_sink):
    tq = min(512, n)
    tk = _pick_tile(m, (768, 512, 256))
    nq, nk = n // tq, m // tk
    hq = N_HEADS * HEAD_DIM
    return pl.pallas_call(
        functools.partial(_flash_kernel, has_sink=has_sink, nk=nk),
        grid=(nb, nq, nk),
        in_specs=[pl.BlockSpec(memory_space=pltpu.SMEM),
                  pl.BlockSpec((tq, hq), lambda b, i, k: (b * nq + i, 0)),
                  pl.BlockSpec((tk, 2 * LANES), lambda b, i, k: (b * nk + k, 0)),
                  pl.BlockSpec((tk, 2 * LANES), lambda b, i, k: (b * nk + k, 0))],
        out_specs=pl.BlockSpec((tq, hq), lambda b, i, k: (b * nq + i, 0)),
        out_shape=jax.ShapeDtypeStruct((nb * n, hq), BF16),
        scratch_shapes=[pltpu.VMEM((N_HEADS, tq, LANES), F32),
                        pltpu.VMEM((N_HEADS, tq, LANES), F32),
                        pltpu.VMEM((tq, hq), F32)],
        compiler_params=_params(("parallel", "parallel", "arbitrary")),
        name="flash_attn",
    )(sink, q, kd, vd)


def _window_kernel(sink_ref, q_ref, kp_ref, km_ref, kn_ref, vp_ref, vm_ref, vn_ref, kc_ref, vc_ref,
                   o_ref, *, n, tq):
    i = pl.program_id(1)
    span = tq + 2 * WINDOW
    q0 = i * tq
    r = lax.broadcasted_iota(jnp.int32, (tq, span), 0)
    c = lax.broadcasted_iota(jnp.int32, (tq, span), 1)
    kpos = c + (q0 - WINDOW)
    ok = (c >= r) & (c <= r + 2 * WINDOW) & (kpos >= 0) & (kpos < n)
    lane_q = lax.broadcasted_iota(jnp.int32, (tq, LANES), 1)
    lo_q = lane_q < HALF
    k_all = jnp.concatenate([kp_ref[...], km_ref[...], kn_ref[...], kc_ref[...]], axis=0)
    v_all = jnp.concatenate([vp_ref[...], vm_ref[...], vn_ref[...], vc_ref[...]], axis=0)
    bias = jnp.concatenate([jnp.where(ok, 0.0, NEG_BIG), jnp.zeros((tq, kc_ref.shape[0]), F32)], axis=1)
    pairs_per_kv = N_HEADS // N_KV // 2
    for j in range(N_KV):
        js = slice(j * LANES, (j + 1) * LANES)
        kmats, vstack = _pair_operands(k_all[:, js], v_all[:, js])
        for pp in range(pairs_per_kv):
            hp = j * pairs_per_kv + pp
            qp = q_ref[:, hp * LANES:(hp + 1) * LANES]
            ps, ls = [], []
            for par in range(2):
                h = 2 * hp + par
                s = _dot_nt(qp, kmats[par]) + bias
                snk = sink_ref[h]
                m = jnp.maximum(jnp.max(s, axis=1, keepdims=True), snk)
                p = jnp.exp(s - m)
                ls.append(jnp.sum(p, axis=1, keepdims=True) + jnp.exp(snk - m))
                ps.append(p.astype(BF16))
            o = _dot(jnp.concatenate(ps, axis=1), vstack)
            o_ref[:, hp * LANES:(hp + 1) * LANES] = (o / jnp.where(lo_q, ls[0], ls[1])).astype(BF16)


def window_attn(q, kd, vd, kdc, vdc, sink, nb, n, mc):
    tq = 2 * WINDOW
    nq = n // tq
    wb = n // WINDOW
    hq = N_HEADS * HEAD_DIM
    prev = pl.BlockSpec((WINDOW, 2 * LANES), lambda b, i: (b * wb + jnp.maximum(2 * i - 1, 0), 0))
    main = pl.BlockSpec((tq, 2 * LANES), lambda b, i: (b * nq + i, 0))
    nxt = pl.BlockSpec((WINDOW, 2 * LANES), lambda b, i: (b * wb + jnp.minimum(2 * i + 2, wb - 1), 0))
    ctx = pl.BlockSpec((mc, 2 * LANES), lambda b, i: (b, 0))
    return pl.pallas_call(
        functools.partial(_window_kernel, n=n, tq=tq),
        grid=(nb, nq),
        in_specs=[pl.BlockSpec(memory_space=pltpu.SMEM),
                  pl.BlockSpec((tq, hq), lambda b, i: (b * nq + i, 0)),
                  prev, main, nxt, prev, main, nxt, ctx, ctx],
        out_specs=pl.BlockSpec((tq, hq), lambda b, i: (b * nq + i, 0)),
        out_shape=jax.ShapeDtypeStruct((nb * n, hq), BF16),
        compiler_params=_params(("parallel", "parallel")),
        name="window_attn",
    )(sink, q, kd, kd, kd, vd, vd, vd, kdc, vdc)


HALO = 16


def _conv_kernel(xm_ref, xp_ref, xn_ref, bm_ref, bp_ref, bn_ref, wx_ref, bx_ref, wb_ref, bb_ref,
                 ox_ref, ob_ref, *, nt):
    i = pl.program_id(1)
    has_prev = jnp.where(i > 0, 1.0, 0.0)
    has_next = jnp.where(i < nt - 1, 1.0, 0.0)

    def conv(m_ref, p_ref, n_ref, w_ref, b_ref, o_ref):
        x = m_ref[...].astype(F32)
        tl = x.shape[0]
        row = lax.broadcasted_iota(jnp.int32, x.shape, 0)
        before = p_ref[...].astype(F32)[HALO - 1:HALO, :] * has_prev
        after = n_ref[...].astype(F32)[0:1, :] * has_next
        xm1 = jnp.where(row == 0, before, pltpu.roll(x, 1, 0))
        xp1 = jnp.where(row == tl - 1, after, pltpu.roll(x, tl - 1, 0))
        w = w_ref[...]
        y = xm1 * w[0:1, :] + x * w[1:2, :] + xp1 * w[2:3, :] + b_ref[...]
        o_ref[...] = _silu(y).astype(BF16)

    conv(xm_ref, xp_ref, xn_ref, wx_ref, bx_ref, ox_ref)
    conv(bm_ref, bp_ref, bn_ref, wb_ref, bb_ref, ob_ref)


def ssm_conv(p, conv_w, conv_b, nb, n):
    tl = min(512, n)
    nt = n // tl
    hb = n // HALO
    hpt = tl // HALO
    cx, cb = SSM_INNER, 2 * SSM_GROUPS * SSM_STATE

    def specs(width, col):
        cblk = col // width
        return (pl.BlockSpec((tl, width), lambda b, i: (b * nt + i, cblk)),
                pl.BlockSpec((HALO, width), lambda b, i: (b * hb + jnp.maximum(i * hpt - 1, 0), cblk)),
                pl.BlockSpec((HALO, width), lambda b, i: (b * hb + jnp.minimum((i + 1) * hpt, hb - 1), cblk)))

    const = lambda shp: pl.BlockSpec(shp, lambda b, i: (0, 0))
    xm, xp, xn = specs(cx, C_XS)
    bm, bp, bn = specs(cb, C_BC)
    return pl.pallas_call(
        functools.partial(_conv_kernel, nt=nt),
        grid=(nb, nt),
        in_specs=[xm, xp, xn, bm, bp, bn, const((3, cx)), const((1, cx)), const((3, cb)), const((1, cb))],
        out_specs=[pl.BlockSpec((tl, cx), lambda b, i: (b * nt + i, 0)),
                   pl.BlockSpec((tl, cb), lambda b, i: (b * nt + i, 0))],
        out_shape=[jax.ShapeDtypeStruct((nb * n, cx), BF16), jax.ShapeDtypeStruct((nb * n, cb), BF16)],
        compiler_params=_params(("parallel", "parallel")),
        name="ssm_conv",
    )(p, p, p, p, p, p, conv_w[:, :cx], conv_b[:cx].reshape(1, cx), conv_w[:, cx:], conv_b[cx:].reshape(1, cb))


def _ssd_chunk(d, r0, xs_ref, bc_ref, dt_ref, bias, aneg, tri, rep, st_scr, y_ref):
    q = SSM_CHUNK
    rows = slice(r0, r0 + q)
    gw = SSM_INNER // SSM_GROUPS
    dt = _softplus(dt_ref[rows, :] + bias)
    a = dt * aneg
    ac = _exact_left(tri, a, pieces=2)
    act = ac.T
    acx = _exact_right(ac, rep, pieces=2)
    dtx = _dot(dt.astype(BF16), rep)
    totx = acx[q - 1:q, :] if d == 0 else acx[0:1, :]
    xd = xs_ref[rows, :].astype(F32) * dtx
    xd_b = xd.astype(BF16)
    xe = (xd * jnp.exp(totx - acx)).astype(BF16)
    ein = jnp.exp(acx)
    keep = tri > 0
    lane = lax.broadcasted_iota(jnp.int32, (q, LANES), 1)
    lo = lane < HALF
    zero = jnp.zeros((q, LANES), BF16)
    hpg = SSM_HEADS // SSM_GROUPS
    for g in range(SSM_GROUPS):
        bg = bc_ref[rows, g * SSM_STATE:(g + 1) * SSM_STATE]
        cg = bc_ref[rows, (SSM_GROUPS + g) * SSM_STATE:(SSM_GROUPS + g + 1) * SSM_STATE]
        cb = _dot_nt(cg, bg)
        st = st_scr[d, g]
        yoff = _dot(cg, st.astype(BF16)) * ein[:, g * gw:(g + 1) * gw]
        for hp in range(hpg // 2):
            gs = []
            for par in range(2):
                h = g * hpg + 2 * hp + par
                seg = ac[:, h:h + 1] - act[h:h + 1, :]
                gs.append((cb * jnp.exp(jnp.where(keep, seg, NEG_BIG))).astype(BF16))
            c0 = g * gw + hp * LANES
            xp = xd_b[:, c0:c0 + LANES]
            xstack = jnp.concatenate([jnp.where(lo, xp, zero), jnp.where(lo, zero, xp)], axis=0)
            ydiag = _dot(jnp.concatenate(gs, axis=1), xstack)
            y_ref[rows, c0:c0 + LANES] = (ydiag + yoff[:, hp * LANES:(hp + 1) * LANES]).astype(BF16)
        bgt = bg.astype(F32).T.astype(BF16)
        cs = _dot(bgt, xe[:, g * gw:(g + 1) * gw])
        st_scr[d, g] = st * jnp.exp(totx[:, g * gw:(g + 1) * gw]) + cs


def _ssd_kernel(xf_ref, bcf_ref, dtf_ref, xb_ref, bcb_ref, dtb_ref, bias_ref, aneg_ref, tri_ref, rep_ref,
                s0_ref, yf_ref, yb_ref, sfin_ref, st_scr, *, nsteps):
    k = pl.program_id(1)

    @pl.when(k == 0)
    def _():
        st_scr[...] = s0_ref[...]

    rep = rep_ref[...]
    for c in range(SSD_CHUNKS_PER_STEP):
        _ssd_chunk(0, c * SSM_CHUNK, xf_ref, bcf_ref, dtf_ref, bias_ref[0], aneg_ref[0], tri_ref[0], rep,
                   st_scr, yf_ref)
        _ssd_chunk(1, (SSD_CHUNKS_PER_STEP - 1 - c) * SSM_CHUNK, xb_ref, bcb_ref, dtb_ref, bias_ref[1],
                   aneg_ref[1], tri_ref[1], rep, st_scr, yb_ref)

    @pl.when(k == nsteps - 1)
    def _():
        sfin_ref[...] = st_scr[...]


SSD_CHUNKS_PER_STEP = 2


def ssd_scan(u_xs, u_bc, dt_raw, bias2, aneg2, s0, nb, n):
    q = SSM_CHUNK
    nc = n // q
    gw = SSM_INNER // SSM_GROUPS
    idx = np.arange(q)
    tri = np.stack([idx[:, None] >= idx[None, :], idx[:, None] <= idx[None, :]]).astype(np.float32)
    rep = (np.arange(LANES)[:, None] == np.arange(SSM_INNER)[None, :] // SSM_P).astype(np.float32)

    ns = nc // SSD_CHUNKS_PER_STEP
    rb = q * SSD_CHUNKS_PER_STEP
    fwd = lambda b, k: b * ns + k
    bwd = lambda b, k: b * ns + (ns - 1 - k)
    cbc = 2 * SSM_GROUPS * SSM_STATE
    state = pl.BlockSpec((None, 2, SSM_GROUPS, SSM_STATE, gw), lambda b, k: (b, 0, 0, 0, 0))
    const = lambda shp: pl.BlockSpec(shp, lambda b, k: (0,) * len(shp))
    yf, yb, sfin = pl.pallas_call(
        functools.partial(_ssd_kernel, nsteps=ns),
        grid=(nb, ns),
        in_specs=[pl.BlockSpec((rb, SSM_INNER), lambda b, k: (fwd(b, k), 0)),
                  pl.BlockSpec((rb, cbc), lambda b, k: (fwd(b, k), 0)),
                  pl.BlockSpec((rb, LANES), lambda b, k: (fwd(b, k), 0)),
                  pl.BlockSpec((rb, SSM_INNER), lambda b, k: (bwd(b, k), 0)),
                  pl.BlockSpec((rb, cbc), lambda b, k: (bwd(b, k), 0)),
                  pl.BlockSpec((rb, LANES), lambda b, k: (bwd(b, k), 1)),
                  const((2, 1, LANES)), const((2, 1, LANES)), const((2, q, q)), const((LANES, SSM_INNER)),
                  state],
        out_specs=[pl.BlockSpec((rb, SSM_INNER), lambda b, k: (fwd(b, k), 0)),
                   pl.BlockSpec((rb, SSM_INNER), lambda b, k: (bwd(b, k), 0)),
                   state],
        out_shape=[jax.ShapeDtypeStruct((nb * n, SSM_INNER), BF16),
                   jax.ShapeDtypeStruct((nb * n, SSM_INNER), BF16),
                   jax.ShapeDtypeStruct((nb, 2, SSM_GROUPS, SSM_STATE, gw), F32)],
        scratch_shapes=[pltpu.VMEM((2, SSM_GROUPS, SSM_STATE, gw), F32)],
        compiler_params=_params(("parallel", "arbitrary")),
        name="ssd_scan",
    )(u_xs, u_bc, dt_raw, u_xs, u_bc, dt_raw, bias2, aneg2, jnp.asarray(tri, BF16), jnp.asarray(rep, BF16), s0)
    return (yf, yb), sfin


def _post_kernel(oa_ref, ow_ref, yf_ref, yb_ref, xs_ref, z_ref, gt_ref, x_ref,
                 dsk_ref, nw_ref, wa_ref, ww_ref, ws_ref, wo_ref, gpost_ref, gate_ref,
                 gpre_ref, sh_ref, sc_ref, wrh_ref, wrl_ref,
                 xo_ref, h_ref, lg_ref):
    y = yf_ref[...].astype(F32) + yb_ref[...].astype(F32) + dsk_ref[...] * xs_ref[...].astype(F32)
    u = y * _silu(z_ref[...].astype(F32))
    gw = SSM_INNER // SSM_GROUPS
    ys = jnp.concatenate([_rms(u[:, g * gw:(g + 1) * gw]) for g in range(SSM_GROUPS)], axis=1)
    ys = (ys * nw_ref[...]).astype(BF16)
    d = x_ref.shape[1]
    ga = jax.nn.sigmoid(gt_ref[:, 0:d].astype(F32))
    gw_ = jax.nn.sigmoid(gt_ref[:, d:2 * d].astype(F32))
    gs = jax.nn.sigmoid(gt_ref[:, 2 * d:3 * d].astype(F32))
    m = ga * _dot(oa_ref[...], wa_ref[...]) + gw_ * _dot(ow_ref[...], ww_ref[...]) + gs * _dot(ys, ws_ref[...])
    ml = _dot(m.astype(BF16), wo_ref[...])
    xn = x_ref[...] + gate_ref[...] * (_rms(ml) * gpost_ref[...])
    xo_ref[...] = xn
    h = (_rms(xn) * gpre_ref[...]) * (1.0 + sc_ref[...]) + sh_ref[...]
    _store_rows_as_slabs(h_ref, h)
    hb = h.astype(BF16)
    hl =(h - hb.astype(F32)).astype(BF16)
    lg_ref[...] = _dot_nt(wrh_ref[...], hb) + _dot_nt(wrh_ref[...], hl) + _dot_nt(wrl_ref[...], hb)


def post_mixer(oa, ow, y2, u_xs, p, x, dskip, norm_w, wa, ww, ws, wo, gpost, gate, gpre, shift, scale,
               wr_hi, wr_lo, rows_per_batch):
    t, d = x.shape
    tm = min(256, rows_per_batch)
    tpb = rows_per_batch // tm
    nt = t // tm
    hq = N_HEADS * HEAD_DIM
    row = lambda w, c=0: pl.BlockSpec((tm, w), lambda i: (i, c // w))
    const = lambda shp: pl.BlockSpec(shp, lambda i: (0,) * len(shp))
    mod = pl.BlockSpec((None, 1, d), lambda i: (i // tpb, 0, 0))
    return pl.pallas_call(
        _post_kernel,
        grid=(nt,),
        in_specs=[row(hq), row(hq), row(SSM_INNER), row(SSM_INNER),
                  row(SSM_INNER), row(SSM_INNER, C_Z), row(3 * d, C_GATES), row(d),
                  const((1, SSM_INNER)), const((1, SSM_INNER)),
                  const((hq, d)), const((hq, d)), const((SSM_INNER, d)), const((d, d)),
                  const((1, d)), mod, const((1, d)), mod, mod,
                  const((N_EXPERTS, d)), const((N_EXPERTS, d))],
        out_specs=[row(d), pl.BlockSpec((tm * SLAB, LANES), lambda i: (i, 0)),
                   pl.BlockSpec((N_EXPERTS, tm), lambda i: (0, i))],
        out_shape=[jax.ShapeDtypeStruct((t, d), F32), jax.ShapeDtypeStruct((t * SLAB, LANES), U32),
                   jax.ShapeDtypeStruct((N_EXPERTS, t), F32)],
        compiler_params=_params(("parallel",)),
        name="post_mixer",
    )(oa, ow, y2[0], y2[1], u_xs, p, p, x, dskip, norm_w, wa, ww, ws, wo, gpost, gate, gpre, shift, scale,
      wr_hi, wr_lo)


def _route_kernel(lg_ref, b_ref, ei_ref, w_ref):
    scores = jax.nn.sigmoid(lg_ref[...])
    sel = scores + b_ref[...]
    tt = sel.shape[1]
    per = N_EXPERTS // N_EXPERT_GROUPS
    r8 = lax.broadcasted_iota(jnp.int32, (per, tt), 0).astype(F32)
    ninf = -jnp.inf

    def argmax_rows(x, rows, nrows):
        m = jnp.max(x, axis=0, keepdims=True)
        idx = jnp.min(jnp.where(x == m, rows, float(nrows)), axis=0, keepdims=True)
        return m, idx

    gscores = []
    for g in range(N_EXPERT_GROUPS):
        blk = sel[g * per:(g + 1) * per, :]
        m1, i1 = argmax_rows(blk, r8, per)
        m2 = jnp.max(jnp.where(r8 == i1, ninf, blk), axis=0, keepdims=True)
        gscores.append(m1 + m2)
    cur = jnp.concatenate(gscores, axis=0)
    rg = lax.broadcasted_iota(jnp.int32, cur.shape, 0).astype(F32)
    chosen = jnp.zeros(cur.shape, F32)
    for _ in range(TOPK_GROUPS):
        _, gi = argmax_rows(cur, rg, N_EXPERT_GROUPS)
        hit = rg == gi
        chosen = jnp.where(hit, 1.0, chosen)
        cur = jnp.where(hit, ninf, cur)
    gmask = jnp.concatenate([jnp.broadcast_to(chosen[g:g + 1, :], (per, tt)) for g in range(N_EXPERT_GROUPS)],
                            axis=0)
    cur = jnp.where(gmask > 0, sel, ninf)
    re = lax.broadcasted_iota(jnp.int32, cur.shape, 0).astype(F32)
    idxs, ws = [], []
    for _ in range(TOP_K):
        _, ei = argmax_rows(cur, re, N_EXPERTS)
        hit = re == ei
        ws.append(jnp.sum(jnp.where(hit, scores, 0.0), axis=0, keepdims=True))
        idxs.append(ei)
        cur = jnp.where(hit, ninf, cur)
    w = jnp.concatenate(ws, axis=0)
    w_ref[...] = w / jnp.sum(w, axis=0, keepdims=True) * ROUTED_SCALE
    ei_ref[...] = jnp.concatenate(idxs, axis=0).astype(jnp.int32)


def route(logits_t, b_router):
    e, t = logits_t.shape
    tt = 512
    return pl.pallas_call(
        _route_kernel,
        grid=(t // tt,),
        in_specs=[pl.BlockSpec((e, tt), lambda i: (0, i)), pl.BlockSpec((e, 1), lambda i: (0, 0))],
        out_specs=[pl.BlockSpec((TOP_K, tt), lambda i: (0, i)), pl.BlockSpec((TOP_K, tt), lambda i: (0, i))],
        out_shape=[jax.ShapeDtypeStruct((TOP_K, t), jnp.int32), jax.ShapeDtypeStruct((TOP_K, t), F32)],
        compiler_params=_params(("parallel",)),
        name="route",
    )(logits_t, b_router.reshape(e, 1))


MOE_ROWS = 512
PLAN_TOKENS = 512


def _moe_geometry(t):
    nblk = -(-(t * TOP_K + N_EXPERTS * (MOE_ROWS - 1)) // MOE_ROWS)
    return nblk, nblk * MOE_ROWS


def _plan_kernel(ei_ref, ut_ref, tril_ref, dest_ref, tab_ref, be_ref, cnt_scr, run_scr):
    ph = pl.program_id(0)
    i = pl.program_id(1)
    ei = ei_ref[...]
    tt = ei.shape[1]
    re = lax.broadcasted_iota(jnp.int32, (N_EXPERTS, tt), 0)
    hits = [re == ei[k:k + 1, :] for k in range(TOP_K)]
    oh = jnp.zeros((N_EXPERTS, tt), F32)
    for k in range(TOP_K):
        oh = oh + jnp.where(hits[k], 1.0, 0.0)

    @pl.when((ph == 0) & (i == 0))
    def _():
        cnt_scr[...] = jnp.zeros(cnt_scr.shape, F32)

    @pl.when(ph == 0)
    def _():
        cnt_scr[...] = cnt_scr[...] + jnp.sum(oh, axis=1, keepdims=True)

    @pl.when((ph == 1) & (i == 0))
    def _():
        cnt = cnt_scr[...]
        padded = ((cnt.astype(jnp.int32) + (MOE_ROWS - 1)) & (-MOE_ROWS)).astype(F32)
        pad_end = _exact_left(tril_ref[...], padded)
        pad_start = pad_end - padded
        run_scr[...] = pad_start
        tab_ref[0] = pad_start
        tab_ref[1] = pad_end
        nbp = be_ref.shape[1]
        blk0 = (lax.broadcasted_iota(jnp.int32, (N_EXPERTS, nbp), 1) * MOE_ROWS).astype(F32)
        be = jnp.sum(jnp.where(pad_end[:, :1] <= blk0, 1.0, 0.0), axis=0, keepdims=True)
        be_ref[...] = jnp.broadcast_to(jnp.minimum(be, N_EXPERTS - 1.0), be_ref.shape).astype(jnp.int32)

    @pl.when(ph == 1)
    def _():
        cin = _dot(oh.astype(BF16), ut_ref[...])
        pos = run_scr[:, :1] + (cin - oh)
        rows = [jnp.sum(jnp.where(hits[k], pos, 0.0), axis=0, keepdims=True) for k in range(TOP_K)]
        dest_ref[...] = jnp.concatenate(rows, axis=0).astype(jnp.int32)
        run_scr[...] = run_scr[...] + cin[:, tt - 1:tt]


def moe_plan(eidx_t):
    k, t = eidx_t.shape
    tt = PLAN_TOKENS
    nt = t // tt
    nblk, _ = _moe_geometry(t)
    nbp = -(-nblk // LANES) * LANES
    ut = jnp.asarray(np.arange(tt)[:, None] <= np.arange(tt)[None, :], BF16)
    tril = jnp.asarray(np.arange(N_EXPERTS)[:, None] >= np.arange(N_EXPERTS)[None, :], BF16)
    return pl.pallas_call(
        _plan_kernel,
        grid=(2, nt),
        in_specs=[pl.BlockSpec((k, tt), lambda p, i: (0, i)),
                  pl.BlockSpec((tt, tt), lambda p, i: (0, 0)),
                  pl.BlockSpec((N_EXPERTS, N_EXPERTS), lambda p, i: (0, 0))],
        out_specs=[pl.BlockSpec((k, tt), lambda p, i: (0, i * p)),
                   pl.BlockSpec((2, N_EXPERTS, LANES), lambda p, i: (0, 0, 0)),
                   pl.BlockSpec((8, nbp), lambda p, i: (0, 0))],
        out_shape=[jax.ShapeDtypeStruct((k, t), jnp.int32),
                   jax.ShapeDtypeStruct((2, N_EXPERTS, LANES), F32),
                   jax.ShapeDtypeStruct((8, nbp), jnp.int32)],
        scratch_shapes=[pltpu.VMEM((N_EXPERTS, LANES), F32), pltpu.VMEM((N_EXPERTS, LANES), F32)],
        compiler_params=_params(("arbitrary", "arbitrary")),
        name="moe_plan",
    )(eidx_t, ut, tril)


def _dispatch_kernel(dest_ref, pstart_ref, pend_ref, h_ref, xs_hbm, zero_scr, sem):
    i = pl.program_id(0)
    tt = h_ref.shape[0] // SLAB

    def slab(ref, row, n=1):
        return ref.at[pl.ds(pl.multiple_of(row * SLAB, SLAB), n * SLAB), :]

    @pl.when(i == 0)
    def _():
        zero_scr[...] = jnp.zeros(zero_scr.shape, U32)
        nblk = xs_hbm.shape[0] // (MOE_ROWS * SLAB)
        n_used = pend_ref[N_EXPERTS - 1] // MOE_ROWS

        def zero_block(row0):
            return pltpu.make_async_copy(zero_scr, slab(xs_hbm, row0, MOE_ROWS), sem)

        def seg_start(e, c):
            @pl.when(pend_ref[e] > pstart_ref[e])
            def _():
                zero_block(pend_ref[e] - MOE_ROWS).start()
            return c

        def seg_wait(e, c):
            @pl.when(pend_ref[e] > pstart_ref[e])
            def _():
                zero_block(pend_ref[e] - MOE_ROWS).wait()
            return c

        def tail_start(b, c):
            zero_block(b * MOE_ROWS).start()
            return c

        def tail_wait(b, c):
            zero_block(b * MOE_ROWS).wait()
            return c

        lax.fori_loop(0, N_EXPERTS, seg_start, 0)
        lax.fori_loop(n_used, nblk, tail_start, 0)
        lax.fori_loop(0, N_EXPERTS, seg_wait, 0)
        lax.fori_loop(n_used, nblk, tail_wait, 0)

    def issue(t, c):
        for k in range(TOP_K):
            pltpu.make_async_copy(slab(h_ref, t), slab(xs_hbm, dest_ref[k, t]), sem).start(priority=k % 2)
        return c

    lax.fori_loop(0, tt, issue, 0)
    for k in range(TOP_K):
        pltpu.make_async_copy(h_ref, slab(xs_hbm, 0, tt), sem).wait()


def moe_dispatch(dest_t, pad_start, pad_end, h):
    t = h.shape[0] // SLAB
    tt = PLAN_TOKENS
    _, cap = _moe_geometry(t)
    smem = pl.BlockSpec(memory_space=pltpu.SMEM)
    return pl.pallas_call(
        _dispatch_kernel,
        grid=(t // tt,),
        in_specs=[pl.BlockSpec((TOP_K, tt), lambda i: (0, i), memory_space=pltpu.SMEM), smem, smem,
                  pl.BlockSpec((tt * SLAB, LANES), lambda i: (i, 0))],
        out_specs=pl.BlockSpec(memory_space=pl.ANY),
        out_shape=jax.ShapeDtypeStruct((cap * SLAB, LANES), U32),
        scratch_shapes=[pltpu.VMEM((MOE_ROWS * SLAB, LANES), U32), pltpu.SemaphoreType.DMA(())],
        compiler_params=_params(("arbitrary",)),
        name="moe_dispatch",
    )(dest_t, pad_start, pad_end, h)


def _expert_kernel(be_ref, nu_ref, x_ref, wg_ref, wu_ref, wd_ref, o_ref):
    used = pl.program_id(0) < nu_ref[0]

    @pl.when(used)
    def _():
        x = _load_slabs_as_rows(x_ref).astype(BF16)
        hg = _dot(x, wg_ref[...].astype(BF16))
        hu = _dot(x, wu_ref[...].astype(BF16))
        hb = (_silu(hg) * hu).astype(BF16)
        _store_rows_as_slabs(o_ref, _dot(hb, wd_ref[...].astype(BF16)))

    @pl.when(jnp.logical_not(used))
    def _():
        o_ref[...] = jnp.zeros(o_ref.shape, U32)


def expert_ffn(blk_e, n_used, xs, we_gate, we_up, we_down, layer):
    d, de = we_gate.shape[2:]
    nblk = xs.shape[0] // (MOE_ROWS * SLAB)
    blk = lambda i, nu: jnp.minimum(i, nu[0] - 1)
    grid_spec = pltpu.PrefetchScalarGridSpec(
        num_scalar_prefetch=2,
        grid=(nblk,),
        in_specs=[pl.BlockSpec((MOE_ROWS * SLAB, LANES), lambda i, be, nu: (blk(i, nu), 0)),
                  pl.BlockSpec((None, None, d, de), lambda i, be, nu: (layer, be[blk(i, nu)], 0, 0)),
                  pl.BlockSpec((None, None, d, de), lambda i, be, nu: (layer, be[blk(i, nu)], 0, 0)),
                  pl.BlockSpec((None, None, de, d), lambda i, be, nu: (layer, be[blk(i, nu)], 0, 0))],
        out_specs=pl.BlockSpec((MOE_ROWS * SLAB, LANES), lambda i, be, nu: (i, 0)),
    )
    return pl.pallas_call(
        _expert_kernel,
        grid_spec=grid_spec,
        out_shape=jax.ShapeDtypeStruct(xs.shape, U32),
        compiler_params=_params(("arbitrary",)),
        name="expert_ffn",
    )(blk_e, n_used, xs, we_gate, we_up, we_down)


def _ffn_tail_kernel(dest_ref, dnext_ref, w_ref, h_ref, x_ref, wg_ref, wu_ref, wd_ref, gpost_ref, gate_ref, ys_hbm,
                     xo_ref, buf, sem, *, nt):
    i = pl.program_id(0)
    tt = h_ref.shape[0] // SLAB
    cur = i % 2

    def slab(ref, row, n=1):
        return ref.at[pl.ds(pl.multiple_of(row * SLAB, SLAB), n * SLAB), :]

    def gather(idx_ref, slot):
        def issue(t, c):
            for k in range(TOP_K):
                pltpu.make_async_copy(slab(ys_hbm, idx_ref[k, t]), slab(buf.at[slot, k], t),
                                      sem.at[slot]).start(priority=k % 2)
            return c
        lax.fori_loop(0, tt, issue, 0)

    @pl.when(i == 0)
    def _():
        gather(dest_ref, 0)

    @pl.when(i + 1 < nt)
    def _():
        gather(dnext_ref, 1 - cur)

    h = _load_slabs_as_rows(h_ref).astype(BF16)
    hs = (_silu(_dot(h, wg_ref[...])) * _dot(h, wu_ref[...])).astype(BF16)
    f = _dot(hs, wd_ref[...])
    w = w_ref[...]
    for k in range(TOP_K):
        pltpu.make_async_copy(slab(ys_hbm, 0, tt), buf.at[cur, k], sem.at[cur]).wait()
    for k in range(TOP_K):
        f = f + _load_slabs_as_rows(buf.at[cur, k]) * w[:, k:k + 1]
    xo_ref[...] = x_ref[...] + gate_ref[...] * (_rms(f) * gpost_ref[...])


def ffn_tail(dest_t, wts, h, x, ys, wsg, wsu, wsd, gpost, gate, rows_per_batch):
    t, d = x.shape
    ds = wsg.shape[1]
    tm = min(256, rows_per_batch)
    tpb = rows_per_batch // tm
    nt = t // tm
    row = pl.BlockSpec((tm, d), lambda i: (i, 0))
    const = lambda shp: pl.BlockSpec(shp, lambda i: (0,) * len(shp))
    return pl.pallas_call(
        functools.partial(_ffn_tail_kernel, nt=nt),
        grid=(nt,),
        in_specs=[pl.BlockSpec((TOP_K, tm), lambda i: (0, i), memory_space=pltpu.SMEM),
                  pl.BlockSpec((TOP_K, tm), lambda i: (0, jnp.minimum(i + 1, nt - 1)), memory_space=pltpu.SMEM),
                  pl.BlockSpec((tm, TOP_K), lambda i: (i, 0)),
                  pl.BlockSpec((tm * SLAB, LANES), lambda i: (i, 0)),
                  row, const((d, ds)), const((d, ds)), const((ds, d)), const((1, d)),
                  pl.BlockSpec((None, 1, d), lambda i: (i // tpb, 0, 0)),
                  pl.BlockSpec(memory_space=pl.ANY)],
        out_specs=row,
        out_shape=jax.ShapeDtypeStruct((t, d), F32),
        scratch_shapes=[pltpu.VMEM((2, TOP_K, tm * SLAB, LANES), U32), pltpu.SemaphoreType.DMA((2,))],
        compiler_params=_params(("arbitrary",)),
        name="ffn_tail",
    )(dest_t, dest_t, wts, h, x, wsg, wsu, wsd, gpost, gate, ys)


def _moe_routed(h, logits_t, b_router, we_gate, we_up, we_down, layer):
    eidx_t, wts_t = route(logits_t, b_router)
    dest_t, tabs, blk_e = moe_plan(eidx_t)
    pad_start = tabs[0, :, 0].astype(jnp.int32)
    pad_end = tabs[1, :, 0].astype(jnp.int32)
    n_used = (pad_end[N_EXPERTS - 1:] // MOE_ROWS).astype(jnp.int32)
    xs = moe_dispatch(dest_t, pad_start, pad_end, h)
    ys = expert_ffn(blk_e[0], n_used, xs, we_gate, we_up, we_down, layer)
    return dest_t, wts_t.T, ys


def _reorder_w_in(w):
    qa, ka, va, qw, kw, vw, z, xs, bm, cm, dt, gates = jnp.split(
        w, [512, 640, 768, 1280, 1408, 1536, 2560, 3584, 3840, 4096, 4128], axis=1)
    w_main = jnp.concatenate([qa, qw, z, xs, gates, ka, va, kw, vw, bm, cm], axis=1).astype(BF16)
    pad = jnp.zeros((w.shape[0], LANES - SSM_HEADS), w.dtype)
    w_dt = jnp.concatenate([dt[:, :SSM_HEADS], pad, dt[:, SSM_HEADS:], pad], axis=1).astype(BF16)
    return w_main, w_dt


def _pad_lanes(v):
    return jnp.pad(v, ((0, 0), (0, LANES - v.shape[1])))


def kernel(x, c, ctx, c_ctx, w_ada, b_ada, g_mix_pre, g_mix_post, g_ffn_pre, g_ffn_post, w_in, g_q_a, g_k_a, sink_w, ssm_conv_w, ssm_conv_b, ssm_dt_bias, ssm_a_log, ssm_d, ssm_norm, w_br_a, w_br_w, w_br_s, w_out, w_router, b_router, we_gate, we_up, we_down, ws_gate, ws_up, ws_down):
    nb, n, d = x.shape
    mc = ctx.shape[1]
    depth = w_in.shape[0]
    t_lat, t_ctx = nb * n, nb * mc
    cos, sin = rope_tables(n)
    xl = x.reshape(t_lat, d)
    xc = ctx.reshape(t_ctx, d)
    c8 = jnp.concatenate([c, c_ctx[None, :], jnp.zeros((8 - nb - 1, d), F32)], axis=0)
    zeros_sink = jnp.zeros((N_HEADS,), F32)
    s_zero = jnp.zeros((nb, 2, SSM_GROUPS, SSM_STATE, SSM_INNER // SSM_GROUPS), F32)
    dummy_tab = jnp.zeros((mc, N_HEADS * HEAD_DIM), F32)

    for i in range(depth):
        last = i == depth - 1
        mod = ada_mod(c8, w_ada, b_ada[i], i)
        mod_l = [mod[:nb, k * d:(k + 1) * d].reshape(nb, 1, d) for k in range(6)]
        mod_c = [mod[nb:nb + 1, k * d:(k + 1) * d].reshape(1, 1, d) for k in range(6)]
        w_main, w_dt = _reorder_w_in(w_in[i])
        gq = jnp.tile(g_q_a[i], N_HEADS)[None, :]
        gk = jnp.tile(g_k_a[i], N_KV)[None, :]
        bias2 = _pad_lanes(ssm_dt_bias[i].reshape(2, SSM_HEADS)).reshape(2, 1, LANES)
        aneg2 = _pad_lanes(-jnp.exp(ssm_a_log[i].astype(F32))).reshape(2, 1, LANES)
        dskip = jnp.repeat(ssm_d[i], SSM_P)[None, :]
        norm_w = ssm_norm[i][None, :]
        wa, ww, ws, wo = (w_br_a[i].astype(BF16), w_br_w[i].astype(BF16), w_br_s[i].astype(BF16),
                          w_out[i].astype(BF16))
        wr_t = w_router[i].T
        wr_hi = wr_t.astype(BF16)
        wr_lo = (wr_t - wr_hi.astype(F32)).astype(BF16)
        sink = sink_w[i].astype(F32)

        p_c, dt_c = in_proj(xc, g_mix_pre[i], mod_c[0], mod_c[1], w_main, w_dt, t_ctx)
        qa_c, qw_c, kda_c, vda_c, kdw_c, vdw_c = attn_prep(p_c, dummy_tab, dummy_tab, gq, gk, nb, mc, rope=False)
        uxs_c, ubc_c = ssm_conv(p_c, ssm_conv_w[i], ssm_conv_b[i], nb, mc)
        y_c, s_fin = ssd_scan(uxs_c, ubc_c, dt_c, bias2, aneg2, s_zero, nb, mc)

        p_l, dt_l = in_proj(xl, g_mix_pre[i], mod_l[0], mod_l[1], w_main, w_dt, n)
        qa, qw, kda, vda, kdw, vdw = attn_prep(p_l, cos, sin, gq, gk, nb, n, rope=True)
        m_all = n + mc
        kd_all = jnp.concatenate([kda.reshape(nb, n, -1), kda_c.reshape(nb, mc, -1)], axis=1).reshape(nb * m_all, -1)
        vd_all = jnp.concatenate([vda.reshape(nb, n, -1), vda_c.reshape(nb, mc, -1)], axis=1).reshape(nb * m_all, -1)
        score_bound = (math.sqrt(HEAD_DIM) * jnp.max(jnp.abs(g_q_a[i])) * jnp.max(jnp.abs(g_k_a[i]))).reshape(1)
        oa = flash_attn_bounded(qa, kd_all, vd_all, score_bound.astype(F32), nb, n, m_all)
        ow = window_attn(qw, kdw, vdw, kdw_c, vdw_c, sink, nb, n, mc)
        uxs, ubc = ssm_conv(p_l, ssm_conv_w[i], ssm_conv_b[i], nb, n)
        y_l, _ = ssd_scan(uxs, ubc, dt_l, bias2, aneg2, s_fin, nb, n)
        xl, h_l, lg_l = post_mixer(oa, ow, y_l, uxs, p_l, xl, dskip, norm_w, wa, ww, ws, wo,
                                   g_mix_post[i][None, :], mod_l[2], g_ffn_pre[i][None, :], mod_l[3], mod_l[4],
                                   wr_hi, wr_lo, n)
        wsg, wsu, wsd = ws_gate[i].astype(BF16), ws_up[i].astype(BF16), ws_down[i].astype(BF16)
        moe_w = (we_gate, we_up, we_down, i)
        if last:
            dest_t, wts, ys = _moe_routed(h_l, lg_l, b_router[i], *moe_w)
            xl = ffn_tail(dest_t, wts, h_l, xl, ys, wsg, wsu, wsd, g_ffn_post[i][None, :], mod_l[5], n)
        else:
            oa_c = flash_attn(qa_c, kda_c, vda_c, zeros_sink, nb, mc, mc, has_sink=False)
            ow_c = flash_attn(qw_c, kdw_c, vdw_c, sink, nb, mc, mc, has_sink=True)
            xc, h_c, lg_c = post_mixer(oa_c, ow_c, y_c, uxs_c, p_c, xc, dskip, norm_w, wa, ww, ws, wo,
                                       g_mix_post[i][None, :], mod_c[2], g_ffn_pre[i][None, :], mod_c[3], mod_c[4],
                                       wr_hi, wr_lo, t_ctx)
            h_all = jnp.concatenate([h_l, h_c], axis=0)
            lg_all = jnp.concatenate([lg_l, lg_c], axis=1)
            dest_t, wts, ys = _moe_routed(h_all, lg_all, b_router[i], *moe_w)
            xl = ffn_tail(dest_t[:, :t_lat], wts[:t_lat], h_l, xl, ys, wsg, wsu, wsd,
                          g_ffn_post[i][None, :], mod_l[5], n)
            xc = ffn_tail(dest_t[:, t_lat:], wts[t_lat:], h_c, xc, ys, wsg, wsu, wsd,
                          g_ffn_post[i][None, :], mod_c[5], t_ctx)
    return xl.reshape(nb, n, d)
```

```python
import functools
import math

import jax
import jax.numpy as jnp
import numpy as np
from jax import lax
from jax.experimental import pallas as pl
from jax.experimental.pallas import tpu as pltpu

F32 = jnp.float32
BF16 = jnp.bfloat16

HEAD_DIM = 64
N_HEADS = 8
N_KV = 2
GRID_W = 64
ROPE_THETA = 10000.0
WINDOW = 128
SSM_HEADS = 16
SSM_P = 64
SSM_INNER = SSM_HEADS * SSM_P
SSM_GROUPS = 2
SSM_STATE = 128
SSM_CHUNK = 128
N_EXPERTS = 64
TOP_K = 8
N_EXPERT_GROUPS = 8
TOPK_GROUPS = 4
ROUTED_SCALE = 2.5
EPS = 1e-6

LANES = 128
HALF = LANES // 2
VMEM_LIMIT = 56 * 1024 * 1024
NEG_BIG = -1e30

C_QA, C_QW, C_Z, C_XS, C_GATES = 0, 512, 1024, 2048, 3072
C_KA, C_VA, C_KW, C_VW, C_BC = 6144, 6272, 6400, 6528, 6656
P_WIDTH = 7168


def _params(sem, vmem=VMEM_LIMIT):
    return pltpu.CompilerParams(dimension_semantics=sem, vmem_limit_bytes=vmem)


def _silu(x):
    return x * jax.nn.sigmoid(x)


U32 = jnp.uint32
SLAB = 4
HI_MASK = 0xFFFF0000


def _store_rows_as_slabs(ref, x):
    r = x.shape[0]

    def bits(v):
        return pltpu.bitcast(v.astype(BF16).astype(F32), U32)

    for c in range(SLAB):
        lo = bits(x[:, c * LANES:(c + 1) * LANES]) >> 16
        hi = bits(x[:, (SLAB + c) * LANES:(SLAB + c + 1) * LANES]) & jnp.uint32(HI_MASK)
        ref[pl.ds(c, r, stride=SLAB), :] = hi | lo


def _load_slabs_as_rows(ref):
    r = ref.shape[0] // SLAB
    words = [ref[pl.ds(c, r, stride=SLAB), :] for c in range(SLAB)]
    los = [pltpu.bitcast(w << 16, F32) for w in words]
    his = [pltpu.bitcast(w & jnp.uint32(HI_MASK), F32) for w in words]
    return jnp.concatenate(los + his, axis=1)


def _softplus(x):
    return jnp.maximum(x, 0.0) + jnp.log(1.0 + jnp.exp(-jnp.abs(x)))


def _rms(x, eps=EPS):
    return x * lax.rsqrt(jnp.mean(x * x, axis=-1, keepdims=True) + eps)


def _split3(a):
    a1 = a.astype(BF16)
    r1 = a - a1.astype(F32)
    a2 = r1.astype(BF16)
    a3 = (r1 - a2.astype(F32)).astype(BF16)
    return a1, a2, a3


def _dot(a, b):
    return jnp.dot(a, b, preferred_element_type=F32)


def _dot_nt(a, b):
    return lax.dot_general(a, b, (((1,), (1,)), ((), ())), preferred_element_type=F32)


def _exact_right(a, r01, pieces=3):
    return sum(_dot(p, r01) for p in _split3(a)[:pieces])


def _exact_left(m01, a, pieces=3):
    return sum(_dot(m01, p) for p in _split3(a)[:pieces])


def _ada_kernel(c_ref, w_ref, b_ref, o_ref):
    h = _silu(c_ref[...])
    o_ref[...] = jnp.dot(h, w_ref[...], preferred_element_type=F32,
                         precision=lax.Precision.HIGHEST) + b_ref[...]


def ada_mod(c8, w_all, b, layer):
    _, d, n = w_all.shape
    tn = 1536
    return pl.pallas_call(
        _ada_kernel,
        grid=(n // tn,),
        in_specs=[pl.BlockSpec((8, d), lambda j: (0, 0)),
                  pl.BlockSpec((None, d, tn), lambda j: (layer, 0, j)),
                  pl.BlockSpec((1, tn), lambda j: (0, j))],
        out_specs=pl.BlockSpec((8, tn), lambda j: (0, j)),
        out_shape=jax.ShapeDtypeStruct((8, n), F32),
        compiler_params=_params(("parallel",)),
        name="ada_mod",
    )(c8, w_all, b.reshape(1, n))


def _inproj_kernel(x_ref, g_ref, sh_ref, sc_ref, w_ref, wdt_ref, o_ref, odt_ref, h_scr):
    @pl.when(pl.program_id(1) == 0)
    def _():
        h = _rms(x_ref[...]) * g_ref[...]
        h = h * (1.0 + sc_ref[...]) + sh_ref[...]
        hb = h.astype(BF16)
        h_scr[...] = hb
        odt_ref[...] = _dot(hb, wdt_ref[...])

    o_ref[...] = _dot(h_scr[...], w_ref[...]).astype(BF16)


def in_proj(x, g, shift, scale, w_main, w_dt, rows_per_batch):
    t, d = x.shape
    n = w_main.shape[1]
    tm = min(1024, rows_per_batch)
    tn = n // 4
    tpb = rows_per_batch // tm
    mod_spec = pl.BlockSpec((None, 1, d), lambda i, j: (i // tpb, 0, 0))
    return pl.pallas_call(
        _inproj_kernel,
        grid=(t // tm, n // tn),
        in_specs=[pl.BlockSpec((tm, d), lambda i, j: (i, 0)),
                  pl.BlockSpec((1, d), lambda i, j: (0, 0)),
                  mod_spec, mod_spec,
                  pl.BlockSpec((d, tn), lambda i, j: (0, j)),
                  pl.BlockSpec((d, 2 * LANES), lambda i, j: (0, 0))],
        out_specs=[pl.BlockSpec((tm, tn), lambda i, j: (i, j)),
                   pl.BlockSpec((tm, 2 * LANES), lambda i, j: (i, 0))],
        out_shape=[jax.ShapeDtypeStruct((t, n), BF16),
                   jax.ShapeDtypeStruct((t, 2 * LANES), F32)],
        scratch_shapes=[pltpu.VMEM((tm, d), BF16)],
        compiler_params=_params(("parallel", "arbitrary")),
        name="in_proj",
    )(x, g.reshape(1, d), shift, scale, w_main, w_dt)


def _rope(x, cos, sin):
    w = x.shape[-1]
    lane = lax.broadcasted_iota(jnp.int32, x.shape, 1)
    first = (lane % 32) < 16
    swapped = jnp.where(first, pltpu.roll(x, w - 16, 1), pltpu.roll(x, 16, 1))
    return x * cos + swapped * sin


def _dup_halves(x):
    lane = lax.broadcasted_iota(jnp.int32, x.shape, 1)
    lo = lane < HALF
    r = pltpu.roll(x, HALF, 1)
    return jnp.concatenate([jnp.where(lo, x, r), jnp.where(lo, r, x)], axis=1)


def _prep_kernel(qa_ref, qw_ref, ka_ref, va_ref, kw_ref, vw_ref, cos_ref, sin_ref,
                 gq_ref, gk_ref, bdq_ref, bdk_ref,
                 qa_o, qw_o, kda_o, vda_o, kdw_o, vdw_o, *, rope):
    scale = HEAD_DIM ** -0.5
    inv_hd = 1.0 / HEAD_DIM

    def headnorm(x, g, bd):
        ss = _dot((x * x).astype(BF16), bd) * inv_hd
        return x * lax.rsqrt(ss + EPS) * g

    qa = headnorm(qa_ref[...].astype(F32), gq_ref[...], bdq_ref[...])
    ka = headnorm(ka_ref[...].astype(F32), gk_ref[...], bdk_ref[...])
    qw = qw_ref[...].astype(F32)
    kw = kw_ref[...].astype(F32)
    if rope:
        cos = cos_ref[...]
        sin = sin_ref[...]
        qa = _rope(qa, cos, sin)
        qw = _rope(qw, cos, sin)
        ka = _rope(ka, cos[:, :LANES], sin[:, :LANES])
        kw = _rope(kw, cos[:, :LANES], sin[:, :LANES])
    qa_o[...] = (qa * scale).astype(BF16)
    qw_o[...] = (qw * scale).astype(BF16)
    kda_o[...] = _dup_halves(ka).astype(BF16)
    kdw_o[...] = _dup_halves(kw).astype(BF16)
    vda_o[...] = _dup_halves(va_ref[...].astype(F32)).astype(BF16)
    vdw_o[...] = _dup_halves(vw_ref[...].astype(F32)).astype(BF16)


def attn_prep(p, cos, sin, gq, gk, nb, n, rope):
    t = nb * n
    tm = min(512, n)
    spb = n // tm
    hq = N_HEADS * HEAD_DIM
    hk = N_KV * HEAD_DIM
    bdq = (np.arange(hq)[:, None] // HEAD_DIM == np.arange(hq)[None, :] // HEAD_DIM)
    bdq = jnp.asarray(bdq, BF16)
    bdk = bdq[:hk, :hk]
    qspec = lambda c: pl.BlockSpec((tm, hq), lambda s, b: (b * spb + s, c // hq))
    kspec = lambda c: pl.BlockSpec((tm, hk), lambda s, b: (b * spb + s, c // hk))
    tab = pl.BlockSpec((tm, hq), lambda s, b: (s, 0))
    const = lambda shp: pl.BlockSpec(shp, lambda s, b: (0, 0))
    oq = pl.BlockSpec((tm, hq), lambda s, b: (b * spb + s, 0))
    ok = pl.BlockSpec((tm, 2 * hk), lambda s, b: (b * spb + s, 0))
    return pl.pallas_call(
        functools.partial(_prep_kernel, rope=rope),
        grid=(spb, nb),
        in_specs=[qspec(C_QA), qspec(C_QW), kspec(C_KA), kspec(C_VA), kspec(C_KW), kspec(C_VW),
                  tab, tab, const((1, hq)), const((1, hk)), const((hq, hq)), const((hk, hk))],
        out_specs=[oq, oq, ok, ok, ok, ok],
        out_shape=[jax.ShapeDtypeStruct((t, hq), BF16)] * 2 + [jax.ShapeDtypeStruct((t, 2 * hk), BF16)] * 4,
        compiler_params=_params(("parallel", "arbitrary")),
        name="attn_prep",
    )(p, p, p, p, p, p, cos, sin, gq, gk, bdq, bdk)


def rope_tables(n):
    rows = n // GRID_W
    row = jnp.repeat(jnp.arange(rows, dtype=F32), GRID_W)
    col = jnp.tile(jnp.arange(GRID_W, dtype=F32), rows)
    axis_dim = HEAD_DIM // 2
    inv_freq = ROPE_THETA ** (-jnp.arange(0, axis_dim, 2, dtype=F32) / axis_dim)
    ang_r = row[:, None] * inv_freq[None, :]
    ang_c = col[:, None] * inv_freq[None, :]
    cr, sr, cc, sc = jnp.cos(ang_r), jnp.sin(ang_r), jnp.cos(ang_c), jnp.sin(ang_c)
    cos = jnp.concatenate([cr, cr, cc, cc], axis=1)
    sin = jnp.concatenate([-sr, sr, -sc, sc], axis=1)
    return jnp.tile(cos, (1, N_HEADS)), jnp.tile(sin, (1, N_HEADS))


def _pair_operands(kd, vd):
    lane = lax.broadcasted_iota(jnp.int32, kd.shape, 1)
    lo = lane < HALF
    zero = jnp.zeros_like(kd)
    kmats = (jnp.where(lo, kd, zero), jnp.where(lo, zero, kd))
    vstack = jnp.concatenate([jnp.where(lo, vd, zero), jnp.where(lo, zero, vd)], axis=0)
    return kmats, vstack


def _flash_kernel(sink_ref, q_ref, k_ref, v_ref, o_ref, m_scr, l_scr, acc_scr, *, has_sink, nk):
    ki = pl.program_id(2)
    tq = q_ref.shape[0]

    @pl.when(ki == 0)
    def _():
        m_scr[...] = jnp.full(m_scr.shape, NEG_BIG, F32)
        l_scr[...] = jnp.zeros(l_scr.shape, F32)
        acc_scr[...] = jnp.zeros(acc_scr.shape, F32)

    lane_q = lax.broadcasted_iota(jnp.int32, (tq, LANES), 1)
    lo_q = lane_q < HALF
    pairs_per_kv = N_HEADS // N_KV // 2
    for j in range(N_KV):
        kmats, vstack = _pair_operands(k_ref[:, j * LANES:(j + 1) * LANES],
                                       v_ref[:, j * LANES:(j + 1) * LANES])
        for pp in range(pairs_per_kv):
            hp = j * pairs_per_kv + pp
            qp = q_ref[:, hp * LANES:(hp + 1) * LANES]
            ps, alphas = [], []
            for par in range(2):
                h = 2 * hp + par
                s = _dot_nt(qp, kmats[par])
                m_prev = m_scr[h]
                m_new = jnp.maximum(m_prev, jnp.max(s, axis=1, keepdims=True))
                alpha = jnp.exp(m_prev - m_new)
                p = jnp.exp(s - m_new[:, :1])
                l_scr[h] = alpha * l_scr[h] + jnp.sum(p, axis=1, keepdims=True)
                m_scr[h] = m_new
                ps.append(p.astype(BF16))
                alphas.append(alpha)
            pv = _dot(jnp.concatenate(ps, axis=1), vstack)
            sl = slice(hp * LANES, (hp + 1) * LANES)
            acc_scr[:, sl] = acc_scr[:, sl] * jnp.where(lo_q, alphas[0], alphas[1]) + pv

    @pl.when(ki == nk - 1)
    def _():
        for hp in range(N_HEADS // 2):
            ls = []
            for par in range(2):
                h = 2 * hp + par
                l = l_scr[h]
                if has_sink:
                    l = l + jnp.exp(sink_ref[h] - m_scr[h])
                ls.append(l)
            sl = slice(hp * LANES, (hp + 1) * LANES)
            o_ref[:, sl] = (acc_scr[:, sl] / jnp.where(lo_q, ls[0], ls[1])).astype(BF16)


def _flash_bounded_kernel(c_ref, q_ref, k_ref, v_ref, o_ref, lmin_ref, l_scr, acc_scr, *, nk):
    ki = pl.program_id(2)
    tq = q_ref.shape[0]
    tk = k_ref.shape[0]

    @pl.when(ki == 0)
    def _():
        l_scr[...] = jnp.zeros(l_scr.shape, F32)
        acc_scr[...] = jnp.zeros(acc_scr.shape, F32)

    c = c_ref[0]
    pairs_per_kv = N_HEADS // N_KV // 2
    for j in range(N_KV):
        kmats, vstack = _pair_operands(k_ref[:, j * LANES:(j + 1) * LANES],
                                       v_ref[:, j * LANES:(j + 1) * LANES])
        for pp in range(pairs_per_kv):
            hp = j * pairs_per_kv + pp
            qp = q_ref[:, hp * LANES:(hp + 1) * LANES]
            ps = []
            for par in range(2):
                h = 2 * hp + par
                p = jnp.exp(_dot_nt(qp, kmats[par]) - c)
                part = p[:, 0:LANES]
                for cb in range(1, tk // LANES):
                    part = part + p[:, cb * LANES:(cb + 1) * LANES]
                l_scr[h] = l_scr[h] + part
                ps.append(p.astype(BF16))
            sl = slice(hp * LANES, (hp + 1) * LANES)
            acc_scr[:, sl] = acc_scr[:, sl] + _dot(jnp.concatenate(ps, axis=1), vstack)

    @pl.when(ki == nk - 1)
    def _():
        lo_q = lax.broadcasted_iota(jnp.int32, (tq, LANES), 1) < HALF
        mins = []
        for hp in range(N_HEADS // 2):
            ls = [jnp.sum(l_scr[2 * hp + par], axis=1, keepdims=True) for par in range(2)]
            sl = slice(hp * LANES, (hp + 1) * LANES)
            o_ref[:, sl] = (acc_scr[:, sl] / jnp.where(lo_q, ls[0], ls[1])).astype(BF16)
            mins += [jnp.broadcast_to(jnp.min(l, axis=0, keepdims=True), (1, LANES)) for l in ls]
        lmin_ref[...] = jnp.concatenate(mins, axis=0)


FLASH_MIN_DENOM = 1e-30


def flash_attn_bounded(q, kd, vd, bound, nb, n, m):
    tq = min(2048, n)
    tk = _pick_tile(m, (768, 512, 256))
    nq, nk = n // tq, m // tk
    hq = N_HEADS * HEAD_DIM
    o, lmin = pl.pallas_call(
        functools.partial(_flash_bounded_kernel, nk=nk),
        grid=(nb, nq, nk),
        in_specs=[pl.BlockSpec(memory_space=pltpu.SMEM),
                  pl.BlockSpec((tq, hq), lambda b, i, k: (b * nq + i, 0)),
                  pl.BlockSpec((tk, 2 * LANES), lambda b, i, k: (b * nk + k, 0)),
                  pl.BlockSpec((tk, 2 * LANES), lambda b, i, k: (b * nk + k, 0))],
        out_specs=[pl.BlockSpec((tq, hq), lambda b, i, k: (b * nq + i, 0)),
                   pl.BlockSpec((N_HEADS, LANES), lambda b, i, k: (b * nq + i, 0))],
        out_shape=[jax.ShapeDtypeStruct((nb * n, hq), BF16),
                   jax.ShapeDtypeStruct((nb * nq * N_HEADS, LANES), F32)],
        scratch_shapes=[pltpu.VMEM((N_HEADS, tq, LANES), F32),
                        pltpu.VMEM((tq, hq), F32)],
        compiler_params=_params(("parallel", "parallel", "arbitrary")),
        name="flash_attn_bounded",
    )(bound, q, kd, vd)
    ok = jnp.min(lmin) > FLASH_MIN_DENOM
    return lax.cond(ok, lambda: o,
                    lambda: flash_attn(q, kd, vd, jnp.zeros((N_HEADS,), F32), nb, n, m, has_sink=False))


def _pick_tile(m, cands):
    for c in cands:
        if m % c == 0:
            return c
    raise ValueError(f"no tile for {m}")


def flash_attn(q, kd, vd, sink, nb, n, m, has_sink):
    tq = min(512, n)
    tk = _pick_tile(m, (768, 512, 256))
    nq, nk = n // tq, m // tk
    hq = N_HEADS * HEAD_DIM
    return pl.pallas_call(
        functools.partial(_flash_kernel, has_sink=has_sink, nk=nk),
        grid=(nb, nq, nk),
        in_specs=[pl.BlockSpec(memory_space=pltpu.SMEM),
                  pl.BlockSpec((tq, hq), lambda b, i, k: (b * nq + i, 0)),
                  pl.BlockSpec((tk, 2 * LANES), lambda b, i, k: (b * nk + k, 0)),
                  pl.BlockSpec((tk, 2 * LANES), lambda b, i, k: (b * nk + k, 0))],
        out_specs=pl.BlockSpec((tq, hq), lambda b, i, k: (b * nq + i, 0)),
        out_shape=jax.ShapeDtypeStruct((nb * n, hq), BF16),
        scratch_shapes=[pltpu.VMEM((N_HEADS, tq, LANES), F32),
                        pltpu.VMEM((N_HEADS, tq, LANES), F32),
                        pltpu.VMEM((tq, hq), F32)],
        compiler_params=_params(("parallel", "parallel", "arbitrary")),
        name="flash_attn",
    )(sink, q, kd, vd)


def _window_kernel(sink_ref, q_ref, kp_ref, km_ref, kn_ref, vp_ref, vm_ref, vn_ref, kc_ref, vc_ref,
                   o_ref, *, n, tq):
    i = pl.program_id(1)
    span = tq + 2 * WINDOW
    q0 = i * tq
    r = lax.broadcasted_iota(jnp.int32, (tq, span), 0)
    c = lax.broadcasted_iota(jnp.int32, (tq, span), 1)
    kpos = c + (q0 - WINDOW)
    ok = (c >= r) & (c <= r + 2 * WINDOW) & (kpos >= 0) & (kpos < n)
    lane_q = lax.broadcasted_iota(jnp.int32, (tq, LANES), 1)
    lo_q = lane_q < HALF
    k_all = jnp.concatenate([kp_ref[...], km_ref[...], kn_ref[...], kc_ref[...]], axis=0)
    v_all = jnp.concatenate([vp_ref[...], vm_ref[...], vn_ref[...], vc_ref[...]], axis=0)
    bias = jnp.concatenate([jnp.where(ok, 0.0, NEG_BIG), jnp.zeros((tq, kc_ref.shape[0]), F32)], axis=1)
    pairs_per_kv = N_HEADS // N_KV // 2
    for j in range(N_KV):
        js = slice(j * LANES, (j + 1) * LANES)
        kmats, vstack = _pair_operands(k_all[:, js], v_all[:, js])
        for pp in range(pairs_per_kv):
            hp = j * pairs_per_kv + pp
            qp = q_ref[:, hp * LANES:(hp + 1) * LANES]
            ps, ls = [], []
            for par in range(2):
                h = 2 * hp + par
                s = _dot_nt(qp, kmats[par]) + bias
                snk = sink_ref[h]
                m = jnp.maximum(jnp.max(s, axis=1, keepdims=True), snk)
                p = jnp.exp(s - m)
                ls.append(jnp.sum(p, axis=1, keepdims=True) + jnp.exp(snk - m))
                ps.append(p.astype(BF16))
            o = _dot(jnp.concatenate(ps, axis=1), vstack)
            o_ref[:, hp * LANES:(hp + 1) * LANES] = (o / jnp.where(lo_q, ls[0], ls[1])).astype(BF16)


def window_attn(q, kd, vd, kdc, vdc, sink, nb, n, mc):
    tq = 2 * WINDOW
    nq = n // tq
    wb = n // WINDOW
    hq = N_HEADS * HEAD_DIM
    prev = pl.BlockSpec((WINDOW, 2 * LANES), lambda b, i: (b * wb + jnp.maximum(2 * i - 1, 0), 0))
    main = pl.BlockSpec((tq, 2 * LANES), lambda b, i: (b * nq + i, 0))
    nxt = pl.BlockSpec((WINDOW, 2 * LANES), lambda b, i: (b * wb + jnp.minimum(2 * i + 2, wb - 1), 0))
    ctx = pl.BlockSpec((mc, 2 * LANES), lambda b, i: (b, 0))
    return pl.pallas_call(
        functools.partial(_window_kernel, n=n, tq=tq),
        grid=(nb, nq),
        in_specs=[pl.BlockSpec(memory_space=pltpu.SMEM),
                  pl.BlockSpec((tq, hq), lambda b, i: (b * nq + i, 0)),
                  prev, main, nxt, prev, main, nxt, ctx, ctx],
        out_specs=pl.BlockSpec((tq, hq), lambda b, i: (b * nq + i, 0)),
        out_shape=jax.ShapeDtypeStruct((nb * n, hq), BF16),
        compiler_params=_params(("parallel", "parallel")),
        name="window_attn",
    )(sink, q, kd, kd, kd, vd, vd, vd, kdc, vdc)


HALO = 16


def _conv_kernel(xm_ref, xp_ref, xn_ref, bm_ref, bp_ref, bn_ref, wx_ref, bx_ref, wb_ref, bb_ref,
                 ox_ref, ob_ref, *, nt):
    i = pl.program_id(1)
    has_prev = jnp.where(i > 0, 1.0, 0.0)
    has_next = jnp.where(i < nt - 1, 1.0, 0.0)

    def conv(m_ref, p_ref, n_ref, w_ref, b_ref, o_ref):
        x = m_ref[...].astype(F32)
        tl = x.shape[0]
        row = lax.broadcasted_iota(jnp.int32, x.shape, 0)
        before = p_ref[...].astype(F32)[HALO - 1:HALO, :] * has_prev
        after = n_ref[...].astype(F32)[0:1, :] * has_next
        xm1 = jnp.where(row == 0, before, pltpu.roll(x, 1, 0))
        xp1 = jnp.where(row == tl - 1, after, pltpu.roll(x, tl - 1, 0))
        w = w_ref[...]
        y = xm1 * w[0:1, :] + x * w[1:2, :] + xp1 * w[2:3, :] + b_ref[...]
        o_ref[...] = _silu(y).astype(BF16)

    conv(xm_ref, xp_ref, xn_ref, wx_ref, bx_ref, ox_ref)
    conv(bm_ref, bp_ref, bn_ref, wb_ref, bb_ref, ob_ref)


def ssm_conv(p, conv_w, conv_b, nb, n):
    tl = min(512, n)
    nt = n // tl
    hb = n // HALO
    hpt = tl // HALO
    cx, cb = SSM_INNER, 2 * SSM_GROUPS * SSM_STATE

    def specs(width, col):
        cblk = col // width
        return (pl.BlockSpec((tl, width), lambda b, i: (b * nt + i, cblk)),
                pl.BlockSpec((HALO, width), lambda b, i: (b * hb + jnp.maximum(i * hpt - 1, 0), cblk)),
                pl.BlockSpec((HALO, width), lambda b, i: (b * hb + jnp.minimum((i + 1) * hpt, hb - 1), cblk)))

    const = lambda shp: pl.BlockSpec(shp, lambda b, i: (0, 0))
    xm, xp, xn = specs(cx, C_XS)
    bm, bp, bn = specs(cb, C_BC)
    return pl.pallas_call(
        functools.partial(_conv_kernel, nt=nt),
        grid=(nb, nt),
        in_specs=[xm, xp, xn, bm, bp, bn, const((3, cx)), const((1, cx)), const((3, cb)), const((1, cb))],
        out_specs=[pl.BlockSpec((tl, cx), lambda b, i: (b * nt + i, 0)),
                   pl.BlockSpec((tl, cb), lambda b, i: (b * nt + i, 0))],
        out_shape=[jax.ShapeDtypeStruct((nb * n, cx), BF16), jax.ShapeDtypeStruct((nb * n, cb), BF16)],
        compiler_params=_params(("parallel", "parallel")),
        name="ssm_conv",
    )(p, p, p, p, p, p, conv_w[:, :cx], conv_b[:cx].reshape(1, cx), conv_w[:, cx:], conv_b[cx:].reshape(1, cb))


def _ssd_chunk(d, r0, xs_ref, bc_ref, dt_ref, bias, aneg, tri, rep, st_scr, y_ref):
    q = SSM_CHUNK
    rows = slice(r0, r0 + q)
    gw = SSM_INNER // SSM_GROUPS
    dt = _softplus(dt_ref[rows, :] + bias)
    a = dt * aneg
    ac = _exact_left(tri, a, pieces=2)
    act = ac.T
    acx = _exact_right(ac, rep, pieces=2)
    dtx = _dot(dt.astype(BF16), rep)
    totx = acx[q - 1:q, :] if d == 0 else acx[0:1, :]
    xd = xs_ref[rows, :].astype(F32) * dtx
    xd_b = xd.astype(BF16)
    xe = (xd * jnp.exp(totx - acx)).astype(BF16)
    ein = jnp.exp(acx)
    keep = tri > 0
    lane = lax.broadcasted_iota(jnp.int32, (q, LANES), 1)
    lo = lane < HALF
    zero = jnp.zeros((q, LANES), BF16)
    hpg = SSM_HEADS // SSM_GROUPS
    for g in range(SSM_GROUPS):
        bg = bc_ref[rows, g * SSM_STATE:(g + 1) * SSM_STATE]
        cg = bc_ref[rows, (SSM_GROUPS + g) * SSM_STATE:(SSM_GROUPS + g + 1) * SSM_STATE]
        cb = _dot_nt(cg, bg)
        st = st_scr[d, g]
        yoff = _dot(cg, st.astype(BF16)) * ein[:, g * gw:(g + 1) * gw]
        for hp in range(hpg // 2):
            gs = []
            for par in range(2):
                h = g * hpg + 2 * hp + par
                seg = ac[:, h:h + 1] - act[h:h + 1, :]
                gs.append((cb * jnp.exp(jnp.where(keep, seg, NEG_BIG))).astype(BF16))
            c0 = g * gw + hp * LANES
            xp = xd_b[:, c0:c0 + LANES]
            xstack = jnp.concatenate([jnp.where(lo, xp, zero), jnp.where(lo, zero, xp)], axis=0)
            ydiag = _dot(jnp.concatenate(gs, axis=1), xstack)
            y_ref[rows, c0:c0 + LANES] = (ydiag + yoff[:, hp * LANES:(hp + 1) * LANES]).astype(BF16)
        bgt = bg.astype(F32).T.astype(BF16)
        cs = _dot(bgt, xe[:, g * gw:(g + 1) * gw])
        st_scr[d, g] = st * jnp.exp(totx[:, g * gw:(g + 1) * gw]) + cs


def _ssd_kernel(xf_ref, bcf_ref, dtf_ref, xb_ref, bcb_ref, dtb_ref, bias_ref, aneg_ref, tri_ref, rep_ref,
                s0_ref, yf_ref, yb_ref, sfin_ref, st_scr, *, nsteps):
    k = pl.program_id(1)

    @pl.when(k == 0)
    def _():
        st_scr[...] = s0_ref[...]

    rep = rep_ref[...]
    for c in range(SSD_CHUNKS_PER_STEP):
        _ssd_chunk(0, c * SSM_CHUNK, xf_ref, bcf_ref, dtf_ref, bias_ref[0], aneg_ref[0], tri_ref[0], rep,
                   st_scr, yf_ref)
        _ssd_chunk(1, (SSD_CHUNKS_PER_STEP - 1 - c) * SSM_CHUNK, xb_ref, bcb_ref, dtb_ref, bias_ref[1],
                   aneg_ref[1], tri_ref[1], rep, st_scr, yb_ref)

    @pl.when(k == nsteps - 1)
    def _():
        sfin_ref[...] = st_scr[...]


SSD_CHUNKS_PER_STEP = 2


def ssd_scan(u_xs, u_bc, dt_raw, bias2, aneg2, s0, nb, n):
    q = SSM_CHUNK
    nc = n // q
    gw = SSM_INNER // SSM_GROUPS
    idx = np.arange(q)
    tri = np.stack([idx[:, None] >= idx[None, :], idx[:, None] <= idx[None, :]]).astype(np.float32)
    rep = (np.arange(LANES)[:, None] == np.arange(SSM_INNER)[None, :] // SSM_P).astype(np.float32)

    ns = nc // SSD_CHUNKS_PER_STEP
    rb = q * SSD_CHUNKS_PER_STEP
    fwd = lambda b, k: b * ns + k
    bwd = lambda b, k: b * ns + (ns - 1 - k)
    cbc = 2 * SSM_GROUPS * SSM_STATE
    state = pl.BlockSpec((None, 2, SSM_GROUPS, SSM_STATE, gw), lambda b, k: (b, 0, 0, 0, 0))
    const = lambda shp: pl.BlockSpec(shp, lambda b, k: (0,) * len(shp))
    yf, yb, sfin = pl.pallas_call(
        functools.partial(_ssd_kernel, nsteps=ns),
        grid=(nb, ns),
        in_specs=[pl.BlockSpec((rb, SSM_INNER), lambda b, k: (fwd(b, k), 0)),
                  pl.BlockSpec((rb, cbc), lambda b, k: (fwd(b, k), 0)),
                  pl.BlockSpec((rb, LANES), lambda b, k: (fwd(b, k), 0)),
                  pl.BlockSpec((rb, SSM_INNER), lambda b, k: (bwd(b, k), 0)),
                  pl.BlockSpec((rb, cbc), lambda b, k: (bwd(b, k), 0)),
                  pl.BlockSpec((rb, LANES), lambda b, k: (bwd(b, k), 1)),
                  const((2, 1, LANES)), const((2, 1, LANES)), const((2, q, q)), const((LANES, SSM_INNER)),
                  state],
        out_specs=[pl.BlockSpec((rb, SSM_INNER), lambda b, k: (fwd(b, k), 0)),
                   pl.BlockSpec((rb, SSM_INNER), lambda b, k: (bwd(b, k), 0)),
                   state],
        out_shape=[jax.ShapeDtypeStruct((nb * n, SSM_INNER), BF16),
                   jax.ShapeDtypeStruct((nb * n, SSM_INNER), BF16),
                   jax.ShapeDtypeStruct((nb, 2, SSM_GROUPS, SSM_STATE, gw), F32)],
        scratch_shapes=[pltpu.VMEM((2, SSM_GROUPS, SSM_STATE, gw), F32)],
        compiler_params=_params(("parallel", "arbitrary")),
        name="ssd_scan",
    )(u_xs, u_bc, dt_raw, u_xs, u_bc, dt_raw, bias2, aneg2, jnp.asarray(tri, BF16), jnp.asarray(rep, BF16), s0)
    return (yf, yb), sfin


def _post_kernel(oa_ref, ow_ref, yf_ref, yb_ref, xs_ref, z_ref, gt_ref, x_ref,
                 dsk_ref, nw_ref, wa_ref, ww_ref, ws_ref, wo_ref, gpost_ref, gate_ref,
                 gpre_ref, sh_ref, sc_ref, wrh_ref, wrl_ref,
                 xo_ref, h_ref, lg_ref):
    y = yf_ref[...].astype(F32) + yb_ref[...].astype(F32) + dsk_ref[...] * xs_ref[...].astype(F32)
    u = y * _silu(z_ref[...].astype(F32))
    gw = SSM_INNER // SSM_GROUPS
    ys = jnp.concatenate([_rms(u[:, g * gw:(g + 1) * gw]) for g in range(SSM_GROUPS)], axis=1)
    ys = (ys * nw_ref[...]).astype(BF16)
    d = x_ref.shape[1]
    ga = jax.nn.sigmoid(gt_ref[:, 0:d].astype(F32))
    gw_ = jax.nn.sigmoid(gt_ref[:, d:2 * d].astype(F32))
    gs = jax.nn.sigmoid(gt_ref[:, 2 * d:3 * d].astype(F32))
    m = ga * _dot(oa_ref[...], wa_ref[...]) + gw_ * _dot(ow_ref[...], ww_ref[...]) + gs * _dot(ys, ws_ref[...])
    ml = _dot(m.astype(BF16), wo_ref[...])
    xn = x_ref[...] + gate_ref[...] * (_rms(ml) * gpost_ref[...])
    xo_ref[...] = xn
    h = (_rms(xn) * gpre_ref[...]) * (1.0 + sc_ref[...]) + sh_ref[...]
    _store_rows_as_slabs(h_ref, h)
    hb = h.astype(BF16)
    hl =(h - hb.astype(F32)).astype(BF16)
    lg_ref[...] = _dot_nt(wrh_ref[...], hb) + _dot_nt(wrh_ref[...], hl) + _dot_nt(wrl_ref[...], hb)


def post_mixer(oa, ow, y2, u_xs, p, x, dskip, norm_w, wa, ww, ws, wo, gpost, gate, gpre, shift, scale,
               wr_hi, wr_lo, rows_per_batch):
    t, d = x.shape
    tm = min(256, rows_per_batch)
    tpb = rows_per_batch // tm
    nt = t // tm
    hq = N_HEADS * HEAD_DIM
    row = lambda w, c=0: pl.BlockSpec((tm, w), lambda i: (i, c // w))
    const = lambda shp: pl.BlockSpec(shp, lambda i: (0,) * len(shp))
    mod = pl.BlockSpec((None, 1, d), lambda i: (i // tpb, 0, 0))
    return pl.pallas_call(
        _post_kernel,
        grid=(nt,),
        in_specs=[row(hq), row(hq), row(SSM_INNER), row(SSM_INNER),
                  row(SSM_INNER), row(SSM_INNER, C_Z), row(3 * d, C_GATES), row(d),
                  const((1, SSM_INNER)), const((1, SSM_INNER)),
                  const((hq, d)), const((hq, d)), const((SSM_INNER, d)), const((d, d)),
                  const((1, d)), mod, const((1, d)), mod, mod,
                  const((N_EXPERTS, d)), const((N_EXPERTS, d))],
        out_specs=[row(d), pl.BlockSpec((tm * SLAB, LANES), lambda i: (i, 0)),
                   pl.BlockSpec((N_EXPERTS, tm), lambda i: (0, i))],
        out_shape=[jax.ShapeDtypeStruct((t, d), F32), jax.ShapeDtypeStruct((t * SLAB, LANES), U32),
                   jax.ShapeDtypeStruct((N_EXPERTS, t), F32)],
        compiler_params=_params(("parallel",)),
        name="post_mixer",
    )(oa, ow, y2[0], y2[1], u_xs, p, p, x, dskip, norm_w, wa, ww, ws, wo, gpost, gate, gpre, shift, scale,
      wr_hi, wr_lo)


def _route_kernel(lg_ref, b_ref, ei_ref, w_ref):
    scores = jax.nn.sigmoid(lg_ref[...])
    sel = scores + b_ref[...]
    tt = sel.shape[1]
    per = N_EXPERTS // N_EXPERT_GROUPS
    r8 = lax.broadcasted_iota(jnp.int32, (per, tt), 0).astype(F32)
    ninf = -jnp.inf

    def argmax_rows(x, rows, nrows):
        m = jnp.max(x, axis=0, keepdims=True)
        idx = jnp.min(jnp.where(x == m, rows, float(nrows)), axis=0, keepdims=True)
        return m, idx

    gscores = []
    for g in range(N_EXPERT_GROUPS):
        blk = sel[g * per:(g + 1) * per, :]
        m1, i1 = argmax_rows(blk, r8, per)
        m2 = jnp.max(jnp.where(r8 == i1, ninf, blk), axis=0, keepdims=True)
        gscores.append(m1 + m2)
    cur = jnp.concatenate(gscores, axis=0)
    rg = lax.broadcasted_iota(jnp.int32, cur.shape, 0).astype(F32)
    chosen = jnp.zeros(cur.shape, F32)
    for _ in range(TOPK_GROUPS):
        _, gi = argmax_rows(cur, rg, N_EXPERT_GROUPS)
        hit = rg == gi
        chosen = jnp.where(hit, 1.0, chosen)
        cur = jnp.where(hit, ninf, cur)
    gmask = jnp.concatenate([jnp.broadcast_to(chosen[g:g + 1, :], (per, tt)) for g in range(N_EXPERT_GROUPS)],
                            axis=0)
    cur = jnp.where(gmask > 0, sel, ninf)
    re = lax.broadcasted_iota(jnp.int32, cur.shape, 0).astype(F32)
    idxs, ws = [], []
    for _ in range(TOP_K):
        _, ei = argmax_rows(cur, re, N_EXPERTS)
        hit = re == ei
        ws.append(jnp.sum(jnp.where(hit, scores, 0.0), axis=0, keepdims=True))
        idxs.append(ei)
        cur = jnp.where(hit, ninf, cur)
    w = jnp.concatenate(ws, axis=0)
    w_ref[...] = w / jnp.sum(w, axis=0, keepdims=True) * ROUTED_SCALE
    ei_ref[...] = jnp.concatenate(idxs, axis=0).astype(jnp.int32)


def route(logits_t, b_router):
    e, t = logits_t.shape
    tt = 512
    return pl.pallas_call(
        _route_kernel,
        grid=(t // tt,),
        in_specs=[pl.BlockSpec((e, tt), lambda i: (0, i)), pl.BlockSpec((e, 1), lambda i: (0, 0))],
        out_specs=[pl.BlockSpec((TOP_K, tt), lambda i: (0, i)), pl.BlockSpec((TOP_K, tt), lambda i: (0, i))],
        out_shape=[jax.ShapeDtypeStruct((TOP_K, t), jnp.int32), jax.ShapeDtypeStruct((TOP_K, t), F32)],
        compiler_params=_params(("parallel",)),
        name="route",
    )(logits_t, b_router.reshape(e, 1))


MOE_ROWS = 1024
PLAN_TOKENS = 512


def _moe_geometry(t):
    nblk = -(-(t * TOP_K + N_EXPERTS * (MOE_ROWS - 1)) // MOE_ROWS)
    return nblk, nblk * MOE_ROWS


def _plan_kernel(ei_ref, ut_ref, tril_ref, dest_ref, tab_ref, be_ref, cnt_scr, run_scr):
    ph = pl.program_id(0)
    i = pl.program_id(1)
    ei = ei_ref[...]
    tt = ei.shape[1]
    re = lax.broadcasted_iota(jnp.int32, (N_EXPERTS, tt), 0)
    hits = [re == ei[k:k + 1, :] for k in range(TOP_K)]
    oh = jnp.zeros((N_EXPERTS, tt), F32)
    for k in range(TOP_K):
        oh = oh + jnp.where(hits[k], 1.0, 0.0)

    @pl.when((ph == 0) & (i == 0))
    def _():
        cnt_scr[...] = jnp.zeros(cnt_scr.shape, F32)

    @pl.when(ph == 0)
    def _():
        cnt_scr[...] = cnt_scr[...] + jnp.sum(oh, axis=1, keepdims=True)

    @pl.when((ph == 1) & (i == 0))
    def _():
        cnt = cnt_scr[...]
        padded = ((cnt.astype(jnp.int32) + (MOE_ROWS - 1)) & (-MOE_ROWS)).astype(F32)
        pad_end = _exact_left(tril_ref[...], padded)
        pad_start = pad_end - padded
        run_scr[...] = pad_start
        tab_ref[0] = pad_start
        tab_ref[1] = pad_end
        nbp = be_ref.shape[1]
        blk0 = (lax.broadcasted_iota(jnp.int32, (N_EXPERTS, nbp), 1) * MOE_ROWS).astype(F32)
        be = jnp.sum(jnp.where(pad_end[:, :1] <= blk0, 1.0, 0.0), axis=0, keepdims=True)
        be_ref[...] = jnp.broadcast_to(jnp.minimum(be, N_EXPERTS - 1.0), be_ref.shape).astype(jnp.int32)

    @pl.when(ph == 1)
    def _():
        cin = _dot(oh.astype(BF16), ut_ref[...])
        pos = run_scr[:, :1] + (cin - oh)
        rows = [jnp.sum(jnp.where(hits[k], pos, 0.0), axis=0, keepdims=True) for k in range(TOP_K)]
        dest_ref[...] = jnp.concatenate(rows, axis=0).astype(jnp.int32)
        run_scr[...] = run_scr[...] + cin[:, tt - 1:tt]


def moe_plan(eidx_t):
    k, t = eidx_t.shape
    tt = PLAN_TOKENS
    nt = t // tt
    nblk, _ = _moe_geometry(t)
    nbp = -(-nblk // LANES) * LANES
    ut = jnp.asarray(np.arange(tt)[:, None] <= np.arange(tt)[None, :], BF16)
    tril = jnp.asarray(np.arange(N_EXPERTS)[:, None] >= np.arange(N_EXPERTS)[None, :], BF16)
    return pl.pallas_call(
        _plan_kernel,
        grid=(2, nt),
        in_specs=[pl.BlockSpec((k, tt), lambda p, i: (0, i)),
                  pl.BlockSpec((tt, tt), lambda p, i: (0, 0)),
                  pl.BlockSpec((N_EXPERTS, N_EXPERTS), lambda p, i: (0, 0))],
        out_specs=[pl.BlockSpec((k, tt), lambda p, i: (0, i * p)),
                   pl.BlockSpec((2, N_EXPERTS, LANES), lambda p, i: (0, 0, 0)),
                   pl.BlockSpec((8, nbp), lambda p, i: (0, 0))],
        out_shape=[jax.ShapeDtypeStruct((k, t), jnp.int32),
                   jax.ShapeDtypeStruct((2, N_EXPERTS, LANES), F32),
                   jax.ShapeDtypeStruct((8, nbp), jnp.int32)],
        scratch_shapes=[pltpu.VMEM((N_EXPERTS, LANES), F32), pltpu.VMEM((N_EXPERTS, LANES), F32)],
        compiler_params=_params(("arbitrary", "arbitrary")),
        name="moe_plan",
    )(eidx_t, ut, tril)


def _dispatch_kernel(dest_ref, pstart_ref, pend_ref, h_ref, xs_hbm, zero_scr, sem):
    i = pl.program_id(0)
    tt = h_ref.shape[0] // SLAB

    def slab(ref, row, n=1):
        return ref.at[pl.ds(pl.multiple_of(row * SLAB, SLAB), n * SLAB), :]

    @pl.when(i == 0)
    def _():
        zero_scr[...] = jnp.zeros(zero_scr.shape, U32)
        nblk = xs_hbm.shape[0] // (MOE_ROWS * SLAB)
        n_used = pend_ref[N_EXPERTS - 1] // MOE_ROWS

        def zero_block(row0):
            return pltpu.make_async_copy(zero_scr, slab(xs_hbm, row0, MOE_ROWS), sem)

        def seg_start(e, c):
            @pl.when(pend_ref[e] > pstart_ref[e])
            def _():
                zero_block(pend_ref[e] - MOE_ROWS).start()
            return c

        def seg_wait(e, c):
            @pl.when(pend_ref[e] > pstart_ref[e])
            def _():
                zero_block(pend_ref[e] - MOE_ROWS).wait()
            return c

        def tail_start(b, c):
            zero_block(b * MOE_ROWS).start()
            return c

        def tail_wait(b, c):
            zero_block(b * MOE_ROWS).wait()
            return c

        lax.fori_loop(0, N_EXPERTS, seg_start, 0)
        lax.fori_loop(n_used, nblk, tail_start, 0)
        lax.fori_loop(0, N_EXPERTS, seg_wait, 0)
        lax.fori_loop(n_used, nblk, tail_wait, 0)

    def issue(t, c):
        for k in range(TOP_K):
            pltpu.make_async_copy(slab(h_ref, t), slab(xs_hbm, dest_ref[k, t]), sem).start(priority=k % 2)
        return c

    lax.fori_loop(0, tt, issue, 0)
    for k in range(TOP_K):
        pltpu.make_async_copy(h_ref, slab(xs_hbm, 0, tt), sem).wait()


def moe_dispatch(dest_t, pad_start, pad_end, h):
    t = h.shape[0] // SLAB
    tt = PLAN_TOKENS
    _, cap = _moe_geometry(t)
    smem = pl.BlockSpec(memory_space=pltpu.SMEM)
    return pl.pallas_call(
        _dispatch_kernel,
        grid=(t // tt,),
        in_specs=[pl.BlockSpec((TOP_K, tt), lambda i: (0, i), memory_space=pltpu.SMEM), smem, smem,
                  pl.BlockSpec((tt * SLAB, LANES), lambda i: (i, 0))],
        out_specs=pl.BlockSpec(memory_space=pl.ANY),
        out_shape=jax.ShapeDtypeStruct((cap * SLAB, LANES), U32),
        scratch_shapes=[pltpu.VMEM((MOE_ROWS * SLAB, LANES), U32), pltpu.SemaphoreType.DMA(())],
        compiler_params=_params(("arbitrary",)),
        name="moe_dispatch",
    )(dest_t, pad_start, pad_end, h)


def _expert_kernel(be_ref, nu_ref, x_ref, wg_ref, wu_ref, wd_ref, o_ref):
    used = pl.program_id(0) < nu_ref[0]

    @pl.when(used)
    def _():
        x = _load_slabs_as_rows(x_ref).astype(BF16)
        hg = _dot(x, wg_ref[...].astype(BF16))
        hu = _dot(x, wu_ref[...].astype(BF16))
        hb = (_silu(hg) * hu).astype(BF16)
        _store_rows_as_slabs(o_ref, _dot(hb, wd_ref[...].astype(BF16)))

    @pl.when(jnp.logical_not(used))
    def _():
        o_ref[...] = jnp.zeros(o_ref.shape, U32)


def expert_ffn(blk_e, n_used, xs, we_gate, we_up, we_down, layer):
    d, de = we_gate.shape[2:]
    nblk = xs.shape[0] // (MOE_ROWS * SLAB)
    blk = lambda i, nu: jnp.minimum(i, nu[0] - 1)
    grid_spec = pltpu.PrefetchScalarGridSpec(
        num_scalar_prefetch=2,
        grid=(nblk,),
        in_specs=[pl.BlockSpec((MOE_ROWS * SLAB, LANES), lambda i, be, nu: (blk(i, nu), 0)),
                  pl.BlockSpec((None, None, d, de), lambda i, be, nu: (layer, be[blk(i, nu)], 0, 0)),
                  pl.BlockSpec((None, None, d, de), lambda i, be, nu: (layer, be[blk(i, nu)], 0, 0)),
                  pl.BlockSpec((None, None, de, d), lambda i, be, nu: (layer, be[blk(i, nu)], 0, 0))],
        out_specs=pl.BlockSpec((MOE_ROWS * SLAB, LANES), lambda i, be, nu: (i, 0)),
    )
    return pl.pallas_call(
        _expert_kernel,
        grid_spec=grid_spec,
        out_shape=jax.ShapeDtypeStruct(xs.shape, U32),
        compiler_params=_params(("arbitrary",)),
        name="expert_ffn",
    )(blk_e, n_used, xs, we_gate, we_up, we_down)


def _ffn_tail_kernel(dest_ref, dnext_ref, w_ref, h_ref, x_ref, wg_ref, wu_ref, wd_ref, gpost_ref, gate_ref, ys_hbm,
                     xo_ref, buf, sem, *, nt):
    i = pl.program_id(0)
    tt = h_ref.shape[0] // SLAB
    cur = i % 2

    def slab(ref, row, n=1):
        return ref.at[pl.ds(pl.multiple_of(row * SLAB, SLAB), n * SLAB), :]

    def gather(idx_ref, slot):
        def issue(t, c):
            for k in range(TOP_K):
                pltpu.make_async_copy(slab(ys_hbm, idx_ref[k, t]), slab(buf.at[slot, k], t),
                                      sem.at[slot]).start(priority=k % 2)
            return c
        lax.fori_loop(0, tt, issue, 0)

    @pl.when(i == 0)
    def _():
        gather(dest_ref, 0)

    @pl.when(i + 1 < nt)
    def _():
        gather(dnext_ref, 1 - cur)

    h = _load_slabs_as_rows(h_ref).astype(BF16)
    hs = (_silu(_dot(h, wg_ref[...])) * _dot(h, wu_ref[...])).astype(BF16)
    f = _dot(hs, wd_ref[...])
    w = w_ref[...]
    for k in range(TOP_K):
        pltpu.make_async_copy(slab(ys_hbm, 0, tt), buf.at[cur, k], sem.at[cur]).wait()
    for k in range(TOP_K):
        f = f + _load_slabs_as_rows(buf.at[cur, k]) * w[:, k:k + 1]
    xo_ref[...] = x_ref[...] + gate_ref[...] * (_rms(f) * gpost_ref[...])


def ffn_tail(dest_t, wts, h, x, ys, wsg, wsu, wsd, gpost, gate, rows_per_batch):
    t, d = x.shape
    ds = wsg.shape[1]
    tm = min(256, rows_per_batch)
    tpb = rows_per_batch // tm
    nt = t // tm
    row = pl.BlockSpec((tm, d), lambda i: (i, 0))
    const = lambda shp: pl.BlockSpec(shp, lambda i: (0,) * len(shp))
    return pl.pallas_call(
        functools.partial(_ffn_tail_kernel, nt=nt),
        grid=(nt,),
        in_specs=[pl.BlockSpec((TOP_K, tm), lambda i: (0, i), memory_space=pltpu.SMEM),
                  pl.BlockSpec((TOP_K, tm), lambda i: (0, jnp.minimum(i + 1, nt - 1)), memory_space=pltpu.SMEM),
                  pl.BlockSpec((tm, TOP_K), lambda i: (i, 0)),
                  pl.BlockSpec((tm * SLAB, LANES), lambda i: (i, 0)),
                  row, const((d, ds)), const((d, ds)), const((ds, d)), const((1, d)),
                  pl.BlockSpec((None, 1, d), lambda i: (i // tpb, 0, 0)),
                  pl.BlockSpec(memory_space=pl.ANY)],
        out_specs=row,
        out_shape=jax.ShapeDtypeStruct((t, d), F32),
        scratch_shapes=[pltpu.VMEM((2, TOP_K, tm * SLAB, LANES), U32), pltpu.SemaphoreType.DMA((2,))],
        compiler_params=_params(("arbitrary",)),
        name="ffn_tail",
    )(dest_t, dest_t, wts, h, x, wsg, wsu, wsd, gpost, gate, ys)


def _moe_routed(h, logits_t, b_router, we_gate, we_up, we_down, layer):
    eidx_t, wts_t = route(logits_t, b_router)
    dest_t, tabs, blk_e = moe_plan(eidx_t)
    pad_start = tabs[0, :, 0].astype(jnp.int32)
    pad_end = tabs[1, :, 0].astype(jnp.int32)
    n_used = (pad_end[N_EXPERTS - 1:] // MOE_ROWS).astype(jnp.int32)
    xs = moe_dispatch(dest_t, pad_start, pad_end, h)
    ys = expert_ffn(blk_e[0], n_used, xs, we_gate, we_up, we_down, layer)
    return dest_t, wts_t.T, ys


def _reorder_w_in(w):
    qa, ka, va, qw, kw, vw, z, xs, bm, cm, dt, gates = jnp.split(
        w, [512, 640, 768, 1280, 1408, 1536, 2560, 3584, 3840, 4096, 4128], axis=1)
    w_main = jnp.concatenate([qa, qw, z, xs, gates, ka, va, kw, vw, bm, cm], axis=1).astype(BF16)
    pad = jnp.zeros((w.shape[0], LANES - SSM_HEADS), w.dtype)
    w_dt = jnp.concatenate([dt[:, :SSM_HEADS], pad, dt[:, SSM_HEADS:], pad], axis=1).astype(BF16)
    return w_main, w_dt


def _pad_lanes(v):
    return jnp.pad(v, ((0, 0), (0, LANES - v.shape[1])))


def kernel(x, c, ctx, c_ctx, w_ada, b_ada, g_mix_pre, g_mix_post, g_ffn_pre, g_ffn_post, w_in, g_q_a, g_k_a, sink_w, ssm_conv_w, ssm_conv_b, ssm_dt_bias, ssm_a_log, ssm_d, ssm_norm, w_br_a, w_br_w, w_br_s, w_out, w_router, b_router, we_gate, we_up, we_down, ws_gate, ws_up, ws_down):
    nb, n, d = x.shape
    mc = ctx.shape[1]
    depth = w_in.shape[0]
    t_lat, t_ctx = nb * n, nb * mc
    cos, sin = rope_tables(n)
    xl = x.reshape(t_lat, d)
    xc = ctx.reshape(t_ctx, d)
    c8 = jnp.concatenate([c, c_ctx[None, :], jnp.zeros((8 - nb - 1, d), F32)], axis=0)
    zeros_sink = jnp.zeros((N_HEADS,), F32)
    s_zero = jnp.zeros((nb, 2, SSM_GROUPS, SSM_STATE, SSM_INNER // SSM_GROUPS), F32)
    dummy_tab = jnp.zeros((mc, N_HEADS * HEAD_DIM), F32)

    for i in range(depth):
        last = i == depth - 1
        mod = ada_mod(c8, w_ada, b_ada[i], i)
        mod_l = [mod[:nb, k * d:(k + 1) * d].reshape(nb, 1, d) for k in range(6)]
        mod_c = [mod[nb:nb + 1, k * d:(k + 1) * d].reshape(1, 1, d) for k in range(6)]
        w_main, w_dt = _reorder_w_in(w_in[i])
        gq = jnp.tile(g_q_a[i], N_HEADS)[None, :]
        gk = jnp.tile(g_k_a[i], N_KV)[None, :]
        bias2 = _pad_lanes(ssm_dt_bias[i].reshape(2, SSM_HEADS)).reshape(2, 1, LANES)
        aneg2 = _pad_lanes(-jnp.exp(ssm_a_log[i].astype(F32))).reshape(2, 1, LANES)
        dskip = jnp.repeat(ssm_d[i], SSM_P)[None, :]
        norm_w = ssm_norm[i][None, :]
        wa, ww, ws, wo = (w_br_a[i].astype(BF16), w_br_w[i].astype(BF16), w_br_s[i].astype(BF16),
                          w_out[i].astype(BF16))
        wr_t = w_router[i].T
        wr_hi = wr_t.astype(BF16)
        wr_lo = (wr_t - wr_hi.astype(F32)).astype(BF16)
        sink = sink_w[i].astype(F32)

        p_c, dt_c = in_proj(xc, g_mix_pre[i], mod_c[0], mod_c[1], w_main, w_dt, t_ctx)
        qa_c, qw_c, kda_c, vda_c, kdw_c, vdw_c = attn_prep(p_c, dummy_tab, dummy_tab, gq, gk, nb, mc, rope=False)
        uxs_c, ubc_c = ssm_conv(p_c, ssm_conv_w[i], ssm_conv_b[i], nb, mc)
        y_c, s_fin = ssd_scan(uxs_c, ubc_c, dt_c, bias2, aneg2, s_zero, nb, mc)

        p_l, dt_l = in_proj(xl, g_mix_pre[i], mod_l[0], mod_l[1], w_main, w_dt, n)
        qa, qw, kda, vda, kdw, vdw = attn_prep(p_l, cos, sin, gq, gk, nb, n, rope=True)
        m_all = n + mc
        kd_all = jnp.concatenate([kda.reshape(nb, n, -1), kda_c.reshape(nb, mc, -1)], axis=1).reshape(nb * m_all, -1)
        vd_all = jnp.concatenate([vda.reshape(nb, n, -1), vda_c.reshape(nb, mc, -1)], axis=1).reshape(nb * m_all, -1)
        score_bound = (math.sqrt(HEAD_DIM) * jnp.max(jnp.abs(g_q_a[i])) * jnp.max(jnp.abs(g_k_a[i]))).reshape(1)
        oa = flash_attn_bounded(qa, kd_all, vd_all, score_bound.astype(F32), nb, n, m_all)
        ow = window_attn(qw, kdw, vdw, kdw_c, vdw_c, sink, nb, n, mc)
        uxs, ubc = ssm_conv(p_l, ssm_conv_w[i], ssm_conv_b[i], nb, n)
        y_l, _ = ssd_scan(uxs, ubc, dt_l, bias2, aneg2, s_fin, nb, n)
        xl, h_l, lg_l = post_mixer(oa, ow, y_l, uxs, p_l, xl, dskip, norm_w, wa, ww, ws, wo,
                                   g_mix_post[i][None, :], mod_l[2], g_ffn_pre[i][None, :], mod_l[3], mod_l[4],
                                   wr_hi, wr_lo, n)
        wsg, wsu, wsd = ws_gate[i].astype(BF16), ws_up[i].astype(BF16), ws_down[i].astype(BF16)
        moe_w = (we_gate, we_up, we_down, i)
        if last:
            dest_t, wts, ys = _moe_routed(h_l, lg_l, b_router[i], *moe_w)
            xl = ffn_tail(dest_t, wts, h_l, xl, ys, wsg, wsu, wsd, g_ffn_post[i][None, :], mod_l[5], n)
        else:
            oa_c = flash_attn(qa_c, kda_c, vda_c, zeros_sink, nb, mc, mc, has_sink=False)
            ow_c = flash_attn(qw_c, kdw_c, vdw_c, sink, nb, mc, mc, has_sink=True)
            xc, h_c, lg_c = post_mixer(oa_c, ow_c, y_c, uxs_c, p_c, xc, dskip, norm_w, wa, ww, ws, wo,
                                       g_mix_post[i][None, :], mod_c[2], g_ffn_pre[i][None, :], mod_c[3], mod_c[4],
                                       wr_hi, wr_lo, t_ctx)
            h_all = jnp.concatenate([h_l, h_c], axis=0)
            lg_all = jnp.concatenate([lg_l, lg_c], axis=1)
            dest_t, wts, ys = _moe_routed(h_all, lg_all, b_router[i], *moe_w)
            xl = ffn_tail(dest_t[:, :t_lat], wts[:t_lat], h_l, xl, ys, wsg, wsu, wsd,
                          g_ffn_post[i][None, :], mod_l[5], n)
            xc = ffn_tail(dest_t[:, t_lat:], wts[t_lat:], h_c, xc, ys, wsg, wsu, wsd,
                          g_ffn_post[i][None, :], mod_c[5], t_ctx)
    return xl.reshape(nb, n, d)
```
